```python
import math
import jax
import jax.numpy as jnp
from jax import lax
import numpy as np

D_MODEL = 1024
BATCH = 4
SEQ = 4096
DEPTH = 2
DEC_BATCH = 32
DEC_SEQ = 4
PAST_LEN = 8192
PAGE_SIZE = 128

N_EVEN = (DEPTH + 1) // 2
N_ODD = DEPTH // 2
H_A = 4
DK_A = D_MODEL // 16
DV_A = D_MODEL // 8
CHUNK_RET = 128
H_B = 4
DH_B = D_MODEL // 16
DV_B = 2 * DH_B
Q_BLOCK = 128
H_C = 4
DK_C = D_MODEL // 8
DV_C = D_MODEL // 4
GATE_RANK = 16
GATE_TAU = 16.0
CHUNK_GLA = 64
D_FF = ((8 * D_MODEL // 3 + 127) // 128) * 128
CONV_W = 3
ROPE_THETA = 10000.0
EPS = 1e-6
AB_WIDTHS = (H_A * DK_A, H_A * DK_A, H_A * DV_A, H_A * DV_A, H_B * 2 * DH_B, H_B * 2 * DH_B, H_B * DV_B)
MIX_AB = H_A * DV_A + H_B * DV_B
C_WIDTHS = (H_C * DK_C, H_C * DK_C, H_C * DV_C, H_C * DV_C, GATE_RANK)
MIX_C = H_C * DV_C

kernel_name = 'hybrid_retention_diffattn_gla_convffn_step'


def _cuts(widths):
    return [int(v) for v in np.cumsum(widths)[:-1]]


def _chunk(L, C):
    return C if L % C == 0 else L


def _rmsnorm(x, g=None):
    xf = x.astype(jnp.float32)
    y = xf * lax.rsqrt(jnp.mean(xf * xf, axis=-1, keepdims=True) + EPS)
    if g is not None:
        y = y * g.astype(jnp.float32)
    return y.astype(x.dtype)


def _rotary(x, pos):
    d = x.shape[-1]
    inv = ROPE_THETA ** (-jnp.arange(0, d, 2, dtype=jnp.float32) / d)
    ang = pos.astype(jnp.float32)[:, None] * inv[None, :]
    shape = (ang.shape[0],) + (1,) * (x.ndim - 3) + (d // 2,)
    cos = jnp.cos(ang).reshape(shape)
    sin = jnp.sin(ang).reshape(shape)
    xf = x.astype(jnp.float32)
    x1, x2 = xf[..., : d // 2], xf[..., d // 2:]
    return jnp.concatenate([x1 * cos - x2 * sin, x2 * cos + x1 * sin], axis=-1).astype(x.dtype)


def _adaln(c, w, b):
    return jnp.split(jax.nn.silu(c) @ w + b, 6, axis=-1)


def _modulate(x, g, shift, scale):
    return _rmsnorm(x, g) * (1.0 + scale[:, None, :]) + shift[:, None, :]


def _residual(x, y, g, gate):
    return x + gate[:, None, :] * _rmsnorm(y, g)


def _retention(q, k, v, s0):
    B, L, H, dk = q.shape
    dv = v.shape[-1]
    C = _chunk(L, CHUNK_RET)
    n = L // C
    log_gamma = jnp.log1p(-jnp.exp2(-5.0 - jnp.arange(H, dtype=jnp.float32)))
    idx = jnp.arange(C, dtype=jnp.float32)
    rel = idx[:, None] - idx[None, :]
    dec_intra = jnp.exp(jnp.where(rel[None] >= 0, rel[None] * log_gamma[:, None, None], -jnp.inf))
    dec_q = jnp.exp((idx + 1.0)[:, None] * log_gamma[None, :])
    dec_k = jnp.exp((C - 1.0 - idx)[:, None] * log_gamma[None, :])
    dec_c = jnp.exp(C * log_gamma)

    def to_chunks(t):
        return t.astype(jnp.float32).reshape(B, n, C, H, t.shape[-1]).transpose(1, 0, 2, 3, 4)

    def step(S, xs):
        qc, kc, vc = xs
        sc = jnp.einsum('bihd,bjhd->bhij', qc, kc) * dec_intra[None]
        o = (jnp.einsum('bhij,bjhe->bihe', sc, vc)
             + jnp.einsum('bihd,bhde->bihe', qc, S) * dec_q[None, :, :, None])
        S = S * dec_c[None, :, None, None] + jnp.einsum('bjhd,bjhe->bhde', kc * dec_k[None, :, :, None], vc)
        return S, o

    S, o = lax.scan(step, s0.astype(jnp.float32), (to_chunks(q), to_chunks(k), to_chunks(v)))
    return o.transpose(1, 0, 2, 3, 4).reshape(B, L, H, dv), S


def _gla(q, k, v, logg, s0):
    B, L, H, dk = q.shape
    dv = v.shape[-1]
    C = _chunk(L, CHUNK_GLA)
    n = L // C
    causal = jnp.tril(jnp.ones((C, C), dtype=bool))

    def to_chunks(t):
        return t.astype(jnp.float32).reshape(B, n, C, H, t.shape[-1]).transpose(1, 0, 2, 3, 4)

    def step(S, xs):
        qc, kc, vc, gc = xs
        b = jnp.cumsum(gc, axis=1)
        diff = b[:, :, None] - b[:, None, :]
        dec = jnp.exp(jnp.where(causal[None, :, :, None, None], diff, -jnp.inf))
        att = jnp.einsum('bihd,bjhd,bijhd->bhij', qc, kc, dec)
        o = jnp.einsum('bhij,bjhe->bihe', att, vc) + jnp.einsum('bihd,bhde->bihe', qc * jnp.exp(b), S)
        b_last = b[:, -1]
        S = S * jnp.exp(b_last)[..., None] + jnp.einsum('bjhd,bjhe->bhde', kc * jnp.exp(b_last[:, None] - b), vc)
        return S, o

    S, o = lax.scan(step, s0.astype(jnp.float32), (to_chunks(q), to_chunks(k), to_chunks(v), to_chunks(logg)))
    return o.transpose(1, 0, 2, 3, 4).reshape(B, L, H, dv), S


def _diff_attend(q, k, v, q_pos, k_pos, lam):
    s = jnp.einsum('bqhmd,bkhmd->bhmqk', q, k).astype(jnp.float32) * (DH_B ** -0.5)
    mask = k_pos[None, :] <= q_pos[:, None]
    p = jax.nn.softmax(jnp.where(mask, s, -jnp.inf), axis=-1)
    a = p[:, :, 0] - lam * p[:, :, 1]
    return jnp.einsum('bhqk,bkhe->bqhe', a.astype(v.dtype), v)


def _diff_attn_blocks(q, k, v, lam):
    B, L = q.shape[:2]
    nb = L // Q_BLOCK
    pos = jnp.arange(L)
    qb = q.reshape(B, nb, Q_BLOCK, H_B, 2, DH_B).transpose(1, 0, 2, 3, 4, 5)
    pb = pos.reshape(nb, Q_BLOCK)
    out = lax.map(lambda xs: _diff_attend(xs[0], k, v, xs[1], pos, lam), (qb, pb))
    return out.transpose(1, 0, 2, 3, 4).reshape(B, L, H_B, DV_B)


def _ab_mixer(h, pos, s0, k_past, v_past, w_in, w_out, lam, lam_init, g_diff):
    B, L, _ = h.shape
    qa, ka, va, ga, qb, kb, vb = jnp.split(h @ w_in, _cuts(AB_WIDTHS), axis=-1)
    qa = _rotary(qa.reshape(B, L, H_A, DK_A), pos)
    ka = _rotary(ka.reshape(B, L, H_A, DK_A), pos) * (DK_A ** -0.5)
    o_ret, s_new = _retention(qa, ka, va.reshape(B, L, H_A, DV_A), s0)
    qb = _rotary(qb.reshape(B, L, H_B, 2, DH_B), pos)
    kb = _rotary(kb.reshape(B, L, H_B, 2, DH_B), pos)
    vb = vb.reshape(B, L, H_B, DV_B)
    if k_past is None:
        o_dif = _diff_attn_blocks(qb, kb, vb, lam)
    else:
        k_all = jnp.concatenate([k_past.astype(kb.dtype), kb], axis=1)
        v_all = jnp.concatenate([v_past.astype(vb.dtype), vb], axis=1)
        o_dif = _diff_attend(qb, k_all, v_all, pos, jnp.arange(k_all.shape[1]), lam)
    ret = _rmsnorm(o_ret.astype(h.dtype)) * jax.nn.silu(ga.reshape(B, L, H_A, DV_A))
    dif = _rmsnorm(o_dif, g_diff) * (1.0 - lam_init)
    mix = jnp.concatenate([ret.reshape(B, L, -1), dif.reshape(B, L, -1)], axis=-1)
    return mix @ w_out, s_new.astype(h.dtype), kb.reshape(B, L, H_B, 2 * DH_B), vb


def _gla_mixer(h, s0, w_in, w_gate, b_gate, g_norm, w_out):
    B, L, _ = h.shape
    q, k, v, r, a = jnp.split(h @ w_in, _cuts(C_WIDTHS), axis=-1)
    q = q.reshape(B, L, H_C, DK_C) * (DK_C ** -0.5)
    k = k.reshape(B, L, H_C, DK_C)
    v = v.reshape(B, L, H_C, DV_C)
    logg = (jax.nn.log_sigmoid((a @ w_gate + b_gate).astype(jnp.float32)) / GATE_TAU).reshape(B, L, H_C, DK_C)
    o, s_new = _gla(q, k, v, logg, s0)
    o = _rmsnorm(o.astype(h.dtype), g_norm) * jax.nn.silu(r.reshape(B, L, H_C, DV_C))
    return o.reshape(B, L, MIX_C) @ w_out, s_new.astype(h.dtype)


def _conv_ffn(h, buf, w_up, conv_w, conv_b, w_down):
    L = h.shape[1]
    up = jnp.concatenate([buf.astype(h.dtype), h @ w_up], axis=1)
    y = conv_b
    for i in range(CONV_W):
        y = y + conv_w[i] * up[:, i:i + L]
    a, b = jnp.split(y, 2, axis=-1)
    return (jax.nn.gelu(a) * b) @ w_down, up[:, L:]


def setup_inputs(seed: int = 0) -> dict:
    key = jax.random.key(seed)
    ks = iter(jax.random.split(key, 40))

    def nrm(shape, scale):
        return jax.random.normal(next(ks), shape, jnp.float32) * scale

    n_pages = PAST_LEN // PAGE_SIZE
    n_used = DEC_BATCH * n_pages
    n_phys = n_used + n_used // 4
    page_table = jax.random.permutation(next(ks), n_phys)[:n_used].reshape(DEC_BATCH, n_pages).astype(jnp.int32)
    d_in_ab = sum(AB_WIDTHS)
    d_in_c = sum(C_WIDTHS)
    return {
        'x_prompt': nrm((BATCH, SEQ, D_MODEL), 1.0),
        'x_sample': nrm((DEC_BATCH, DEC_SEQ, D_MODEL), 1.0),
        'cache_k': nrm((N_EVEN, n_phys, PAGE_SIZE, H_B, 2 * DH_B), 1.0),
        'cache_v': nrm((N_EVEN, n_phys, PAGE_SIZE, H_B, DV_B), 1.0),
        'state_ret': nrm((N_EVEN, DEC_BATCH, H_A, DK_A, DV_A), 1.0),
        'state_gla': nrm((N_ODD, DEC_BATCH, H_C, DK_C, DV_C), 1.0),
        'state_conv': nrm((DEPTH, DEC_BATCH, CONV_W - 1, 2 * D_FF), 1.0),
        'page_table': page_table,
        'c_prompt': nrm((BATCH, D_MODEL), 1.0),
        'c_sample': nrm((DEC_BATCH, D_MODEL), 1.0),
        'w_ada': nrm((DEPTH, D_MODEL, 6 * D_MODEL), 0.3 * D_MODEL ** -0.5),
        'b_ada': nrm((DEPTH, 6 * D_MODEL), 0.01),
        'g_pre_mix': 1.0 + nrm((DEPTH, D_MODEL), 0.05),
        'g_post_mix': 1.0 + nrm((DEPTH, D_MODEL), 0.05),
        'g_pre_ffn': 1.0 + nrm((DEPTH, D_MODEL), 0.05),
        'g_post_ffn': 1.0 + nrm((DEPTH, D_MODEL), 0.05),
        'w_in_ab': nrm((N_EVEN, D_MODEL, d_in_ab), D_MODEL ** -0.5),
        'w_out_ab': nrm((N_EVEN, MIX_AB, D_MODEL), MIX_AB ** -0.5),
        'lam_q1': nrm((N_EVEN, DH_B), 0.1),
        'lam_k1': nrm((N_EVEN, DH_B), 0.1),
        'lam_q2': nrm((N_EVEN, DH_B), 0.1),
        'lam_k2': nrm((N_EVEN, DH_B), 0.1),
        'g_diff': 1.0 + nrm((N_EVEN, DV_B), 0.05),
        'w_in_c': nrm((N_ODD, D_MODEL, d_in_c), D_MODEL ** -0.5),
        'w_gate_c': nrm((N_ODD, GATE_RANK, H_C * DK_C), GATE_RANK ** -0.5),
        'b_gate_c': nrm((N_ODD, H_C * DK_C), 0.1),
        'g_gla': 1.0 + nrm((N_ODD, DV_C), 0.05),
        'w_out_c': nrm((N_ODD, MIX_C, D_MODEL), MIX_C ** -0.5),
        'w_up': nrm((DEPTH, D_MODEL, 2 * D_FF), D_MODEL ** -0.5),
        'conv_w': nrm((DEPTH, CONV_W, 2 * D_FF), CONV_W ** -0.5),
        'conv_b': nrm((DEPTH, 2 * D_FF), 0.01),
        'w_down': nrm((DEPTH, D_FF, D_MODEL), D_FF ** -0.5),
    }


def reference(x_prompt, x_sample, cache_k, cache_v, state_ret, state_gla, state_conv, page_table,
              c_prompt, c_sample, w_ada, b_ada, g_pre_mix, g_post_mix, g_pre_ffn, g_post_ffn,
              w_in_ab, w_out_ab, lam_q1, lam_k1, lam_q2, lam_k2, g_diff,
              w_in_c, w_gate_c, b_gate_c, g_gla, w_out_c, w_up, conv_w, conv_b, w_down):
    pos_p = jnp.arange(SEQ)
    pos_s = PAST_LEN + jnp.arange(DEC_SEQ)
    xp, xs = x_prompt, x_sample
    k_p, v_p, k_s, v_s, ret_p, ret_s, gla_p, gla_s, conv_p, conv_s = ([] for _ in range(10))
    for l in range(DEPTH):
        mp = _adaln(c_prompt, w_ada[l], b_ada[l])
        ms = _adaln(c_sample, w_ada[l], b_ada[l])
        hp = _modulate(xp, g_pre_mix[l], mp[0], mp[1])
        hs = _modulate(xs, g_pre_mix[l], ms[0], ms[1])
        if l % 2 == 0:
            e = l // 2
            lam_init = 0.8 - 0.6 * math.exp(-0.3 * l)
            lam = (jnp.exp(jnp.sum(lam_q1[e].astype(jnp.float32) * lam_k1[e].astype(jnp.float32)))
                   - jnp.exp(jnp.sum(lam_q2[e].astype(jnp.float32) * lam_k2[e].astype(jnp.float32))) + lam_init)
            s0 = jnp.zeros((BATCH, H_A, DK_A, DV_A), jnp.float32)
            yp, sp, kp_new, vp_new = _ab_mixer(hp, pos_p, s0, None, None, w_in_ab[e], w_out_ab[e],
                                               lam, lam_init, g_diff[e])
            k_past = cache_k[e][page_table].reshape(DEC_BATCH, PAST_LEN, H_B, 2, DH_B)
            v_past = cache_v[e][page_table].reshape(DEC_BATCH, PAST_LEN, H_B, DV_B)
            ys, ss, ks_new, vs_new = _ab_mixer(hs, pos_s, state_ret[e], k_past, v_past, w_in_ab[e], w_out_ab[e],
                                               lam, lam_init, g_diff[e])
            k_p.append(kp_new)
            v_p.append(vp_new)
            k_s.append(ks_new)
            v_s.append(vs_new)
            ret_p.append(sp)
            ret_s.append(ss)
        else:
            o = l // 2
            s0 = jnp.zeros((BATCH, H_C, DK_C, DV_C), jnp.float32)
            yp, sp = _gla_mixer(hp, s0, w_in_c[o], w_gate_c[o], b_gate_c[o], g_gla[o], w_out_c[o])
            ys, ss = _gla_mixer(hs, state_gla[o], w_in_c[o], w_gate_c[o], b_gate_c[o], g_gla[o], w_out_c[o])
            gla_p.append(sp)
            gla_s.append(ss)
        xp = _residual(xp, yp, g_post_mix[l], mp[2])
        xs = _residual(xs, ys, g_post_mix[l], ms[2])
        hp = _modulate(xp, g_pre_ffn[l], mp[3], mp[4])
        hs = _modulate(xs, g_pre_ffn[l], ms[3], ms[4])
        buf_p = jnp.zeros((BATCH, CONV_W - 1, 2 * D_FF), hp.dtype)
        fp, cbp = _conv_ffn(hp, buf_p, w_up[l], conv_w[l], conv_b[l], w_down[l])
        fs, cbs = _conv_ffn(hs, state_conv[l], w_up[l], conv_w[l], conv_b[l], w_down[l])
        conv_p.append(cbp)
        conv_s.append(cbs)
        xp = _residual(xp, fp, g_post_ffn[l], mp[5])
        xs = _residual(xs, fs, g_post_ffn[l], ms[5])
    return (xp, xs, jnp.stack(k_p), jnp.stack(v_p), jnp.stack(k_s), jnp.stack(v_s),
            jnp.stack(ret_p), jnp.stack(ret_s), jnp.stack(gla_p), jnp.stack(gla_s),
            jnp.stack(conv_p), jnp.stack(conv_s))
```

```python
import functools
import math

import numpy as np
import jax
import jax.numpy as jnp
from jax import lax
from jax.experimental import pallas as pl
from jax.experimental.pallas import tpu as pltpu

D_MODEL = 1024
BATCH = 4
SEQ = 4096
DEPTH = 2
DEC_BATCH = 32
DEC_SEQ = 4
PAST_LEN = 8192
PAGE_SIZE = 128
N_EVEN = (DEPTH + 1) // 2
N_ODD = DEPTH // 2
H_A = 4
DK_A = D_MODEL // 16
DV_A = D_MODEL // 8
CHUNK_RET = 128
H_B = 4
DH_B = D_MODEL // 16
DV_B = 2 * DH_B
H_C = 4
DK_C = D_MODEL // 8
DV_C = D_MODEL // 4
GATE_RANK = 16
GATE_TAU = 16.0
CHUNK_GLA = 64
D_FF = ((8 * D_MODEL // 3 + 127) // 128) * 128
CONV_W = 3
ROPE_THETA = 10000.0
EPS = 1e-6

LANES = 128
VMEM_LIMIT = 56 * 1024 * 1024
ROW_TILE = 512
SAMPLE_PAD = 16
FF_CHUNK = 256
PAGES_PER_STEP = 4
BF16 = jnp.bfloat16
F32 = jnp.float32


def _cparams(*sem):
    return pltpu.CompilerParams(dimension_semantics=sem, vmem_limit_bytes=VMEM_LIMIT)


def _dot(a, b):
    return jnp.dot(a, b, preferred_element_type=F32)


def _dot_nt(a, b):
    return lax.dot_general(a, b, (((1,), (1,)), ((), ())), preferred_element_type=F32)


def _dot_tn(a, b):
    return lax.dot_general(a, b, (((0,), (0,)), ((), ())), preferred_element_type=F32)


def _rms(x):
    return x * lax.rsqrt(jnp.mean(x * x, axis=-1, keepdims=True) + EPS)


def _modulate(x, g, shift, scale):
    return (_rms(x) * g) * (1.0 + scale) + shift


def _rope(x, cos, sin_signed):
    w = x.shape[-1]
    fwd = pltpu.roll(x, 32, axis=1)
    bwd = pltpu.roll(x, w - 32, axis=1)
    reps = w // LANES
    first_half = (lax.broadcasted_iota(jnp.int32, x.shape, 1) % 64) < 32
    partner = jnp.where(first_half, bwd, fwd)
    return (x * jnp.concatenate([cos] * reps, axis=1)
            + partner * jnp.concatenate([sin_signed] * reps, axis=1))


def _ada_kernel(c_ref, w_ref, b_ref, o_ref):
    c = c_ref[...]
    s = c * jax.nn.sigmoid(c)
    o_ref[0] = jnp.dot(s, w_ref[0], preferred_element_type=F32, precision=lax.Precision.HIGHEST) + b_ref[0]


def _ada(c_all, w_ada, b_ada):
    rows = c_all.shape[0]
    tn = 1536
    return pl.pallas_call(
        _ada_kernel,
        grid=(DEPTH, 6 * D_MODEL // tn),
        in_specs=[pl.BlockSpec((rows, D_MODEL), lambda l, j: (0, 0)),
                  pl.BlockSpec((1, D_MODEL, tn), lambda l, j: (l, 0, j)),
                  pl.BlockSpec((1, 1, tn), lambda l, j: (l, 0, j))],
        out_specs=pl.BlockSpec((1, rows, tn), lambda l, j: (l, 0, j)),
        out_shape=jax.ShapeDtypeStruct((DEPTH, rows, 6 * D_MODEL), F32),
        compiler_params=_cparams("arbitrary", "arbitrary"),
        name="adaln",
    )(c_all, w_ada, b_ada.reshape(DEPTH, 1, 6 * D_MODEL))


def _row_cfg(rows, seq_len):
    tm = min(ROW_TILE, rows)
    per_token = seq_len < tm
    tiles_per_seq = 1 if per_token else seq_len // tm
    mod_spec = (pl.BlockSpec((1, tm, D_MODEL), lambda i: (i, 0, 0)) if per_token
                else pl.BlockSpec((1, 1, D_MODEL), lambda i: (i // tiles_per_seq, 0, 0)))
    return tm, per_token, tiles_per_seq, mod_spec


def _mod_arg(m, rows, seq_len, tm, per_token):
    if per_token:
        return jnp.repeat(m, seq_len, axis=0).reshape(rows // tm, tm, D_MODEL)
    return m.reshape(m.shape[0], 1, D_MODEL)


def _inab_kernel(x_ref, sh_ref, sc_ref, g_ref, w_ref, cos_ref, sin_ref,
                 qa_ref, ka_ref, va_ref, ga_ref, qb_ref, kb32_ref, kb16_ref, vb32_ref, vb16_ref):
    h = _modulate(x_ref[...], g_ref[...], sh_ref[0], sc_ref[0]).astype(BF16)
    cos = cos_ref[...]
    sin = sin_ref[...]

    def proj(lo, width):
        return _dot(h, w_ref[:, lo:lo + width])

    qa_ref[...] = _rope(proj(0, 256), cos, sin).astype(BF16)
    ka_ref[...] = (_rope(proj(256, 256), cos, sin) * (DK_A ** -0.5)).astype(BF16)
    va_ref[...] = proj(512, 512).astype(BF16)
    ga_ref[...] = proj(1024, 512).astype(BF16)
    qb_ref[...] = (_rope(proj(1536, 512), cos, sin) * (DH_B ** -0.5)).astype(BF16)
    kb = _rope(proj(2048, 512), cos, sin)
    kb32_ref[...] = kb
    kb16_ref[...] = kb.astype(BF16)
    vb = proj(2560, 512)
    vb32_ref[...] = vb
    vb16_ref[...] = vb.astype(BF16)


def _in_proj_ab(x, shift, scale, g_pre, w_bf, cos, sin, seq_len):
    rows = x.shape[0]
    tm, per_token, tps, mod_spec = _row_cfg(rows, seq_len)
    pos_map = (lambda i: (i, 0)) if per_token else (lambda i: (i % tps, 0))
    row = lambda w: pl.BlockSpec((tm, w), lambda i: (i, 0))
    widths = (256, 256, 512, 512, 512, 512, 512, 512, 512)
    dtypes = (BF16, BF16, BF16, BF16, BF16, F32, BF16, F32, BF16)
    return pl.pallas_call(
        _inab_kernel,
        grid=(rows // tm,),
        in_specs=[row(D_MODEL), mod_spec, mod_spec,
                  pl.BlockSpec((1, D_MODEL), lambda i: (0, 0)),
                  pl.BlockSpec(w_bf.shape, lambda i: (0, 0)),
                  pl.BlockSpec((tm, LANES), pos_map), pl.BlockSpec((tm, LANES), pos_map)],
        out_specs=[row(w) for w in widths],
        out_shape=[jax.ShapeDtypeStruct((rows, w), d) for w, d in zip(widths, dtypes)],
        compiler_params=_cparams("arbitrary"),
        name="in_proj_ab",
    )(x, _mod_arg(shift, rows, seq_len, tm, per_token), _mod_arg(scale, rows, seq_len, tm, per_token),
      g_pre.reshape(1, D_MODEL), w_bf, cos, sin)


def _inc_kernel(x_ref, sh_ref, sc_ref, g_ref, w_ref, wg_ref, bg_ref,
                q_ref, k_ref, v_ref, r_ref, lg_ref):
    h = _modulate(x_ref[...], g_ref[...], sh_ref[0], sc_ref[0]).astype(BF16)

    def proj(lo, width):
        return _dot(h, w_ref[:, lo:lo + width])

    q_ref[...] = (proj(0, 512) * (DK_C ** -0.5)).astype(BF16)
    k_ref[...] = proj(512, 512).astype(BF16)
    v_ref[...] = proj(1024, 1024).astype(BF16)
    r_ref[...] = proj(2048, 1024).astype(BF16)
    a = proj(3072, LANES)
    z = _dot(a.astype(BF16), wg_ref[...]) + bg_ref[...]
    softplus_neg = jnp.maximum(-z, 0.0) + jnp.log1p(jnp.exp(-jnp.abs(z)))
    lg_ref[...] = -softplus_neg / GATE_TAU


def _in_proj_c(x, shift, scale, g_pre, w_bf, wg_bf, b_gate, seq_len):
    rows = x.shape[0]
    tm, per_token, tps, mod_spec = _row_cfg(rows, seq_len)
    row = lambda w: pl.BlockSpec((tm, w), lambda i: (i, 0))
    widths = (512, 512, 1024, 1024, 512)
    dtypes = (BF16, BF16, BF16, BF16, F32)
    full = lambda a: pl.BlockSpec(a.shape, lambda i: (0, 0))
    bg = b_gate.reshape(1, -1)
    return pl.pallas_call(
        _inc_kernel,
        grid=(rows // tm,),
        in_specs=[row(D_MODEL), mod_spec, mod_spec, pl.BlockSpec((1, D_MODEL), lambda i: (0, 0)),
                  full(w_bf), full(wg_bf), full(bg)],
        out_specs=[row(w) for w in widths],
        out_shape=[jax.ShapeDtypeStruct((rows, w), d) for w, d in zip(widths, dtypes)],
        compiler_params=_cparams("arbitrary"),
        name="in_proj_c",
    )(x, _mod_arg(shift, rows, seq_len, tm, per_token), _mod_arg(scale, rows, seq_len, tm, per_token),
      g_pre.reshape(1, D_MODEL), w_bf, wg_bf, bg)


def _outproj_kernel(n_parts, *refs):
    parts = refs[:n_parts]
    ws = refs[n_parts:2 * n_parts]
    x_ref, gate_ref, g_ref, o_ref = refs[2 * n_parts:]
    y = _dot(parts[0][...], ws[0][...])
    for p, w in zip(parts[1:], ws[1:]):
        y = y + _dot(p[...], w[...])
    o_ref[...] = x_ref[...] + gate_ref[0] * (_rms(y) * g_ref[...])


def _out_proj(parts, weights, x, gate, g_post, seq_len):
    rows = x.shape[0]
    tm, per_token, tps, mod_spec = _row_cfg(rows, seq_len)
    n = len(parts)
    return pl.pallas_call(
        functools.partial(_outproj_kernel, n),
        grid=(rows // tm,),
        in_specs=([pl.BlockSpec((tm, p.shape[1]), lambda i: (i, 0)) for p in parts]
                  + [pl.BlockSpec(w.shape, lambda i: (0, 0)) for w in weights]
                  + [pl.BlockSpec((tm, D_MODEL), lambda i: (i, 0)), mod_spec,
                     pl.BlockSpec((1, D_MODEL), lambda i: (0, 0))]),
        out_specs=pl.BlockSpec((tm, D_MODEL), lambda i: (i, 0)),
        out_shape=jax.ShapeDtypeStruct((rows, D_MODEL), F32),
        compiler_params=_cparams("arbitrary"),
        name="out_proj",
    )(*parts, *weights, x, _mod_arg(gate, rows, seq_len, tm, per_token), g_post.reshape(1, D_MODEL))


def _ffn_kernel(per_token, seq_len, tiles_per_seq, *refs):
    if per_token:
        (x_ref, sh_ref, sc_ref, gate_ref, gpre_ref, gpost_ref, wup_ref, cw_ref, cb_ref, wdn_ref,
         s1_ref, s2_ref, xo_ref, conv_ref, h_ref, acc_ref) = refs
    else:
        (x_ref, sh_ref, sc_ref, gate_ref, gpre_ref, gpost_ref, wup_ref, cw_ref, cb_ref, wdn_ref,
         xo_ref, conv_ref, h_ref, acc_ref, carry_ref) = refs
    tm = x_ref.shape[0]
    x = x_ref[...]
    h_ref[...] = _modulate(x, gpre_ref[...], sh_ref[0], sc_ref[0]).astype(BF16)
    acc_ref[...] = jnp.zeros_like(acc_ref)
    row = lax.broadcasted_iota(jnp.int32, (tm, FF_CHUNK), 0)
    if per_token:
        tau = row % seq_len
    else:
        seq_start = (pl.program_id(0) % tiles_per_seq) == 0

        @pl.when(pl.program_id(0) == 0)
        def _():
            carry_ref[...] = jnp.zeros_like(carry_ref)

    def conv_half(lo):
        u = _dot(h_ref[...], wup_ref[:, pl.ds(lo, FF_CHUNK)])
        u1 = pltpu.roll(u, 1, axis=0)
        u2 = pltpu.roll(u, 2, axis=0)
        if per_token:
            u1 = jnp.where(tau == 0, s1_ref[:, pl.ds(lo, FF_CHUNK)], u1)
            u2 = jnp.where(tau < 2, s2_ref[:, pl.ds(lo, FF_CHUNK)], u2)
            conv_ref[:, pl.ds(lo, FF_CHUNK)] = u
        else:
            prev = jnp.where(seq_start, 0.0, carry_ref[:, pl.ds(lo, FF_CHUNK)])
            u1 = jnp.where(row == 0, prev[7:8], u1)
            u2 = jnp.where(row == 0, prev[6:7], jnp.where(row == 1, prev[7:8], u2))
            carry_ref[:, pl.ds(lo, FF_CHUNK)] = u[tm - 8:]
            conv_ref[0, :, pl.ds(lo, FF_CHUNK)] = u[tm - 2:]
        cw = cw_ref[:, pl.ds(lo, FF_CHUNK)]
        return cb_ref[:, pl.ds(lo, FF_CHUNK)] + cw[0:1] * u2 + cw[1:2] * u1 + cw[2:3] * u

    def body(c, carry):
        lo = pl.multiple_of(c * FF_CHUNK, FF_CHUNK)
        ya = conv_half(lo)
        yb = conv_half(pl.multiple_of(lo + D_FF, FF_CHUNK))
        g = (jax.nn.gelu(ya) * yb).astype(BF16)
        acc_ref[...] += _dot(g, wdn_ref[pl.ds(lo, FF_CHUNK), :])
        return carry

    lax.fori_loop(0, D_FF // FF_CHUNK, body, 0)
    xo_ref[...] = x + gate_ref[0] * (_rms(acc_ref[...]) * gpost_ref[...])


def _conv_ffn(x, shift, scale, gate, g_pre, g_post, wup_bf, conv_w, conv_b, wdn_bf, seq_len, state=None):
    rows = x.shape[0]
    tm, per_token, tps, mod_spec = _row_cfg(rows, seq_len)
    full = lambda a: pl.BlockSpec(a.shape, lambda i: (0, 0))
    cb = conv_b.reshape(1, 2 * D_FF)
    in_specs = [pl.BlockSpec((tm, D_MODEL), lambda i: (i, 0)), mod_spec, mod_spec, mod_spec,
                pl.BlockSpec((1, D_MODEL), lambda i: (0, 0)), pl.BlockSpec((1, D_MODEL), lambda i: (0, 0)),
                full(wup_bf), full(conv_w), full(cb), full(wdn_bf)]
    args = [x] + [_mod_arg(m, rows, seq_len, tm, per_token) for m in (shift, scale, gate)] + [
        g_pre.reshape(1, D_MODEL), g_post.reshape(1, D_MODEL), wup_bf, conv_w, cb, wdn_bf]
    scratch = [pltpu.VMEM((tm, D_MODEL), BF16), pltpu.VMEM((tm, D_MODEL), F32)]
    if per_token:
        batch = rows // seq_len
        zeros = jnp.zeros((batch, seq_len - 1, 2 * D_FF), F32)
        s1 = jnp.concatenate([state[:, 1:2], zeros], axis=1).reshape(rows, 2 * D_FF)
        s2 = jnp.concatenate([state, zeros[:, 1:]], axis=1).reshape(rows, 2 * D_FF)
        in_specs += [pl.BlockSpec((tm, 2 * D_FF), lambda i: (i, 0))] * 2
        args += [s1, s2]
        conv_spec = pl.BlockSpec((tm, 2 * D_FF), lambda i: (i, 0))
        conv_shape = jax.ShapeDtypeStruct((rows, 2 * D_FF), F32)
    else:
        scratch.append(pltpu.VMEM((8, 2 * D_FF), F32))
        conv_spec = pl.BlockSpec((1, 2, 2 * D_FF), lambda i: (i // tps, 0, 0))
        conv_shape = jax.ShapeDtypeStruct((rows // seq_len, 2, 2 * D_FF), F32)
    return pl.pallas_call(
        functools.partial(_ffn_kernel, per_token, seq_len, tps),
        grid=(rows // tm,),
        in_specs=in_specs,
        out_specs=[pl.BlockSpec((tm, D_MODEL), lambda i: (i, 0)), conv_spec],
        out_shape=[jax.ShapeDtypeStruct((rows, D_MODEL), F32), conv_shape],
        scratch_shapes=scratch,
        compiler_params=_cparams("arbitrary"),
        name="conv_ffn",
    )(*args)


def _ret_tables(chunk, valid):
    log_gamma = jnp.log1p(-jnp.exp2(-5.0 - jnp.arange(H_A, dtype=F32)))
    idx = jnp.arange(chunk, dtype=F32)
    rel = idx[:, None] - idx[None, :]
    intra = jnp.exp(jnp.where(rel[None] >= 0, rel[None] * log_gamma[:, None, None], -jnp.inf))
    dq = jnp.exp((idx + 1.0)[None, :] * log_gamma[:, None])
    dk = jnp.where(idx[None, :] < valid, jnp.exp((valid - 1.0 - idx)[None, :] * log_gamma[:, None]), 0.0)
    dc = jnp.exp(valid * log_gamma)
    rep = lambda t: jnp.broadcast_to(t[..., None], t.shape + (LANES,))
    dc_rows = rep(jnp.repeat(dc, DK_A).reshape(H_A // 2, 2 * DK_A))
    return intra, rep(dq), rep(dk), dc_rows


def _ret_kernel(chunk, n_chunks, q_ref, k_ref, v_ref, ga_ref, s0_ref, intra_ref, dq_ref, dk_ref, dc_ref,
                o_ref, s_ref, st_ref):
    i = pl.program_id(1)

    @pl.when(i == 0)
    def _():
        st_ref[...] = s0_ref[0]

    lane = lax.broadcasted_iota(jnp.int32, (chunk, LANES), 1)
    for c in range(n_chunks):
        rows = slice(c * chunk, (c + 1) * chunk)
        for pair in range(H_A // 2):
            lanes = slice(pair * LANES, (pair + 1) * LANES)
            qc = q_ref[0, rows, lanes]
            kc = k_ref[0, rows, lanes]
            s_pair = st_ref[pair]
            s_bf = s_pair.astype(BF16)
            upd = s_pair * dc_ref[pair]
            for side in range(2):
                hd = 2 * pair + side
                mine = (lane >= 64) if side else (lane < 64)
                hl = slice(hd * DV_A, (hd + 1) * DV_A)
                qh = jnp.where(mine, qc, jnp.zeros_like(qc))
                vh = v_ref[0, rows, hl]
                sc = _dot_nt(qh, kc) * intra_ref[hd]
                o = _dot(sc.astype(BF16), vh) + _dot(qh, s_bf) * dq_ref[hd]
                kh = jnp.where(mine, kc.astype(F32) * dk_ref[hd], 0.0).astype(BF16)
                upd = upd + _dot_tn(kh, vh)
                gate = ga_ref[0, rows, hl].astype(F32)
                o_ref[0, rows, hl] = (_rms(o) * (gate * jax.nn.sigmoid(gate))).astype(BF16)
            st_ref[pair] = upd

    @pl.when(i == pl.num_programs(1) - 1)
    def _():
        s_ref[0] = st_ref[...]


def _retention(q, k, v, ga, s0, chunk, valid):
    b, l, _ = q.shape
    blk = min(l, 4 * chunk)
    n_chunks = blk // chunk
    tables = _ret_tables(chunk, valid)
    seq = lambda w: pl.BlockSpec((1, blk, w), lambda bi, i: (bi, i, 0))
    const = lambda a: pl.BlockSpec(a.shape, lambda bi, i: (0,) * a.ndim)
    state = pl.BlockSpec((1, H_A // 2, 2 * DK_A, DV_A), lambda bi, i: (bi, 0, 0, 0))
    o, s = pl.pallas_call(
        functools.partial(_ret_kernel, chunk, n_chunks),
        grid=(b, l // blk),
        in_specs=[seq(256), seq(256), seq(512), seq(512), state] + [const(t) for t in tables],
        out_specs=[seq(512), state],
        out_shape=[jax.ShapeDtypeStruct((b, l, H_A * DV_A), BF16),
                   jax.ShapeDtypeStruct((b, H_A // 2, 2 * DK_A, DV_A), F32)],
        scratch_shapes=[pltpu.VMEM((H_A // 2, 2 * DK_A, DV_A), F32)],
        compiler_params=_cparams("arbitrary", "arbitrary"),
        name="retention",
    )(q, k, v, ga, s0.reshape(b, H_A // 2, 2 * DK_A, DV_A), *tables)
    return o, s.reshape(b, H_A, DK_A, DV_A)


def _lam_value(lam_ref, lam_init):
    lv = lam_ref[...]
    return (jnp.exp(jnp.sum(lv[0:1] * lv[1:2], axis=-1, keepdims=True))
            - jnp.exp(jnp.sum(lv[2:3] * lv[3:4], axis=-1, keepdims=True)) + lam_init)


def _dattn_kernel(lam_init, tq, tk, q_ref, k_ref, v_ref, lam_ref, g_ref, o_ref, m_ref, l_ref, acc_ref):
    qi = pl.program_id(2)
    ki = pl.program_id(3)

    @pl.when(ki == 0)
    def _():
        m_ref[...] = jnp.full_like(m_ref, -jnp.inf)
        l_ref[...] = jnp.zeros_like(l_ref)
        acc_ref[...] = jnp.zeros_like(acc_ref)

    @pl.when(ki <= qi)
    def _():
        q = q_ref[0]
        k = k_ref[0]
        v = v_ref[0]
        lane = lax.broadcasted_iota(jnp.int32, q.shape, 1)
        visible = (lax.broadcasted_iota(jnp.int32, (tq, tk), 1) + ki * tk
                   <= lax.broadcasted_iota(jnp.int32, (tq, tk), 0) + qi * tq)
        for sub in range(2):
            mine = (lane >= DH_B) if sub else (lane < DH_B)
            s = _dot_nt(jnp.where(mine, q, jnp.zeros_like(q)), k)
            s = jnp.where(visible, s, -jnp.inf)
            m_old = m_ref[sub]
            m_new = jnp.maximum(m_old, jnp.max(s, axis=-1, keepdims=True))
            alpha = jnp.exp(m_old - m_new)
            p = jnp.exp(s - m_new)
            l_ref[sub] = alpha * l_ref[sub] + jnp.sum(p, axis=-1, keepdims=True)
            acc_ref[sub] = alpha * acc_ref[sub] + _dot(p.astype(BF16), v)
            m_ref[sub] = m_new

    @pl.when(ki == qi)
    def _():
        lam = _lam_value(lam_ref, lam_init)
        o = acc_ref[0] / l_ref[0] - lam * (acc_ref[1] / l_ref[1])
        o_ref[0] = ((_rms(o) * g_ref[...]) * (1.0 - lam_init)).astype(BF16)


def _diff_attn_prompt(qb, kb, vb, lam_vecs, g_diff, lam_init):
    b, l, _ = qb.shape
    tq = tk = 512
    nq = l // tq
    return pl.pallas_call(
        functools.partial(_dattn_kernel, lam_init, tq, tk),
        grid=(b, H_B, nq, nq),
        in_specs=[pl.BlockSpec((1, tq, LANES), lambda bi, h, qi, ki: (bi, qi, h)),
                  pl.BlockSpec((1, tk, LANES), lambda bi, h, qi, ki: (bi, jnp.minimum(ki, qi), h)),
                  pl.BlockSpec((1, tk, LANES), lambda bi, h, qi, ki: (bi, jnp.minimum(ki, qi), h)),
                  pl.BlockSpec(lam_vecs.shape, lambda bi, h, qi, ki: (0, 0)),
                  pl.BlockSpec((1, DV_B), lambda bi, h, qi, ki: (0, 0))],
        out_specs=pl.BlockSpec((1, tq, LANES), lambda bi, h, qi, ki: (bi, qi, h)),
        out_shape=jax.ShapeDtypeStruct((b, l, H_B * DV_B), BF16),
        scratch_shapes=[pltpu.VMEM((2, tq, 1), F32), pltpu.VMEM((2, tq, 1), F32),
                        pltpu.VMEM((2, tq, DV_B), F32)],
        compiler_params=_cparams("arbitrary", "arbitrary", "arbitrary", "arbitrary"),
        name="diff_attn_prompt",
    )(qb, kb, vb, lam_vecs, g_diff.reshape(1, DV_B))


Q_ROWS = 8 * H_B * 2


def _dattn_dec_kernel(lam_init, n_pages, pt_ref, q_ref, kn_ref, vn_ref, *refs):
    k_refs = refs[:n_pages]
    v_refs = refs[n_pages:2 * n_pages]
    lam_ref, g_ref, o_ref, m_ref, l_ref, acc_ref = refs[2 * n_pages:]
    step = pl.program_id(1)
    q = q_ref[0]

    def absorb(s, v):
        m_old = m_ref[...]
        m_new = jnp.maximum(m_old, jnp.max(s, axis=-1, keepdims=True))
        alpha = jnp.exp(m_old - m_new)
        p = jnp.exp(s - m_new)
        l_ref[...] = alpha * l_ref[...] + jnp.sum(p, axis=-1, keepdims=True)
        acc_ref[...] = alpha * acc_ref[...] + _dot(p.astype(BF16), v)
        m_ref[...] = m_new

    @pl.when(step == 0)
    def _():
        m_ref[...] = jnp.full_like(m_ref, -jnp.inf)
        l_ref[...] = jnp.zeros_like(l_ref)
        acc_ref[...] = jnp.zeros_like(acc_ref)
        s = _dot_nt(q, kn_ref[0])
        tok = lax.broadcasted_iota(jnp.int32, s.shape, 0) % 8
        key = lax.broadcasted_iota(jnp.int32, s.shape, 1)
        s = jnp.where((key <= tok) & (key < DEC_SEQ), s, -jnp.inf)
        absorb(s, vn_ref[0])

    for kr, vr in zip(k_refs, v_refs):
        absorb(_dot_nt(q, kr[0].astype(BF16)), vr[0].astype(BF16))

    @pl.when(step == pl.num_programs(1) - 1)
    def _():
        lam = _lam_value(lam_ref, lam_init)
        o = acc_ref[...] / l_ref[...]
        for h in range(H_B):
            lanes = slice(h * DV_B, (h + 1) * DV_B)
            d = o[8 * h:8 * h + 8, lanes] - lam * o[8 * H_B + 8 * h:8 * H_B + 8 * h + 8, lanes]
            o_ref[0, :, lanes] = ((_rms(d) * g_ref[...]) * (1.0 - lam_init)).astype(BF16)


def _diff_attn_decode(q_rows, k_new, v_new, cache_k, cache_v, page_table, lam_vecs, g_diff, lam_init):
    b = q_rows.shape[0]
    n_steps = page_table.shape[1] // PAGES_PER_STEP
    per_b = lambda r: pl.BlockSpec((1, r, 512), lambda bi, s, pt: (bi, 0, 0))

    def page_spec(j):
        return pl.BlockSpec((1, PAGE_SIZE, 512), lambda bi, s, pt: (pt[bi, s * PAGES_PER_STEP + j], 0, 0))

    pages = [page_spec(j) for j in range(PAGES_PER_STEP)]
    grid_spec = pltpu.PrefetchScalarGridSpec(
        num_scalar_prefetch=1,
        grid=(b, n_steps),
        in_specs=[per_b(Q_ROWS), per_b(SAMPLE_PAD), per_b(SAMPLE_PAD)] + pages + pages + [
            pl.BlockSpec(lam_vecs.shape, lambda bi, s, pt: (0, 0)),
            pl.BlockSpec((1, DV_B), lambda bi, s, pt: (0, 0))],
        out_specs=per_b(8),
        scratch_shapes=[pltpu.VMEM((Q_ROWS, 1), F32), pltpu.VMEM((Q_ROWS, 1), F32),
                        pltpu.VMEM((Q_ROWS, 512), F32)],
    )
    return pl.pallas_call(
        functools.partial(_dattn_dec_kernel, lam_init, PAGES_PER_STEP),
        grid_spec=grid_spec,
        out_shape=jax.ShapeDtypeStruct((b, 8, 512), BF16),
        compiler_params=_cparams("arbitrary", "arbitrary"),
        name="diff_attn_decode",
    )(page_table, q_rows, k_new, v_new, *([cache_k] * PAGES_PER_STEP), *([cache_v] * PAGES_PER_STEP),
      lam_vecs, g_diff.reshape(1, DV_B))


def _decode_q_rows(qb):
    b = qb.shape[0]
    lane = np.arange(512)
    sub = np.arange(2)[:, None, None, None]
    head = np.arange(H_B)[None, :, None, None]
    keep = (lane[None, None, None, :] // DV_B == head) & ((lane[None, None, None, :] % DV_B) // DH_B == sub)
    tok = jnp.pad(qb, ((0, 0), (0, 8 - DEC_SEQ), (0, 0)))[:, None, None]
    return jnp.where(jnp.asarray(keep)[None], tok, jnp.zeros_like(tok)).reshape(b, Q_ROWS, 512)


def _gla_tables(chunk):
    idx = np.arange(chunk)
    i, t = idx[:, None], idx[None, :]
    mats = [t <= i, t > i]
    masks = [i == t]
    s = 1
    while s < chunk:
        same = (t // s) == (i // s)
        odd = ((i // s) % 2) == 1
        mats.append(same & np.where(odd, t <= i, t > i))
        masks.append(((i // (2 * s)) == (t // (2 * s))) & odd & (((t // s) % 2) == 0))
        s *= 2
    return (jnp.asarray(np.concatenate(mats, axis=0).astype(np.float32), BF16),
            jnp.asarray(np.stack(masks).astype(np.float32)))


def _gla_kernel(chunk, n_chunks, valid, q_ref, k_ref, g_ref, v_ref, r_ref, s0_ref, gn_ref, m_ref, p_ref,
                o_ref, s_ref, st_ref):
    i = pl.program_id(2)

    @pl.when(i == 0)
    def _():
        st_ref[...] = s0_ref[0, 0].T

    n_levels = p_ref.shape[0] - 1
    m_all = m_ref[...]
    live = lax.broadcasted_iota(jnp.int32, (chunk, 1), 0) < valid
    for c in range(n_chunks):
        rows = slice(c * chunk, (c + 1) * chunk)
        q = q_ref[0, rows, :]
        k = k_ref[0, rows, :]
        g = g_ref[0, rows, :]
        v = v_ref[0, rows, :]
        if valid < chunk:
            k = jnp.where(live, k, jnp.zeros_like(k))
            v = jnp.where(live, v, jnp.zeros_like(v))
            g = jnp.where(live, g, 0.0)
        g_hi = g.astype(BF16)
        rem = g - g_hi.astype(F32)
        g_mid = rem.astype(BF16)
        g_lo = (rem - g_mid.astype(F32)).astype(BF16)
        e = jnp.exp(_dot(m_all, g_hi) + _dot(m_all, g_mid) + _dot(m_all, g_lo))
        qf = q.astype(F32)
        kf = k.astype(F32)
        att = _dot_nt(q, k) * p_ref[0]
        for lv in range(n_levels):
            el = e[(2 + lv) * chunk:(3 + lv) * chunk]
            att = att + _dot_nt((qf * el).astype(BF16), (kf * el).astype(BF16)) * p_ref[1 + lv]
        st = st_ref[...]
        o = _dot(att.astype(BF16), v) + _dot_nt((qf * e[0:chunk]).astype(BF16), st.astype(BF16))
        st_ref[...] = st * e[chunk - 1:chunk] + _dot_tn(v, (kf * e[chunk:2 * chunk]).astype(BF16))
        gate = r_ref[0, rows, :].astype(F32)
        o_ref[0, rows, :] = ((_rms(o) * gn_ref[...]) * (gate * jax.nn.sigmoid(gate))).astype(BF16)

    @pl.when(i == pl.num_programs(2) - 1)
    def _():
        s_ref[0, 0] = st_ref[...].T


def _gla(q, k, g, v, r, s0, g_norm, chunk, valid):
    b, l, _ = q.shape
    blk = min(l, 4 * chunk)
    n_chunks = blk // chunk
    m_all, masks = _gla_tables(chunk)
    qk = pl.BlockSpec((1, blk, DK_C), lambda bi, h, i: (bi, i, h))
    vr = pl.BlockSpec((1, blk, DV_C), lambda bi, h, i: (bi, i, h))
    state = pl.BlockSpec((1, 1, DK_C, DV_C), lambda bi, h, i: (bi, h, 0, 0))
    const = lambda a: pl.BlockSpec(a.shape, lambda bi, h, i: (0,) * a.ndim)
    gn = g_norm.reshape(1, DV_C)
    return pl.pallas_call(
        functools.partial(_gla_kernel, chunk, n_chunks, valid),
        grid=(b, H_C, l // blk),
        in_specs=[qk, qk, qk, vr, vr, state, const(gn), const(m_all), const(masks)],
        out_specs=[vr, state],
        out_shape=[jax.ShapeDtypeStruct((b, l, H_C * DV_C), BF16),
                   jax.ShapeDtypeStruct((b, H_C, DK_C, DV_C), F32)],
        scratch_shapes=[pltpu.VMEM((DV_C, DK_C), F32)],
        compiler_params=_cparams("arbitrary", "arbitrary", "arbitrary"),
        name="gla",
    )(q, k, g, v, r, s0, gn, m_all, masks)


def _rope_tables(pos):
    inv = ROPE_THETA ** (-jnp.arange(0, DK_A, 2, dtype=F32) / DK_A)
    ang = pos.astype(F32)[:, None] * inv[None, :]
    cos = jnp.tile(jnp.cos(ang), (1, 4))
    sin = jnp.tile(jnp.concatenate([-jnp.sin(ang), jnp.sin(ang)], axis=1), (1, 2))
    return cos, sin


def _pad_seq(t, batch, seq_len):
    t = t.reshape(batch, seq_len, t.shape[-1])
    return jnp.pad(t, ((0, 0), (0, SAMPLE_PAD - seq_len), (0, 0)))


def _unpad_seq(t, seq_len):
    return t[:, :seq_len].reshape(t.shape[0] * seq_len, t.shape[-1])


def kernel(x_prompt, x_sample, cache_k, cache_v, state_ret, state_gla, state_conv, page_table, c_prompt, c_sample, w_ada, b_ada, g_pre_mix, g_post_mix, g_pre_ffn, g_post_ffn, w_in_ab, w_out_ab, lam_q1, lam_k1, lam_q2, lam_k2, g_diff, w_in_c, w_gate_c, b_gate_c, g_gla, w_out_c, w_up, conv_w, conv_b, w_down):
    rows_p, rows_s = BATCH * SEQ, DEC_BATCH * DEC_SEQ
    xp = x_prompt.reshape(rows_p, D_MODEL)
    xs = x_sample.reshape(rows_s, D_MODEL)

    n_c = BATCH + DEC_BATCH
    c_all = jnp.pad(jnp.concatenate([c_prompt, c_sample], axis=0), ((0, (-n_c) % 8), (0, 0)))
    ada = _ada(c_all, w_ada, b_ada)

    cos_p, sin_p = _rope_tables(jnp.arange(SEQ))
    cos_s, sin_s = _rope_tables(jnp.tile(PAST_LEN + jnp.arange(DEC_SEQ), DEC_BATCH))

    k_p, v_p, k_s, v_s, ret_p, ret_s, gla_p, gla_s, conv_p, conv_s = ([] for _ in range(10))
    for l in range(DEPTH):
        mp = jnp.split(ada[l, :BATCH], 6, axis=-1)
        ms = jnp.split(ada[l, BATCH:n_c], 6, axis=-1)
        if l % 2 == 0:
            e = l // 2
            lam_init = 0.8 - 0.6 * math.exp(-0.3 * l)
            lam_vecs = jnp.stack([lam_q1[e], lam_k1[e], lam_q2[e], lam_k2[e]]).astype(F32)
            w_in = w_in_ab[e].astype(BF16)
            w_out = w_out_ab[e].astype(BF16)
            w_parts = [w_out[:H_A * DV_A], w_out[H_A * DV_A:]]

            qa, ka, va, ga, qb, kb32, kb16, vb32, vb16 = _in_proj_ab(
                xp, mp[0], mp[1], g_pre_mix[l], w_in, cos_p, sin_p, SEQ)
            seq3 = lambda t: t.reshape(BATCH, SEQ, t.shape[-1])
            ret, sp = _retention(seq3(qa), seq3(ka), seq3(va), seq3(ga),
                                 jnp.zeros((BATCH, H_A, DK_A, DV_A), F32), CHUNK_RET, CHUNK_RET)
            dif = _diff_attn_prompt(seq3(qb), seq3(kb16), seq3(vb16), lam_vecs, g_diff[e], lam_init)
            xp = _out_proj([ret.reshape(rows_p, -1), dif.reshape(rows_p, -1)], w_parts, xp, mp[2],
                           g_post_mix[l], SEQ)
            k_p.append(kb32.reshape(BATCH, SEQ, H_B, 2 * DH_B))
            v_p.append(vb32.reshape(BATCH, SEQ, H_B, DV_B))
            ret_p.append(sp)

            qa, ka, va, ga, qb, kb32, kb16, vb32, vb16 = _in_proj_ab(
                xs, ms[0], ms[1], g_pre_mix[l], w_in, cos_s, sin_s, DEC_SEQ)
            pad = lambda t: _pad_seq(t, DEC_BATCH, DEC_SEQ)
            ret, ss = _retention(pad(qa), pad(ka), pad(va), pad(ga), state_ret[e], SAMPLE_PAD, DEC_SEQ)
            n_phys = cache_k.shape[1]
            dif = _diff_attn_decode(
                _decode_q_rows(qb.reshape(DEC_BATCH, DEC_SEQ, 512)), pad(kb16), pad(vb16),
                cache_k[e].reshape(n_phys, PAGE_SIZE, 512), cache_v[e].reshape(n_phys, PAGE_SIZE, 512),
                page_table, lam_vecs, g_diff[e], lam_init)
            xs = _out_proj([_unpad_seq(ret, DEC_SEQ), _unpad_seq(dif, DEC_SEQ)], w_parts, xs, ms[2],
                           g_post_mix[l], DEC_SEQ)
            k_s.append(kb32.reshape(DEC_BATCH, DEC_SEQ, H_B, 2 * DH_B))
            v_s.append(vb32.reshape(DEC_BATCH, DEC_SEQ, H_B, DV_B))
            ret_s.append(ss)
        else:
            o = l // 2
            w_in = jnp.pad(w_in_c[o], ((0, 0), (0, LANES - GATE_RANK))).astype(BF16)
            w_gate = jnp.pad(w_gate_c[o], ((0, LANES - GATE_RANK), (0, 0))).astype(BF16)
            w_out = w_out_c[o].astype(BF16)

            q, k, v, r, lg = _in_proj_c(xp, mp[0], mp[1], g_pre_mix[l], w_in, w_gate, b_gate_c[o], SEQ)
            seq3 = lambda t: t.reshape(BATCH, SEQ, t.shape[-1])
            og, sp = _gla(seq3(q), seq3(k), seq3(lg), seq3(v), seq3(r),
                          jnp.zeros((BATCH, H_C, DK_C, DV_C), F32), g_gla[o], CHUNK_GLA, CHUNK_GLA)
            xp = _out_proj([og.reshape(rows_p, -1)], [w_out], xp, mp[2], g_post_mix[l], SEQ)
            gla_p.append(sp)

            q, k, v, r, lg = _in_proj_c(xs, ms[0], ms[1], g_pre_mix[l], w_in, w_gate, b_gate_c[o], DEC_SEQ)
            pad = lambda t: _pad_seq(t, DEC_BATCH, DEC_SEQ)
            og, ss = _gla(pad(q), pad(k), pad(lg), pad(v), pad(r), state_gla[o], g_gla[o], SAMPLE_PAD, DEC_SEQ)
            xs = _out_proj([_unpad_seq(og, DEC_SEQ)], [w_out], xs, ms[2], g_post_mix[l], DEC_SEQ)
            gla_s.append(ss)

        wup = w_up[l].astype(BF16)
        wdn = w_down[l].astype(BF16)
        xp, cbp = _conv_ffn(xp, mp[3], mp[4], mp[5], g_pre_ffn[l], g_post_ffn[l], wup, conv_w[l], conv_b[l],
                            wdn, SEQ)
        xs, up_s = _conv_ffn(xs, ms[3], ms[4], ms[5], g_pre_ffn[l], g_post_ffn[l], wup, conv_w[l], conv_b[l],
                             wdn, DEC_SEQ, state=state_conv[l])
        conv_p.append(cbp)
        conv_s.append(up_s.reshape(DEC_BATCH, DEC_SEQ, 2 * D_FF)[:, DEC_SEQ - (CONV_W - 1):])

    return (xp.reshape(BATCH, SEQ, D_MODEL), xs.reshape(DEC_BATCH, DEC_SEQ, D_MODEL),
            jnp.stack(k_p), jnp.stack(v_p), jnp.stack(k_s), jnp.stack(v_s),
            jnp.stack(ret_p), jnp.stack(ret_s), jnp.stack(gla_p), jnp.stack(gla_s),
            jnp.stack(conv_p), jnp.stack(conv_s))
```

```python
import functools
import math

import numpy as np
import jax
import jax.numpy as jnp
from jax import lax
from jax.experimental import pallas as pl
from jax.experimental.pallas import tpu as pltpu

D_MODEL = 1024
BATCH = 4
SEQ = 4096
DEPTH = 2
DEC_BATCH = 32
DEC_SEQ = 4
PAST_LEN = 8192
PAGE_SIZE = 128
N_EVEN = (DEPTH + 1) // 2
N_ODD = DEPTH // 2
H_A = 4
DK_A = D_MODEL // 16
DV_A = D_MODEL // 8
CHUNK_RET = 128
H_B = 4
DH_B = D_MODEL // 16
DV_B = 2 * DH_B
H_C = 4
DK_C = D_MODEL // 8
DV_C = D_MODEL // 4
GATE_RANK = 16
GATE_TAU = 16.0
CHUNK_GLA = 64
D_FF = ((8 * D_MODEL // 3 + 127) // 128) * 128
CONV_W = 3
ROPE_THETA = 10000.0
EPS = 1e-6

LANES = 128
VMEM_LIMIT = 56 * 1024 * 1024
ROW_TILE = 512
SAMPLE_PAD = 16
FF_CHUNK = 256
PAGES_PER_STEP = 16
LOG2E = math.log2(math.e)
BF16 = jnp.bfloat16
F32 = jnp.float32


def _cparams(*sem):
    return pltpu.CompilerParams(dimension_semantics=sem, vmem_limit_bytes=VMEM_LIMIT)


def _dot(a, b):
    return jnp.dot(a, b, preferred_element_type=F32)


def _dot_nt(a, b):
    return lax.dot_general(a, b, (((1,), (1,)), ((), ())), preferred_element_type=F32)


def _dot_tn(a, b):
    return lax.dot_general(a, b, (((0,), (0,)), ((), ())), preferred_element_type=F32)


def _rms(x):
    return x * lax.rsqrt(jnp.mean(x * x, axis=-1, keepdims=True) + EPS)


def _modulate(x, g, shift, scale):
    return (_rms(x) * g) * (1.0 + scale) + shift


def _rope(x, cos, sin_signed):
    w = x.shape[-1]
    fwd = pltpu.roll(x, 32, axis=1)
    bwd = pltpu.roll(x, w - 32, axis=1)
    reps = w // LANES
    first_half = (lax.broadcasted_iota(jnp.int32, x.shape, 1) % 64) < 32
    partner = jnp.where(first_half, bwd, fwd)
    return (x * jnp.concatenate([cos] * reps, axis=1)
            + partner * jnp.concatenate([sin_signed] * reps, axis=1))


def _ada_kernel(c_ref, w_ref, b_ref, o_ref):
    c = c_ref[...]
    s = c * jax.nn.sigmoid(c)
    o_ref[0] = jnp.dot(s, w_ref[0], preferred_element_type=F32, precision=lax.Precision.HIGHEST) + b_ref[0]


def _ada(c_all, w_ada, b_ada):
    rows = c_all.shape[0]
    tn = 1536
    return pl.pallas_call(
        _ada_kernel,
        grid=(DEPTH, 6 * D_MODEL // tn),
        in_specs=[pl.BlockSpec((rows, D_MODEL), lambda l, j: (0, 0)),
                  pl.BlockSpec((1, D_MODEL, tn), lambda l, j: (l, 0, j)),
                  pl.BlockSpec((1, 1, tn), lambda l, j: (l, 0, j))],
        out_specs=pl.BlockSpec((1, rows, tn), lambda l, j: (l, 0, j)),
        out_shape=jax.ShapeDtypeStruct((DEPTH, rows, 6 * D_MODEL), F32),
        compiler_params=_cparams("arbitrary", "arbitrary"),
        name="adaln",
    )(c_all, w_ada, b_ada.reshape(DEPTH, 1, 6 * D_MODEL))


def _row_cfg(rows, seq_len):
    tm = min(ROW_TILE, rows)
    per_token = seq_len < tm
    tiles_per_seq = 1 if per_token else seq_len // tm
    mod_spec = (pl.BlockSpec((1, tm, D_MODEL), lambda i: (i, 0, 0)) if per_token
                else pl.BlockSpec((1, 1, D_MODEL), lambda i: (i // tiles_per_seq, 0, 0)))
    return tm, per_token, tiles_per_seq, mod_spec


def _mod_arg(m, rows, seq_len, tm, per_token):
    if per_token:
        return jnp.repeat(m, seq_len, axis=0).reshape(rows // tm, tm, D_MODEL)
    return m.reshape(m.shape[0], 1, D_MODEL)


def _inab_kernel(x_ref, sh_ref, sc_ref, g_ref, w_ref, cos_ref, sin_ref,
                 qa_ref, ka_ref, va_ref, ga_ref, qb_ref, kb32_ref, kb16_ref, vb32_ref, vb16_ref):
    h = _modulate(x_ref[...], g_ref[...], sh_ref[0], sc_ref[0]).astype(BF16)
    cos = cos_ref[...]
    sin = sin_ref[...]

    def proj(lo, width):
        return _dot(h, w_ref[:, lo:lo + width])

    qa_ref[...] = _rope(proj(0, 256), cos, sin).astype(BF16)
    ka_ref[...] = (_rope(proj(256, 256), cos, sin) * (DK_A ** -0.5)).astype(BF16)
    va_ref[...] = proj(512, 512).astype(BF16)
    ga_ref[...] = proj(1024, 512).astype(BF16)
    qb_ref[...] = (_rope(proj(1536, 512), cos, sin) * (DH_B ** -0.5 * LOG2E)).astype(BF16)
    kb = _rope(proj(2048, 512), cos, sin)
    kb32_ref[...] = kb
    kb16_ref[...] = kb.astype(BF16)
    vb = proj(2560, 512)
    vb32_ref[...] = vb
    vb16_ref[...] = vb.astype(BF16)


def _in_proj_ab(x, shift, scale, g_pre, w_bf, cos, sin, seq_len):
    rows = x.shape[0]
    tm, per_token, tps, mod_spec = _row_cfg(rows, seq_len)
    pos_map = (lambda i: (i, 0)) if per_token else (lambda i: (i % tps, 0))
    row = lambda w: pl.BlockSpec((tm, w), lambda i: (i, 0))
    widths = (256, 256, 512, 512, 512, 512, 512, 512, 512)
    dtypes = (BF16, BF16, BF16, BF16, BF16, F32, BF16, F32, BF16)
    return pl.pallas_call(
        _inab_kernel,
        grid=(rows // tm,),
        in_specs=[row(D_MODEL), mod_spec, mod_spec,
                  pl.BlockSpec((1, D_MODEL), lambda i: (0, 0)),
                  pl.BlockSpec(w_bf.shape, lambda i: (0, 0)),
                  pl.BlockSpec((tm, LANES), pos_map), pl.BlockSpec((tm, LANES), pos_map)],
        out_specs=[row(w) for w in widths],
        out_shape=[jax.ShapeDtypeStruct((rows, w), d) for w, d in zip(widths, dtypes)],
        compiler_params=_cparams("arbitrary"),
        name="in_proj_ab",
    )(x, _mod_arg(shift, rows, seq_len, tm, per_token), _mod_arg(scale, rows, seq_len, tm, per_token),
      g_pre.reshape(1, D_MODEL), w_bf, cos, sin)


def _inc_kernel(x_ref, sh_ref, sc_ref, g_ref, w_ref, wg_ref, bg_ref,
                q_ref, k_ref, v_ref, r_ref, lg_ref):
    h = _modulate(x_ref[...], g_ref[...], sh_ref[0], sc_ref[0]).astype(BF16)

    def proj(lo, width):
        return _dot(h, w_ref[:, lo:lo + width])

    q_ref[...] = (proj(0, 512) * (DK_C ** -0.5)).astype(BF16)
    k_ref[...] = proj(512, 512).astype(BF16)
    v_ref[...] = proj(1024, 1024).astype(BF16)
    r_ref[...] = proj(2048, 1024).astype(BF16)
    a = proj(3072, LANES)
    z = _dot(a.astype(BF16), wg_ref[...]) + bg_ref[...]
    softplus_neg = jnp.maximum(-z, 0.0) + jnp.log1p(jnp.exp(-jnp.abs(z)))
    lg_ref[...] = -softplus_neg / GATE_TAU


def _in_proj_c(x, shift, scale, g_pre, w_bf, wg_bf, b_gate, seq_len):
    rows = x.shape[0]
    tm, per_token, tps, mod_spec = _row_cfg(rows, seq_len)
    row = lambda w: pl.BlockSpec((tm, w), lambda i: (i, 0))
    widths = (512, 512, 1024, 1024, 512)
    dtypes = (BF16, BF16, BF16, BF16, F32)
    full = lambda a: pl.BlockSpec(a.shape, lambda i: (0, 0))
    bg = b_gate.reshape(1, -1)
    return pl.pallas_call(
        _inc_kernel,
        grid=(rows // tm,),
        in_specs=[row(D_MODEL), mod_spec, mod_spec, pl.BlockSpec((1, D_MODEL), lambda i: (0, 0)),
                  full(w_bf), full(wg_bf), full(bg)],
        out_specs=[row(w) for w in widths],
        out_shape=[jax.ShapeDtypeStruct((rows, w), d) for w, d in zip(widths, dtypes)],
        compiler_params=_cparams("arbitrary"),
        name="in_proj_c",
    )(x, _mod_arg(shift, rows, seq_len, tm, per_token), _mod_arg(scale, rows, seq_len, tm, per_token),
      g_pre.reshape(1, D_MODEL), w_bf, wg_bf, bg)


def _outproj_kernel(n_parts, *refs):
    parts = refs[:n_parts]
    ws = refs[n_parts:2 * n_parts]
    x_ref, gate_ref, g_ref, o_ref = refs[2 * n_parts:]
    y = _dot(parts[0][...], ws[0][...])
    for p, w in zip(parts[1:], ws[1:]):
        y = y + _dot(p[...], w[...])
    o_ref[...] = x_ref[...] + gate_ref[0] * (_rms(y) * g_ref[...])


def _out_proj(parts, weights, x, gate, g_post, seq_len):
    rows = x.shape[0]
    tm, per_token, tps, mod_spec = _row_cfg(rows, seq_len)
    n = len(parts)
    return pl.pallas_call(
        functools.partial(_outproj_kernel, n),
        grid=(rows // tm,),
        in_specs=([pl.BlockSpec((tm, p.shape[1]), lambda i: (i, 0)) for p in parts]
                  + [pl.BlockSpec(w.shape, lambda i: (0, 0)) for w in weights]
                  + [pl.BlockSpec((tm, D_MODEL), lambda i: (i, 0)), mod_spec,
                     pl.BlockSpec((1, D_MODEL), lambda i: (0, 0))]),
        out_specs=pl.BlockSpec((tm, D_MODEL), lambda i: (i, 0)),
        out_shape=jax.ShapeDtypeStruct((rows, D_MODEL), F32),
        compiler_params=_cparams("arbitrary"),
        name="out_proj",
    )(*parts, *weights, x, _mod_arg(gate, rows, seq_len, tm, per_token), g_post.reshape(1, D_MODEL))


def _ffn_kernel(per_token, seq_len, tiles_per_seq, *refs):
    if per_token:
        (x_ref, sh_ref, sc_ref, gate_ref, gpre_ref, gpost_ref, wup_ref, cw_ref, cb_ref, wdn_ref,
         s1_ref, s2_ref, xo_ref, conv_ref, h_ref, acc_ref) = refs
    else:
        (x_ref, sh_ref, sc_ref, gate_ref, gpre_ref, gpost_ref, wup_ref, cw_ref, cb_ref, wdn_ref,
         xo_ref, conv_ref, h_ref, acc_ref, carry_ref) = refs
    tm = x_ref.shape[0]
    x = x_ref[...]
    h_ref[...] = _modulate(x, gpre_ref[...], sh_ref[0], sc_ref[0]).astype(BF16)
    acc_ref[...] = jnp.zeros_like(acc_ref)
    row = lax.broadcasted_iota(jnp.int32, (tm, FF_CHUNK), 0)
    if per_token:
        tau = row % seq_len
    else:
        seq_start = (pl.program_id(0) % tiles_per_seq) == 0

        @pl.when(pl.program_id(0) == 0)
        def _():
            carry_ref[...] = jnp.zeros_like(carry_ref)

    def conv_half(lo):
        u = _dot(h_ref[...], wup_ref[:, pl.ds(lo, FF_CHUNK)])
        u1 = pltpu.roll(u, 1, axis=0)
        u2 = pltpu.roll(u, 2, axis=0)
        if per_token:
            u1 = jnp.where(tau == 0, s1_ref[:, pl.ds(lo, FF_CHUNK)], u1)
            u2 = jnp.where(tau < 2, s2_ref[:, pl.ds(lo, FF_CHUNK)], u2)
            conv_ref[:, pl.ds(lo, FF_CHUNK)] = u
        else:
            prev = jnp.where(seq_start, 0.0, carry_ref[:, pl.ds(lo, FF_CHUNK)])
            u1 = jnp.where(row == 0, prev[7:8], u1)
            u2 = jnp.where(row == 0, prev[6:7], jnp.where(row == 1, prev[7:8], u2))
            carry_ref[:, pl.ds(lo, FF_CHUNK)] = u[tm - 8:]
            conv_ref[0, :, pl.ds(lo, FF_CHUNK)] = u[tm - 2:]
        cw = cw_ref[:, pl.ds(lo, FF_CHUNK)]
        return cb_ref[:, pl.ds(lo, FF_CHUNK)] + cw[0:1] * u2 + cw[1:2] * u1 + cw[2:3] * u

    def body(c, carry):
        lo = pl.multiple_of(c * FF_CHUNK, FF_CHUNK)
        ya = conv_half(lo)
        yb = conv_half(pl.multiple_of(lo + D_FF, FF_CHUNK))
        g = (jax.nn.gelu(ya) * yb).astype(BF16)
        acc_ref[...] += _dot(g, wdn_ref[pl.ds(lo, FF_CHUNK), :])
        return carry

    lax.fori_loop(0, D_FF // FF_CHUNK, body, 0)
    xo_ref[...] = x + gate_ref[0] * (_rms(acc_ref[...]) * gpost_ref[...])


def _conv_ffn(x, shift, scale, gate, g_pre, g_post, wup_bf, conv_w, conv_b, wdn_bf, seq_len, state=None):
    rows = x.shape[0]
    tm, per_token, tps, mod_spec = _row_cfg(rows, seq_len)
    full = lambda a: pl.BlockSpec(a.shape, lambda i: (0, 0))
    cb = conv_b.reshape(1, 2 * D_FF)
    in_specs = [pl.BlockSpec((tm, D_MODEL), lambda i: (i, 0)), mod_spec, mod_spec, mod_spec,
                pl.BlockSpec((1, D_MODEL), lambda i: (0, 0)), pl.BlockSpec((1, D_MODEL), lambda i: (0, 0)),
                full(wup_bf), full(conv_w), full(cb), full(wdn_bf)]
    args = [x] + [_mod_arg(m, rows, seq_len, tm, per_token) for m in (shift, scale, gate)] + [
        g_pre.reshape(1, D_MODEL), g_post.reshape(1, D_MODEL), wup_bf, conv_w, cb, wdn_bf]
    scratch = [pltpu.VMEM((tm, D_MODEL), BF16), pltpu.VMEM((tm, D_MODEL), F32)]
    if per_token:
        batch = rows // seq_len
        zeros = jnp.zeros((batch, seq_len - 1, 2 * D_FF), F32)
        s1 = jnp.concatenate([state[:, 1:2], zeros], axis=1).reshape(rows, 2 * D_FF)
        s2 = jnp.concatenate([state, zeros[:, 1:]], axis=1).reshape(rows, 2 * D_FF)
        in_specs += [pl.BlockSpec((tm, 2 * D_FF), lambda i: (i, 0))] * 2
        args += [s1, s2]
        conv_spec = pl.BlockSpec((tm, 2 * D_FF), lambda i: (i, 0))
        conv_shape = jax.ShapeDtypeStruct((rows, 2 * D_FF), F32)
    else:
        scratch.append(pltpu.VMEM((8, 2 * D_FF), F32))
        conv_spec = pl.BlockSpec((1, 2, 2 * D_FF), lambda i: (i // tps, 0, 0))
        conv_shape = jax.ShapeDtypeStruct((rows // seq_len, 2, 2 * D_FF), F32)
    return pl.pallas_call(
        functools.partial(_ffn_kernel, per_token, seq_len, tps),
        grid=(rows // tm,),
        in_specs=in_specs,
        out_specs=[pl.BlockSpec((tm, D_MODEL), lambda i: (i, 0)), conv_spec],
        out_shape=[jax.ShapeDtypeStruct((rows, D_MODEL), F32), conv_shape],
        scratch_shapes=scratch,
        compiler_params=_cparams("arbitrary"),
        name="conv_ffn",
    )(*args)


def _ret_tables(chunk, valid):
    log_gamma = jnp.log1p(-jnp.exp2(-5.0 - jnp.arange(H_A, dtype=F32)))
    idx = jnp.arange(chunk, dtype=F32)
    rel = idx[:, None] - idx[None, :]
    intra = jnp.exp(jnp.where(rel[None] >= 0, rel[None] * log_gamma[:, None, None], -jnp.inf))
    dq = jnp.exp((idx + 1.0)[None, :] * log_gamma[:, None])
    dk = jnp.where(idx[None, :] < valid, jnp.exp((valid - 1.0 - idx)[None, :] * log_gamma[:, None]), 0.0)
    dc = jnp.exp(valid * log_gamma)
    rep = lambda t: jnp.broadcast_to(t[..., None], t.shape + (LANES,))
    dc_rows = rep(jnp.repeat(dc, DK_A).reshape(H_A // 2, 2 * DK_A))
    return intra, rep(dq), rep(dk), dc_rows


def _ret_kernel(chunk, n_chunks, q_ref, k_ref, v_ref, ga_ref, s0_ref, intra_ref, dq_ref, dk_ref, dc_ref,
                o_ref, s_ref, st_ref):
    i = pl.program_id(1)

    @pl.when(i == 0)
    def _():
        st_ref[...] = s0_ref[0]

    lane = lax.broadcasted_iota(jnp.int32, (chunk, LANES), 1)
    for c in range(n_chunks):
        rows = slice(c * chunk, (c + 1) * chunk)
        for pair in range(H_A // 2):
            lanes = slice(pair * LANES, (pair + 1) * LANES)
            qc = q_ref[0, rows, lanes]
            kc = k_ref[0, rows, lanes]
            s_pair = st_ref[pair]
            s_bf = s_pair.astype(BF16)
            upd = s_pair * dc_ref[pair]
            for side in range(2):
                hd = 2 * pair + side
                mine = (lane >= 64) if side else (lane < 64)
                hl = slice(hd * DV_A, (hd + 1) * DV_A)
                qh = jnp.where(mine, qc, jnp.zeros_like(qc))
                vh = v_ref[0, rows, hl]
                sc = _dot_nt(qh, kc) * intra_ref[hd]
                o = _dot(sc.astype(BF16), vh) + _dot(qh, s_bf) * dq_ref[hd]
                kh = jnp.where(mine, kc.astype(F32) * dk_ref[hd], 0.0).astype(BF16)
                upd = upd + _dot_tn(kh, vh)
                gate = ga_ref[0, rows, hl].astype(F32)
                o_ref[0, rows, hl] = (_rms(o) * (gate * jax.nn.sigmoid(gate))).astype(BF16)
            st_ref[pair] = upd

    @pl.when(i == pl.num_programs(1) - 1)
    def _():
        s_ref[0] = st_ref[...]


def _retention(q, k, v, ga, s0, chunk, valid):
    b, l, _ = q.shape
    blk = min(l, 4 * chunk)
    n_chunks = blk // chunk
    tables = _ret_tables(chunk, valid)
    seq = lambda w: pl.BlockSpec((1, blk, w), lambda bi, i: (bi, i, 0))
    const = lambda a: pl.BlockSpec(a.shape, lambda bi, i: (0,) * a.ndim)
    state = pl.BlockSpec((1, H_A // 2, 2 * DK_A, DV_A), lambda bi, i: (bi, 0, 0, 0))
    o, s = pl.pallas_call(
        functools.partial(_ret_kernel, chunk, n_chunks),
        grid=(b, l // blk),
        in_specs=[seq(256), seq(256), seq(512), seq(512), state] + [const(t) for t in tables],
        out_specs=[seq(512), state],
        out_shape=[jax.ShapeDtypeStruct((b, l, H_A * DV_A), BF16),
                   jax.ShapeDtypeStruct((b, H_A // 2, 2 * DK_A, DV_A), F32)],
        scratch_shapes=[pltpu.VMEM((H_A // 2, 2 * DK_A, DV_A), F32)],
        compiler_params=_cparams("arbitrary", "arbitrary"),
        name="retention",
    )(q, k, v, ga, s0.reshape(b, H_A // 2, 2 * DK_A, DV_A), *tables)
    return o, s.reshape(b, H_A, DK_A, DV_A)


def _lam_value(lam_ref, lam_init):
    lv = lam_ref[...]
    return (jnp.exp(jnp.sum(lv[0:1] * lv[1:2], axis=-1, keepdims=True))
            - jnp.exp(jnp.sum(lv[2:3] * lv[3:4], axis=-1, keepdims=True)) + lam_init)


def _dattn_kernel(lam_init, tq, q_ref, k_ref, v_ref, lam_ref, g_ref, o_ref, vt_ref, acc_ref):
    qi = pl.program_id(2)

    @pl.when(qi == 0)
    def _():
        vt_ref[...] = v_ref[0].astype(F32).T.astype(BF16)

    q = q_ref[0]
    lane = lax.broadcasted_iota(jnp.int32, q.shape, 1)
    q_sub = (jnp.where(lane < DH_B, q, jnp.zeros_like(q)), jnp.where(lane >= DH_B, q, jnp.zeros_like(q)))
    acc_ref[...] = jnp.zeros_like(acc_ref)

    def absorb(kb, stats, masked):
        start = pl.multiple_of(kb * tq, tq)
        k = k_ref[0, pl.ds(start, tq), :]
        vt = vt_ref[:, pl.ds(start, tq)]
        out = []
        for sub in range(2):
            m_old, l_old = stats[2 * sub], stats[2 * sub + 1]
            st = _dot_nt(k, q_sub[sub])
            if masked:
                st = jnp.where(lax.broadcasted_iota(jnp.int32, st.shape, 0)
                               <= lax.broadcasted_iota(jnp.int32, st.shape, 1), st, -jnp.inf)
            m_new = jnp.maximum(m_old, jnp.max(st, axis=0, keepdims=True))
            alpha = jnp.exp2(m_old - m_new)
            pt = jnp.exp2(st - m_new)
            out += [m_new, alpha * l_old + jnp.sum(pt, axis=0, keepdims=True)]
            acc_ref[sub] = alpha * acc_ref[sub] + _dot(vt, pt.astype(BF16))
        return tuple(out)

    init = (jnp.full((1, tq), -jnp.inf, F32), jnp.zeros((1, tq), F32)) * 2
    stats = lax.fori_loop(0, qi, lambda kb, st: absorb(kb, st, False), init)
    stats = absorb(qi, stats, True)
    lam = _lam_value(lam_ref, lam_init)
    o = (acc_ref[0] / stats[1] - lam * (acc_ref[1] / stats[3])).T
    o_ref[0] = ((_rms(o) * g_ref[...]) * (1.0 - lam_init)).astype(BF16)


def _diff_attn_prompt(qb, kb, vb, lam_vecs, g_diff, lam_init):
    b, l, _ = qb.shape
    tq = 512
    whole = pl.BlockSpec((1, l, LANES), lambda bi, h, qi: (bi, 0, h))
    return pl.pallas_call(
        functools.partial(_dattn_kernel, lam_init, tq),
        grid=(b, H_B, l // tq),
        in_specs=[pl.BlockSpec((1, tq, LANES), lambda bi, h, qi: (bi, qi, h)), whole, whole,
                  pl.BlockSpec(lam_vecs.shape, lambda bi, h, qi: (0, 0)),
                  pl.BlockSpec((1, DV_B), lambda bi, h, qi: (0, 0))],
        out_specs=pl.BlockSpec((1, tq, LANES), lambda bi, h, qi: (bi, qi, h)),
        out_shape=jax.ShapeDtypeStruct((b, l, H_B * DV_B), BF16),
        scratch_shapes=[pltpu.VMEM((DV_B, l), BF16), pltpu.VMEM((2, DV_B, tq), F32)],
        compiler_params=_cparams("arbitrary", "arbitrary", "arbitrary"),
        name="diff_attn_prompt",
    )(qb, kb, vb, lam_vecs, g_diff.reshape(1, DV_B))


Q_ROWS = H_B * 2 * 8
KV_ROWS = PAGE_SIZE * H_B


def _dattn_dec_kernel(lam_init, n_pages, pt_ref, q_ref, kn_ref, vn_ref, *refs):
    k_refs = refs[:n_pages]
    v_refs = refs[n_pages:2 * n_pages]
    lam_ref, g_ref, o_ref, m_ref, l_ref, acc_ref = refs[2 * n_pages:]
    step = pl.program_id(1)
    q = q_ref[0]

    @pl.when(step == 0)
    def _():
        s = _dot_nt(q, kn_ref[0])
        r = lax.broadcasted_iota(jnp.int32, s.shape, 0)
        c = lax.broadcasted_iota(jnp.int32, s.shape, 1)
        s = jnp.where((c % H_B == r // 16) & (c // H_B <= r % 8), s, -jnp.inf)
        m = jnp.max(s, axis=-1, keepdims=True)
        p = jnp.exp2(s - m)
        m_ref[...] = m
        l_ref[...] = jnp.sum(p, axis=-1, keepdims=True)
        acc_ref[...] = _dot(p.astype(BF16), vn_ref[0])

    own_head = (lax.broadcasted_iota(jnp.int32, (Q_ROWS, KV_ROWS), 1) % H_B
                == lax.broadcasted_iota(jnp.int32, (Q_ROWS, KV_ROWS), 0) // 16)
    bias = jnp.where(own_head, 0.0, -jnp.inf)
    scores = [_dot_nt(q, kr[0].astype(BF16)) + bias for kr in k_refs]
    m_old = m_ref[...]
    m_new = m_old
    for s in scores:
        m_new = jnp.maximum(m_new, jnp.max(s, axis=-1, keepdims=True))
    alpha = jnp.exp2(m_old - m_new)
    l_new = alpha * l_ref[...]
    acc = alpha * acc_ref[...]
    for s, vr in zip(scores, v_refs):
        p = jnp.exp2(s - m_new)
        l_new = l_new + jnp.sum(p, axis=-1, keepdims=True)
        acc = acc + _dot(p.astype(BF16), vr[0].astype(BF16))
    m_ref[...] = m_new
    l_ref[...] = l_new
    acc_ref[...] = acc

    @pl.when(step == pl.num_programs(1) - 1)
    def _():
        lam = _lam_value(lam_ref, lam_init)
        o = acc / l_new
        for h in range(H_B):
            d = o[16 * h:16 * h + 8] - lam * o[16 * h + 8:16 * h + 16]
            o_ref[0, :, h * DV_B:(h + 1) * DV_B] = ((_rms(d) * g_ref[...]) * (1.0 - lam_init)).astype(BF16)


def _diff_attn_decode(q_rows, k_new, v_new, cache_k, cache_v, page_base, page_table, lam_vecs, g_diff, lam_init):
    b = q_rows.shape[0]
    n_steps = page_table.shape[1] // PAGES_PER_STEP
    per_b = lambda r, w: pl.BlockSpec((1, r, w), lambda bi, s, pt: (bi, 0, 0))

    def page_spec(j):
        return pl.BlockSpec((1, KV_ROWS, LANES),
                            lambda bi, s, pt: (page_base + pt[bi, s * PAGES_PER_STEP + j], 0, 0))

    pages = [page_spec(j) for j in range(PAGES_PER_STEP)]
    grid_spec = pltpu.PrefetchScalarGridSpec(
        num_scalar_prefetch=1,
        grid=(b, n_steps),
        in_specs=[per_b(Q_ROWS, LANES), per_b(DEC_SEQ * H_B, LANES), per_b(DEC_SEQ * H_B, LANES)]
        + pages + pages + [pl.BlockSpec(lam_vecs.shape, lambda bi, s, pt: (0, 0)),
                           pl.BlockSpec((1, DV_B), lambda bi, s, pt: (0, 0))],
        out_specs=per_b(8, H_B * DV_B),
        scratch_shapes=[pltpu.VMEM((Q_ROWS, 1), F32), pltpu.VMEM((Q_ROWS, 1), F32),
                        pltpu.VMEM((Q_ROWS, DV_B), F32)],
    )
    return pl.pallas_call(
        functools.partial(_dattn_dec_kernel, lam_init, PAGES_PER_STEP),
        grid_spec=grid_spec,
        out_shape=jax.ShapeDtypeStruct((b, 8, H_B * DV_B), BF16),
        compiler_params=_cparams("arbitrary", "arbitrary"),
        name="diff_attn_decode",
    )(page_table, q_rows, k_new, v_new, *([cache_k] * PAGES_PER_STEP), *([cache_v] * PAGES_PER_STEP),
      lam_vecs, g_diff.reshape(1, DV_B))


def _decode_q_rows(qb):
    b = qb.shape[0]
    q = qb.reshape(b, DEC_SEQ, H_B, DV_B).transpose(0, 2, 1, 3)
    q = jnp.pad(q, ((0, 0), (0, 0), (0, 8 - DEC_SEQ), (0, 0)))[:, :, None]
    keep = (np.arange(DV_B)[None, :] // DH_B) == np.arange(2)[:, None]
    keep = jnp.asarray(keep)[None, None, :, None, :]
    return jnp.where(keep, q, jnp.zeros_like(q)).reshape(b, Q_ROWS, DV_B)


def _gla_tables(chunk):
    idx = np.arange(chunk)
    i, t = idx[:, None], idx[None, :]
    mats = [t <= i, t > i]
    masks = [i == t]
    s = 1
    while s < chunk:
        same = (t // s) == (i // s)
        odd = ((i // s) % 2) == 1
        mats.append(same & np.where(odd, t <= i, t > i))
        masks.append(((i // (2 * s)) == (t // (2 * s))) & odd & (((t // s) % 2) == 0))
        s *= 2
    return (jnp.asarray(np.concatenate(mats, axis=0).astype(np.float32), BF16),
            jnp.asarray(np.stack(masks).astype(np.float32)))


def _gla_kernel(chunk, n_chunks, valid, q_ref, k_ref, g_ref, v_ref, r_ref, s0_ref, gn_ref, m_ref, p_ref,
                o_ref, s_ref, st_ref):
    i = pl.program_id(2)

    @pl.when(i == 0)
    def _():
        st_ref[...] = s0_ref[0, 0].T

    n_levels = p_ref.shape[0] - 1
    m_all = m_ref[...]
    live = lax.broadcasted_iota(jnp.int32, (chunk, 1), 0) < valid
    for c in range(n_chunks):
        rows = slice(c * chunk, (c + 1) * chunk)
        q = q_ref[0, rows, :]
        k = k_ref[0, rows, :]
        g = g_ref[0, rows, :]
        v = v_ref[0, rows, :]
        if valid < chunk:
            k = jnp.where(live, k, jnp.zeros_like(k))
            v = jnp.where(live, v, jnp.zeros_like(v))
            g = jnp.where(live, g, 0.0)
        g_hi = g.astype(BF16)
        rem = g - g_hi.astype(F32)
        g_mid = rem.astype(BF16)
        g_lo = (rem - g_mid.astype(F32)).astype(BF16)
        e = jnp.exp(_dot(m_all, g_hi) + _dot(m_all, g_mid) + _dot(m_all, g_lo))
        qf = q.astype(F32)
        kf = k.astype(F32)
        att = _dot_nt(q, k) * p_ref[0]
        for lv in range(n_levels):
            el = e[(2 + lv) * chunk:(3 + lv) * chunk]
            att = att + _dot_nt((qf * el).astype(BF16), (kf * el).astype(BF16)) * p_ref[1 + lv]
        st = st_ref[...]
        o = _dot(att.astype(BF16), v) + _dot_nt((qf * e[0:chunk]).astype(BF16), st.astype(BF16))
        st_ref[...] = st * e[chunk - 1:chunk] + _dot_tn(v, (kf * e[chunk:2 * chunk]).astype(BF16))
        gate = r_ref[0, rows, :].astype(F32)
        o_ref[0, rows, :] = ((_rms(o) * gn_ref[...]) * (gate * jax.nn.sigmoid(gate))).astype(BF16)

    @pl.when(i == pl.num_programs(2) - 1)
    def _():
        s_ref[0, 0] = st_ref[...].T


def _gla(q, k, g, v, r, s0, g_norm, chunk, valid):
    b, l, _ = q.shape
    blk = min(l, 4 * chunk)
    n_chunks = blk // chunk
    m_all, masks = _gla_tables(chunk)
    qk = pl.BlockSpec((1, blk, DK_C), lambda bi, h, i: (bi, i, h))
    vr = pl.BlockSpec((1, blk, DV_C), lambda bi, h, i: (bi, i, h))
    state = pl.BlockSpec((1, 1, DK_C, DV_C), lambda bi, h, i: (bi, h, 0, 0))
    const = lambda a: pl.BlockSpec(a.shape, lambda bi, h, i: (0,) * a.ndim)
    gn = g_norm.reshape(1, DV_C)
    return pl.pallas_call(
        functools.partial(_gla_kernel, chunk, n_chunks, valid),
        grid=(b, H_C, l // blk),
        in_specs=[qk, qk, qk, vr, vr, state, const(gn), const(m_all), const(masks)],
        out_specs=[vr, state],
        out_shape=[jax.ShapeDtypeStruct((b, l, H_C * DV_C), BF16),
                   jax.ShapeDtypeStruct((b, H_C, DK_C, DV_C), F32)],
        scratch_shapes=[pltpu.VMEM((DV_C, DK_C), F32)],
        compiler_params=_cparams("arbitrary", "arbitrary", "arbitrary"),
        name="gla",
    )(q, k, g, v, r, s0, gn, m_all, masks)


def _rope_tables(pos):
    inv = ROPE_THETA ** (-jnp.arange(0, DK_A, 2, dtype=F32) / DK_A)
    ang = pos.astype(F32)[:, None] * inv[None, :]
    cos = jnp.tile(jnp.cos(ang), (1, 4))
    sin = jnp.tile(jnp.concatenate([-jnp.sin(ang), jnp.sin(ang)], axis=1), (1, 2))
    return cos, sin


def _pad_seq(t, batch, seq_len):
    t = t.reshape(batch, seq_len, t.shape[-1])
    return jnp.pad(t, ((0, 0), (0, SAMPLE_PAD - seq_len), (0, 0)))


def _unpad_seq(t, seq_len):
    return t[:, :seq_len].reshape(t.shape[0] * seq_len, t.shape[-1])


def kernel(x_prompt, x_sample, cache_k, cache_v, state_ret, state_gla, state_conv, page_table, c_prompt, c_sample, w_ada, b_ada, g_pre_mix, g_post_mix, g_pre_ffn, g_post_ffn, w_in_ab, w_out_ab, lam_q1, lam_k1, lam_q2, lam_k2, g_diff, w_in_c, w_gate_c, b_gate_c, g_gla, w_out_c, w_up, conv_w, conv_b, w_down):
    rows_p, rows_s = BATCH * SEQ, DEC_BATCH * DEC_SEQ
    xp = x_prompt.reshape(rows_p, D_MODEL)
    xs = x_sample.reshape(rows_s, D_MODEL)

    n_c = BATCH + DEC_BATCH
    c_all = jnp.pad(jnp.concatenate([c_prompt, c_sample], axis=0), ((0, (-n_c) % 8), (0, 0)))
    ada = _ada(c_all, w_ada, b_ada)

    cos_p, sin_p = _rope_tables(jnp.arange(SEQ))
    cos_s, sin_s = _rope_tables(jnp.tile(PAST_LEN + jnp.arange(DEC_SEQ), DEC_BATCH))

    k_p, v_p, k_s, v_s, ret_p, ret_s, gla_p, gla_s, conv_p, conv_s = ([] for _ in range(10))
    for l in range(DEPTH):
        mp = jnp.split(ada[l, :BATCH], 6, axis=-1)
        ms = jnp.split(ada[l, BATCH:n_c], 6, axis=-1)
        if l % 2 == 0:
            e = l // 2
            lam_init = 0.8 - 0.6 * math.exp(-0.3 * l)
            lam_vecs = jnp.stack([lam_q1[e], lam_k1[e], lam_q2[e], lam_k2[e]]).astype(F32)
            w_in = w_in_ab[e].astype(BF16)
            w_out = w_out_ab[e].astype(BF16)
            w_parts = [w_out[:H_A * DV_A], w_out[H_A * DV_A:]]

            qa, ka, va, ga, qb, kb32, kb16, vb32, vb16 = _in_proj_ab(
                xp, mp[0], mp[1], g_pre_mix[l], w_in, cos_p, sin_p, SEQ)
            seq3 = lambda t: t.reshape(BATCH, SEQ, t.shape[-1])
            ret, sp = _retention(seq3(qa), seq3(ka), seq3(va), seq3(ga),
                                 jnp.zeros((BATCH, H_A, DK_A, DV_A), F32), CHUNK_RET, CHUNK_RET)
            dif = _diff_attn_prompt(seq3(qb), seq3(kb16), seq3(vb16), lam_vecs, g_diff[e], lam_init)
            xp = _out_proj([ret.reshape(rows_p, -1), dif.reshape(rows_p, -1)], w_parts, xp, mp[2],
                           g_post_mix[l], SEQ)
            k_p.append(kb32.reshape(BATCH, SEQ, H_B, 2 * DH_B))
            v_p.append(vb32.reshape(BATCH, SEQ, H_B, DV_B))
            ret_p.append(sp)

            qa, ka, va, ga, qb, kb32, kb16, vb32, vb16 = _in_proj_ab(
                xs, ms[0], ms[1], g_pre_mix[l], w_in, cos_s, sin_s, DEC_SEQ)
            pad = lambda t: _pad_seq(t, DEC_BATCH, DEC_SEQ)
            ret, ss = _retention(pad(qa), pad(ka), pad(va), pad(ga), state_ret[e], SAMPLE_PAD, DEC_SEQ)
            n_phys = cache_k.shape[1]
            kv_rows = lambda t: t.reshape(DEC_BATCH, DEC_SEQ * H_B, DV_B)
            dif = _diff_attn_decode(
                _decode_q_rows(qb.reshape(DEC_BATCH, DEC_SEQ, 512)), kv_rows(kb16), kv_rows(vb16),
                cache_k.reshape(N_EVEN * n_phys, KV_ROWS, LANES), cache_v.reshape(N_EVEN * n_phys, KV_ROWS, LANES),
                e * n_phys, page_table, lam_vecs, g_diff[e], lam_init)
            xs = _out_proj([_unpad_seq(ret, DEC_SEQ), _unpad_seq(dif, DEC_SEQ)], w_parts, xs, ms[2],
                           g_post_mix[l], DEC_SEQ)
            k_s.append(kb32.reshape(DEC_BATCH, DEC_SEQ, H_B, 2 * DH_B))
            v_s.append(vb32.reshape(DEC_BATCH, DEC_SEQ, H_B, DV_B))
            ret_s.append(ss)
        else:
            o = l // 2
            w_in = jnp.pad(w_in_c[o], ((0, 0), (0, LANES - GATE_RANK))).astype(BF16)
            w_gate = jnp.pad(w_gate_c[o], ((0, LANES - GATE_RANK), (0, 0))).astype(BF16)
            w_out = w_out_c[o].astype(BF16)

            q, k, v, r, lg = _in_proj_c(xp, mp[0], mp[1], g_pre_mix[l], w_in, w_gate, b_gate_c[o], SEQ)
            seq3 = lambda t: t.reshape(BATCH, SEQ, t.shape[-1])
            og, sp = _gla(seq3(q), seq3(k), seq3(lg), seq3(v), seq3(r),
                          jnp.zeros((BATCH, H_C, DK_C, DV_C), F32), g_gla[o], CHUNK_GLA, CHUNK_GLA)
            xp = _out_proj([og.reshape(rows_p, -1)], [w_out], xp, mp[2], g_post_mix[l], SEQ)
            gla_p.append(sp)

            q, k, v, r, lg = _in_proj_c(xs, ms[0], ms[1], g_pre_mix[l], w_in, w_gate, b_gate_c[o], DEC_SEQ)
            pad = lambda t: _pad_seq(t, DEC_BATCH, DEC_SEQ)
            og, ss = _gla(pad(q), pad(k), pad(lg), pad(v), pad(r), state_gla[o], g_gla[o], SAMPLE_PAD, DEC_SEQ)
            xs = _out_proj([_unpad_seq(og, DEC_SEQ)], [w_out], xs, ms[2], g_post_mix[l], DEC_SEQ)
            gla_s.append(ss)

        wup = w_up[l].astype(BF16)
        wdn = w_down[l].astype(BF16)
        xp, cbp = _conv_ffn(xp, mp[3], mp[4], mp[5], g_pre_ffn[l], g_post_ffn[l], wup, conv_w[l], conv_b[l],
                            wdn, SEQ)
        xs, up_s = _conv_ffn(xs, ms[3], ms[4], ms[5], g_pre_ffn[l], g_post_ffn[l], wup, conv_w[l], conv_b[l],
                             wdn, DEC_SEQ, state=state_conv[l])
        conv_p.append(cbp)
        conv_s.append(up_s.reshape(DEC_BATCH, DEC_SEQ, 2 * D_FF)[:, DEC_SEQ - (CONV_W - 1):])

    return (xp.reshape(BATCH, SEQ, D_MODEL), xs.reshape(DEC_BATCH, DEC_SEQ, D_MODEL),
            jnp.stack(k_p), jnp.stack(v_p), jnp.stack(k_s), jnp.stack(v_s),
            jnp.stack(ret_p), jnp.stack(ret_s), jnp.stack(gla_p), jnp.stack(gla_s),
            jnp.stack(conv_p), jnp.stack(conv_s))
```

```python
import functools
import math

import numpy as np
import jax
import jax.numpy as jnp
from jax import lax
from jax.experimental import pallas as pl
from jax.experimental.pallas import tpu as pltpu

D_MODEL = 1024
BATCH = 4
SEQ = 4096
DEPTH = 2
DEC_BATCH = 32
DEC_SEQ = 4
PAST_LEN = 8192
PAGE_SIZE = 128
N_EVEN = (DEPTH + 1) // 2
N_ODD = DEPTH // 2
H_A = 4
DK_A = D_MODEL // 16
DV_A = D_MODEL // 8
CHUNK_RET = 128
H_B = 4
DH_B = D_MODEL // 16
DV_B = 2 * DH_B
H_C = 4
DK_C = D_MODEL // 8
DV_C = D_MODEL // 4
GATE_RANK = 16
GATE_TAU = 16.0
CHUNK_GLA = 64
D_FF = ((8 * D_MODEL // 3 + 127) // 128) * 128
CONV_W = 3
ROPE_THETA = 10000.0
EPS = 1e-6

LANES = 128
VMEM_LIMIT = 56 * 1024 * 1024
ROW_TILE = 512
SAMPLE_PAD = 16
FF_CHUNK = 256
PAGES_PER_STEP = 16
LOG2E = math.log2(math.e)
BF16 = jnp.bfloat16
F32 = jnp.float32


def _cparams(*sem):
    return pltpu.CompilerParams(dimension_semantics=sem, vmem_limit_bytes=VMEM_LIMIT)


def _dot(a, b):
    return jnp.dot(a, b, preferred_element_type=F32)


def _dot_nt(a, b):
    return lax.dot_general(a, b, (((1,), (1,)), ((), ())), preferred_element_type=F32)


def _dot_tn(a, b):
    return lax.dot_general(a, b, (((0,), (0,)), ((), ())), preferred_element_type=F32)


def _rms(x):
    return x * lax.rsqrt(jnp.mean(x * x, axis=-1, keepdims=True) + EPS)


def _modulate(x, g, shift, scale):
    return (_rms(x) * g) * (1.0 + scale) + shift


def _rope(x, cos, sin_signed):
    w = x.shape[-1]
    fwd = pltpu.roll(x, 32, axis=1)
    bwd = pltpu.roll(x, w - 32, axis=1)
    reps = w // LANES
    first_half = (lax.broadcasted_iota(jnp.int32, x.shape, 1) % 64) < 32
    partner = jnp.where(first_half, bwd, fwd)
    return (x * jnp.concatenate([cos] * reps, axis=1)
            + partner * jnp.concatenate([sin_signed] * reps, axis=1))


def _ada_kernel(c_ref, w_ref, b_ref, o_ref):
    c = c_ref[...]
    s = c * jax.nn.sigmoid(c)
    o_ref[0] = jnp.dot(s, w_ref[0], preferred_element_type=F32, precision=lax.Precision.HIGHEST) + b_ref[0]


def _ada(c_all, w_ada, b_ada):
    rows = c_all.shape[0]
    tn = 1536
    return pl.pallas_call(
        _ada_kernel,
        grid=(DEPTH, 6 * D_MODEL // tn),
        in_specs=[pl.BlockSpec((rows, D_MODEL), lambda l, j: (0, 0)),
                  pl.BlockSpec((1, D_MODEL, tn), lambda l, j: (l, 0, j)),
                  pl.BlockSpec((1, 1, tn), lambda l, j: (l, 0, j))],
        out_specs=pl.BlockSpec((1, rows, tn), lambda l, j: (l, 0, j)),
        out_shape=jax.ShapeDtypeStruct((DEPTH, rows, 6 * D_MODEL), F32),
        compiler_params=_cparams("arbitrary", "arbitrary"),
        name="adaln",
    )(c_all, w_ada, b_ada.reshape(DEPTH, 1, 6 * D_MODEL))


def _row_cfg(rows, seq_len):
    tm = min(ROW_TILE, rows)
    per_token = seq_len < tm
    tiles_per_seq = 1 if per_token else seq_len // tm
    mod_spec = (pl.BlockSpec((1, tm, D_MODEL), lambda i: (i, 0, 0)) if per_token
                else pl.BlockSpec((1, 1, D_MODEL), lambda i: (i // tiles_per_seq, 0, 0)))
    return tm, per_token, tiles_per_seq, mod_spec


def _mod_arg(m, rows, seq_len, tm, per_token):
    if per_token:
        return jnp.repeat(m, seq_len, axis=0).reshape(rows // tm, tm, D_MODEL)
    return m.reshape(m.shape[0], 1, D_MODEL)


def _inab_kernel(x_ref, sh_ref, sc_ref, g_ref, w_ref, cos_ref, sin_ref,
                 qa_ref, ka_ref, va_ref, ga_ref, qb_ref, kb32_ref, kb16_ref, vb32_ref, vb16_ref):
    h = _modulate(x_ref[...], g_ref[...], sh_ref[0], sc_ref[0]).astype(BF16)
    cos = cos_ref[...]
    sin = sin_ref[...]

    def proj(lo, width):
        return _dot(h, w_ref[:, lo:lo + width])

    qa_ref[...] = _rope(proj(0, 256), cos, sin).astype(BF16)
    ka_ref[...] = (_rope(proj(256, 256), cos, sin) * (DK_A ** -0.5)).astype(BF16)
    va_ref[...] = proj(512, 512).astype(BF16)
    ga_ref[...] = proj(1024, 512).astype(BF16)
    qb_ref[...] = (_rope(proj(1536, 512), cos, sin) * (DH_B ** -0.5 * LOG2E)).astype(BF16)
    kb = _rope(proj(2048, 512), cos, sin)
    kb32_ref[...] = kb
    kb16_ref[...] = kb.astype(BF16)
    vb = proj(2560, 512)
    vb32_ref[...] = vb
    vb16_ref[...] = vb.astype(BF16)


def _in_proj_ab(x, shift, scale, g_pre, w_bf, cos, sin, seq_len):
    rows = x.shape[0]
    tm, per_token, tps, mod_spec = _row_cfg(rows, seq_len)
    pos_map = (lambda i: (i, 0)) if per_token else (lambda i: (i % tps, 0))
    row = lambda w: pl.BlockSpec((tm, w), lambda i: (i, 0))
    widths = (256, 256, 512, 512, 512, 512, 512, 512, 512)
    dtypes = (BF16, BF16, BF16, BF16, BF16, F32, BF16, F32, BF16)
    return pl.pallas_call(
        _inab_kernel,
        grid=(rows // tm,),
        in_specs=[row(D_MODEL), mod_spec, mod_spec,
                  pl.BlockSpec((1, D_MODEL), lambda i: (0, 0)),
                  pl.BlockSpec(w_bf.shape, lambda i: (0, 0)),
                  pl.BlockSpec((tm, LANES), pos_map), pl.BlockSpec((tm, LANES), pos_map)],
        out_specs=[row(w) for w in widths],
        out_shape=[jax.ShapeDtypeStruct((rows, w), d) for w, d in zip(widths, dtypes)],
        compiler_params=_cparams("arbitrary"),
        name="in_proj_ab",
    )(x, _mod_arg(shift, rows, seq_len, tm, per_token), _mod_arg(scale, rows, seq_len, tm, per_token),
      g_pre.reshape(1, D_MODEL), w_bf, cos, sin)


def _inc_kernel(x_ref, sh_ref, sc_ref, g_ref, w_ref, wg_ref, bg_ref,
                q_ref, k_ref, v_ref, r_ref, lg_ref):
    h = _modulate(x_ref[...], g_ref[...], sh_ref[0], sc_ref[0]).astype(BF16)

    def proj(lo, width):
        return _dot(h, w_ref[:, lo:lo + width])

    q_ref[...] = (proj(0, 512) * (DK_C ** -0.5)).astype(BF16)
    k_ref[...] = proj(512, 512).astype(BF16)
    v_ref[...] = proj(1024, 1024).astype(BF16)
    r_ref[...] = proj(2048, 1024).astype(BF16)
    a = proj(3072, LANES)
    z = _dot(a.astype(BF16), wg_ref[...]) + bg_ref[...]
    softplus_neg = jnp.maximum(-z, 0.0) + jnp.log1p(jnp.exp(-jnp.abs(z)))
    lg_ref[...] = -softplus_neg / GATE_TAU


def _in_proj_c(x, shift, scale, g_pre, w_bf, wg_bf, b_gate, seq_len):
    rows = x.shape[0]
    tm, per_token, tps, mod_spec = _row_cfg(rows, seq_len)
    row = lambda w: pl.BlockSpec((tm, w), lambda i: (i, 0))
    widths = (512, 512, 1024, 1024, 512)
    dtypes = (BF16, BF16, BF16, BF16, F32)
    full = lambda a: pl.BlockSpec(a.shape, lambda i: (0, 0))
    bg = b_gate.reshape(1, -1)
    return pl.pallas_call(
        _inc_kernel,
        grid=(rows // tm,),
        in_specs=[row(D_MODEL), mod_spec, mod_spec, pl.BlockSpec((1, D_MODEL), lambda i: (0, 0)),
                  full(w_bf), full(wg_bf), full(bg)],
        out_specs=[row(w) for w in widths],
        out_shape=[jax.ShapeDtypeStruct((rows, w), d) for w, d in zip(widths, dtypes)],
        compiler_params=_cparams("arbitrary"),
        name="in_proj_c",
    )(x, _mod_arg(shift, rows, seq_len, tm, per_token), _mod_arg(scale, rows, seq_len, tm, per_token),
      g_pre.reshape(1, D_MODEL), w_bf, wg_bf, bg)


def _outproj_kernel(n_parts, *refs):
    parts = refs[:n_parts]
    ws = refs[n_parts:2 * n_parts]
    x_ref, gate_ref, g_ref, o_ref = refs[2 * n_parts:]
    y = _dot(parts[0][...], ws[0][...])
    for p, w in zip(parts[1:], ws[1:]):
        y = y + _dot(p[...], w[...])
    o_ref[...] = x_ref[...] + gate_ref[0] * (_rms(y) * g_ref[...])


def _out_proj(parts, weights, x, gate, g_post, seq_len):
    rows = x.shape[0]
    tm, per_token, tps, mod_spec = _row_cfg(rows, seq_len)
    n = len(parts)
    return pl.pallas_call(
        functools.partial(_outproj_kernel, n),
        grid=(rows // tm,),
        in_specs=([pl.BlockSpec((tm, p.shape[1]), lambda i: (i, 0)) for p in parts]
                  + [pl.BlockSpec(w.shape, lambda i: (0, 0)) for w in weights]
                  + [pl.BlockSpec((tm, D_MODEL), lambda i: (i, 0)), mod_spec,
                     pl.BlockSpec((1, D_MODEL), lambda i: (0, 0))]),
        out_specs=pl.BlockSpec((tm, D_MODEL), lambda i: (i, 0)),
        out_shape=jax.ShapeDtypeStruct((rows, D_MODEL), F32),
        compiler_params=_cparams("arbitrary"),
        name="out_proj",
    )(*parts, *weights, x, _mod_arg(gate, rows, seq_len, tm, per_token), g_post.reshape(1, D_MODEL))


def _ffn_kernel(per_token, seq_len, tiles_per_seq, *refs):
    if per_token:
        (x_ref, sh_ref, sc_ref, gate_ref, gpre_ref, gpost_ref, wup_ref, cw_ref, cb_ref, wdn_ref,
         s1_ref, s2_ref, xo_ref, conv_ref, h_ref, acc_ref, u_ref) = refs
    else:
        (x_ref, sh_ref, sc_ref, gate_ref, gpre_ref, gpost_ref, wup_ref, cw_ref, cb_ref, wdn_ref,
         xo_ref, conv_ref, h_ref, acc_ref, u_ref) = refs
    tm = x_ref.shape[0]
    halo = h_ref.shape[0] - tm
    x = x_ref[...]
    if per_token:
        tau = lax.broadcasted_iota(jnp.int32, (tm, FF_CHUNK), 0) % seq_len
    else:
        seq_start = (pl.program_id(0) % tiles_per_seq) == 0

        @pl.when(seq_start)
        def _():
            h_ref[:halo] = jnp.zeros((halo, D_MODEL), BF16)

        @pl.when(jnp.logical_not(seq_start))
        def _():
            h_ref[:halo] = h_ref[tm:]
    h_ref[halo:] = _modulate(x, gpre_ref[...], sh_ref[0], sc_ref[0]).astype(BF16)
    acc_ref[...] = jnp.zeros_like(acc_ref)

    def up(lo):
        return _dot(h_ref[...], wup_ref[:, pl.ds(lo, FF_CHUNK)])

    def conv(u, lo):
        cols = pl.ds(lo, FF_CHUNK)
        cw = cw_ref[:, cols]
        cb = cb_ref[:, cols]
        taps = lambda u2, u1, u0: cb + cw[0:1] * u2 + cw[1:2] * u1 + cw[2:3] * u0
        if per_token:
            u1 = jnp.where(tau == 0, s1_ref[:, cols], pltpu.roll(u, 1, axis=0))
            u2 = jnp.where(tau < 2, s2_ref[:, cols], pltpu.roll(u, 2, axis=0))
            conv_ref[:, cols] = u
            return taps(u2, u1, u)
        conv_ref[0, :, cols] = u[halo + tm - 2:]
        return taps(u[halo - 2:halo - 2 + tm], u[halo - 1:halo - 1 + tm], u[halo:])

    def produce(slot, lo):
        u_ref[slot, 0] = up(lo)
        u_ref[slot, 1] = up(lo + D_FF)

    def consume(slot, lo):
        g = (jax.nn.gelu(conv(u_ref[slot, 0], lo)) * conv(u_ref[slot, 1], lo + D_FF)).astype(BF16)
        acc_ref[...] += _dot(g, wdn_ref[pl.ds(lo, FF_CHUNK), :])

    def body(j, carry):
        lo = pl.multiple_of(j * (2 * FF_CHUNK), 2 * FF_CHUNK)
        produce(1, lo + FF_CHUNK)
        consume(0, lo)
        produce(0, lo + 2 * FF_CHUNK)
        consume(1, lo + FF_CHUNK)
        return carry

    n_chunks = D_FF // FF_CHUNK
    produce(0, 0)
    lax.fori_loop(0, n_chunks // 2, body, 0)
    consume(0, (n_chunks - 1) * FF_CHUNK)
    xo_ref[...] = x + gate_ref[0] * (_rms(acc_ref[...]) * gpost_ref[...])


def _conv_ffn(x, shift, scale, gate, g_pre, g_post, wup_bf, conv_w, conv_b, wdn_bf, seq_len, state=None):
    rows = x.shape[0]
    tm, per_token, tps, mod_spec = _row_cfg(rows, seq_len)
    full = lambda a: pl.BlockSpec(a.shape, lambda i: (0, 0))
    cb = conv_b.reshape(1, 2 * D_FF)
    in_specs = [pl.BlockSpec((tm, D_MODEL), lambda i: (i, 0)), mod_spec, mod_spec, mod_spec,
                pl.BlockSpec((1, D_MODEL), lambda i: (0, 0)), pl.BlockSpec((1, D_MODEL), lambda i: (0, 0)),
                full(wup_bf), full(conv_w), full(cb), full(wdn_bf)]
    args = [x] + [_mod_arg(m, rows, seq_len, tm, per_token) for m in (shift, scale, gate)] + [
        g_pre.reshape(1, D_MODEL), g_post.reshape(1, D_MODEL), wup_bf, conv_w, cb, wdn_bf]
    halo = 0 if per_token else SAMPLE_PAD
    scratch = [pltpu.VMEM((halo + tm, D_MODEL), BF16), pltpu.VMEM((tm, D_MODEL), F32),
               pltpu.VMEM((2, 2, halo + tm, FF_CHUNK), F32)]
    if per_token:
        batch = rows // seq_len
        zeros = jnp.zeros((batch, seq_len - 1, 2 * D_FF), F32)
        s1 = jnp.concatenate([state[:, 1:2], zeros], axis=1).reshape(rows, 2 * D_FF)
        s2 = jnp.concatenate([state, zeros[:, 1:]], axis=1).reshape(rows, 2 * D_FF)
        in_specs += [pl.BlockSpec((tm, 2 * D_FF), lambda i: (i, 0))] * 2
        args += [s1, s2]
        conv_spec = pl.BlockSpec((tm, 2 * D_FF), lambda i: (i, 0))
        conv_shape = jax.ShapeDtypeStruct((rows, 2 * D_FF), F32)
    else:
        conv_spec = pl.BlockSpec((1, 2, 2 * D_FF), lambda i: (i // tps, 0, 0))
        conv_shape = jax.ShapeDtypeStruct((rows // seq_len, 2, 2 * D_FF), F32)
    return pl.pallas_call(
        functools.partial(_ffn_kernel, per_token, seq_len, tps),
        grid=(rows // tm,),
        in_specs=in_specs,
        out_specs=[pl.BlockSpec((tm, D_MODEL), lambda i: (i, 0)), conv_spec],
        out_shape=[jax.ShapeDtypeStruct((rows, D_MODEL), F32), conv_shape],
        scratch_shapes=scratch,
        compiler_params=_cparams("arbitrary"),
        name="conv_ffn",
    )(*args)


def _ret_tables(chunk, valid):
    log_gamma = jnp.log1p(-jnp.exp2(-5.0 - jnp.arange(H_A, dtype=F32)))
    idx = jnp.arange(chunk, dtype=F32)
    rel = idx[:, None] - idx[None, :]
    intra = jnp.exp(jnp.where(rel[None] >= 0, rel[None] * log_gamma[:, None, None], -jnp.inf))
    dq = jnp.exp((idx + 1.0)[None, :] * log_gamma[:, None])
    dk = jnp.where(idx[None, :] < valid, jnp.exp((valid - 1.0 - idx)[None, :] * log_gamma[:, None]), 0.0)
    dc = jnp.exp(valid * log_gamma)
    rep = lambda t: jnp.broadcast_to(t[..., None], t.shape + (LANES,))
    dc_rows = rep(jnp.repeat(dc, DK_A).reshape(H_A // 2, 2 * DK_A))
    return intra, rep(dq), rep(dk), dc_rows


def _ret_kernel(chunk, n_chunks, q_ref, k_ref, v_ref, ga_ref, s0_ref, intra_ref, dq_ref, dk_ref, dc_ref,
                o_ref, s_ref, st_ref):
    i = pl.program_id(1)

    @pl.when(i == 0)
    def _():
        st_ref[...] = s0_ref[0]

    lane = lax.broadcasted_iota(jnp.int32, (chunk, LANES), 1)
    for c in range(n_chunks):
        rows = slice(c * chunk, (c + 1) * chunk)
        for pair in range(H_A // 2):
            lanes = slice(pair * LANES, (pair + 1) * LANES)
            qc = q_ref[0, rows, lanes]
            kc = k_ref[0, rows, lanes]
            s_pair = st_ref[pair]
            s_bf = s_pair.astype(BF16)
            upd = s_pair * dc_ref[pair]
            for side in range(2):
                hd = 2 * pair + side
                mine = (lane >= 64) if side else (lane < 64)
                hl = slice(hd * DV_A, (hd + 1) * DV_A)
                qh = jnp.where(mine, qc, jnp.zeros_like(qc))
                vh = v_ref[0, rows, hl]
                sc = _dot_nt(qh, kc) * intra_ref[hd]
                o = _dot(sc.astype(BF16), vh) + _dot(qh, s_bf) * dq_ref[hd]
                kh = jnp.where(mine, kc.astype(F32) * dk_ref[hd], 0.0).astype(BF16)
                upd = upd + _dot_tn(kh, vh)
                gate = ga_ref[0, rows, hl].astype(F32)
                o_ref[0, rows, hl] = (_rms(o) * (gate * jax.nn.sigmoid(gate))).astype(BF16)
            st_ref[pair] = upd

    @pl.when(i == pl.num_programs(1) - 1)
    def _():
        s_ref[0] = st_ref[...]


def _retention(q, k, v, ga, s0, chunk, valid):
    b, l, _ = q.shape
    blk = min(l, 4 * chunk)
    n_chunks = blk // chunk
    tables = _ret_tables(chunk, valid)
    seq = lambda w: pl.BlockSpec((1, blk, w), lambda bi, i: (bi, i, 0))
    const = lambda a: pl.BlockSpec(a.shape, lambda bi, i: (0,) * a.ndim)
    state = pl.BlockSpec((1, H_A // 2, 2 * DK_A, DV_A), lambda bi, i: (bi, 0, 0, 0))
    o, s = pl.pallas_call(
        functools.partial(_ret_kernel, chunk, n_chunks),
        grid=(b, l // blk),
        in_specs=[seq(256), seq(256), seq(512), seq(512), state] + [const(t) for t in tables],
        out_specs=[seq(512), state],
        out_shape=[jax.ShapeDtypeStruct((b, l, H_A * DV_A), BF16),
                   jax.ShapeDtypeStruct((b, H_A // 2, 2 * DK_A, DV_A), F32)],
        scratch_shapes=[pltpu.VMEM((H_A // 2, 2 * DK_A, DV_A), F32)],
        compiler_params=_cparams("arbitrary", "arbitrary"),
        name="retention",
    )(q, k, v, ga, s0.reshape(b, H_A // 2, 2 * DK_A, DV_A), *tables)
    return o, s.reshape(b, H_A, DK_A, DV_A)


def _lam_value(lam_ref, lam_init):
    lv = lam_ref[...]
    return (jnp.exp(jnp.sum(lv[0:1] * lv[1:2], axis=-1, keepdims=True))
            - jnp.exp(jnp.sum(lv[2:3] * lv[3:4], axis=-1, keepdims=True)) + lam_init)


def _dattn_kernel(lam_init, tq, q_ref, k_ref, v_ref, lam_ref, g_ref, o_ref, vt_ref, st_ref, m_ref, l_ref, acc_ref):
    qi = pl.program_id(2)

    @pl.when(qi == 0)
    def _():
        vt_ref[...] = v_ref[0].astype(F32).T.astype(BF16)

    q = q_ref[0]
    lane = lax.broadcasted_iota(jnp.int32, q.shape, 1)
    q2 = jnp.concatenate([jnp.where(lane < DH_B, q, jnp.zeros_like(q)),
                          jnp.where(lane >= DH_B, q, jnp.zeros_like(q))], axis=0)
    m_ref[...] = jnp.full_like(m_ref, -jnp.inf)
    l_ref[...] = jnp.zeros_like(l_ref)
    acc_ref[...] = jnp.zeros_like(acc_ref)

    def scores(slot, kb, masked):
        start = pl.multiple_of(kb * tq, tq)
        st = _dot_nt(k_ref[0, pl.ds(start, tq), :], q2)
        if masked:
            key = lax.broadcasted_iota(jnp.int32, st.shape, 0)
            qry = jnp.bitwise_and(lax.broadcasted_iota(jnp.int32, st.shape, 1), tq - 1)
            st = jnp.where(key <= qry, st, -jnp.inf)
        st_ref[slot] = st

    def absorb(slot, kb):
        start = pl.multiple_of(kb * tq, tq)
        st = st_ref[slot]
        m_old = m_ref[...]
        m_new = jnp.maximum(m_old, jnp.max(st, axis=0, keepdims=True))
        alpha = jnp.exp2(m_old - m_new)
        pt = jnp.exp2(st - m_new)
        l_ref[...] = alpha * l_ref[...] + jnp.sum(pt, axis=0, keepdims=True)
        acc_ref[...] = alpha * acc_ref[...] + _dot(vt_ref[:, pl.ds(start, tq)], pt.astype(BF16))
        m_ref[...] = m_new

    pairs = lax.shift_right_logical(qi, 1)
    odd = jnp.bitwise_and(qi, 1) == 1
    scores(0, qi, True)

    def body(j, carry):
        scores(1, 2 * j, False)
        absorb(0, jnp.where(j == 0, qi, 2 * j - 1))
        scores(0, 2 * j + 1, False)
        absorb(1, 2 * j)
        return carry

    lax.fori_loop(0, pairs, body, 0)

    @pl.when(odd)
    def _():
        scores(1, qi - 1, False)

    absorb(0, jnp.where(pairs == 0, qi, 2 * pairs - 1))

    @pl.when(odd)
    def _():
        absorb(1, qi - 1)

    lam = _lam_value(lam_ref, lam_init)
    inv_l = 1.0 / l_ref[...]
    acc = acc_ref[...] * inv_l
    o = (acc[:, :tq] - lam * acc[:, tq:]).T
    o_ref[0] = ((_rms(o) * g_ref[...]) * (1.0 - lam_init)).astype(BF16)


def _diff_attn_prompt(qb, kb, vb, lam_vecs, g_diff, lam_init):
    b, l, _ = qb.shape
    tq = 512
    whole = pl.BlockSpec((1, l, LANES), lambda bi, h, qi: (bi, 0, h))
    return pl.pallas_call(
        functools.partial(_dattn_kernel, lam_init, tq),
        grid=(b, H_B, l // tq),
        in_specs=[pl.BlockSpec((1, tq, LANES), lambda bi, h, qi: (bi, qi, h)), whole, whole,
                  pl.BlockSpec(lam_vecs.shape, lambda bi, h, qi: (0, 0)),
                  pl.BlockSpec((1, DV_B), lambda bi, h, qi: (0, 0))],
        out_specs=pl.BlockSpec((1, tq, LANES), lambda bi, h, qi: (bi, qi, h)),
        out_shape=jax.ShapeDtypeStruct((b, l, H_B * DV_B), BF16),
        scratch_shapes=[pltpu.VMEM((DV_B, l), BF16), pltpu.VMEM((2, tq, 2 * tq), F32),
                        pltpu.VMEM((1, 2 * tq), F32), pltpu.VMEM((1, 2 * tq), F32),
                        pltpu.VMEM((DV_B, 2 * tq), F32)],
        compiler_params=_cparams("arbitrary", "arbitrary", "arbitrary"),
        name="diff_attn_prompt",
    )(qb, kb, vb, lam_vecs, g_diff.reshape(1, DV_B))


Q_ROWS = H_B * 2 * 8
KV_ROWS = PAGE_SIZE * H_B


def _dattn_dec_kernel(lam_init, n_pages, pt_ref, q_ref, kn_ref, vn_ref, *refs):
    k_refs = refs[:n_pages]
    v_refs = refs[n_pages:2 * n_pages]
    lam_ref, g_ref, o_ref, m_ref, l_ref, acc_ref = refs[2 * n_pages:]
    step = pl.program_id(1)
    q = q_ref[0]

    @pl.when(step == 0)
    def _():
        s = _dot_nt(q, kn_ref[0])
        r = lax.broadcasted_iota(jnp.int32, s.shape, 0)
        c = lax.broadcasted_iota(jnp.int32, s.shape, 1)
        s = jnp.where((c % H_B == r // 16) & (c // H_B <= r % 8), s, -jnp.inf)
        m = jnp.max(s, axis=-1, keepdims=True)
        p = jnp.exp2(s - m)
        m_ref[...] = m
        l_ref[...] = jnp.sum(p, axis=-1, keepdims=True)
        acc_ref[...] = _dot(p.astype(BF16), vn_ref[0])

    own_head = (lax.broadcasted_iota(jnp.int32, (Q_ROWS, KV_ROWS), 1) % H_B
                == lax.broadcasted_iota(jnp.int32, (Q_ROWS, KV_ROWS), 0) // 16)
    bias = jnp.where(own_head, 0.0, -jnp.inf)
    scores = [_dot_nt(q, kr[0].astype(BF16)) + bias for kr in k_refs]
    m_old = m_ref[...]
    m_new = m_old
    for s in scores:
        m_new = jnp.maximum(m_new, jnp.max(s, axis=-1, keepdims=True))
    alpha = jnp.exp2(m_old - m_new)
    l_new = alpha * l_ref[...]
    acc = alpha * acc_ref[...]
    for s, vr in zip(scores, v_refs):
        p = jnp.exp2(s - m_new)
        l_new = l_new + jnp.sum(p, axis=-1, keepdims=True)
        acc = acc + _dot(p.astype(BF16), vr[0].astype(BF16))
    m_ref[...] = m_new
    l_ref[...] = l_new
    acc_ref[...] = acc

    @pl.when(step == pl.num_programs(1) - 1)
    def _():
        lam = _lam_value(lam_ref, lam_init)
        o = acc / l_new
        for h in range(H_B):
            d = o[16 * h:16 * h + 8] - lam * o[16 * h + 8:16 * h + 16]
            o_ref[0, :, h * DV_B:(h + 1) * DV_B] = ((_rms(d) * g_ref[...]) * (1.0 - lam_init)).astype(BF16)


def _diff_attn_decode(q_rows, k_new, v_new, cache_k, cache_v, page_base, page_table, lam_vecs, g_diff, lam_init):
    b = q_rows.shape[0]
    n_steps = page_table.shape[1] // PAGES_PER_STEP
    per_b = lambda r, w: pl.BlockSpec((1, r, w), lambda bi, s, pt: (bi, 0, 0))

    def page_spec(j):
        return pl.BlockSpec((1, KV_ROWS, LANES),
                            lambda bi, s, pt: (page_base + pt[bi, s * PAGES_PER_STEP + j], 0, 0))

    pages = [page_spec(j) for j in range(PAGES_PER_STEP)]
    grid_spec = pltpu.PrefetchScalarGridSpec(
        num_scalar_prefetch=1,
        grid=(b, n_steps),
        in_specs=[per_b(Q_ROWS, LANES), per_b(DEC_SEQ * H_B, LANES), per_b(DEC_SEQ * H_B, LANES)]
        + pages + pages + [pl.BlockSpec(lam_vecs.shape, lambda bi, s, pt: (0, 0)),
                           pl.BlockSpec((1, DV_B), lambda bi, s, pt: (0, 0))],
        out_specs=per_b(8, H_B * DV_B),
        scratch_shapes=[pltpu.VMEM((Q_ROWS, 1), F32), pltpu.VMEM((Q_ROWS, 1), F32),
                        pltpu.VMEM((Q_ROWS, DV_B), F32)],
    )
    return pl.pallas_call(
        functools.partial(_dattn_dec_kernel, lam_init, PAGES_PER_STEP),
        grid_spec=grid_spec,
        out_shape=jax.ShapeDtypeStruct((b, 8, H_B * DV_B), BF16),
        compiler_params=_cparams("arbitrary", "arbitrary"),
        name="diff_attn_decode",
    )(page_table, q_rows, k_new, v_new, *([cache_k] * PAGES_PER_STEP), *([cache_v] * PAGES_PER_STEP),
      lam_vecs, g_diff.reshape(1, DV_B))


def _decode_q_rows(qb):
    b = qb.shape[0]
    q = qb.reshape(b, DEC_SEQ, H_B, DV_B).transpose(0, 2, 1, 3)
    q = jnp.pad(q, ((0, 0), (0, 0), (0, 8 - DEC_SEQ), (0, 0)))[:, :, None]
    keep = (np.arange(DV_B)[None, :] // DH_B) == np.arange(2)[:, None]
    keep = jnp.asarray(keep)[None, None, :, None, :]
    return jnp.where(keep, q, jnp.zeros_like(q)).reshape(b, Q_ROWS, DV_B)


def _gla_tables(chunk):
    idx = np.arange(chunk)
    i, t = idx[:, None], idx[None, :]
    mats = [t <= i, t > i]
    masks = [i == t]
    s = 1
    while s < chunk:
        same = (t // s) == (i // s)
        odd = ((i // s) % 2) == 1
        mats.append(same & np.where(odd, t <= i, t > i))
        masks.append(((i // (2 * s)) == (t // (2 * s))) & odd & (((t // s) % 2) == 0))
        s *= 2
    return (jnp.asarray(np.concatenate(mats, axis=0).astype(np.float32), BF16),
            jnp.asarray(np.stack(masks).astype(np.float32)))


def _gla_kernel(chunk, n_chunks, valid, q_ref, k_ref, g_ref, v_ref, r_ref, s0_ref, gn_ref, m_ref, p_ref,
                o_ref, s_ref, st_ref):
    i = pl.program_id(1)

    @pl.when(i == 0)
    def _():
        for h in range(H_C):
            st_ref[h] = s0_ref[0, h].T

    n_levels = p_ref.shape[0] - 1
    m_all = m_ref[...]
    live = lax.broadcasted_iota(jnp.int32, (chunk, 1), 0) < valid
    for c in range(n_chunks):
        rows = slice(c * chunk, (c + 1) * chunk)
        g = g_ref[0, rows, :]
        if valid < chunk:
            g = jnp.where(live, g, 0.0)
        g_hi = g.astype(BF16)
        g_lo = (g - g_hi.astype(F32)).astype(BF16)
        e_all = jnp.exp(_dot(m_all, g_hi) + _dot(m_all, g_lo))
        for h in range(H_C):
            kl = slice(h * DK_C, (h + 1) * DK_C)
            vl = slice(h * DV_C, (h + 1) * DV_C)
            e = e_all[:, kl]
            q = q_ref[0, rows, kl]
            k = k_ref[0, rows, kl]
            v = v_ref[0, rows, vl]
            if valid < chunk:
                k = jnp.where(live, k, jnp.zeros_like(k))
                v = jnp.where(live, v, jnp.zeros_like(v))
            qf = q.astype(F32)
            kf = k.astype(F32)
            att = _dot_nt(q, k) * p_ref[0]
            for lv in range(n_levels):
                el = e[(2 + lv) * chunk:(3 + lv) * chunk]
                att = att + _dot_nt((qf * el).astype(BF16), (kf * el).astype(BF16)) * p_ref[1 + lv]
            st = st_ref[h]
            o = _dot(att.astype(BF16), v) + _dot_nt((qf * e[0:chunk]).astype(BF16), st.astype(BF16))
            st_ref[h] = st * e[chunk - 1:chunk] + _dot_tn(v, (kf * e[chunk:2 * chunk]).astype(BF16))
            gate = r_ref[0, rows, vl].astype(F32)
            o_ref[0, rows, vl] = ((_rms(o) * gn_ref[...]) * (gate * jax.nn.sigmoid(gate))).astype(BF16)

    @pl.when(i == pl.num_programs(1) - 1)
    def _():
        for h in range(H_C):
            s_ref[0, h] = st_ref[h].T


def _gla(q, k, g, v, r, s0, g_norm, chunk, valid):
    b, l, _ = q.shape
    blk = min(l, 4 * chunk)
    n_chunks = blk // chunk
    m_all, masks = _gla_tables(chunk)
    qk = pl.BlockSpec((1, blk, H_C * DK_C), lambda bi, i: (bi, i, 0))
    vr = pl.BlockSpec((1, blk, H_C * DV_C), lambda bi, i: (bi, i, 0))
    state = pl.BlockSpec((1, H_C, DK_C, DV_C), lambda bi, i: (bi, 0, 0, 0))
    const = lambda a: pl.BlockSpec(a.shape, lambda bi, i: (0,) * a.ndim)
    gn = g_norm.reshape(1, DV_C)
    return pl.pallas_call(
        functools.partial(_gla_kernel, chunk, n_chunks, valid),
        grid=(b, l // blk),
        in_specs=[qk, qk, qk, vr, vr, state, const(gn), const(m_all), const(masks)],
        out_specs=[vr, state],
        out_shape=[jax.ShapeDtypeStruct((b, l, H_C * DV_C), BF16),
                   jax.ShapeDtypeStruct((b, H_C, DK_C, DV_C), F32)],
        scratch_shapes=[pltpu.VMEM((H_C, DV_C, DK_C), F32)],
        compiler_params=_cparams("arbitrary", "arbitrary"),
        name="gla",
    )(q, k, g, v, r, s0, gn, m_all, masks)


def _rope_tables(pos):
    inv = ROPE_THETA ** (-jnp.arange(0, DK_A, 2, dtype=F32) / DK_A)
    ang = pos.astype(F32)[:, None] * inv[None, :]
    cos = jnp.tile(jnp.cos(ang), (1, 4))
    sin = jnp.tile(jnp.concatenate([-jnp.sin(ang), jnp.sin(ang)], axis=1), (1, 2))
    return cos, sin


def _pad_seq(t, batch, seq_len):
    t = t.reshape(batch, seq_len, t.shape[-1])
    return jnp.pad(t, ((0, 0), (0, SAMPLE_PAD - seq_len), (0, 0)))


def _unpad_seq(t, seq_len):
    return t[:, :seq_len].reshape(t.shape[0] * seq_len, t.shape[-1])


def kernel(x_prompt, x_sample, cache_k, cache_v, state_ret, state_gla, state_conv, page_table, c_prompt, c_sample, w_ada, b_ada, g_pre_mix, g_post_mix, g_pre_ffn, g_post_ffn, w_in_ab, w_out_ab, lam_q1, lam_k1, lam_q2, lam_k2, g_diff, w_in_c, w_gate_c, b_gate_c, g_gla, w_out_c, w_up, conv_w, conv_b, w_down):
    rows_p, rows_s = BATCH * SEQ, DEC_BATCH * DEC_SEQ
    xp = x_prompt.reshape(rows_p, D_MODEL)
    xs = x_sample.reshape(rows_s, D_MODEL)

    n_c = BATCH + DEC_BATCH
    c_all = jnp.pad(jnp.concatenate([c_prompt, c_sample], axis=0), ((0, (-n_c) % 8), (0, 0)))
    ada = _ada(c_all, w_ada, b_ada)

    cos_p, sin_p = _rope_tables(jnp.arange(SEQ))
    cos_s, sin_s = _rope_tables(jnp.tile(PAST_LEN + jnp.arange(DEC_SEQ), DEC_BATCH))

    k_p, v_p, k_s, v_s, ret_p, ret_s, gla_p, gla_s, conv_p, conv_s = ([] for _ in range(10))
    for l in range(DEPTH):
        mp = jnp.split(ada[l, :BATCH], 6, axis=-1)
        ms = jnp.split(ada[l, BATCH:n_c], 6, axis=-1)
        if l % 2 == 0:
            e = l // 2
            lam_init = 0.8 - 0.6 * math.exp(-0.3 * l)
            lam_vecs = jnp.stack([lam_q1[e], lam_k1[e], lam_q2[e], lam_k2[e]]).astype(F32)
            w_in = w_in_ab[e].astype(BF16)
            w_out = w_out_ab[e].astype(BF16)
            w_parts = [w_out[:H_A * DV_A], w_out[H_A * DV_A:]]

            qa, ka, va, ga, qb, kb32, kb16, vb32, vb16 = _in_proj_ab(
                xp, mp[0], mp[1], g_pre_mix[l], w_in, cos_p, sin_p, SEQ)
            seq3 = lambda t: t.reshape(BATCH, SEQ, t.shape[-1])
            ret, sp = _retention(seq3(qa), seq3(ka), seq3(va), seq3(ga),
                                 jnp.zeros((BATCH, H_A, DK_A, DV_A), F32), CHUNK_RET, CHUNK_RET)
            dif = _diff_attn_prompt(seq3(qb), seq3(kb16), seq3(vb16), lam_vecs, g_diff[e], lam_init)
            xp = _out_proj([ret.reshape(rows_p, -1), dif.reshape(rows_p, -1)], w_parts, xp, mp[2],
                           g_post_mix[l], SEQ)
            k_p.append(kb32.reshape(BATCH, SEQ, H_B, 2 * DH_B))
            v_p.append(vb32.reshape(BATCH, SEQ, H_B, DV_B))
            ret_p.append(sp)

            qa, ka, va, ga, qb, kb32, kb16, vb32, vb16 = _in_proj_ab(
                xs, ms[0], ms[1], g_pre_mix[l], w_in, cos_s, sin_s, DEC_SEQ)
            pad = lambda t: _pad_seq(t, DEC_BATCH, DEC_SEQ)
            ret, ss = _retention(pad(qa), pad(ka), pad(va), pad(ga), state_ret[e], SAMPLE_PAD, DEC_SEQ)
            n_phys = cache_k.shape[1]
            kv_rows = lambda t: t.reshape(DEC_BATCH, DEC_SEQ * H_B, DV_B)
            dif = _diff_attn_decode(
                _decode_q_rows(qb.reshape(DEC_BATCH, DEC_SEQ, 512)), kv_rows(kb16), kv_rows(vb16),
                cache_k.reshape(N_EVEN * n_phys, KV_ROWS, LANES), cache_v.reshape(N_EVEN * n_phys, KV_ROWS, LANES),
                e * n_phys, page_table, lam_vecs, g_diff[e], lam_init)
            xs = _out_proj([_unpad_seq(ret, DEC_SEQ), _unpad_seq(dif, DEC_SEQ)], w_parts, xs, ms[2],
                           g_post_mix[l], DEC_SEQ)
            k_s.append(kb32.reshape(DEC_BATCH, DEC_SEQ, H_B, 2 * DH_B))
            v_s.append(vb32.reshape(DEC_BATCH, DEC_SEQ, H_B, DV_B))
            ret_s.append(ss)
        else:
            o = l // 2
            w_in = jnp.pad(w_in_c[o], ((0, 0), (0, LANES - GATE_RANK))).astype(BF16)
            w_gate = jnp.pad(w_gate_c[o], ((0, LANES - GATE_RANK), (0, 0))).astype(BF16)
            w_out = w_out_c[o].astype(BF16)

            q, k, v, r, lg = _in_proj_c(xp, mp[0], mp[1], g_pre_mix[l], w_in, w_gate, b_gate_c[o], SEQ)
            seq3 = lambda t: t.reshape(BATCH, SEQ, t.shape[-1])
            og, sp = _gla(seq3(q), seq3(k), seq3(lg), seq3(v), seq3(r),
                          jnp.zeros((BATCH, H_C, DK_C, DV_C), F32), g_gla[o], CHUNK_GLA, CHUNK_GLA)
            xp = _out_proj([og.reshape(rows_p, -1)], [w_out], xp, mp[2], g_post_mix[l], SEQ)
            gla_p.append(sp)

            q, k, v, r, lg = _in_proj_c(xs, ms[0], ms[1], g_pre_mix[l], w_in, w_gate, b_gate_c[o], DEC_SEQ)
            pad = lambda t: _pad_seq(t, DEC_BATCH, DEC_SEQ)
            og, ss = _gla(pad(q), pad(k), pad(lg), pad(v), pad(r), state_gla[o], g_gla[o], SAMPLE_PAD, DEC_SEQ)
            xs = _out_proj([_unpad_seq(og, DEC_SEQ)], [w_out], xs, ms[2], g_post_mix[l], DEC_SEQ)
            gla_s.append(ss)

        wup = w_up[l].astype(BF16)
        wdn = w_down[l].astype(BF16)
        xp, cbp = _conv_ffn(xp, mp[3], mp[4], mp[5], g_pre_ffn[l], g_post_ffn[l], wup, conv_w[l], conv_b[l],
                            wdn, SEQ)
        xs, up_s = _conv_ffn(xs, ms[3], ms[4], ms[5], g_pre_ffn[l], g_post_ffn[l], wup, conv_w[l], conv_b[l],
                             wdn, DEC_SEQ, state=state_conv[l])
        conv_p.append(cbp)
        conv_s.append(up_s.reshape(DEC_BATCH, DEC_SEQ, 2 * D_FF)[:, DEC_SEQ - (CONV_W - 1):])

    return (xp.reshape(BATCH, SEQ, D_MODEL), xs.reshape(DEC_BATCH, DEC_SEQ, D_MODEL),
            jnp.stack(k_p), jnp.stack(v_p), jnp.stack(k_s), jnp.stack(v_s),
            jnp.stack(ret_p), jnp.stack(ret_s), jnp.stack(gla_p), jnp.stack(gla_s),
            jnp.stack(conv_p), jnp.stack(conv_s))
```

```python
import functools
import math

import numpy as np
import jax
import jax.numpy as jnp
from jax import lax
from jax.experimental import pallas as pl
from jax.experimental.pallas import tpu as pltpu

D_MODEL = 1024
BATCH = 4
SEQ = 4096
DEPTH = 2
DEC_BATCH = 32
DEC_SEQ = 4
PAST_LEN = 8192
PAGE_SIZE = 128
N_EVEN = (DEPTH + 1) // 2
N_ODD = DEPTH // 2
H_A = 4
DK_A = D_MODEL // 16
DV_A = D_MODEL // 8
CHUNK_RET = 128
H_B = 4
DH_B = D_MODEL // 16
DV_B = 2 * DH_B
H_C = 4
DK_C = D_MODEL // 8
DV_C = D_MODEL // 4
GATE_RANK = 16
GATE_TAU = 16.0
CHUNK_GLA = 64
D_FF = ((8 * D_MODEL // 3 + 127) // 128) * 128
CONV_W = 3
ROPE_THETA = 10000.0
EPS = 1e-6

LANES = 128
VMEM_LIMIT = 56 * 1024 * 1024
ROW_TILE = 512
FFN_ROW_TILE = 512
SAMPLE_PAD = 16
FF_CHUNK = 256
PAGES_PER_STEP = 16
LOG2E = math.log2(math.e)
BF16 = jnp.bfloat16
F32 = jnp.float32


def _cparams(*sem):
    return pltpu.CompilerParams(dimension_semantics=sem, vmem_limit_bytes=VMEM_LIMIT)


def _dot(a, b):
    return jnp.dot(a, b, preferred_element_type=F32)


def _dot_nt(a, b):
    return lax.dot_general(a, b, (((1,), (1,)), ((), ())), preferred_element_type=F32)


def _dot_tn(a, b):
    return lax.dot_general(a, b, (((0,), (0,)), ((), ())), preferred_element_type=F32)


def _rms(x):
    return x * lax.rsqrt(jnp.mean(x * x, axis=-1, keepdims=True) + EPS)


def _modulate(x, g, shift, scale):
    return (_rms(x) * g) * (1.0 + scale) + shift


def _rope(x, cos, sin_signed):
    w = x.shape[-1]
    fwd = pltpu.roll(x, 32, axis=1)
    bwd = pltpu.roll(x, w - 32, axis=1)
    reps = w // LANES
    first_half = (lax.broadcasted_iota(jnp.int32, x.shape, 1) % 64) < 32
    partner = jnp.where(first_half, bwd, fwd)
    return (x * jnp.concatenate([cos] * reps, axis=1)
            + partner * jnp.concatenate([sin_signed] * reps, axis=1))


def _ada_kernel(c_ref, w_ref, b_ref, o_ref):
    c = c_ref[...]
    s = c * jax.nn.sigmoid(c)
    o_ref[0] = jnp.dot(s, w_ref[0], preferred_element_type=F32, precision=lax.Precision.HIGHEST) + b_ref[0]


def _ada(c_all, w_ada, b_ada):
    rows = c_all.shape[0]
    tn = 1536
    return pl.pallas_call(
        _ada_kernel,
        grid=(DEPTH, 6 * D_MODEL // tn),
        in_specs=[pl.BlockSpec((rows, D_MODEL), lambda l, j: (0, 0)),
                  pl.BlockSpec((1, D_MODEL, tn), lambda l, j: (l, 0, j)),
                  pl.BlockSpec((1, 1, tn), lambda l, j: (l, 0, j))],
        out_specs=pl.BlockSpec((1, rows, tn), lambda l, j: (l, 0, j)),
        out_shape=jax.ShapeDtypeStruct((DEPTH, rows, 6 * D_MODEL), F32),
        compiler_params=_cparams("arbitrary", "arbitrary"),
        name="adaln",
    )(c_all, w_ada, b_ada.reshape(DEPTH, 1, 6 * D_MODEL))


def _row_cfg(rows, seq_len, row_tile=ROW_TILE):
    tm = min(row_tile, rows)
    per_token = seq_len < tm
    tiles_per_seq = 1 if per_token else seq_len // tm
    mod_spec = (pl.BlockSpec((1, tm, D_MODEL), lambda i: (i, 0, 0)) if per_token
                else pl.BlockSpec((1, 1, D_MODEL), lambda i: (i // tiles_per_seq, 0, 0)))
    return tm, per_token, tiles_per_seq, mod_spec


def _mod_arg(m, rows, seq_len, tm, per_token):
    if per_token:
        return jnp.repeat(m, seq_len, axis=0).reshape(rows // tm, tm, D_MODEL)
    return m.reshape(m.shape[0], 1, D_MODEL)


def _inab_kernel(x_ref, sh_ref, sc_ref, g_ref, w_ref, cos_ref, sin_ref,
                 qa_ref, ka_ref, va_ref, ga_ref, qb_ref, kb32_ref, kb16_ref, vb32_ref, vb16_ref):
    h = _modulate(x_ref[...], g_ref[...], sh_ref[0], sc_ref[0]).astype(BF16)
    cos = cos_ref[...]
    sin = sin_ref[...]

    def proj(lo, width):
        return _dot(h, w_ref[:, lo:lo + width])

    qa_ref[...] = _rope(proj(0, 256), cos, sin).astype(BF16)
    ka_ref[...] = (_rope(proj(256, 256), cos, sin) * (DK_A ** -0.5)).astype(BF16)
    va_ref[...] = proj(512, 512).astype(BF16)
    ga_ref[...] = proj(1024, 512).astype(BF16)
    qb_ref[...] = (_rope(proj(1536, 512), cos, sin) * (DH_B ** -0.5 * LOG2E)).astype(BF16)
    kb = _rope(proj(2048, 512), cos, sin)
    kb16_ref[...] = kb.astype(BF16)
    vb = proj(2560, 512)
    vb16_ref[...] = vb.astype(BF16)
    tm = kb.shape[0]
    for h in range(H_B):
        kb32_ref[pl.ds(h, tm, stride=H_B), :] = kb[:, h * DV_B:(h + 1) * DV_B]
        vb32_ref[pl.ds(h, tm, stride=H_B), :] = vb[:, h * DV_B:(h + 1) * DV_B]


def _in_proj_ab(x, shift, scale, g_pre, w_bf, cos, sin, seq_len):
    rows = x.shape[0]
    tm, per_token, tps, mod_spec = _row_cfg(rows, seq_len)
    pos_map = (lambda i: (i, 0)) if per_token else (lambda i: (i % tps, 0))
    row = lambda w: pl.BlockSpec((tm, w), lambda i: (i, 0))
    widths = (256, 256, 512, 512, 512, 512, 512, 512, 512)
    dtypes = (BF16, BF16, BF16, BF16, BF16, F32, BF16, F32, BF16)
    cache_rows = lambda w, d: d == F32
    out_spec = lambda w, d: (pl.BlockSpec((tm * H_B, DV_B), lambda i: (i, 0)) if cache_rows(w, d) else row(w))
    out_shape = lambda w, d: jax.ShapeDtypeStruct((rows * H_B, DV_B) if cache_rows(w, d) else (rows, w), d)
    return pl.pallas_call(
        _inab_kernel,
        grid=(rows // tm,),
        in_specs=[row(D_MODEL), mod_spec, mod_spec,
                  pl.BlockSpec((1, D_MODEL), lambda i: (0, 0)),
                  pl.BlockSpec(w_bf.shape, lambda i: (0, 0)),
                  pl.BlockSpec((tm, LANES), pos_map), pl.BlockSpec((tm, LANES), pos_map)],
        out_specs=[out_spec(w, d) for w, d in zip(widths, dtypes)],
        out_shape=[out_shape(w, d) for w, d in zip(widths, dtypes)],
        compiler_params=_cparams("arbitrary"),
        name="in_proj_ab",
    )(x, _mod_arg(shift, rows, seq_len, tm, per_token), _mod_arg(scale, rows, seq_len, tm, per_token),
      g_pre.reshape(1, D_MODEL), w_bf, cos, sin)


def _inc_kernel(x_ref, sh_ref, sc_ref, g_ref, w_ref, wg_ref, bg_ref,
                q_ref, k_ref, v_ref, r_ref, lg_ref):
    h = _modulate(x_ref[...], g_ref[...], sh_ref[0], sc_ref[0]).astype(BF16)

    def proj(lo, width):
        return _dot(h, w_ref[:, lo:lo + width])

    q_ref[...] = (proj(0, 512) * (DK_C ** -0.5)).astype(BF16)
    k_ref[...] = proj(512, 512).astype(BF16)
    v_ref[...] = proj(1024, 1024).astype(BF16)
    r_ref[...] = proj(2048, 1024).astype(BF16)
    a = proj(3072, LANES)
    z = _dot(a.astype(BF16), wg_ref[...]) + bg_ref[...]
    softplus_neg = jnp.maximum(-z, 0.0) + jnp.log1p(jnp.exp(-jnp.abs(z)))
    lg_ref[...] = -softplus_neg / GATE_TAU


def _in_proj_c(x, shift, scale, g_pre, w_bf, wg_bf, b_gate, seq_len):
    rows = x.shape[0]
    tm, per_token, tps, mod_spec = _row_cfg(rows, seq_len)
    row = lambda w: pl.BlockSpec((tm, w), lambda i: (i, 0))
    widths = (512, 512, 1024, 1024, 512)
    dtypes = (BF16, BF16, BF16, BF16, F32)
    full = lambda a: pl.BlockSpec(a.shape, lambda i: (0, 0))
    bg = b_gate.reshape(1, -1)
    return pl.pallas_call(
        _inc_kernel,
        grid=(rows // tm,),
        in_specs=[row(D_MODEL), mod_spec, mod_spec, pl.BlockSpec((1, D_MODEL), lambda i: (0, 0)),
                  full(w_bf), full(wg_bf), full(bg)],
        out_specs=[row(w) for w in widths],
        out_shape=[jax.ShapeDtypeStruct((rows, w), d) for w, d in zip(widths, dtypes)],
        compiler_params=_cparams("arbitrary"),
        name="in_proj_c",
    )(x, _mod_arg(shift, rows, seq_len, tm, per_token), _mod_arg(scale, rows, seq_len, tm, per_token),
      g_pre.reshape(1, D_MODEL), w_bf, wg_bf, bg)


def _outproj_kernel(n_parts, *refs):
    parts = refs[:n_parts]
    ws = refs[n_parts:2 * n_parts]
    x_ref, gate_ref, g_ref, o_ref = refs[2 * n_parts:]
    y = _dot(parts[0][...], ws[0][...])
    for p, w in zip(parts[1:], ws[1:]):
        y = y + _dot(p[...], w[...])
    o_ref[...] = x_ref[...] + gate_ref[0] * (_rms(y) * g_ref[...])


def _out_proj(parts, weights, x, gate, g_post, seq_len):
    rows = x.shape[0]
    tm, per_token, tps, mod_spec = _row_cfg(rows, seq_len)
    n = len(parts)
    return pl.pallas_call(
        functools.partial(_outproj_kernel, n),
        grid=(rows // tm,),
        in_specs=([pl.BlockSpec((tm, p.shape[1]), lambda i: (i, 0)) for p in parts]
                  + [pl.BlockSpec(w.shape, lambda i: (0, 0)) for w in weights]
                  + [pl.BlockSpec((tm, D_MODEL), lambda i: (i, 0)), mod_spec,
                     pl.BlockSpec((1, D_MODEL), lambda i: (0, 0))]),
        out_specs=pl.BlockSpec((tm, D_MODEL), lambda i: (i, 0)),
        out_shape=jax.ShapeDtypeStruct((rows, D_MODEL), F32),
        compiler_params=_cparams("arbitrary"),
        name="out_proj",
    )(*parts, *weights, x, _mod_arg(gate, rows, seq_len, tm, per_token), g_post.reshape(1, D_MODEL))


def _ffn_kernel(per_token, seq_len, tiles_per_seq, *refs):
    if per_token:
        (x_ref, sh_ref, sc_ref, gate_ref, gpre_ref, gpost_ref, wup_ref, cw_ref, cb_ref, wdn_ref,
         s1_ref, s2_ref, xo_ref, conv_ref, h_ref, acc_ref, *u_refs) = refs
    else:
        (x_ref, sh_ref, sc_ref, gate_ref, gpre_ref, gpost_ref, wup_ref, cw_ref, cb_ref, wdn_ref,
         xo_ref, conv_ref, h_ref, acc_ref, *u_refs) = refs
    tm = x_ref.shape[0]
    halo = h_ref.shape[0] - tm
    x = x_ref[...]
    if per_token:
        tau = lax.broadcasted_iota(jnp.int32, (tm, FF_CHUNK), 0) % seq_len
    else:
        seq_start = (pl.program_id(0) % tiles_per_seq) == 0

        @pl.when(seq_start)
        def _():
            h_ref[:halo] = jnp.zeros((halo, D_MODEL), BF16)

        @pl.when(jnp.logical_not(seq_start))
        def _():
            h_ref[:halo] = h_ref[tm:]
    h_ref[halo:] = _modulate(x, gpre_ref[...], sh_ref[0], sc_ref[0]).astype(BF16)
    acc_ref[...] = jnp.zeros_like(acc_ref)

    def up(lo):
        return _dot(h_ref[...], wup_ref[:, pl.ds(lo, FF_CHUNK)])

    def conv(slot, half, lo):
        cols = pl.ds(lo, FF_CHUNK)
        cw = cw_ref[:, cols]
        cb = cb_ref[:, cols]
        taps = lambda u2, u1, u0: cb + cw[0:1] * u2 + cw[1:2] * u1 + cw[2:3] * u0
        if per_token:
            u = u_refs[slot][half]
            u1 = jnp.where(tau == 0, s1_ref[:, cols], pltpu.roll(u, 1, axis=0))
            u2 = jnp.where(tau < 2, s2_ref[:, cols], pltpu.roll(u, 2, axis=0))
            conv_ref[:, cols] = u
            return taps(u2, u1, u)
        conv_ref[0, :, cols] = u_refs[slot][half, halo + tm - 2:]
        window = lambda back: u_refs[slot][half, halo - back:halo - back + tm]
        return taps(window(2), window(1), window(0))

    def produce(slot, lo):
        u_refs[slot][0] = up(lo)
        u_refs[slot][1] = up(lo + D_FF)

    def consume(slot, lo):
        g = (jax.nn.gelu(conv(slot, 0, lo)) * conv(slot, 1, lo + D_FF)).astype(BF16)
        acc_ref[...] += _dot(g, wdn_ref[pl.ds(lo, FF_CHUNK), :])

    def body(j, carry):
        lo = pl.multiple_of(j * (2 * FF_CHUNK), 2 * FF_CHUNK)
        produce(1, lo + FF_CHUNK)
        consume(0, lo)
        produce(0, lo + 2 * FF_CHUNK)
        consume(1, lo + FF_CHUNK)
        return carry

    n_chunks = D_FF // FF_CHUNK
    produce(0, 0)
    lax.fori_loop(0, n_chunks // 2, body, 0)
    consume(0, (n_chunks - 1) * FF_CHUNK)
    xo_ref[...] = x + gate_ref[0] * (_rms(acc_ref[...]) * gpost_ref[...])


def _conv_ffn(x, shift, scale, gate, g_pre, g_post, wup_bf, conv_w, conv_b, wdn_bf, seq_len, state=None):
    rows = x.shape[0]
    tm, per_token, tps, mod_spec = _row_cfg(rows, seq_len, FFN_ROW_TILE)
    full = lambda a: pl.BlockSpec(a.shape, lambda i: (0, 0), pipeline_mode=pl.Buffered(1))
    cb = conv_b.reshape(1, 2 * D_FF)
    in_specs = [pl.BlockSpec((tm, D_MODEL), lambda i: (i, 0)), mod_spec, mod_spec, mod_spec,
                pl.BlockSpec((1, D_MODEL), lambda i: (0, 0)), pl.BlockSpec((1, D_MODEL), lambda i: (0, 0)),
                full(wup_bf), full(conv_w), full(cb), full(wdn_bf)]
    args = [x] + [_mod_arg(m, rows, seq_len, tm, per_token) for m in (shift, scale, gate)] + [
        g_pre.reshape(1, D_MODEL), g_post.reshape(1, D_MODEL), wup_bf, conv_w, cb, wdn_bf]
    halo = 0 if per_token else SAMPLE_PAD
    scratch = [pltpu.VMEM((halo + tm, D_MODEL), BF16), pltpu.VMEM((tm, D_MODEL), F32),
               pltpu.VMEM((2, halo + tm, FF_CHUNK), F32), pltpu.VMEM((2, halo + tm, FF_CHUNK), F32)]
    if per_token:
        batch = rows // seq_len
        zeros = jnp.zeros((batch, seq_len - 1, 2 * D_FF), F32)
        s1 = jnp.concatenate([state[:, 1:2], zeros], axis=1).reshape(rows, 2 * D_FF)
        s2 = jnp.concatenate([state, zeros[:, 1:]], axis=1).reshape(rows, 2 * D_FF)
        in_specs += [pl.BlockSpec((tm, 2 * D_FF), lambda i: (i, 0))] * 2
        args += [s1, s2]
        conv_spec = pl.BlockSpec((tm, 2 * D_FF), lambda i: (i, 0))
        conv_shape = jax.ShapeDtypeStruct((rows, 2 * D_FF), F32)
    else:
        conv_spec = pl.BlockSpec((1, 2, 2 * D_FF), lambda i: (i // tps, 0, 0))
        conv_shape = jax.ShapeDtypeStruct((rows // seq_len, 2, 2 * D_FF), F32)
    return pl.pallas_call(
        functools.partial(_ffn_kernel, per_token, seq_len, tps),
        grid=(rows // tm,),
        in_specs=in_specs,
        out_specs=[pl.BlockSpec((tm, D_MODEL), lambda i: (i, 0)), conv_spec],
        out_shape=[jax.ShapeDtypeStruct((rows, D_MODEL), F32), conv_shape],
        scratch_shapes=scratch,
        compiler_params=_cparams("arbitrary"),
        name="conv_ffn",
    )(*args)


def _ret_tables(chunk, valid):
    log_gamma = jnp.log1p(-jnp.exp2(-5.0 - jnp.arange(H_A, dtype=F32)))
    idx = jnp.arange(chunk, dtype=F32)
    rel = idx[:, None] - idx[None, :]
    intra = jnp.exp(jnp.where(rel[None] >= 0, rel[None] * log_gamma[:, None, None], -jnp.inf))
    dq = jnp.exp((idx + 1.0)[None, :] * log_gamma[:, None])
    dk = jnp.where(idx[None, :] < valid, jnp.exp((valid - 1.0 - idx)[None, :] * log_gamma[:, None]), 0.0)
    dc = jnp.exp(valid * log_gamma)
    rep = lambda t: jnp.broadcast_to(t[..., None], t.shape + (LANES,))
    dc_rows = rep(jnp.repeat(dc, DK_A).reshape(H_A // 2, 2 * DK_A))
    return intra, rep(dq), rep(dk), dc_rows


def _ret_kernel(chunk, n_chunks, q_ref, k_ref, v_ref, ga_ref, s0_ref, intra_ref, dq_ref, dk_ref, dc_ref,
                o_ref, s_ref, st_ref):
    i = pl.program_id(1)

    @pl.when(i == 0)
    def _():
        st_ref[...] = s0_ref[0]

    lane = lax.broadcasted_iota(jnp.int32, (chunk, LANES), 1)
    for c in range(n_chunks):
        rows = slice(c * chunk, (c + 1) * chunk)
        for pair in range(H_A // 2):
            lanes = slice(pair * LANES, (pair + 1) * LANES)
            qc = q_ref[0, rows, lanes]
            kc = k_ref[0, rows, lanes]
            s_pair = st_ref[pair]
            s_bf = s_pair.astype(BF16)
            upd = s_pair * dc_ref[pair]
            for side in range(2):
                hd = 2 * pair + side
                mine = (lane >= 64) if side else (lane < 64)
                hl = slice(hd * DV_A, (hd + 1) * DV_A)
                qh = jnp.where(mine, qc, jnp.zeros_like(qc))
                vh = v_ref[0, rows, hl]
                sc = _dot_nt(qh, kc) * intra_ref[hd]
                o = _dot(sc.astype(BF16), vh) + _dot(qh, s_bf) * dq_ref[hd]
                kh = jnp.where(mine, kc.astype(F32) * dk_ref[hd], 0.0).astype(BF16)
                upd = upd + _dot_tn(kh, vh)
                gate = ga_ref[0, rows, hl].astype(F32)
                o_ref[0, rows, hl] = (_rms(o) * (gate * jax.nn.sigmoid(gate))).astype(BF16)
            st_ref[pair] = upd

    @pl.when(i == pl.num_programs(1) - 1)
    def _():
        s_ref[0] = st_ref[...]


def _retention(q, k, v, ga, s0, chunk, valid):
    b, l, _ = q.shape
    blk = min(l, 4 * chunk)
    n_chunks = blk // chunk
    tables = _ret_tables(chunk, valid)
    seq = lambda w: pl.BlockSpec((1, blk, w), lambda bi, i: (bi, i, 0))
    const = lambda a: pl.BlockSpec(a.shape, lambda bi, i: (0,) * a.ndim)
    state = pl.BlockSpec((1, H_A // 2, 2 * DK_A, DV_A), lambda bi, i: (bi, 0, 0, 0))
    o, s = pl.pallas_call(
        functools.partial(_ret_kernel, chunk, n_chunks),
        grid=(b, l // blk),
        in_specs=[seq(256), seq(256), seq(512), seq(512), state] + [const(t) for t in tables],
        out_specs=[seq(512), state],
        out_shape=[jax.ShapeDtypeStruct((b, l, H_A * DV_A), BF16),
                   jax.ShapeDtypeStruct((b, H_A // 2, 2 * DK_A, DV_A), F32)],
        scratch_shapes=[pltpu.VMEM((H_A // 2, 2 * DK_A, DV_A), F32)],
        compiler_params=_cparams("arbitrary", "arbitrary"),
        name="retention",
    )(q, k, v, ga, s0.reshape(b, H_A // 2, 2 * DK_A, DV_A), *tables)
    return o, s.reshape(b, H_A, DK_A, DV_A)


def _lam_value(lam_ref, lam_init):
    lv = lam_ref[...]
    return (jnp.exp(jnp.sum(lv[0:1] * lv[1:2], axis=-1, keepdims=True))
            - jnp.exp(jnp.sum(lv[2:3] * lv[3:4], axis=-1, keepdims=True)) + lam_init)


def _dattn_kernel(lam_init, tq, q_ref, k_ref, v_ref, lam_ref, g_ref, o_ref, vt_ref, st_ref, m_ref, l_ref, acc_ref):
    qi = pl.program_id(2)

    @pl.when(qi == 0)
    def _():
        vt_ref[...] = v_ref[0].astype(F32).T.astype(BF16)

    q = q_ref[0]
    lane = lax.broadcasted_iota(jnp.int32, q.shape, 1)
    q2 = jnp.concatenate([jnp.where(lane < DH_B, q, jnp.zeros_like(q)),
                          jnp.where(lane >= DH_B, q, jnp.zeros_like(q))], axis=0)
    m_ref[...] = jnp.full_like(m_ref, -jnp.inf)
    l_ref[...] = jnp.zeros_like(l_ref)
    acc_ref[...] = jnp.zeros_like(acc_ref)

    def scores(slot, kb, masked):
        start = pl.multiple_of(kb * tq, tq)
        st = _dot_nt(k_ref[0, pl.ds(start, tq), :], q2)
        if masked:
            key = lax.broadcasted_iota(jnp.int32, st.shape, 0)
            qry = jnp.bitwise_and(lax.broadcasted_iota(jnp.int32, st.shape, 1), tq - 1)
            st = jnp.where(key <= qry, st, -jnp.inf)
        st_ref[slot] = st

    def absorb(slot, kb):
        start = pl.multiple_of(kb * tq, tq)
        st = st_ref[slot]
        m_old = m_ref[...]
        m_new = jnp.maximum(m_old, jnp.max(st, axis=0, keepdims=True))
        alpha = jnp.exp2(m_old - m_new)
        pt = jnp.exp2(st - m_new)
        l_ref[...] = alpha * l_ref[...] + jnp.sum(pt, axis=0, keepdims=True)
        acc_ref[...] = alpha * acc_ref[...] + _dot(vt_ref[:, pl.ds(start, tq)], pt.astype(BF16))
        m_ref[...] = m_new

    pairs = lax.shift_right_logical(qi, 1)
    odd = jnp.bitwise_and(qi, 1) == 1
    scores(0, qi, True)

    def body(j, carry):
        scores(1, 2 * j, False)
        absorb(0, jnp.where(j == 0, qi, 2 * j - 1))
        scores(0, 2 * j + 1, False)
        absorb(1, 2 * j)
        return carry

    lax.fori_loop(0, pairs, body, 0)

    @pl.when(odd)
    def _():
        scores(1, qi - 1, False)

    absorb(0, jnp.where(pairs == 0, qi, 2 * pairs - 1))

    @pl.when(odd)
    def _():
        absorb(1, qi - 1)

    lam = _lam_value(lam_ref, lam_init)
    inv_l = 1.0 / l_ref[...]
    acc = acc_ref[...] * inv_l
    o = (acc[:, :tq] - lam * acc[:, tq:]).T
    o_ref[0] = ((_rms(o) * g_ref[...]) * (1.0 - lam_init)).astype(BF16)


def _diff_attn_prompt(qb, kb, vb, lam_vecs, g_diff, lam_init):
    b, l, _ = qb.shape
    tq = 512
    whole = pl.BlockSpec((1, l, LANES), lambda bi, h, qi: (bi, 0, h))
    return pl.pallas_call(
        functools.partial(_dattn_kernel, lam_init, tq),
        grid=(b, H_B, l // tq),
        in_specs=[pl.BlockSpec((1, tq, LANES), lambda bi, h, qi: (bi, qi, h)), whole, whole,
                  pl.BlockSpec(lam_vecs.shape, lambda bi, h, qi: (0, 0)),
                  pl.BlockSpec((1, DV_B), lambda bi, h, qi: (0, 0))],
        out_specs=pl.BlockSpec((1, tq, LANES), lambda bi, h, qi: (bi, qi, h)),
        out_shape=jax.ShapeDtypeStruct((b, l, H_B * DV_B), BF16),
        scratch_shapes=[pltpu.VMEM((DV_B, l), BF16), pltpu.VMEM((2, tq, 2 * tq), F32),
                        pltpu.VMEM((1, 2 * tq), F32), pltpu.VMEM((1, 2 * tq), F32),
                        pltpu.VMEM((DV_B, 2 * tq), F32)],
        compiler_params=_cparams("arbitrary", "arbitrary", "arbitrary"),
        name="diff_attn_prompt",
    )(qb, kb, vb, lam_vecs, g_diff.reshape(1, DV_B))


Q_ROWS = H_B * 2 * 8
KV_ROWS = PAGE_SIZE * H_B


def _dattn_dec_kernel(lam_init, n_pages, pt_ref, q_ref, kn_ref, vn_ref, *refs):
    k_refs = refs[:n_pages]
    v_refs = refs[n_pages:2 * n_pages]
    lam_ref, g_ref, o_ref, m_ref, l_ref, acc_ref = refs[2 * n_pages:]
    step = pl.program_id(1)
    q = q_ref[0]

    @pl.when(step == 0)
    def _():
        s = _dot_nt(q, kn_ref[0])
        r = lax.broadcasted_iota(jnp.int32, s.shape, 0)
        c = lax.broadcasted_iota(jnp.int32, s.shape, 1)
        s = jnp.where((c % H_B == r // 16) & (c // H_B <= r % 8), s, -jnp.inf)
        m = jnp.max(s, axis=-1, keepdims=True)
        p = jnp.exp2(s - m)
        m_ref[...] = m
        l_ref[...] = jnp.sum(p, axis=-1, keepdims=True)
        acc_ref[...] = _dot(p.astype(BF16), vn_ref[0])

    own_head = (lax.broadcasted_iota(jnp.int32, (Q_ROWS, KV_ROWS), 1) % H_B
                == lax.broadcasted_iota(jnp.int32, (Q_ROWS, KV_ROWS), 0) // 16)
    bias = jnp.where(own_head, 0.0, -jnp.inf)
    scores = [_dot_nt(q, kr[0].astype(BF16)) + bias for kr in k_refs]
    m_old = m_ref[...]
    m_new = m_old
    for s in scores:
        m_new = jnp.maximum(m_new, jnp.max(s, axis=-1, keepdims=True))
    alpha = jnp.exp2(m_old - m_new)
    l_new = alpha * l_ref[...]
    acc = alpha * acc_ref[...]
    for s, vr in zip(scores, v_refs):
        p = jnp.exp2(s - m_new)
        l_new = l_new + jnp.sum(p, axis=-1, keepdims=True)
        acc = acc + _dot(p.astype(BF16), vr[0].astype(BF16))
    m_ref[...] = m_new
    l_ref[...] = l_new
    acc_ref[...] = acc

    @pl.when(step == pl.num_programs(1) - 1)
    def _():
        lam = _lam_value(lam_ref, lam_init)
        o = acc / l_new
        for h in range(H_B):
            d = o[16 * h:16 * h + 8] - lam * o[16 * h + 8:16 * h + 16]
            o_ref[0, :, h * DV_B:(h + 1) * DV_B] = ((_rms(d) * g_ref[...]) * (1.0 - lam_init)).astype(BF16)


def _diff_attn_decode(q_rows, k_new, v_new, cache_k, cache_v, page_base, page_table, lam_vecs, g_diff, lam_init):
    b = q_rows.shape[0]
    n_steps = page_table.shape[1] // PAGES_PER_STEP
    per_b = lambda r, w: pl.BlockSpec((1, r, w), lambda bi, s, pt: (bi, 0, 0))

    def page_spec(j):
        return pl.BlockSpec((1, KV_ROWS, LANES),
                            lambda bi, s, pt: (page_base + pt[bi, s * PAGES_PER_STEP + j], 0, 0))

    pages = [page_spec(j) for j in range(PAGES_PER_STEP)]
    grid_spec = pltpu.PrefetchScalarGridSpec(
        num_scalar_prefetch=1,
        grid=(b, n_steps),
        in_specs=[per_b(Q_ROWS, LANES), per_b(DEC_SEQ * H_B, LANES), per_b(DEC_SEQ * H_B, LANES)]
        + pages + pages + [pl.BlockSpec(lam_vecs.shape, lambda bi, s, pt: (0, 0)),
                           pl.BlockSpec((1, DV_B), lambda bi, s, pt: (0, 0))],
        out_specs=per_b(8, H_B * DV_B),
        scratch_shapes=[pltpu.VMEM((Q_ROWS, 1), F32), pltpu.VMEM((Q_ROWS, 1), F32),
                        pltpu.VMEM((Q_ROWS, DV_B), F32)],
    )
    return pl.pallas_call(
        functools.partial(_dattn_dec_kernel, lam_init, PAGES_PER_STEP),
        grid_spec=grid_spec,
        out_shape=jax.ShapeDtypeStruct((b, 8, H_B * DV_B), BF16),
        compiler_params=_cparams("arbitrary", "arbitrary"),
        name="diff_attn_decode",
    )(page_table, q_rows, k_new, v_new, *([cache_k] * PAGES_PER_STEP), *([cache_v] * PAGES_PER_STEP),
      lam_vecs, g_diff.reshape(1, DV_B))


def _decode_q_rows(qb):
    b = qb.shape[0]
    q = qb.reshape(b, DEC_SEQ, H_B, DV_B).transpose(0, 2, 1, 3)
    q = jnp.pad(q, ((0, 0), (0, 0), (0, 8 - DEC_SEQ), (0, 0)))[:, :, None]
    keep = (np.arange(DV_B)[None, :] // DH_B) == np.arange(2)[:, None]
    keep = jnp.asarray(keep)[None, None, :, None, :]
    return jnp.where(keep, q, jnp.zeros_like(q)).reshape(b, Q_ROWS, DV_B)


def _gla_tables(chunk):
    idx = np.arange(chunk)
    i, t = idx[:, None], idx[None, :]
    mats = [t <= i, t > i]
    masks = [i == t]
    s = 1
    while s < chunk:
        same = (t // s) == (i // s)
        odd = ((i // s) % 2) == 1
        mats.append(same & np.where(odd, t <= i, t > i))
        masks.append(((i // (2 * s)) == (t // (2 * s))) & odd & (((t // s) % 2) == 0))
        s *= 2
    return (jnp.asarray(np.concatenate(mats, axis=0).astype(np.float32), BF16),
            jnp.asarray(np.stack(masks).astype(np.float32)))


def _gla_kernel(chunk, n_chunks, valid, q_ref, k_ref, g_ref, v_ref, r_ref, s0_ref, gn_ref, m_ref, p_ref,
                o_ref, s_ref, st_ref):
    i = pl.program_id(1)

    @pl.when(i == 0)
    def _():
        for h in range(H_C):
            st_ref[h] = s0_ref[0, h].T

    n_levels = p_ref.shape[0] - 1
    m_all = m_ref[...]
    live = lax.broadcasted_iota(jnp.int32, (chunk, 1), 0) < valid
    for c in range(n_chunks):
        rows = slice(c * chunk, (c + 1) * chunk)
        g = g_ref[0, rows, :]
        if valid < chunk:
            g = jnp.where(live, g, 0.0)
        g_hi = g.astype(BF16)
        g_lo = (g - g_hi.astype(F32)).astype(BF16)
        e_all = jnp.exp(_dot(m_all, g_hi) + _dot(m_all, g_lo))
        heads = []
        for h in range(H_C):
            kl = slice(h * DK_C, (h + 1) * DK_C)
            e = e_all[:, kl]
            q = q_ref[0, rows, kl]
            k = k_ref[0, rows, kl]
            v = v_ref[0, rows, h * DV_C:(h + 1) * DV_C]
            if valid < chunk:
                k = jnp.where(live, k, jnp.zeros_like(k))
                v = jnp.where(live, v, jnp.zeros_like(v))
            qf = q.astype(F32)
            kf = k.astype(F32)
            level = lambda lv: e[(2 + lv) * chunk:(3 + lv) * chunk]
            prods = [_dot_nt(q, k)] + [_dot_nt((qf * level(lv)).astype(BF16), (kf * level(lv)).astype(BF16))
                                       for lv in range(n_levels)]
            heads.append(dict(v=v, prods=prods, q_in=(qf * e[0:chunk]).astype(BF16),
                              k_out=(kf * e[chunk:2 * chunk]).astype(BF16), decay=e[chunk - 1:chunk]))
        for hd in heads:
            att = hd["prods"][0] * p_ref[0]
            for lv in range(n_levels):
                att = att + hd["prods"][1 + lv] * p_ref[1 + lv]
            hd["intra"] = _dot(att.astype(BF16), hd["v"])
            hd["update"] = _dot_tn(hd["v"], hd["k_out"])
        for h, hd in enumerate(heads):
            vl = slice(h * DV_C, (h + 1) * DV_C)
            st = st_ref[h]
            o = hd["intra"] + _dot_nt(hd["q_in"], st.astype(BF16))
            st_ref[h] = st * hd["decay"] + hd["update"]
            gate = r_ref[0, rows, vl].astype(F32)
            o_ref[0, rows, vl] = ((_rms(o) * gn_ref[...]) * (gate * jax.nn.sigmoid(gate))).astype(BF16)

    @pl.when(i == pl.num_programs(1) - 1)
    def _():
        for h in range(H_C):
            s_ref[0, h] = st_ref[h].T


def _gla(q, k, g, v, r, s0, g_norm, chunk, valid):
    b, l, _ = q.shape
    blk = min(l, 4 * chunk)
    n_chunks = blk // chunk
    m_all, masks = _gla_tables(chunk)
    qk = pl.BlockSpec((1, blk, H_C * DK_C), lambda bi, i: (bi, i, 0))
    vr = pl.BlockSpec((1, blk, H_C * DV_C), lambda bi, i: (bi, i, 0))
    state = pl.BlockSpec((1, H_C, DK_C, DV_C), lambda bi, i: (bi, 0, 0, 0))
    const = lambda a: pl.BlockSpec(a.shape, lambda bi, i: (0,) * a.ndim)
    gn = g_norm.reshape(1, DV_C)
    return pl.pallas_call(
        functools.partial(_gla_kernel, chunk, n_chunks, valid),
        grid=(b, l // blk),
        in_specs=[qk, qk, qk, vr, vr, state, const(gn), const(m_all), const(masks)],
        out_specs=[vr, state],
        out_shape=[jax.ShapeDtypeStruct((b, l, H_C * DV_C), BF16),
                   jax.ShapeDtypeStruct((b, H_C, DK_C, DV_C), F32)],
        scratch_shapes=[pltpu.VMEM((H_C, DV_C, DK_C), F32)],
        compiler_params=_cparams("arbitrary", "arbitrary"),
        name="gla",
    )(q, k, g, v, r, s0, gn, m_all, masks)


def _rope_tables(pos):
    inv = ROPE_THETA ** (-jnp.arange(0, DK_A, 2, dtype=F32) / DK_A)
    ang = pos.astype(F32)[:, None] * inv[None, :]
    cos = jnp.tile(jnp.cos(ang), (1, 4))
    sin = jnp.tile(jnp.concatenate([-jnp.sin(ang), jnp.sin(ang)], axis=1), (1, 2))
    return cos, sin


def _pad_seq(t, batch, seq_len):
    t = t.reshape(batch, seq_len, t.shape[-1])
    return jnp.pad(t, ((0, 0), (0, SAMPLE_PAD - seq_len), (0, 0)))


def _unpad_seq(t, seq_len):
    return t[:, :seq_len].reshape(t.shape[0] * seq_len, t.shape[-1])


def kernel(x_prompt, x_sample, cache_k, cache_v, state_ret, state_gla, state_conv, page_table, c_prompt, c_sample, w_ada, b_ada, g_pre_mix, g_post_mix, g_pre_ffn, g_post_ffn, w_in_ab, w_out_ab, lam_q1, lam_k1, lam_q2, lam_k2, g_diff, w_in_c, w_gate_c, b_gate_c, g_gla, w_out_c, w_up, conv_w, conv_b, w_down):
    rows_p, rows_s = BATCH * SEQ, DEC_BATCH * DEC_SEQ
    xp = x_prompt.reshape(rows_p, D_MODEL)
    xs = x_sample.reshape(rows_s, D_MODEL)

    n_c = BATCH + DEC_BATCH
    c_all = jnp.pad(jnp.concatenate([c_prompt, c_sample], axis=0), ((0, (-n_c) % 8), (0, 0)))
    ada = _ada(c_all, w_ada, b_ada)

    cos_p, sin_p = _rope_tables(jnp.arange(SEQ))
    cos_s, sin_s = _rope_tables(jnp.tile(PAST_LEN + jnp.arange(DEC_SEQ), DEC_BATCH))

    k_p, v_p, k_s, v_s, ret_p, ret_s, gla_p, gla_s, conv_p, conv_s = ([] for _ in range(10))
    for l in range(DEPTH):
        mp = jnp.split(ada[l, :BATCH], 6, axis=-1)
        ms = jnp.split(ada[l, BATCH:n_c], 6, axis=-1)
        if l % 2 == 0:
            e = l // 2
            lam_init = 0.8 - 0.6 * math.exp(-0.3 * l)
            lam_vecs = jnp.stack([lam_q1[e], lam_k1[e], lam_q2[e], lam_k2[e]]).astype(F32)
            w_in = w_in_ab[e].astype(BF16)
            w_out = w_out_ab[e].astype(BF16)
            w_parts = [w_out[:H_A * DV_A], w_out[H_A * DV_A:]]

            qa, ka, va, ga, qb, kb32, kb16, vb32, vb16 = _in_proj_ab(
                xp, mp[0], mp[1], g_pre_mix[l], w_in, cos_p, sin_p, SEQ)
            seq3 = lambda t: t.reshape(BATCH, SEQ, t.shape[-1])
            ret, sp = _retention(seq3(qa), seq3(ka), seq3(va), seq3(ga),
                                 jnp.zeros((BATCH, H_A, DK_A, DV_A), F32), CHUNK_RET, CHUNK_RET)
            dif = _diff_attn_prompt(seq3(qb), seq3(kb16), seq3(vb16), lam_vecs, g_diff[e], lam_init)
            xp = _out_proj([ret.reshape(rows_p, -1), dif.reshape(rows_p, -1)], w_parts, xp, mp[2],
                           g_post_mix[l], SEQ)
            k_p.append(kb32.reshape(BATCH, SEQ, H_B, 2 * DH_B))
            v_p.append(vb32.reshape(BATCH, SEQ, H_B, DV_B))
            ret_p.append(sp)

            qa, ka, va, ga, qb, kb32, kb16, vb32, vb16 = _in_proj_ab(
                xs, ms[0], ms[1], g_pre_mix[l], w_in, cos_s, sin_s, DEC_SEQ)
            pad = lambda t: _pad_seq(t, DEC_BATCH, DEC_SEQ)
            ret, ss = _retention(pad(qa), pad(ka), pad(va), pad(ga), state_ret[e], SAMPLE_PAD, DEC_SEQ)
            n_phys = cache_k.shape[1]
            kv_rows = lambda t: t.reshape(DEC_BATCH, DEC_SEQ * H_B, DV_B)
            dif = _diff_attn_decode(
                _decode_q_rows(qb.reshape(DEC_BATCH, DEC_SEQ, 512)), kv_rows(kb16), kv_rows(vb16),
                cache_k.reshape(N_EVEN * n_phys, KV_ROWS, LANES), cache_v.reshape(N_EVEN * n_phys, KV_ROWS, LANES),
                e * n_phys, page_table, lam_vecs, g_diff[e], lam_init)
            xs = _out_proj([_unpad_seq(ret, DEC_SEQ), _unpad_seq(dif, DEC_SEQ)], w_parts, xs, ms[2],
                           g_post_mix[l], DEC_SEQ)
            k_s.append(kb32.reshape(DEC_BATCH, DEC_SEQ, H_B, 2 * DH_B))
            v_s.append(vb32.reshape(DEC_BATCH, DEC_SEQ, H_B, DV_B))
            ret_s.append(ss)
        else:
            o = l // 2
            w_in = jnp.pad(w_in_c[o], ((0, 0), (0, LANES - GATE_RANK))).astype(BF16)
            w_gate = jnp.pad(w_gate_c[o], ((0, LANES - GATE_RANK), (0, 0))).astype(BF16)
            w_out = w_out_c[o].astype(BF16)

            q, k, v, r, lg = _in_proj_c(xp, mp[0], mp[1], g_pre_mix[l], w_in, w_gate, b_gate_c[o], SEQ)
            seq3 = lambda t: t.reshape(BATCH, SEQ, t.shape[-1])
            og, sp = _gla(seq3(q), seq3(k), seq3(lg), seq3(v), seq3(r),
                          jnp.zeros((BATCH, H_C, DK_C, DV_C), F32), g_gla[o], CHUNK_GLA, CHUNK_GLA)
            xp = _out_proj([og.reshape(rows_p, -1)], [w_out], xp, mp[2], g_post_mix[l], SEQ)
            gla_p.append(sp)

            q, k, v, r, lg = _in_proj_c(xs, ms[0], ms[1], g_pre_mix[l], w_in, w_gate, b_gate_c[o], DEC_SEQ)
            pad = lambda t: _pad_seq(t, DEC_BATCH, DEC_SEQ)
            og, ss = _gla(pad(q), pad(k), pad(lg), pad(v), pad(r), state_gla[o], g_gla[o], SAMPLE_PAD, DEC_SEQ)
            xs = _out_proj([_unpad_seq(og, DEC_SEQ)], [w_out], xs, ms[2], g_post_mix[l], DEC_SEQ)
            gla_s.append(ss)

        wup = w_up[l].astype(BF16)
        wdn = w_down[l].astype(BF16)
        xp, cbp = _conv_ffn(xp, mp[3], mp[4], mp[5], g_pre_ffn[l], g_post_ffn[l], wup, conv_w[l], conv_b[l],
                            wdn, SEQ)
        xs, up_s = _conv_ffn(xs, ms[3], ms[4], ms[5], g_pre_ffn[l], g_post_ffn[l], wup, conv_w[l], conv_b[l],
                             wdn, DEC_SEQ, state=state_conv[l])
        conv_p.append(cbp)
        conv_s.append(up_s.reshape(DEC_BATCH, DEC_SEQ, 2 * D_FF)[:, DEC_SEQ - (CONV_W - 1):])

    return (xp.reshape(BATCH, SEQ, D_MODEL), xs.reshape(DEC_BATCH, DEC_SEQ, D_MODEL),
            jnp.stack(k_p), jnp.stack(v_p), jnp.stack(k_s), jnp.stack(v_s),
            jnp.stack(ret_p), jnp.stack(ret_s), jnp.stack(gla_p), jnp.stack(gla_s),
            jnp.stack(conv_p), jnp.stack(conv_s))
```

```python
import functools
import math

import numpy as np
import jax
import jax.numpy as jnp
from jax import lax
from jax.experimental import pallas as pl
from jax.experimental.pallas import tpu as pltpu

D_MODEL = 1024
BATCH = 4
SEQ = 4096
DEPTH = 2
DEC_BATCH = 32
DEC_SEQ = 4
PAST_LEN = 8192
PAGE_SIZE = 128
N_EVEN = (DEPTH + 1) // 2
N_ODD = DEPTH // 2
H_A = 4
DK_A = D_MODEL // 16
DV_A = D_MODEL // 8
CHUNK_RET = 128
H_B = 4
DH_B = D_MODEL // 16
DV_B = 2 * DH_B
H_C = 4
DK_C = D_MODEL // 8
DV_C = D_MODEL // 4
GATE_RANK = 16
GATE_TAU = 16.0
CHUNK_GLA = 64
D_FF = ((8 * D_MODEL // 3 + 127) // 128) * 128
CONV_W = 3
ROPE_THETA = 10000.0
EPS = 1e-6

LANES = 128
VMEM_LIMIT = 56 * 1024 * 1024
ROW_TILE = 512
FFN_ROW_TILE = 512
SAMPLE_PAD = 16
FF_CHUNK = 256
PAGES_PER_STEP = 16
LOG2E = math.log2(math.e)
BF16 = jnp.bfloat16
F32 = jnp.float32


def _cparams(*sem):
    return pltpu.CompilerParams(dimension_semantics=sem, vmem_limit_bytes=VMEM_LIMIT)


def _dot(a, b):
    return jnp.dot(a, b, preferred_element_type=F32)


def _dot_nt(a, b):
    return lax.dot_general(a, b, (((1,), (1,)), ((), ())), preferred_element_type=F32)


def _dot_tn(a, b):
    return lax.dot_general(a, b, (((0,), (0,)), ((), ())), preferred_element_type=F32)


def _rms(x):
    return x * lax.rsqrt(jnp.mean(x * x, axis=-1, keepdims=True) + EPS)


def _modulate(x, g, shift, scale):
    return (_rms(x) * g) * (1.0 + scale) + shift


def _rope(x, cos, sin_signed):
    w = x.shape[-1]
    fwd = pltpu.roll(x, 32, axis=1)
    bwd = pltpu.roll(x, w - 32, axis=1)
    reps = w // LANES
    first_half = (lax.broadcasted_iota(jnp.int32, x.shape, 1) % 64) < 32
    partner = jnp.where(first_half, bwd, fwd)
    return (x * jnp.concatenate([cos] * reps, axis=1)
            + partner * jnp.concatenate([sin_signed] * reps, axis=1))


def _ada_kernel(c_ref, w_ref, b_ref, o_ref):
    c = c_ref[...]
    s = c * jax.nn.sigmoid(c)
    o_ref[0] = jnp.dot(s, w_ref[0], preferred_element_type=F32, precision=lax.Precision.HIGHEST) + b_ref[0]


def _ada(c_all, w_ada, b_ada):
    rows = c_all.shape[0]
    tn = 1536
    return pl.pallas_call(
        _ada_kernel,
        grid=(DEPTH, 6 * D_MODEL // tn),
        in_specs=[pl.BlockSpec((rows, D_MODEL), lambda l, j: (0, 0)),
                  pl.BlockSpec((1, D_MODEL, tn), lambda l, j: (l, 0, j)),
                  pl.BlockSpec((1, 1, tn), lambda l, j: (l, 0, j))],
        out_specs=pl.BlockSpec((1, rows, tn), lambda l, j: (l, 0, j)),
        out_shape=jax.ShapeDtypeStruct((DEPTH, rows, 6 * D_MODEL), F32),
        compiler_params=_cparams("arbitrary", "arbitrary"),
        name="adaln",
    )(c_all, w_ada, b_ada.reshape(DEPTH, 1, 6 * D_MODEL))


def _row_cfg(rows, seq_len, row_tile=ROW_TILE):
    tm = min(row_tile, rows)
    per_token = seq_len < tm
    tiles_per_seq = 1 if per_token else seq_len // tm
    mod_spec = (pl.BlockSpec((1, tm, D_MODEL), lambda i: (i, 0, 0)) if per_token
                else pl.BlockSpec((1, 1, D_MODEL), lambda i: (i // tiles_per_seq, 0, 0)))
    return tm, per_token, tiles_per_seq, mod_spec


def _mod_arg(m, rows, seq_len, tm, per_token):
    if per_token:
        return jnp.repeat(m, seq_len, axis=0).reshape(rows // tm, tm, D_MODEL)
    return m.reshape(m.shape[0], 1, D_MODEL)


def _inab_kernel(x_ref, sh_ref, sc_ref, g_ref, w_ref, cos_ref, sin_ref,
                 qa_ref, ka_ref, va_ref, ga_ref, qb_ref, kb32_ref, kb16_ref, vb32_ref, vb16_ref):
    h = _modulate(x_ref[...], g_ref[...], sh_ref[0], sc_ref[0]).astype(BF16)
    cos = cos_ref[...]
    sin = sin_ref[...]

    def proj(lo, width):
        return _dot(h, w_ref[:, lo:lo + width])

    qa_ref[...] = _rope(proj(0, 256), cos, sin).astype(BF16)
    ka_ref[...] = (_rope(proj(256, 256), cos, sin) * (DK_A ** -0.5)).astype(BF16)
    va_ref[...] = proj(512, 512).astype(BF16)
    ga_ref[...] = proj(1024, 512).astype(BF16)
    qb_ref[...] = (_rope(proj(1536, 512), cos, sin) * (DH_B ** -0.5 * LOG2E)).astype(BF16)
    kb = _rope(proj(2048, 512), cos, sin)
    kb16_ref[...] = kb.astype(BF16)
    vb = proj(2560, 512)
    vb16_ref[...] = vb.astype(BF16)
    tm = kb.shape[0]
    for h in range(H_B):
        kb32_ref[pl.ds(h, tm, stride=H_B), :] = kb[:, h * DV_B:(h + 1) * DV_B]
        vb32_ref[pl.ds(h, tm, stride=H_B), :] = vb[:, h * DV_B:(h + 1) * DV_B]


def _in_proj_ab(x, shift, scale, g_pre, w_bf, cos, sin, seq_len):
    rows = x.shape[0]
    tm, per_token, tps, mod_spec = _row_cfg(rows, seq_len)
    pos_map = (lambda i: (i, 0)) if per_token else (lambda i: (i % tps, 0))
    row = lambda w: pl.BlockSpec((tm, w), lambda i: (i, 0))
    widths = (256, 256, 512, 512, 512, 512, 512, 512, 512)
    dtypes = (BF16, BF16, BF16, BF16, BF16, F32, BF16, F32, BF16)
    cache_rows = lambda w, d: d == F32
    out_spec = lambda w, d: (pl.BlockSpec((tm * H_B, DV_B), lambda i: (i, 0)) if cache_rows(w, d) else row(w))
    out_shape = lambda w, d: jax.ShapeDtypeStruct((rows * H_B, DV_B) if cache_rows(w, d) else (rows, w), d)
    return pl.pallas_call(
        _inab_kernel,
        grid=(rows // tm,),
        in_specs=[row(D_MODEL), mod_spec, mod_spec,
                  pl.BlockSpec((1, D_MODEL), lambda i: (0, 0)),
                  pl.BlockSpec(w_bf.shape, lambda i: (0, 0)),
                  pl.BlockSpec((tm, LANES), pos_map), pl.BlockSpec((tm, LANES), pos_map)],
        out_specs=[out_spec(w, d) for w, d in zip(widths, dtypes)],
        out_shape=[out_shape(w, d) for w, d in zip(widths, dtypes)],
        compiler_params=_cparams("arbitrary"),
        name="in_proj_ab",
    )(x, _mod_arg(shift, rows, seq_len, tm, per_token), _mod_arg(scale, rows, seq_len, tm, per_token),
      g_pre.reshape(1, D_MODEL), w_bf, cos, sin)


def _inc_kernel(x_ref, sh_ref, sc_ref, g_ref, w_ref, wg_ref, bg_ref,
                q_ref, k_ref, v_ref, r_ref, lg_ref):
    h = _modulate(x_ref[...], g_ref[...], sh_ref[0], sc_ref[0]).astype(BF16)

    def proj(lo, width):
        return _dot(h, w_ref[:, lo:lo + width])

    q_ref[...] = (proj(0, 512) * (DK_C ** -0.5)).astype(BF16)
    k_ref[...] = proj(512, 512).astype(BF16)
    v_ref[...] = proj(1024, 1024).astype(BF16)
    r_ref[...] = proj(2048, 1024).astype(BF16)
    a = proj(3072, LANES)
    z = _dot(a.astype(BF16), wg_ref[...]) + bg_ref[...]
    softplus_neg = jnp.maximum(-z, 0.0) + jnp.log1p(jnp.exp(-jnp.abs(z)))
    lg_ref[...] = -softplus_neg / GATE_TAU


def _in_proj_c(x, shift, scale, g_pre, w_bf, wg_bf, b_gate, seq_len):
    rows = x.shape[0]
    tm, per_token, tps, mod_spec = _row_cfg(rows, seq_len)
    row = lambda w: pl.BlockSpec((tm, w), lambda i: (i, 0))
    widths = (512, 512, 1024, 1024, 512)
    dtypes = (BF16, BF16, BF16, BF16, F32)
    full = lambda a: pl.BlockSpec(a.shape, lambda i: (0, 0))
    bg = b_gate.reshape(1, -1)
    return pl.pallas_call(
        _inc_kernel,
        grid=(rows // tm,),
        in_specs=[row(D_MODEL), mod_spec, mod_spec, pl.BlockSpec((1, D_MODEL), lambda i: (0, 0)),
                  full(w_bf), full(wg_bf), full(bg)],
        out_specs=[row(w) for w in widths],
        out_shape=[jax.ShapeDtypeStruct((rows, w), d) for w, d in zip(widths, dtypes)],
        compiler_params=_cparams("arbitrary"),
        name="in_proj_c",
    )(x, _mod_arg(shift, rows, seq_len, tm, per_token), _mod_arg(scale, rows, seq_len, tm, per_token),
      g_pre.reshape(1, D_MODEL), w_bf, wg_bf, bg)


def _outproj_kernel(n_parts, *refs):
    parts = refs[:n_parts]
    ws = refs[n_parts:2 * n_parts]
    x_ref, gate_ref, g_ref, o_ref = refs[2 * n_parts:]
    y = _dot(parts[0][...], ws[0][...])
    for p, w in zip(parts[1:], ws[1:]):
        y = y + _dot(p[...], w[...])
    o_ref[...] = x_ref[...] + gate_ref[0] * (_rms(y) * g_ref[...])


def _out_proj(parts, weights, x, gate, g_post, seq_len):
    rows = x.shape[0]
    tm, per_token, tps, mod_spec = _row_cfg(rows, seq_len)
    n = len(parts)
    return pl.pallas_call(
        functools.partial(_outproj_kernel, n),
        grid=(rows // tm,),
        in_specs=([pl.BlockSpec((tm, p.shape[1]), lambda i: (i, 0)) for p in parts]
                  + [pl.BlockSpec(w.shape, lambda i: (0, 0)) for w in weights]
                  + [pl.BlockSpec((tm, D_MODEL), lambda i: (i, 0)), mod_spec,
                     pl.BlockSpec((1, D_MODEL), lambda i: (0, 0))]),
        out_specs=pl.BlockSpec((tm, D_MODEL), lambda i: (i, 0)),
        out_shape=jax.ShapeDtypeStruct((rows, D_MODEL), F32),
        compiler_params=_cparams("arbitrary"),
        name="out_proj",
    )(*parts, *weights, x, _mod_arg(gate, rows, seq_len, tm, per_token), g_post.reshape(1, D_MODEL))


def _ffn_kernel(per_token, seq_len, tiles_per_seq, *refs):
    if per_token:
        (x_ref, sh_ref, sc_ref, gate_ref, gpre_ref, gpost_ref, wup_ref, cw_ref, cb_ref, wdn_ref,
         s1_ref, s2_ref, xo_ref, conv_ref, h_ref, acc_ref, *u_refs) = refs
    else:
        (x_ref, sh_ref, sc_ref, gate_ref, gpre_ref, gpost_ref, wup_ref, cw_ref, cb_ref, wdn_ref,
         xo_ref, conv_ref, h_ref, acc_ref, *u_refs) = refs
    tm = x_ref.shape[0]
    halo = h_ref.shape[0] - tm
    x = x_ref[...]
    if per_token:
        tau = lax.broadcasted_iota(jnp.int32, (tm, FF_CHUNK), 0) % seq_len
    else:
        seq_start = (pl.program_id(0) % tiles_per_seq) == 0

        @pl.when(seq_start)
        def _():
            h_ref[:halo] = jnp.zeros((halo, D_MODEL), BF16)

        @pl.when(jnp.logical_not(seq_start))
        def _():
            h_ref[:halo] = h_ref[tm:]
    h_ref[halo:] = _modulate(x, gpre_ref[...], sh_ref[0], sc_ref[0]).astype(BF16)
    acc_ref[...] = jnp.zeros_like(acc_ref)

    def up(lo):
        return _dot(h_ref[...], wup_ref[:, pl.ds(lo, FF_CHUNK)])

    def conv(slot, half, lo):
        cols = pl.ds(lo, FF_CHUNK)
        cw = cw_ref[:, cols]
        cb = cb_ref[:, cols]
        taps = lambda u2, u1, u0: cb + cw[0:1] * u2 + cw[1:2] * u1 + cw[2:3] * u0
        if per_token:
            u = u_refs[slot][half]
            u1 = jnp.where(tau == 0, s1_ref[:, cols], pltpu.roll(u, 1, axis=0))
            u2 = jnp.where(tau < 2, s2_ref[:, cols], pltpu.roll(u, 2, axis=0))
            conv_ref[:, cols] = u
            return taps(u2, u1, u)
        conv_ref[0, :, cols] = u_refs[slot][half, halo + tm - 2:]
        window = lambda back: u_refs[slot][half, halo - back:halo - back + tm]
        return taps(window(2), window(1), window(0))

    def produce(slot, lo):
        u_refs[slot][0] = up(lo)
        u_refs[slot][1] = up(lo + D_FF)

    def consume(slot, lo):
        g = (jax.nn.gelu(conv(slot, 0, lo)) * conv(slot, 1, lo + D_FF)).astype(BF16)
        acc_ref[...] += _dot(g, wdn_ref[pl.ds(lo, FF_CHUNK), :])

    def body(j, carry):
        lo = pl.multiple_of(j * (2 * FF_CHUNK), 2 * FF_CHUNK)
        produce(1, lo + FF_CHUNK)
        consume(0, lo)
        produce(0, lo + 2 * FF_CHUNK)
        consume(1, lo + FF_CHUNK)
        return carry

    n_chunks = D_FF // FF_CHUNK
    produce(0, 0)
    lax.fori_loop(0, n_chunks // 2, body, 0)
    consume(0, (n_chunks - 1) * FF_CHUNK)
    xo_ref[...] = x + gate_ref[0] * (_rms(acc_ref[...]) * gpost_ref[...])


def _conv_ffn(x, shift, scale, gate, g_pre, g_post, wup_bf, conv_w, conv_b, wdn_bf, seq_len, state=None):
    rows = x.shape[0]
    tm, per_token, tps, mod_spec = _row_cfg(rows, seq_len, FFN_ROW_TILE)
    full = lambda a: pl.BlockSpec(a.shape, lambda i: (0, 0), pipeline_mode=pl.Buffered(1))
    cb = conv_b.reshape(1, 2 * D_FF)
    in_specs = [pl.BlockSpec((tm, D_MODEL), lambda i: (i, 0)), mod_spec, mod_spec, mod_spec,
                pl.BlockSpec((1, D_MODEL), lambda i: (0, 0)), pl.BlockSpec((1, D_MODEL), lambda i: (0, 0)),
                full(wup_bf), full(conv_w), full(cb), full(wdn_bf)]
    args = [x] + [_mod_arg(m, rows, seq_len, tm, per_token) for m in (shift, scale, gate)] + [
        g_pre.reshape(1, D_MODEL), g_post.reshape(1, D_MODEL), wup_bf, conv_w, cb, wdn_bf]
    halo = 0 if per_token else SAMPLE_PAD
    scratch = [pltpu.VMEM((halo + tm, D_MODEL), BF16), pltpu.VMEM((tm, D_MODEL), F32),
               pltpu.VMEM((2, halo + tm, FF_CHUNK), F32), pltpu.VMEM((2, halo + tm, FF_CHUNK), F32)]
    if per_token:
        batch = rows // seq_len
        zeros = jnp.zeros((batch, seq_len - 1, 2 * D_FF), F32)
        s1 = jnp.concatenate([state[:, 1:2], zeros], axis=1).reshape(rows, 2 * D_FF)
        s2 = jnp.concatenate([state, zeros[:, 1:]], axis=1).reshape(rows, 2 * D_FF)
        in_specs += [pl.BlockSpec((tm, 2 * D_FF), lambda i: (i, 0))] * 2
        args += [s1, s2]
        conv_spec = pl.BlockSpec((tm, 2 * D_FF), lambda i: (i, 0))
        conv_shape = jax.ShapeDtypeStruct((rows, 2 * D_FF), F32)
    else:
        conv_spec = pl.BlockSpec((1, 2, 2 * D_FF), lambda i: (i // tps, 0, 0))
        conv_shape = jax.ShapeDtypeStruct((rows // seq_len, 2, 2 * D_FF), F32)
    return pl.pallas_call(
        functools.partial(_ffn_kernel, per_token, seq_len, tps),
        grid=(rows // tm,),
        in_specs=in_specs,
        out_specs=[pl.BlockSpec((tm, D_MODEL), lambda i: (i, 0)), conv_spec],
        out_shape=[jax.ShapeDtypeStruct((rows, D_MODEL), F32), conv_shape],
        scratch_shapes=scratch,
        compiler_params=_cparams("arbitrary"),
        name="conv_ffn",
    )(*args)


def _ret_tables(chunk, valid):
    log_gamma = jnp.log1p(-jnp.exp2(-5.0 - jnp.arange(H_A, dtype=F32)))
    idx = jnp.arange(chunk, dtype=F32)
    rel = idx[:, None] - idx[None, :]
    intra = jnp.exp(jnp.where(rel[None] >= 0, rel[None] * log_gamma[:, None, None], -jnp.inf))
    dq = jnp.exp((idx + 1.0)[None, :] * log_gamma[:, None])
    dk = jnp.where(idx[None, :] < valid, jnp.exp((valid - 1.0 - idx)[None, :] * log_gamma[:, None]), 0.0)
    dc = jnp.exp(valid * log_gamma)
    rep = lambda t: jnp.broadcast_to(t[..., None], t.shape + (LANES,))
    dc_rows = rep(jnp.repeat(dc, DK_A).reshape(H_A // 2, 2 * DK_A))
    return intra, rep(dq), rep(dk), dc_rows


def _ret_kernel(chunk, n_chunks, q_ref, k_ref, v_ref, ga_ref, s0_ref, intra_ref, dq_ref, dk_ref, dc_ref,
                o_ref, s_ref, st_ref):
    i = pl.program_id(1)

    @pl.when(i == 0)
    def _():
        st_ref[...] = s0_ref[0]

    lane = lax.broadcasted_iota(jnp.int32, (chunk, LANES), 1)
    for c in range(n_chunks):
        rows = slice(c * chunk, (c + 1) * chunk)
        for pair in range(H_A // 2):
            lanes = slice(pair * LANES, (pair + 1) * LANES)
            qc = q_ref[0, rows, lanes]
            kc = k_ref[0, rows, lanes]
            s_pair = st_ref[pair]
            s_bf = s_pair.astype(BF16)
            upd = s_pair * dc_ref[pair]
            for side in range(2):
                hd = 2 * pair + side
                mine = (lane >= 64) if side else (lane < 64)
                hl = slice(hd * DV_A, (hd + 1) * DV_A)
                qh = jnp.where(mine, qc, jnp.zeros_like(qc))
                vh = v_ref[0, rows, hl]
                sc = _dot_nt(qh, kc) * intra_ref[hd]
                o = _dot(sc.astype(BF16), vh) + _dot(qh, s_bf) * dq_ref[hd]
                kh = jnp.where(mine, kc.astype(F32) * dk_ref[hd], 0.0).astype(BF16)
                upd = upd + _dot_tn(kh, vh)
                gate = ga_ref[0, rows, hl].astype(F32)
                o_ref[0, rows, hl] = (_rms(o) * (gate * jax.nn.sigmoid(gate))).astype(BF16)
            st_ref[pair] = upd

    @pl.when(i == pl.num_programs(1) - 1)
    def _():
        s_ref[0] = st_ref[...]


def _retention(q, k, v, ga, s0, chunk, valid):
    b, l, _ = q.shape
    blk = min(l, 4 * chunk)
    n_chunks = blk // chunk
    tables = _ret_tables(chunk, valid)
    seq = lambda w: pl.BlockSpec((1, blk, w), lambda bi, i: (bi, i, 0))
    const = lambda a: pl.BlockSpec(a.shape, lambda bi, i: (0,) * a.ndim)
    state = pl.BlockSpec((1, H_A // 2, 2 * DK_A, DV_A), lambda bi, i: (bi, 0, 0, 0))
    o, s = pl.pallas_call(
        functools.partial(_ret_kernel, chunk, n_chunks),
        grid=(b, l // blk),
        in_specs=[seq(256), seq(256), seq(512), seq(512), state] + [const(t) for t in tables],
        out_specs=[seq(512), state],
        out_shape=[jax.ShapeDtypeStruct((b, l, H_A * DV_A), BF16),
                   jax.ShapeDtypeStruct((b, H_A // 2, 2 * DK_A, DV_A), F32)],
        scratch_shapes=[pltpu.VMEM((H_A // 2, 2 * DK_A, DV_A), F32)],
        compiler_params=_cparams("arbitrary", "arbitrary"),
        name="retention",
    )(q, k, v, ga, s0.reshape(b, H_A // 2, 2 * DK_A, DV_A), *tables)
    return o, s.reshape(b, H_A, DK_A, DV_A)


def _lam_value(lam_ref, lam_init):
    lv = lam_ref[...]
    return (jnp.exp(jnp.sum(lv[0:1] * lv[1:2], axis=-1, keepdims=True))
            - jnp.exp(jnp.sum(lv[2:3] * lv[3:4], axis=-1, keepdims=True)) + lam_init)


def _dattn_kernel(lam_init, tq, tk, q_ref, k_ref, v_ref, lam_ref, g_ref, o_ref, vt_ref, st_ref, m_ref, l_ref, acc_ref):
    qi = pl.program_id(2)

    @pl.when(qi == 0)
    def _():
        vt_ref[...] = v_ref[0].astype(F32).T.astype(BF16)

    q = q_ref[0]
    lane = lax.broadcasted_iota(jnp.int32, q.shape, 1)
    q2 = jnp.concatenate([jnp.where(lane < DH_B, q, jnp.zeros_like(q)),
                          jnp.where(lane >= DH_B, q, jnp.zeros_like(q))], axis=0)
    m_ref[...] = jnp.full_like(m_ref, -jnp.inf)
    l_ref[...] = jnp.zeros_like(l_ref)
    acc_ref[...] = jnp.zeros_like(acc_ref)

    def scores(slot, kb, diag_offset=None):
        start = pl.multiple_of(kb * tk, tk)
        st = _dot_nt(k_ref[0, pl.ds(start, tk), :], q2)
        if diag_offset is not None:
            key = lax.broadcasted_iota(jnp.int32, st.shape, 0) + diag_offset
            qry = jnp.bitwise_and(lax.broadcasted_iota(jnp.int32, st.shape, 1), tq - 1)
            st = jnp.where(key <= qry, st, -jnp.inf)
        st_ref[slot] = st

    def absorb(slot, kb):
        start = pl.multiple_of(kb * tk, tk)
        st = st_ref[slot]
        m_old = m_ref[...]
        m_new = jnp.maximum(m_old, jnp.max(st, axis=0, keepdims=True))
        alpha = jnp.exp2(m_old - m_new)
        pt = jnp.exp2(st - m_new)
        l_ref[...] = alpha * l_ref[...] + jnp.sum(pt, axis=0, keepdims=True)
        acc_ref[...] = alpha * acc_ref[...] + _dot(vt_ref[:, pl.ds(start, tk)], pt.astype(BF16))
        m_ref[...] = m_new

    assert tq == 2 * tk
    first_diag = 2 * qi
    scores(0, first_diag, 0)
    scores(1, first_diag + 1, tk)
    absorb(0, first_diag)

    def body(t, carry):
        scores(0, 2 * t)
        absorb(1, jnp.where(t == 0, first_diag + 1, 2 * t - 1))
        scores(1, 2 * t + 1)
        absorb(0, 2 * t)
        return carry

    lax.fori_loop(0, qi, body, 0)
    absorb(1, jnp.where(qi == 0, first_diag + 1, 2 * qi - 1))

    lam = _lam_value(lam_ref, lam_init)
    inv_l = 1.0 / l_ref[...]
    acc = acc_ref[...] * inv_l
    o = (acc[:, :tq] - lam * acc[:, tq:]).T
    o_ref[0] = ((_rms(o) * g_ref[...]) * (1.0 - lam_init)).astype(BF16)


def _diff_attn_prompt(qb, kb, vb, lam_vecs, g_diff, lam_init):
    b, l, _ = qb.shape
    tq, tk = 1024, 512
    whole = pl.BlockSpec((1, l, LANES), lambda bi, h, qi: (bi, 0, h))
    return pl.pallas_call(
        functools.partial(_dattn_kernel, lam_init, tq, tk),
        grid=(b, H_B, l // tq),
        in_specs=[pl.BlockSpec((1, tq, LANES), lambda bi, h, qi: (bi, qi, h)), whole, whole,
                  pl.BlockSpec(lam_vecs.shape, lambda bi, h, qi: (0, 0)),
                  pl.BlockSpec((1, DV_B), lambda bi, h, qi: (0, 0))],
        out_specs=pl.BlockSpec((1, tq, LANES), lambda bi, h, qi: (bi, qi, h)),
        out_shape=jax.ShapeDtypeStruct((b, l, H_B * DV_B), BF16),
        scratch_shapes=[pltpu.VMEM((DV_B, l), BF16), pltpu.VMEM((2, tk, 2 * tq), F32),
                        pltpu.VMEM((1, 2 * tq), F32), pltpu.VMEM((1, 2 * tq), F32),
                        pltpu.VMEM((DV_B, 2 * tq), F32)],
        compiler_params=_cparams("arbitrary", "arbitrary", "arbitrary"),
        name="diff_attn_prompt",
    )(qb, kb, vb, lam_vecs, g_diff.reshape(1, DV_B))


Q_ROWS = H_B * 2 * 8
KV_ROWS = PAGE_SIZE * H_B


def _dattn_dec_kernel(lam_init, n_pages, pt_ref, q_ref, kn_ref, vn_ref, *refs):
    k_refs = refs[:n_pages]
    v_refs = refs[n_pages:2 * n_pages]
    lam_ref, g_ref, o_ref, m_ref, l_ref, acc_ref = refs[2 * n_pages:]
    step = pl.program_id(1)
    q = q_ref[0]

    @pl.when(step == 0)
    def _():
        s = _dot_nt(q, kn_ref[0])
        r = lax.broadcasted_iota(jnp.int32, s.shape, 0)
        c = lax.broadcasted_iota(jnp.int32, s.shape, 1)
        s = jnp.where((c % H_B == r // 16) & (c // H_B <= r % 8), s, -jnp.inf)
        m = jnp.max(s, axis=-1, keepdims=True)
        p = jnp.exp2(s - m)
        m_ref[...] = m
        l_ref[...] = jnp.sum(p, axis=-1, keepdims=True)
        acc_ref[...] = _dot(p.astype(BF16), vn_ref[0])

    own_head = (lax.broadcasted_iota(jnp.int32, (Q_ROWS, KV_ROWS), 1) % H_B
                == lax.broadcasted_iota(jnp.int32, (Q_ROWS, KV_ROWS), 0) // 16)
    bias = jnp.where(own_head, 0.0, -jnp.inf)
    scores = [_dot_nt(q, kr[0].astype(BF16)) + bias for kr in k_refs]
    m_old = m_ref[...]
    m_new = m_old
    for s in scores:
        m_new = jnp.maximum(m_new, jnp.max(s, axis=-1, keepdims=True))
    alpha = jnp.exp2(m_old - m_new)
    l_new = alpha * l_ref[...]
    acc = alpha * acc_ref[...]
    for s, vr in zip(scores, v_refs):
        p = jnp.exp2(s - m_new)
        l_new = l_new + jnp.sum(p, axis=-1, keepdims=True)
        acc = acc + _dot(p.astype(BF16), vr[0].astype(BF16))
    m_ref[...] = m_new
    l_ref[...] = l_new
    acc_ref[...] = acc

    @pl.when(step == pl.num_programs(1) - 1)
    def _():
        lam = _lam_value(lam_ref, lam_init)
        o = acc / l_new
        for h in range(H_B):
            d = o[16 * h:16 * h + 8] - lam * o[16 * h + 8:16 * h + 16]
            o_ref[0, :, h * DV_B:(h + 1) * DV_B] = ((_rms(d) * g_ref[...]) * (1.0 - lam_init)).astype(BF16)


def _diff_attn_decode(q_rows, k_new, v_new, cache_k, cache_v, page_base, page_table, lam_vecs, g_diff, lam_init):
    b = q_rows.shape[0]
    n_steps = page_table.shape[1] // PAGES_PER_STEP
    per_b = lambda r, w: pl.BlockSpec((1, r, w), lambda bi, s, pt: (bi, 0, 0))

    def page_spec(j):
        return pl.BlockSpec((1, KV_ROWS, LANES),
                            lambda bi, s, pt: (page_base + pt[bi, s * PAGES_PER_STEP + j], 0, 0))

    pages = [page_spec(j) for j in range(PAGES_PER_STEP)]
    grid_spec = pltpu.PrefetchScalarGridSpec(
        num_scalar_prefetch=1,
        grid=(b, n_steps),
        in_specs=[per_b(Q_ROWS, LANES), per_b(DEC_SEQ * H_B, LANES), per_b(DEC_SEQ * H_B, LANES)]
        + pages + pages + [pl.BlockSpec(lam_vecs.shape, lambda bi, s, pt: (0, 0)),
                           pl.BlockSpec((1, DV_B), lambda bi, s, pt: (0, 0))],
        out_specs=per_b(8, H_B * DV_B),
        scratch_shapes=[pltpu.VMEM((Q_ROWS, 1), F32), pltpu.VMEM((Q_ROWS, 1), F32),
                        pltpu.VMEM((Q_ROWS, DV_B), F32)],
    )
    return pl.pallas_call(
        functools.partial(_dattn_dec_kernel, lam_init, PAGES_PER_STEP),
        grid_spec=grid_spec,
        out_shape=jax.ShapeDtypeStruct((b, 8, H_B * DV_B), BF16),
        compiler_params=_cparams("arbitrary", "arbitrary"),
        name="diff_attn_decode",
    )(page_table, q_rows, k_new, v_new, *([cache_k] * PAGES_PER_STEP), *([cache_v] * PAGES_PER_STEP),
      lam_vecs, g_diff.reshape(1, DV_B))


def _decode_q_rows(qb):
    b = qb.shape[0]
    q = qb.reshape(b, DEC_SEQ, H_B, DV_B).transpose(0, 2, 1, 3)
    q = jnp.pad(q, ((0, 0), (0, 0), (0, 8 - DEC_SEQ), (0, 0)))[:, :, None]
    keep = (np.arange(DV_B)[None, :] // DH_B) == np.arange(2)[:, None]
    keep = jnp.asarray(keep)[None, None, :, None, :]
    return jnp.where(keep, q, jnp.zeros_like(q)).reshape(b, Q_ROWS, DV_B)


def _gla_tables(chunk):
    idx = np.arange(chunk)
    i, t = idx[:, None], idx[None, :]
    mats = [t <= i, t > i]
    masks = [i == t]
    s = 1
    while s < chunk:
        same = (t // s) == (i // s)
        odd = ((i // s) % 2) == 1
        mats.append(same & np.where(odd, t <= i, t > i))
        masks.append(((i // (2 * s)) == (t // (2 * s))) & odd & (((t // s) % 2) == 0))
        s *= 2
    return (jnp.asarray(np.concatenate(mats, axis=0).astype(np.float32), BF16),
            jnp.asarray(np.stack(masks).astype(np.float32)))


def _gla_kernel(chunk, n_chunks, valid, q_ref, k_ref, g_ref, v_ref, r_ref, s0_ref, gn_ref, m_ref, p_ref,
                o_ref, s_ref, st_ref):
    i = pl.program_id(1)

    @pl.when(i == 0)
    def _():
        for h in range(H_C):
            st_ref[h] = s0_ref[0, h].T

    n_levels = p_ref.shape[0] - 1
    m_all = m_ref[...]
    live = lax.broadcasted_iota(jnp.int32, (chunk, 1), 0) < valid
    for c in range(n_chunks):
        rows = slice(c * chunk, (c + 1) * chunk)
        g = g_ref[0, rows, :]
        if valid < chunk:
            g = jnp.where(live, g, 0.0)
        g_hi = g.astype(BF16)
        g_lo = (g - g_hi.astype(F32)).astype(BF16)
        e_all = jnp.exp(_dot(m_all, g_hi) + _dot(m_all, g_lo))
        heads = []
        for h in range(H_C):
            kl = slice(h * DK_C, (h + 1) * DK_C)
            e = e_all[:, kl]
            q = q_ref[0, rows, kl]
            k = k_ref[0, rows, kl]
            v = v_ref[0, rows, h * DV_C:(h + 1) * DV_C]
            if valid < chunk:
                k = jnp.where(live, k, jnp.zeros_like(k))
                v = jnp.where(live, v, jnp.zeros_like(v))
            qf = q.astype(F32)
            kf = k.astype(F32)
            level = lambda lv: e[(2 + lv) * chunk:(3 + lv) * chunk]
            prods = [_dot_nt(q, k)] + [_dot_nt((qf * level(lv)).astype(BF16), (kf * level(lv)).astype(BF16))
                                       for lv in range(n_levels)]
            heads.append(dict(v=v, prods=prods, q_in=(qf * e[0:chunk]).astype(BF16),
                              k_out=(kf * e[chunk:2 * chunk]).astype(BF16), decay=e[chunk - 1:chunk]))
        for hd in heads:
            att = hd["prods"][0] * p_ref[0]
            for lv in range(n_levels):
                att = att + hd["prods"][1 + lv] * p_ref[1 + lv]
            hd["intra"] = _dot(att.astype(BF16), hd["v"])
            hd["update"] = _dot_tn(hd["v"], hd["k_out"])
        for h, hd in enumerate(heads):
            vl = slice(h * DV_C, (h + 1) * DV_C)
            st = st_ref[h]
            o = hd["intra"] + _dot_nt(hd["q_in"], st.astype(BF16))
            st_ref[h] = st * hd["decay"] + hd["update"]
            gate = r_ref[0, rows, vl].astype(F32)
            o_ref[0, rows, vl] = ((_rms(o) * gn_ref[...]) * (gate * jax.nn.sigmoid(gate))).astype(BF16)

    @pl.when(i == pl.num_programs(1) - 1)
    def _():
        for h in range(H_C):
            s_ref[0, h] = st_ref[h].T


def _gla(q, k, g, v, r, s0, g_norm, chunk, valid):
    b, l, _ = q.shape
    blk = min(l, 4 * chunk)
    n_chunks = blk // chunk
    m_all, masks = _gla_tables(chunk)
    qk = pl.BlockSpec((1, blk, H_C * DK_C), lambda bi, i: (bi, i, 0))
    vr = pl.BlockSpec((1, blk, H_C * DV_C), lambda bi, i: (bi, i, 0))
    state = pl.BlockSpec((1, H_C, DK_C, DV_C), lambda bi, i: (bi, 0, 0, 0))
    const = lambda a: pl.BlockSpec(a.shape, lambda bi, i: (0,) * a.ndim)
    gn = g_norm.reshape(1, DV_C)
    return pl.pallas_call(
        functools.partial(_gla_kernel, chunk, n_chunks, valid),
        grid=(b, l // blk),
        in_specs=[qk, qk, qk, vr, vr, state, const(gn), const(m_all), const(masks)],
        out_specs=[vr, state],
        out_shape=[jax.ShapeDtypeStruct((b, l, H_C * DV_C), BF16),
                   jax.ShapeDtypeStruct((b, H_C, DK_C, DV_C), F32)],
        scratch_shapes=[pltpu.VMEM((H_C, DV_C, DK_C), F32)],
        compiler_params=_cparams("arbitrary", "arbitrary"),
        name="gla",
    )(q, k, g, v, r, s0, gn, m_all, masks)


def _rope_tables(pos):
    inv = ROPE_THETA ** (-jnp.arange(0, DK_A, 2, dtype=F32) / DK_A)
    ang = pos.astype(F32)[:, None] * inv[None, :]
    cos = jnp.tile(jnp.cos(ang), (1, 4))
    sin = jnp.tile(jnp.concatenate([-jnp.sin(ang), jnp.sin(ang)], axis=1), (1, 2))
    return cos, sin


def _pad_seq(t, batch, seq_len):
    t = t.reshape(batch, seq_len, t.shape[-1])
    return jnp.pad(t, ((0, 0), (0, SAMPLE_PAD - seq_len), (0, 0)))


def _unpad_seq(t, seq_len):
    return t[:, :seq_len].reshape(t.shape[0] * seq_len, t.shape[-1])


def kernel(x_prompt, x_sample, cache_k, cache_v, state_ret, state_gla, state_conv, page_table, c_prompt, c_sample, w_ada, b_ada, g_pre_mix, g_post_mix, g_pre_ffn, g_post_ffn, w_in_ab, w_out_ab, lam_q1, lam_k1, lam_q2, lam_k2, g_diff, w_in_c, w_gate_c, b_gate_c, g_gla, w_out_c, w_up, conv_w, conv_b, w_down):
    rows_p, rows_s = BATCH * SEQ, DEC_BATCH * DEC_SEQ
    xp = x_prompt.reshape(rows_p, D_MODEL)
    xs = x_sample.reshape(rows_s, D_MODEL)

    n_c = BATCH + DEC_BATCH
    c_all = jnp.pad(jnp.concatenate([c_prompt, c_sample], axis=0), ((0, (-n_c) % 8), (0, 0)))
    ada = _ada(c_all, w_ada, b_ada)

    cos_p, sin_p = _rope_tables(jnp.arange(SEQ))
    cos_s, sin_s = _rope_tables(jnp.tile(PAST_LEN + jnp.arange(DEC_SEQ), DEC_BATCH))

    k_p, v_p, k_s, v_s, ret_p, ret_s, gla_p, gla_s, conv_p, conv_s = ([] for _ in range(10))
    for l in range(DEPTH):
        mp = jnp.split(ada[l, :BATCH], 6, axis=-1)
        ms = jnp.split(ada[l, BATCH:n_c], 6, axis=-1)
        if l % 2 == 0:
            e = l // 2
            lam_init = 0.8 - 0.6 * math.exp(-0.3 * l)
            lam_vecs = jnp.stack([lam_q1[e], lam_k1[e], lam_q2[e], lam_k2[e]]).astype(F32)
            w_in = w_in_ab[e].astype(BF16)
            w_out = w_out_ab[e].astype(BF16)
            w_parts = [w_out[:H_A * DV_A], w_out[H_A * DV_A:]]

            qa, ka, va, ga, qb, kb32, kb16, vb32, vb16 = _in_proj_ab(
                xp, mp[0], mp[1], g_pre_mix[l], w_in, cos_p, sin_p, SEQ)
            seq3 = lambda t: t.reshape(BATCH, SEQ, t.shape[-1])
            ret, sp = _retention(seq3(qa), seq3(ka), seq3(va), seq3(ga),
                                 jnp.zeros((BATCH, H_A, DK_A, DV_A), F32), CHUNK_RET, CHUNK_RET)
            dif = _diff_attn_prompt(seq3(qb), seq3(kb16), seq3(vb16), lam_vecs, g_diff[e], lam_init)
            xp = _out_proj([ret.reshape(rows_p, -1), dif.reshape(rows_p, -1)], w_parts, xp, mp[2],
                           g_post_mix[l], SEQ)
            k_p.append(kb32.reshape(BATCH, SEQ, H_B, 2 * DH_B))
            v_p.append(vb32.reshape(BATCH, SEQ, H_B, DV_B))
            ret_p.append(sp)

            qa, ka, va, ga, qb, kb32, kb16, vb32, vb16 = _in_proj_ab(
                xs, ms[0], ms[1], g_pre_mix[l], w_in, cos_s, sin_s, DEC_SEQ)
            pad = lambda t: _pad_seq(t, DEC_BATCH, DEC_SEQ)
            ret, ss = _retention(pad(qa), pad(ka), pad(va), pad(ga), state_ret[e], SAMPLE_PAD, DEC_SEQ)
            n_phys = cache_k.shape[1]
            kv_rows = lambda t: t.reshape(DEC_BATCH, DEC_SEQ * H_B, DV_B)
            dif = _diff_attn_decode(
                _decode_q_rows(qb.reshape(DEC_BATCH, DEC_SEQ, 512)), kv_rows(kb16), kv_rows(vb16),
                cache_k.reshape(N_EVEN * n_phys, KV_ROWS, LANES), cache_v.reshape(N_EVEN * n_phys, KV_ROWS, LANES),
                e * n_phys, page_table, lam_vecs, g_diff[e], lam_init)
            xs = _out_proj([_unpad_seq(ret, DEC_SEQ), _unpad_seq(dif, DEC_SEQ)], w_parts, xs, ms[2],
                           g_post_mix[l], DEC_SEQ)
            k_s.append(kb32.reshape(DEC_BATCH, DEC_SEQ, H_B, 2 * DH_B))
            v_s.append(vb32.reshape(DEC_BATCH, DEC_SEQ, H_B, DV_B))
            ret_s.append(ss)
        else:
            o = l // 2
            w_in = jnp.pad(w_in_c[o], ((0, 0), (0, LANES - GATE_RANK))).astype(BF16)
            w_gate = jnp.pad(w_gate_c[o], ((0, LANES - GATE_RANK), (0, 0))).astype(BF16)
            w_out = w_out_c[o].astype(BF16)

            q, k, v, r, lg = _in_proj_c(xp, mp[0], mp[1], g_pre_mix[l], w_in, w_gate, b_gate_c[o], SEQ)
            seq3 = lambda t: t.reshape(BATCH, SEQ, t.shape[-1])
            og, sp = _gla(seq3(q), seq3(k), seq3(lg), seq3(v), seq3(r),
                          jnp.zeros((BATCH, H_C, DK_C, DV_C), F32), g_gla[o], CHUNK_GLA, CHUNK_GLA)
            xp = _out_proj([og.reshape(rows_p, -1)], [w_out], xp, mp[2], g_post_mix[l], SEQ)
            gla_p.append(sp)

            q, k, v, r, lg = _in_proj_c(xs, ms[0], ms[1], g_pre_mix[l], w_in, w_gate, b_gate_c[o], DEC_SEQ)
            pad = lambda t: _pad_seq(t, DEC_BATCH, DEC_SEQ)
            og, ss = _gla(pad(q), pad(k), pad(lg), pad(v), pad(r), state_gla[o], g_gla[o], SAMPLE_PAD, DEC_SEQ)
            xs = _out_proj([_unpad_seq(og, DEC_SEQ)], [w_out], xs, ms[2], g_post_mix[l], DEC_SEQ)
            gla_s.append(ss)

        wup = w_up[l].astype(BF16)
        wdn = w_down[l].astype(BF16)
        xp, cbp = _conv_ffn(xp, mp[3], mp[4], mp[5], g_pre_ffn[l], g_post_ffn[l], wup, conv_w[l], conv_b[l],
                            wdn, SEQ)
        xs, up_s = _conv_ffn(xs, ms[3], ms[4], ms[5], g_pre_ffn[l], g_post_ffn[l], wup, conv_w[l], conv_b[l],
                             wdn, DEC_SEQ, state=state_conv[l])
        conv_p.append(cbp)
        conv_s.append(up_s.reshape(DEC_BATCH, DEC_SEQ, 2 * D_FF)[:, DEC_SEQ - (CONV_W - 1):])

    stack = lambda ts: ts[0][None] if len(ts) == 1 else jnp.stack(ts)
    return (xp.reshape(BATCH, SEQ, D_MODEL), xs.reshape(DEC_BATCH, DEC_SEQ, D_MODEL),
            stack(k_p), stack(v_p), stack(k_s), stack(v_s),
            stack(ret_p), stack(ret_s), stack(gla_p), stack(gla_s),
            stack(conv_p), stack(conv_s))
```

```python
import functools
import itertools
import math

import numpy as np
import jax
import jax.numpy as jnp
from jax import lax
from jax.experimental import pallas as pl
from jax.experimental.pallas import tpu as pltpu

D_MODEL = 1024
BATCH = 4
SEQ = 4096
DEPTH = 2
DEC_BATCH = 32
DEC_SEQ = 4
PAST_LEN = 8192
PAGE_SIZE = 128
N_EVEN = (DEPTH + 1) // 2
N_ODD = DEPTH // 2
H_A = 4
DK_A = D_MODEL // 16
DV_A = D_MODEL // 8
CHUNK_RET = 128
H_B = 4
DH_B = D_MODEL // 16
DV_B = 2 * DH_B
H_C = 4
DK_C = D_MODEL // 8
DV_C = D_MODEL // 4
GATE_RANK = 16
GATE_TAU = 16.0
CHUNK_GLA = 64
D_FF = ((8 * D_MODEL // 3 + 127) // 128) * 128
CONV_W = 3
ROPE_THETA = 10000.0
EPS = 1e-6

LANES = 128
VMEM_LIMIT = 56 * 1024 * 1024
ROW_TILE = 1024
FFN_ROW_TILE = 1024
CHUNKS_PER_STEP = 8
SHORT_SEQS_PER_STEP = 8
SAMPLE_PAD = 16
FF_CHUNK = 256
PAGES_PER_STEP = 16
LOG2E = math.log2(math.e)
BF16 = jnp.bfloat16
F32 = jnp.float32


def _cparams(*sem):
    return pltpu.CompilerParams(dimension_semantics=sem, vmem_limit_bytes=VMEM_LIMIT)


def _dot(a, b):
    return jnp.dot(a, b, preferred_element_type=F32)


def _dot_nt(a, b):
    return lax.dot_general(a, b, (((1,), (1,)), ((), ())), preferred_element_type=F32)


def _dot_tn(a, b):
    return lax.dot_general(a, b, (((0,), (0,)), ((), ())), preferred_element_type=F32)


def _rms(x):
    return x * lax.rsqrt(jnp.mean(x * x, axis=-1, keepdims=True) + EPS)


def _modulate(x, g, shift, scale):
    return (_rms(x) * g) * (1.0 + scale) + shift


def _rope(x, cos, sin_signed):
    w = x.shape[-1]
    fwd = pltpu.roll(x, 32, axis=1)
    bwd = pltpu.roll(x, w - 32, axis=1)
    reps = w // LANES
    first_half = (lax.broadcasted_iota(jnp.int32, x.shape, 1) % 64) < 32
    partner = jnp.where(first_half, bwd, fwd)
    return (x * jnp.concatenate([cos] * reps, axis=1)
            + partner * jnp.concatenate([sin_signed] * reps, axis=1))


def _ada_kernel(c_ref, w_ref, b_ref, o_ref):
    c = c_ref[...]
    s = c * jax.nn.sigmoid(c)
    o_ref[0] = jnp.dot(s, w_ref[0], preferred_element_type=F32, precision=lax.Precision.HIGHEST) + b_ref[0]


def _ada(c_all, w_ada, b_ada):
    rows = c_all.shape[0]
    tn = 1536
    return pl.pallas_call(
        _ada_kernel,
        grid=(DEPTH, 6 * D_MODEL // tn),
        in_specs=[pl.BlockSpec((rows, D_MODEL), lambda l, j: (0, 0)),
                  pl.BlockSpec((1, D_MODEL, tn), lambda l, j: (l, 0, j)),
                  pl.BlockSpec((1, 1, tn), lambda l, j: (l, 0, j))],
        out_specs=pl.BlockSpec((1, rows, tn), lambda l, j: (l, 0, j)),
        out_shape=jax.ShapeDtypeStruct((DEPTH, rows, 6 * D_MODEL), F32),
        compiler_params=_cparams("arbitrary", "arbitrary"),
        name="adaln",
    )(c_all, w_ada, b_ada.reshape(DEPTH, 1, 6 * D_MODEL))


def _row_cfg(rows, seq_len, row_tile=ROW_TILE):
    tm = min(row_tile, rows)
    per_token = seq_len < tm
    tiles_per_seq = 1 if per_token else seq_len // tm
    mod_spec = (pl.BlockSpec((1, tm, D_MODEL), lambda i: (i, 0, 0)) if per_token
                else pl.BlockSpec((1, 1, D_MODEL), lambda i: (i // tiles_per_seq, 0, 0)))
    return tm, per_token, tiles_per_seq, mod_spec


def _mod_arg(m, rows, seq_len, tm, per_token):
    if per_token:
        return jnp.repeat(m, seq_len, axis=0).reshape(rows // tm, tm, D_MODEL)
    return m.reshape(m.shape[0], 1, D_MODEL)


def _inab_kernel(x_ref, sh_ref, sc_ref, g_ref, w_ref, cos_ref, sin_ref,
                 qa_ref, ka_ref, va_ref, ga_ref, qb_ref, kb32_ref, kb16_ref, vb32_ref, vb16_ref):
    h = _modulate(x_ref[...], g_ref[...], sh_ref[0], sc_ref[0]).astype(BF16)
    cos = cos_ref[...]
    sin = sin_ref[...]

    def proj(lo, width):
        return _dot(h, w_ref[:, lo:lo + width])

    qa_ref[...] = _rope(proj(0, 256), cos, sin).astype(BF16)
    ka_ref[...] = (_rope(proj(256, 256), cos, sin) * (DK_A ** -0.5)).astype(BF16)
    va_ref[...] = proj(512, 512).astype(BF16)
    ga_ref[...] = proj(1024, 512).astype(BF16)
    qb_ref[...] = (_rope(proj(1536, 512), cos, sin) * (DH_B ** -0.5 * LOG2E)).astype(BF16)
    kb = _rope(proj(2048, 512), cos, sin)
    kb16_ref[...] = kb.astype(BF16)
    vb = proj(2560, 512)
    vb16_ref[...] = vb.astype(BF16)
    tm = kb.shape[0]
    for h in range(H_B):
        kb32_ref[pl.ds(h, tm, stride=H_B), :] = kb[:, h * DV_B:(h + 1) * DV_B]
        vb32_ref[pl.ds(h, tm, stride=H_B), :] = vb[:, h * DV_B:(h + 1) * DV_B]


def _in_proj_ab(x, shift, scale, g_pre, w_bf, cos, sin, seq_len):
    rows = x.shape[0]
    tm, per_token, tps, mod_spec = _row_cfg(rows, seq_len)
    pos_map = (lambda i: (i, 0)) if per_token else (lambda i: (i % tps, 0))
    row = lambda w: pl.BlockSpec((tm, w), lambda i: (i, 0))
    widths = (256, 256, 512, 512, 512, 512, 512, 512, 512)
    dtypes = (BF16, BF16, BF16, BF16, BF16, F32, BF16, F32, BF16)
    cache_rows = lambda w, d: d == F32
    out_spec = lambda w, d: (pl.BlockSpec((tm * H_B, DV_B), lambda i: (i, 0)) if cache_rows(w, d) else row(w))
    out_shape = lambda w, d: jax.ShapeDtypeStruct((rows * H_B, DV_B) if cache_rows(w, d) else (rows, w), d)
    return pl.pallas_call(
        _inab_kernel,
        grid=(rows // tm,),
        in_specs=[row(D_MODEL), mod_spec, mod_spec,
                  pl.BlockSpec((1, D_MODEL), lambda i: (0, 0)),
                  pl.BlockSpec(w_bf.shape, lambda i: (0, 0)),
                  pl.BlockSpec((tm, LANES), pos_map), pl.BlockSpec((tm, LANES), pos_map)],
        out_specs=[out_spec(w, d) for w, d in zip(widths, dtypes)],
        out_shape=[out_shape(w, d) for w, d in zip(widths, dtypes)],
        compiler_params=_cparams("arbitrary"),
        name="in_proj_ab",
    )(x, _mod_arg(shift, rows, seq_len, tm, per_token), _mod_arg(scale, rows, seq_len, tm, per_token),
      g_pre.reshape(1, D_MODEL), w_bf, cos, sin)


def _inc_kernel(x_ref, sh_ref, sc_ref, g_ref, w_ref, wg_ref, bg_ref,
                q_ref, k_ref, v_ref, r_ref, lg_ref):
    h = _modulate(x_ref[...], g_ref[...], sh_ref[0], sc_ref[0]).astype(BF16)

    def proj(lo, width):
        return _dot(h, w_ref[:, lo:lo + width])

    q_ref[...] = (proj(0, 512) * (DK_C ** -0.5)).astype(BF16)
    k_ref[...] = proj(512, 512).astype(BF16)
    v_ref[...] = proj(1024, 1024).astype(BF16)
    r_ref[...] = proj(2048, 1024).astype(BF16)
    a = proj(3072, LANES)
    z = _dot(a.astype(BF16), wg_ref[...]) + bg_ref[...]
    softplus_neg = jnp.maximum(-z, 0.0) + jnp.log1p(jnp.exp(-jnp.abs(z)))
    lg_ref[...] = -softplus_neg / GATE_TAU


def _in_proj_c(x, shift, scale, g_pre, w_bf, wg_bf, b_gate, seq_len):
    rows = x.shape[0]
    tm, per_token, tps, mod_spec = _row_cfg(rows, seq_len)
    row = lambda w: pl.BlockSpec((tm, w), lambda i: (i, 0))
    widths = (512, 512, 1024, 1024, 512)
    dtypes = (BF16, BF16, BF16, BF16, F32)
    full = lambda a: pl.BlockSpec(a.shape, lambda i: (0, 0))
    bg = b_gate.reshape(1, -1)
    return pl.pallas_call(
        _inc_kernel,
        grid=(rows // tm,),
        in_specs=[row(D_MODEL), mod_spec, mod_spec, pl.BlockSpec((1, D_MODEL), lambda i: (0, 0)),
                  full(w_bf), full(wg_bf), full(bg)],
        out_specs=[row(w) for w in widths],
        out_shape=[jax.ShapeDtypeStruct((rows, w), d) for w, d in zip(widths, dtypes)],
        compiler_params=_cparams("arbitrary"),
        name="in_proj_c",
    )(x, _mod_arg(shift, rows, seq_len, tm, per_token), _mod_arg(scale, rows, seq_len, tm, per_token),
      g_pre.reshape(1, D_MODEL), w_bf, wg_bf, bg)


def _outproj_kernel(n_parts, *refs):
    parts = refs[:n_parts]
    ws = refs[n_parts:2 * n_parts]
    x_ref, gate_ref, g_ref, o_ref = refs[2 * n_parts:]
    y = _dot(parts[0][...], ws[0][...])
    for p, w in zip(parts[1:], ws[1:]):
        y = y + _dot(p[...], w[...])
    o_ref[...] = x_ref[...] + gate_ref[0] * (_rms(y) * g_ref[...])


def _out_proj(parts, weights, x, gate, g_post, seq_len):
    rows = x.shape[0]
    tm, per_token, tps, mod_spec = _row_cfg(rows, seq_len)
    n = len(parts)
    return pl.pallas_call(
        functools.partial(_outproj_kernel, n),
        grid=(rows // tm,),
        in_specs=([pl.BlockSpec((tm, p.shape[1]), lambda i: (i, 0)) for p in parts]
                  + [pl.BlockSpec(w.shape, lambda i: (0, 0)) for w in weights]
                  + [pl.BlockSpec((tm, D_MODEL), lambda i: (i, 0)), mod_spec,
                     pl.BlockSpec((1, D_MODEL), lambda i: (0, 0))]),
        out_specs=pl.BlockSpec((tm, D_MODEL), lambda i: (i, 0)),
        out_shape=jax.ShapeDtypeStruct((rows, D_MODEL), F32),
        compiler_params=_cparams("arbitrary"),
        name="out_proj",
    )(*parts, *weights, x, _mod_arg(gate, rows, seq_len, tm, per_token), g_post.reshape(1, D_MODEL))


def _ffn_kernel(per_token, seq_len, tiles_per_seq, *refs):
    if per_token:
        (x_ref, sh_ref, sc_ref, gate_ref, gpre_ref, gpost_ref, wup_ref, cw_ref, cb_ref, wdn_ref,
         s1_ref, s2_ref, xo_ref, conv_ref, h_ref, acc_ref, *u_refs) = refs
    else:
        (x_ref, sh_ref, sc_ref, gate_ref, gpre_ref, gpost_ref, wup_ref, cw_ref, cb_ref, wdn_ref,
         xo_ref, conv_ref, h_ref, acc_ref, *u_refs) = refs
    tm = x_ref.shape[0]
    halo = h_ref.shape[0] - tm
    x = x_ref[...]
    if per_token:
        tau = lax.broadcasted_iota(jnp.int32, (tm, FF_CHUNK), 0) % seq_len
    else:
        seq_start = (pl.program_id(0) % tiles_per_seq) == 0

        @pl.when(seq_start)
        def _():
            h_ref[:halo] = jnp.zeros((halo, D_MODEL), BF16)

        @pl.when(jnp.logical_not(seq_start))
        def _():
            h_ref[:halo] = h_ref[tm:]
    h_ref[halo:] = _modulate(x, gpre_ref[...], sh_ref[0], sc_ref[0]).astype(BF16)
    acc_ref[...] = jnp.zeros_like(acc_ref)

    def up(lo):
        return _dot(h_ref[...], wup_ref[:, pl.ds(lo, FF_CHUNK)])

    def conv(slot, half, lo):
        cols = pl.ds(lo, FF_CHUNK)
        cw = cw_ref[:, cols]
        cb = cb_ref[:, cols]
        taps = lambda u2, u1, u0: cb + cw[0:1] * u2 + cw[1:2] * u1 + cw[2:3] * u0
        if per_token:
            u = u_refs[slot][half]
            u1 = jnp.where(tau == 0, s1_ref[:, cols], pltpu.roll(u, 1, axis=0))
            u2 = jnp.where(tau < 2, s2_ref[:, cols], pltpu.roll(u, 2, axis=0))
            conv_ref[:, cols] = u
            return taps(u2, u1, u)
        conv_ref[0, :, cols] = u_refs[slot][half, halo + tm - 2:]
        window = lambda back: u_refs[slot][half, halo - back:halo - back + tm]
        return taps(window(2), window(1), window(0))

    def produce(slot, lo):
        u_refs[slot][0] = up(lo)
        u_refs[slot][1] = up(lo + D_FF)

    def consume(slot, lo):
        g = (jax.nn.gelu(conv(slot, 0, lo)) * conv(slot, 1, lo + D_FF)).astype(BF16)
        acc_ref[...] += _dot(g, wdn_ref[pl.ds(lo, FF_CHUNK), :])

    def body(j, carry):
        lo = pl.multiple_of(j * (2 * FF_CHUNK), 2 * FF_CHUNK)
        produce(1, lo + FF_CHUNK)
        consume(0, lo)
        produce(0, lo + 2 * FF_CHUNK)
        consume(1, lo + FF_CHUNK)
        return carry

    n_chunks = D_FF // FF_CHUNK
    produce(0, 0)
    lax.fori_loop(0, n_chunks // 2, body, 0)
    consume(0, (n_chunks - 1) * FF_CHUNK)
    xo_ref[...] = x + gate_ref[0] * (_rms(acc_ref[...]) * gpost_ref[...])


def _conv_ffn(x, shift, scale, gate, g_pre, g_post, wup_bf, conv_w, conv_b, wdn_bf, seq_len, state=None):
    rows = x.shape[0]
    tm, per_token, tps, mod_spec = _row_cfg(rows, seq_len, FFN_ROW_TILE)
    full = lambda a: pl.BlockSpec(a.shape, lambda i: (0, 0), pipeline_mode=pl.Buffered(1))
    cb = conv_b.reshape(1, 2 * D_FF)
    in_specs = [pl.BlockSpec((tm, D_MODEL), lambda i: (i, 0)), mod_spec, mod_spec, mod_spec,
                pl.BlockSpec((1, D_MODEL), lambda i: (0, 0)), pl.BlockSpec((1, D_MODEL), lambda i: (0, 0)),
                full(wup_bf), full(conv_w), full(cb), full(wdn_bf)]
    args = [x] + [_mod_arg(m, rows, seq_len, tm, per_token) for m in (shift, scale, gate)] + [
        g_pre.reshape(1, D_MODEL), g_post.reshape(1, D_MODEL), wup_bf, conv_w, cb, wdn_bf]
    halo = 0 if per_token else SAMPLE_PAD
    scratch = [pltpu.VMEM((halo + tm, D_MODEL), BF16), pltpu.VMEM((tm, D_MODEL), F32),
               pltpu.VMEM((2, halo + tm, FF_CHUNK), F32), pltpu.VMEM((2, halo + tm, FF_CHUNK), F32)]
    if per_token:
        batch = rows // seq_len
        zeros = jnp.zeros((batch, seq_len - 1, 2 * D_FF), F32)
        s1 = jnp.concatenate([state[:, 1:2], zeros], axis=1).reshape(rows, 2 * D_FF)
        s2 = jnp.concatenate([state, zeros[:, 1:]], axis=1).reshape(rows, 2 * D_FF)
        in_specs += [pl.BlockSpec((tm, 2 * D_FF), lambda i: (i, 0))] * 2
        args += [s1, s2]
        conv_spec = pl.BlockSpec((tm, 2 * D_FF), lambda i: (i, 0))
        conv_shape = jax.ShapeDtypeStruct((rows, 2 * D_FF), F32)
    else:
        conv_spec = pl.BlockSpec((1, 2, 2 * D_FF), lambda i: (i // tps, 0, 0))
        conv_shape = jax.ShapeDtypeStruct((rows // seq_len, 2, 2 * D_FF), F32)
    return pl.pallas_call(
        functools.partial(_ffn_kernel, per_token, seq_len, tps),
        grid=(rows // tm,),
        in_specs=in_specs,
        out_specs=[pl.BlockSpec((tm, D_MODEL), lambda i: (i, 0)), conv_spec],
        out_shape=[jax.ShapeDtypeStruct((rows, D_MODEL), F32), conv_shape],
        scratch_shapes=scratch,
        compiler_params=_cparams("arbitrary"),
        name="conv_ffn",
    )(*args)


def _ret_tables(chunk, valid):
    log_gamma = jnp.log1p(-jnp.exp2(-5.0 - jnp.arange(H_A, dtype=F32)))
    idx = jnp.arange(chunk, dtype=F32)
    rel = idx[:, None] - idx[None, :]
    intra = jnp.exp(jnp.where(rel[None] >= 0, rel[None] * log_gamma[:, None, None], -jnp.inf))
    dq = jnp.exp((idx + 1.0)[None, :] * log_gamma[:, None])
    dk = jnp.where(idx[None, :] < valid, jnp.exp((valid - 1.0 - idx)[None, :] * log_gamma[:, None]), 0.0)
    dc = jnp.exp(valid * log_gamma)
    rep = lambda t: jnp.broadcast_to(t[..., None], t.shape + (LANES,))
    dc_rows = rep(jnp.repeat(dc, DK_A).reshape(H_A // 2, 2 * DK_A))
    return intra, rep(dq), rep(dk), dc_rows


def _ret_kernel(chunk, n_chunks, seqs, q_ref, k_ref, v_ref, ga_ref, s0_ref, intra_ref, dq_ref, dk_ref, dc_ref,
                o_ref, s_ref, st_ref):
    i = pl.program_id(1)

    @pl.when(i == 0)
    def _():
        st_ref[...] = s0_ref[...]

    lane = lax.broadcasted_iota(jnp.int32, (chunk, LANES), 1)
    for bi in range(seqs):
        for c in range(n_chunks):
            rows = slice(c * chunk, (c + 1) * chunk)
            for pair in range(H_A // 2):
                lanes = slice(pair * LANES, (pair + 1) * LANES)
                qc = q_ref[bi, rows, lanes]
                kc = k_ref[bi, rows, lanes]
                s_pair = st_ref[bi, pair]
                s_bf = s_pair.astype(BF16)
                upd = s_pair * dc_ref[pair]
                for side in range(2):
                    hd = 2 * pair + side
                    mine = (lane >= 64) if side else (lane < 64)
                    hl = slice(hd * DV_A, (hd + 1) * DV_A)
                    qh = jnp.where(mine, qc, jnp.zeros_like(qc))
                    vh = v_ref[bi, rows, hl]
                    sc = _dot_nt(qh, kc) * intra_ref[hd]
                    o = _dot(sc.astype(BF16), vh) + _dot(qh, s_bf) * dq_ref[hd]
                    kh = jnp.where(mine, kc.astype(F32) * dk_ref[hd], 0.0).astype(BF16)
                    upd = upd + _dot_tn(kh, vh)
                    gate = ga_ref[bi, rows, hl].astype(F32)
                    o_ref[bi, rows, hl] = (_rms(o) * (gate * jax.nn.sigmoid(gate))).astype(BF16)
                st_ref[bi, pair] = upd

    @pl.when(i == pl.num_programs(1) - 1)
    def _():
        s_ref[...] = st_ref[...]


def _seqs_per_step(batch, seq_len, blk):
    return min(batch, SHORT_SEQS_PER_STEP) if seq_len == blk else 1


def _retention(q, k, v, ga, s0, chunk, valid):
    b, l, _ = q.shape
    blk = min(l, CHUNKS_PER_STEP * chunk)
    n_chunks = blk // chunk
    seqs = _seqs_per_step(b, l, blk)
    tables = _ret_tables(chunk, valid)
    seq = lambda w: pl.BlockSpec((seqs, blk, w), lambda bi, i: (bi, i, 0))
    const = lambda a: pl.BlockSpec(a.shape, lambda bi, i: (0,) * a.ndim)
    state = pl.BlockSpec((seqs, H_A // 2, 2 * DK_A, DV_A), lambda bi, i: (bi, 0, 0, 0))
    o, s = pl.pallas_call(
        functools.partial(_ret_kernel, chunk, n_chunks, seqs),
        grid=(b // seqs, l // blk),
        in_specs=[seq(256), seq(256), seq(512), seq(512), state] + [const(t) for t in tables],
        out_specs=[seq(512), state],
        out_shape=[jax.ShapeDtypeStruct((b, l, H_A * DV_A), BF16),
                   jax.ShapeDtypeStruct((b, H_A // 2, 2 * DK_A, DV_A), F32)],
        scratch_shapes=[pltpu.VMEM((seqs, H_A // 2, 2 * DK_A, DV_A), F32)],
        compiler_params=_cparams("arbitrary", "arbitrary"),
        name="retention",
    )(q, k, v, ga, s0.reshape(b, H_A // 2, 2 * DK_A, DV_A), *tables)
    return o, s.reshape(b, H_A, DK_A, DV_A)


def _lam_value(lam_ref, lam_init):
    lv = lam_ref[...]
    return (jnp.exp(jnp.sum(lv[0:1] * lv[1:2], axis=-1, keepdims=True))
            - jnp.exp(jnp.sum(lv[2:3] * lv[3:4], axis=-1, keepdims=True)) + lam_init)


def _dattn_kernel(lam_init, tq, tk, q_ref, k_ref, v_ref, lam_ref, g_ref, o_ref, vt_ref, st_ref, m_ref, l_ref, acc_ref):
    qi = pl.program_id(2)

    @pl.when(qi == 0)
    def _():
        vt_ref[...] = v_ref[0].astype(F32).T.astype(BF16)

    q = q_ref[0]
    lane = lax.broadcasted_iota(jnp.int32, q.shape, 1)
    q2 = jnp.concatenate([jnp.where(lane < DH_B, q, jnp.zeros_like(q)),
                          jnp.where(lane >= DH_B, q, jnp.zeros_like(q))], axis=0)
    m_ref[...] = jnp.full_like(m_ref, -jnp.inf)
    l_ref[...] = jnp.zeros_like(l_ref)
    acc_ref[...] = jnp.zeros_like(acc_ref)

    def scores(slot, kb, diag_offset=None):
        start = pl.multiple_of(kb * tk, tk)
        st = _dot_nt(k_ref[0, pl.ds(start, tk), :], q2)
        if diag_offset is not None:
            key = lax.broadcasted_iota(jnp.int32, st.shape, 0) + diag_offset
            qry = jnp.bitwise_and(lax.broadcasted_iota(jnp.int32, st.shape, 1), tq - 1)
            st = jnp.where(key <= qry, st, -jnp.inf)
        st_ref[slot] = st

    def absorb(slot, kb):
        start = pl.multiple_of(kb * tk, tk)
        st = st_ref[slot]
        m_old = m_ref[...]
        m_new = jnp.maximum(m_old, jnp.max(st, axis=0, keepdims=True))
        alpha = jnp.exp2(m_old - m_new)
        pt = jnp.exp2(st - m_new)
        l_ref[...] = alpha * l_ref[...] + jnp.sum(pt, axis=0, keepdims=True)
        acc_ref[...] = alpha * acc_ref[...] + _dot(vt_ref[:, pl.ds(start, tk)], pt.astype(BF16))
        m_ref[...] = m_new

    assert tq == 2 * tk
    first_diag = 2 * qi
    scores(0, first_diag, 0)
    scores(1, first_diag + 1, tk)
    absorb(0, first_diag)

    def body(t, carry):
        scores(0, 2 * t)
        absorb(1, jnp.where(t == 0, first_diag + 1, 2 * t - 1))
        scores(1, 2 * t + 1)
        absorb(0, 2 * t)
        return carry

    lax.fori_loop(0, qi, body, 0)
    absorb(1, jnp.where(qi == 0, first_diag + 1, 2 * qi - 1))

    lam = _lam_value(lam_ref, lam_init)
    inv_l = 1.0 / l_ref[...]
    acc = acc_ref[...] * inv_l
    o = (acc[:, :tq] - lam * acc[:, tq:]).T
    o_ref[0] = ((_rms(o) * g_ref[...]) * (1.0 - lam_init)).astype(BF16)


def _diff_attn_prompt(qb, kb, vb, lam_vecs, g_diff, lam_init):
    b, l, _ = qb.shape
    tq, tk = 1024, 512
    whole = pl.BlockSpec((1, l, LANES), lambda bi, h, qi: (bi, 0, h))
    return pl.pallas_call(
        functools.partial(_dattn_kernel, lam_init, tq, tk),
        grid=(b, H_B, l // tq),
        in_specs=[pl.BlockSpec((1, tq, LANES), lambda bi, h, qi: (bi, qi, h)), whole, whole,
                  pl.BlockSpec(lam_vecs.shape, lambda bi, h, qi: (0, 0)),
                  pl.BlockSpec((1, DV_B), lambda bi, h, qi: (0, 0))],
        out_specs=pl.BlockSpec((1, tq, LANES), lambda bi, h, qi: (bi, qi, h)),
        out_shape=jax.ShapeDtypeStruct((b, l, H_B * DV_B), BF16),
        scratch_shapes=[pltpu.VMEM((DV_B, l), BF16), pltpu.VMEM((2, tk, 2 * tq), F32),
                        pltpu.VMEM((1, 2 * tq), F32), pltpu.VMEM((1, 2 * tq), F32),
                        pltpu.VMEM((DV_B, 2 * tq), F32)],
        compiler_params=_cparams("arbitrary", "arbitrary", "arbitrary"),
        name="diff_attn_prompt",
    )(qb, kb, vb, lam_vecs, g_diff.reshape(1, DV_B))


Q_ROWS = H_B * 2 * 8
KV_ROWS = PAGE_SIZE * H_B


def _dattn_dec_kernel(lam_init, n_pages, pt_ref, q_ref, kn_ref, vn_ref, *refs):
    k_refs = refs[:n_pages]
    v_refs = refs[n_pages:2 * n_pages]
    lam_ref, g_ref, o_ref, m_ref, l_ref, acc_ref = refs[2 * n_pages:]
    step = pl.program_id(1)
    q = q_ref[0]

    @pl.when(step == 0)
    def _():
        s = _dot_nt(q, kn_ref[0])
        r = lax.broadcasted_iota(jnp.int32, s.shape, 0)
        c = lax.broadcasted_iota(jnp.int32, s.shape, 1)
        s = jnp.where((c % H_B == r // 16) & (c // H_B <= r % 8), s, -jnp.inf)
        m = jnp.max(s, axis=-1, keepdims=True)
        p = jnp.exp2(s - m)
        m_ref[...] = m
        l_ref[...] = jnp.sum(p, axis=-1, keepdims=True)
        acc_ref[...] = _dot(p.astype(BF16), vn_ref[0])

    own_head = (lax.broadcasted_iota(jnp.int32, (Q_ROWS, KV_ROWS), 1) % H_B
                == lax.broadcasted_iota(jnp.int32, (Q_ROWS, KV_ROWS), 0) // 16)
    bias = jnp.where(own_head, 0.0, -jnp.inf)
    scores = [_dot_nt(q, kr[0].astype(BF16)) + bias for kr in k_refs]
    m_old = m_ref[...]
    m_new = m_old
    for s in scores:
        m_new = jnp.maximum(m_new, jnp.max(s, axis=-1, keepdims=True))
    alpha = jnp.exp2(m_old - m_new)
    l_new = alpha * l_ref[...]
    acc = alpha * acc_ref[...]
    for s, vr in zip(scores, v_refs):
        p = jnp.exp2(s - m_new)
        l_new = l_new + jnp.sum(p, axis=-1, keepdims=True)
        acc = acc + _dot(p.astype(BF16), vr[0].astype(BF16))
    m_ref[...] = m_new
    l_ref[...] = l_new
    acc_ref[...] = acc

    @pl.when(step == pl.num_programs(1) - 1)
    def _():
        lam = _lam_value(lam_ref, lam_init)
        o = acc / l_new
        for h in range(H_B):
            d = o[16 * h:16 * h + 8] - lam * o[16 * h + 8:16 * h + 16]
            o_ref[0, :, h * DV_B:(h + 1) * DV_B] = ((_rms(d) * g_ref[...]) * (1.0 - lam_init)).astype(BF16)


def _diff_attn_decode(q_rows, k_new, v_new, cache_k, cache_v, page_base, page_table, lam_vecs, g_diff, lam_init):
    b = q_rows.shape[0]
    n_steps = page_table.shape[1] // PAGES_PER_STEP
    per_b = lambda r, w: pl.BlockSpec((1, r, w), lambda bi, s, pt: (bi, 0, 0))

    def page_spec(j):
        return pl.BlockSpec((1, KV_ROWS, LANES),
                            lambda bi, s, pt: (page_base + pt[bi, s * PAGES_PER_STEP + j], 0, 0))

    pages = [page_spec(j) for j in range(PAGES_PER_STEP)]
    grid_spec = pltpu.PrefetchScalarGridSpec(
        num_scalar_prefetch=1,
        grid=(b, n_steps),
        in_specs=[per_b(Q_ROWS, LANES), per_b(DEC_SEQ * H_B, LANES), per_b(DEC_SEQ * H_B, LANES)]
        + pages + pages + [pl.BlockSpec(lam_vecs.shape, lambda bi, s, pt: (0, 0)),
                           pl.BlockSpec((1, DV_B), lambda bi, s, pt: (0, 0))],
        out_specs=per_b(8, H_B * DV_B),
        scratch_shapes=[pltpu.VMEM((Q_ROWS, 1), F32), pltpu.VMEM((Q_ROWS, 1), F32),
                        pltpu.VMEM((Q_ROWS, DV_B), F32)],
    )
    return pl.pallas_call(
        functools.partial(_dattn_dec_kernel, lam_init, PAGES_PER_STEP),
        grid_spec=grid_spec,
        out_shape=jax.ShapeDtypeStruct((b, 8, H_B * DV_B), BF16),
        compiler_params=_cparams("arbitrary", "arbitrary"),
        name="diff_attn_decode",
    )(page_table, q_rows, k_new, v_new, *([cache_k] * PAGES_PER_STEP), *([cache_v] * PAGES_PER_STEP),
      lam_vecs, g_diff.reshape(1, DV_B))


def _decode_q_rows(qb):
    b = qb.shape[0]
    q = qb.reshape(b, DEC_SEQ, H_B, DV_B).transpose(0, 2, 1, 3)
    q = jnp.pad(q, ((0, 0), (0, 0), (0, 8 - DEC_SEQ), (0, 0)))[:, :, None]
    keep = (np.arange(DV_B)[None, :] // DH_B) == np.arange(2)[:, None]
    keep = jnp.asarray(keep)[None, None, :, None, :]
    return jnp.where(keep, q, jnp.zeros_like(q)).reshape(b, Q_ROWS, DV_B)


def _gla_tables(chunk):
    idx = np.arange(chunk)
    i, t = idx[:, None], idx[None, :]
    mats = [t <= i, t > i]
    masks = [i == t]
    s = 1
    while s < chunk:
        same = (t // s) == (i // s)
        odd = ((i // s) % 2) == 1
        mats.append(same & np.where(odd, t <= i, t > i))
        masks.append(((i // (2 * s)) == (t // (2 * s))) & odd & (((t // s) % 2) == 0))
        s *= 2
    return (jnp.asarray(np.concatenate(mats, axis=0).astype(np.float32), BF16),
            jnp.asarray(np.stack(masks).astype(np.float32)))


def _gla_kernel(chunk, n_chunks, seqs, valid, q_ref, k_ref, g_ref, v_ref, r_ref, s0_ref, gn_ref, m_ref, p_ref,
                o_ref, s_ref, st_ref):
    i = pl.program_id(1)

    @pl.when(i == 0)
    def _():
        for bi in range(seqs):
            for h in range(H_C):
                st_ref[bi, h] = s0_ref[bi, h].T

    n_levels = p_ref.shape[0] - 1
    m_all = m_ref[...]
    live = lax.broadcasted_iota(jnp.int32, (chunk, 1), 0) < valid
    for bi, c in itertools.product(range(seqs), range(n_chunks)):
        rows = slice(c * chunk, (c + 1) * chunk)
        g = g_ref[bi, rows,:]
        if valid < chunk:
            g = jnp.where(live, g, 0.0)
        g_hi = g.astype(BF16)
        g_lo = (g - g_hi.astype(F32)).astype(BF16)
        e_all = jnp.exp(_dot(m_all, g_hi) + _dot(m_all, g_lo))
        heads = []
        for h in range(H_C):
            kl = slice(h * DK_C, (h + 1) * DK_C)
            e = e_all[:, kl]
            q = q_ref[bi, rows,kl]
            k = k_ref[bi, rows,kl]
            v = v_ref[bi, rows,h * DV_C:(h + 1) * DV_C]
            if valid < chunk:
                k = jnp.where(live, k, jnp.zeros_like(k))
                v = jnp.where(live, v, jnp.zeros_like(v))
            qf = q.astype(F32)
            kf = k.astype(F32)
            level = lambda lv: e[(2 + lv) * chunk:(3 + lv) * chunk]
            prods = [_dot_nt(q, k)] + [_dot_nt((qf * level(lv)).astype(BF16), (kf * level(lv)).astype(BF16))
                                       for lv in range(n_levels)]
            heads.append(dict(v=v, prods=prods, q_in=(qf * e[0:chunk]).astype(BF16),
                              k_out=(kf * e[chunk:2 * chunk]).astype(BF16), decay=e[chunk - 1:chunk]))
        for hd in heads:
            att = hd["prods"][0] * p_ref[0]
            for lv in range(n_levels):
                att = att + hd["prods"][1 + lv] * p_ref[1 + lv]
            hd["intra"] = _dot(att.astype(BF16), hd["v"])
            hd["update"] = _dot_tn(hd["v"], hd["k_out"])
        for h, hd in enumerate(heads):
            vl = slice(h * DV_C, (h + 1) * DV_C)
            st = st_ref[bi, h]
            o = hd["intra"] + _dot_nt(hd["q_in"], st.astype(BF16))
            st_ref[bi, h] = st * hd["decay"] + hd["update"]
            gate = r_ref[bi, rows,vl].astype(F32)
            o_ref[bi, rows,vl] = ((_rms(o) * gn_ref[...]) * (gate * jax.nn.sigmoid(gate))).astype(BF16)

    @pl.when(i == pl.num_programs(1) - 1)
    def _():
        for bi in range(seqs):
            for h in range(H_C):
                s_ref[bi, h] = st_ref[bi, h].T


def _gla(q, k, g, v, r, s0, g_norm, chunk, valid):
    b, l, _ = q.shape
    blk = min(l, CHUNKS_PER_STEP * chunk)
    n_chunks = blk // chunk
    seqs = _seqs_per_step(b, l, blk)
    m_all, masks = _gla_tables(chunk)
    qk = pl.BlockSpec((seqs, blk, H_C * DK_C), lambda bi, i: (bi, i, 0))
    vr = pl.BlockSpec((seqs, blk, H_C * DV_C), lambda bi, i: (bi, i, 0))
    state = pl.BlockSpec((seqs, H_C, DK_C, DV_C), lambda bi, i: (bi, 0, 0, 0))
    const = lambda a: pl.BlockSpec(a.shape, lambda bi, i: (0,) * a.ndim)
    gn = g_norm.reshape(1, DV_C)
    return pl.pallas_call(
        functools.partial(_gla_kernel, chunk, n_chunks, seqs, valid),
        grid=(b // seqs, l // blk),
        in_specs=[qk, qk, qk, vr, vr, state, const(gn), const(m_all), const(masks)],
        out_specs=[vr, state],
        out_shape=[jax.ShapeDtypeStruct((b, l, H_C * DV_C), BF16),
                   jax.ShapeDtypeStruct((b, H_C, DK_C, DV_C), F32)],
        scratch_shapes=[pltpu.VMEM((seqs, H_C, DV_C, DK_C), F32)],
        compiler_params=_cparams("arbitrary", "arbitrary"),
        name="gla",
    )(q, k, g, v, r, s0, gn, m_all, masks)


def _rope_tables(pos):
    inv = ROPE_THETA ** (-jnp.arange(0, DK_A, 2, dtype=F32) / DK_A)
    ang = pos.astype(F32)[:, None] * inv[None, :]
    cos = jnp.tile(jnp.cos(ang), (1, 4))
    sin = jnp.tile(jnp.concatenate([-jnp.sin(ang), jnp.sin(ang)], axis=1), (1, 2))
    return cos, sin


def _pad_seq(t, batch, seq_len):
    t = t.reshape(batch, seq_len, t.shape[-1])
    return jnp.pad(t, ((0, 0), (0, SAMPLE_PAD - seq_len), (0, 0)))


def _unpad_seq(t, seq_len):
    return t[:, :seq_len].reshape(t.shape[0] * seq_len, t.shape[-1])


def kernel(x_prompt, x_sample, cache_k, cache_v, state_ret, state_gla, state_conv, page_table, c_prompt, c_sample, w_ada, b_ada, g_pre_mix, g_post_mix, g_pre_ffn, g_post_ffn, w_in_ab, w_out_ab, lam_q1, lam_k1, lam_q2, lam_k2, g_diff, w_in_c, w_gate_c, b_gate_c, g_gla, w_out_c, w_up, conv_w, conv_b, w_down):
    rows_p, rows_s = BATCH * SEQ, DEC_BATCH * DEC_SEQ
    xp = x_prompt.reshape(rows_p, D_MODEL)
    xs = x_sample.reshape(rows_s, D_MODEL)

    n_c = BATCH + DEC_BATCH
    c_all = jnp.pad(jnp.concatenate([c_prompt, c_sample], axis=0), ((0, (-n_c) % 8), (0, 0)))
    ada = _ada(c_all, w_ada, b_ada)

    cos_p, sin_p = _rope_tables(jnp.arange(SEQ))
    cos_s, sin_s = _rope_tables(jnp.tile(PAST_LEN + jnp.arange(DEC_SEQ), DEC_BATCH))

    k_p, v_p, k_s, v_s, ret_p, ret_s, gla_p, gla_s, conv_p, conv_s = ([] for _ in range(10))
    for l in range(DEPTH):
        mp = jnp.split(ada[l, :BATCH], 6, axis=-1)
        ms = jnp.split(ada[l, BATCH:n_c], 6, axis=-1)
        if l % 2 == 0:
            e = l // 2
            lam_init = 0.8 - 0.6 * math.exp(-0.3 * l)
            lam_vecs = jnp.stack([lam_q1[e], lam_k1[e], lam_q2[e], lam_k2[e]]).astype(F32)
            w_in = w_in_ab[e].astype(BF16)
            w_out = w_out_ab[e].astype(BF16)
            w_parts = [w_out[:H_A * DV_A], w_out[H_A * DV_A:]]

            qa, ka, va, ga, qb, kb32, kb16, vb32, vb16 = _in_proj_ab(
                xp, mp[0], mp[1], g_pre_mix[l], w_in, cos_p, sin_p, SEQ)
            seq3 = lambda t: t.reshape(BATCH, SEQ, t.shape[-1])
            ret, sp = _retention(seq3(qa), seq3(ka), seq3(va), seq3(ga),
                                 jnp.zeros((BATCH, H_A, DK_A, DV_A), F32), CHUNK_RET, CHUNK_RET)
            dif = _diff_attn_prompt(seq3(qb), seq3(kb16), seq3(vb16), lam_vecs, g_diff[e], lam_init)
            xp = _out_proj([ret.reshape(rows_p, -1), dif.reshape(rows_p, -1)], w_parts, xp, mp[2],
                           g_post_mix[l], SEQ)
            k_p.append(kb32.reshape(BATCH, SEQ, H_B, 2 * DH_B))
            v_p.append(vb32.reshape(BATCH, SEQ, H_B, DV_B))
            ret_p.append(sp)

            qa, ka, va, ga, qb, kb32, kb16, vb32, vb16 = _in_proj_ab(
                xs, ms[0], ms[1], g_pre_mix[l], w_in, cos_s, sin_s, DEC_SEQ)
            pad = lambda t: _pad_seq(t, DEC_BATCH, DEC_SEQ)
            ret, ss = _retention(pad(qa), pad(ka), pad(va), pad(ga), state_ret[e], SAMPLE_PAD, DEC_SEQ)
            n_phys = cache_k.shape[1]
            kv_rows = lambda t: t.reshape(DEC_BATCH, DEC_SEQ * H_B, DV_B)
            dif = _diff_attn_decode(
                _decode_q_rows(qb.reshape(DEC_BATCH, DEC_SEQ, 512)), kv_rows(kb16), kv_rows(vb16),
                cache_k.reshape(N_EVEN * n_phys, KV_ROWS, LANES), cache_v.reshape(N_EVEN * n_phys, KV_ROWS, LANES),
                e * n_phys, page_table, lam_vecs, g_diff[e], lam_init)
            xs = _out_proj([_unpad_seq(ret, DEC_SEQ), _unpad_seq(dif, DEC_SEQ)], w_parts, xs, ms[2],
                           g_post_mix[l], DEC_SEQ)
            k_s.append(kb32.reshape(DEC_BATCH, DEC_SEQ, H_B, 2 * DH_B))
            v_s.append(vb32.reshape(DEC_BATCH, DEC_SEQ, H_B, DV_B))
            ret_s.append(ss)
        else:
            o = l // 2
            w_in = jnp.pad(w_in_c[o], ((0, 0), (0, LANES - GATE_RANK))).astype(BF16)
            w_gate = jnp.pad(w_gate_c[o], ((0, LANES - GATE_RANK), (0, 0))).astype(BF16)
            w_out = w_out_c[o].astype(BF16)

            q, k, v, r, lg = _in_proj_c(xp, mp[0], mp[1], g_pre_mix[l], w_in, w_gate, b_gate_c[o], SEQ)
            seq3 = lambda t: t.reshape(BATCH, SEQ, t.shape[-1])
            og, sp = _gla(seq3(q), seq3(k), seq3(lg), seq3(v), seq3(r),
                          jnp.zeros((BATCH, H_C, DK_C, DV_C), F32), g_gla[o], CHUNK_GLA, CHUNK_GLA)
            xp = _out_proj([og.reshape(rows_p, -1)], [w_out], xp, mp[2], g_post_mix[l], SEQ)
            gla_p.append(sp)

            q, k, v, r, lg = _in_proj_c(xs, ms[0], ms[1], g_pre_mix[l], w_in, w_gate, b_gate_c[o], DEC_SEQ)
            pad = lambda t: _pad_seq(t, DEC_BATCH, DEC_SEQ)
            og, ss = _gla(pad(q), pad(k), pad(lg), pad(v), pad(r), state_gla[o], g_gla[o], SAMPLE_PAD, DEC_SEQ)
            xs = _out_proj([_unpad_seq(og, DEC_SEQ)], [w_out], xs, ms[2], g_post_mix[l], DEC_SEQ)
            gla_s.append(ss)

        wup = w_up[l].astype(BF16)
        wdn = w_down[l].astype(BF16)
        xp, cbp = _conv_ffn(xp, mp[3], mp[4], mp[5], g_pre_ffn[l], g_post_ffn[l], wup, conv_w[l], conv_b[l],
                            wdn, SEQ)
        xs, up_s = _conv_ffn(xs, ms[3], ms[4], ms[5], g_pre_ffn[l], g_post_ffn[l], wup, conv_w[l], conv_b[l],
                             wdn, DEC_SEQ, state=state_conv[l])
        conv_p.append(cbp)
        conv_s.append(up_s.reshape(DEC_BATCH, DEC_SEQ, 2 * D_FF)[:, DEC_SEQ - (CONV_W - 1):])

    stack = lambda ts: ts[0][None] if len(ts) == 1 else jnp.stack(ts)
    return (xp.reshape(BATCH, SEQ, D_MODEL), xs.reshape(DEC_BATCH, DEC_SEQ, D_MODEL),
            stack(k_p), stack(v_p), stack(k_s), stack(v_s),
            stack(ret_p), stack(ret_s), stack(gla_p), stack(gla_s),
            stack(conv_p), stack(conv_s))
```

```python
import functools
import itertools
import math

import numpy as np
import jax
import jax.numpy as jnp
from jax import lax
from jax.experimental import pallas as pl
from jax.experimental.pallas import tpu as pltpu

D_MODEL = 1024
BATCH = 4
SEQ = 4096
DEPTH = 2
DEC_BATCH = 32
DEC_SEQ = 4
PAST_LEN = 8192
PAGE_SIZE = 128
N_EVEN = (DEPTH + 1) // 2
N_ODD = DEPTH // 2
H_A = 4
DK_A = D_MODEL // 16
DV_A = D_MODEL // 8
CHUNK_RET = 128
H_B = 4
DH_B = D_MODEL // 16
DV_B = 2 * DH_B
H_C = 4
DK_C = D_MODEL // 8
DV_C = D_MODEL // 4
GATE_RANK = 16
GATE_TAU = 16.0
C_WIDTHS_MAIN = (H_C * DK_C, H_C * DK_C, H_C * DV_C, H_C * DV_C)
CHUNK_GLA = 64
D_FF = ((8 * D_MODEL // 3 + 127) // 128) * 128
CONV_W = 3
ROPE_THETA = 10000.0
EPS = 1e-6

LANES = 128
VMEM_LIMIT = 56 * 1024 * 1024
ROW_TILE = 1024
FFN_ROW_TILE = 1024
GLA_MATMUL_LEVEL_SPAN = 8
CHUNKS_PER_STEP = 8
SHORT_SEQS_PER_STEP = 8
SAMPLE_PAD = 16
FF_CHUNK = 256
PAGES_PER_STEP = 16
LOG2E = math.log2(math.e)
BF16 = jnp.bfloat16
F32 = jnp.float32


def _cparams(*sem):
    return pltpu.CompilerParams(dimension_semantics=sem, vmem_limit_bytes=VMEM_LIMIT)


def _dot(a, b):
    return jnp.dot(a, b, preferred_element_type=F32)


def _dot_nt(a, b):
    return lax.dot_general(a, b, (((1,), (1,)), ((), ())), preferred_element_type=F32)


def _dot_tn(a, b):
    return lax.dot_general(a, b, (((0,), (0,)), ((), ())), preferred_element_type=F32)


def _rms(x):
    return x * lax.rsqrt(jnp.mean(x * x, axis=-1, keepdims=True) + EPS)


def _modulate(x, g, shift, scale):
    return (_rms(x) * g) * (1.0 + scale) + shift


def _rope(x, cos, sin_signed):
    w = x.shape[-1]
    fwd = pltpu.roll(x, 32, axis=1)
    bwd = pltpu.roll(x, w - 32, axis=1)
    reps = w // LANES
    first_half = (lax.broadcasted_iota(jnp.int32, x.shape, 1) % 64) < 32
    partner = jnp.where(first_half, bwd, fwd)
    return (x * jnp.concatenate([cos] * reps, axis=1)
            + partner * jnp.concatenate([sin_signed] * reps, axis=1))


def _ada_kernel(c_ref, w_ref, b_ref, o_ref):
    c = c_ref[...]
    s = c * jax.nn.sigmoid(c)
    o_ref[0] = jnp.dot(s, w_ref[0], preferred_element_type=F32, precision=lax.Precision.HIGHEST) + b_ref[0]


def _ada(c_all, w_ada, b_ada):
    rows = c_all.shape[0]
    tn = 1536
    return pl.pallas_call(
        _ada_kernel,
        grid=(DEPTH, 6 * D_MODEL // tn),
        in_specs=[pl.BlockSpec((rows, D_MODEL), lambda l, j: (0, 0)),
                  pl.BlockSpec((1, D_MODEL, tn), lambda l, j: (l, 0, j)),
                  pl.BlockSpec((1, 1, tn), lambda l, j: (l, 0, j))],
        out_specs=pl.BlockSpec((1, rows, tn), lambda l, j: (l, 0, j)),
        out_shape=jax.ShapeDtypeStruct((DEPTH, rows, 6 * D_MODEL), F32),
        compiler_params=_cparams("arbitrary", "arbitrary"),
        name="adaln",
    )(c_all, w_ada, b_ada.reshape(DEPTH, 1, 6 * D_MODEL))


def _row_cfg(rows, seq_len, row_tile=ROW_TILE):
    tm = min(row_tile, rows)
    per_token = seq_len < tm
    tiles_per_seq = 1 if per_token else seq_len // tm
    mod_spec = (pl.BlockSpec((1, tm, D_MODEL), lambda i: (i, 0, 0)) if per_token
                else pl.BlockSpec((1, 1, D_MODEL), lambda i: (i // tiles_per_seq, 0, 0)))
    return tm, per_token, tiles_per_seq, mod_spec


def _mod_arg(m, rows, seq_len, tm, per_token):
    if per_token:
        return jnp.repeat(m, seq_len, axis=0).reshape(rows // tm, tm, D_MODEL)
    return m.reshape(m.shape[0], 1, D_MODEL)


def _inab_kernel(x_ref, sh_ref, sc_ref, g_ref, w_ref, cos_ref, sin_ref,
                 qa_ref, ka_ref, va_ref, ga_ref, qb_ref, kb32_ref, kb16_ref, vb32_ref, vb16_ref):
    h = _modulate(x_ref[...], g_ref[...], sh_ref[0], sc_ref[0]).astype(BF16)
    cos = cos_ref[...]
    sin = sin_ref[...]

    def proj(lo, width):
        return _dot(h, w_ref[:, lo:lo + width])

    qa_ref[...] = _rope(proj(0, 256), cos, sin).astype(BF16)
    ka_ref[...] = (_rope(proj(256, 256), cos, sin) * (DK_A ** -0.5)).astype(BF16)
    va_ref[...] = proj(512, 512).astype(BF16)
    ga_ref[...] = proj(1024, 512).astype(BF16)
    qb_ref[...] = (_rope(proj(1536, 512), cos, sin) * (DH_B ** -0.5 * LOG2E)).astype(BF16)
    kb = _rope(proj(2048, 512), cos, sin)
    kb16_ref[...] = kb.astype(BF16)
    vb = proj(2560, 512)
    vb16_ref[...] = vb.astype(BF16)
    tm = kb.shape[0]
    for h in range(H_B):
        kb32_ref[pl.ds(h, tm, stride=H_B), :] = kb[:, h * DV_B:(h + 1) * DV_B]
        vb32_ref[pl.ds(h, tm, stride=H_B), :] = vb[:, h * DV_B:(h + 1) * DV_B]


def _in_proj_ab(x, shift, scale, g_pre, w_bf, cos, sin, seq_len):
    rows = x.shape[0]
    tm, per_token, tps, mod_spec = _row_cfg(rows, seq_len)
    pos_map = (lambda i: (i, 0)) if per_token else (lambda i: (i % tps, 0))
    row = lambda w: pl.BlockSpec((tm, w), lambda i: (i, 0))
    widths = (256, 256, 512, 512, 512, 512, 512, 512, 512)
    dtypes = (BF16, BF16, BF16, BF16, BF16, F32, BF16, F32, BF16)
    cache_rows = lambda w, d: d == F32
    out_spec = lambda w, d: (pl.BlockSpec((tm * H_B, DV_B), lambda i: (i, 0)) if cache_rows(w, d) else row(w))
    out_shape = lambda w, d: jax.ShapeDtypeStruct((rows * H_B, DV_B) if cache_rows(w, d) else (rows, w), d)
    return pl.pallas_call(
        _inab_kernel,
        grid=(rows // tm,),
        in_specs=[row(D_MODEL), mod_spec, mod_spec,
                  pl.BlockSpec((1, D_MODEL), lambda i: (0, 0)),
                  pl.BlockSpec(w_bf.shape, lambda i: (0, 0)),
                  pl.BlockSpec((tm, LANES), pos_map), pl.BlockSpec((tm, LANES), pos_map)],
        out_specs=[out_spec(w, d) for w, d in zip(widths, dtypes)],
        out_shape=[out_shape(w, d) for w, d in zip(widths, dtypes)],
        compiler_params=_cparams("arbitrary"),
        name="in_proj_ab",
    )(x, _mod_arg(shift, rows, seq_len, tm, per_token), _mod_arg(scale, rows, seq_len, tm, per_token),
      g_pre.reshape(1, D_MODEL), w_bf, cos, sin)


def _inc_kernel(x_ref, sh_ref, sc_ref, g_ref, w_ref, wa_ref, wg_ref, bg_ref,
                q_ref, k_ref, v_ref, r_ref, lg_ref):
    h = _modulate(x_ref[...], g_ref[...], sh_ref[0], sc_ref[0]).astype(BF16)

    def proj(lo, width):
        return _dot(h, w_ref[:, lo:lo + width])

    q_ref[...] = (proj(0, 512) * (DK_C ** -0.5)).astype(BF16)
    k_ref[...] = proj(512, 512).astype(BF16)
    v_ref[...] = proj(1024, 1024).astype(BF16)
    r_ref[...] = proj(2048, 1024).astype(BF16)
    a = _dot(h, wa_ref[...])
    z = _dot(a.astype(BF16), wg_ref[...]) + bg_ref[...]
    softplus_neg = jnp.maximum(-z, 0.0) + jnp.log1p(jnp.exp(-jnp.abs(z)))
    lg_ref[...] = -softplus_neg / GATE_TAU


def _in_proj_c(x, shift, scale, g_pre, w_bf, wa_bf, wg_bf, b_gate, seq_len):
    rows = x.shape[0]
    tm, per_token, tps, mod_spec = _row_cfg(rows, seq_len)
    row = lambda w: pl.BlockSpec((tm, w), lambda i: (i, 0))
    widths = (512, 512, 1024, 1024, 512)
    dtypes = (BF16, BF16, BF16, BF16, F32)
    full = lambda a: pl.BlockSpec(a.shape, lambda i: (0, 0))
    bg = b_gate.reshape(1, -1)
    return pl.pallas_call(
        _inc_kernel,
        grid=(rows // tm,),
        in_specs=[row(D_MODEL), mod_spec, mod_spec, pl.BlockSpec((1, D_MODEL), lambda i: (0, 0)),
                  full(w_bf), full(wa_bf), full(wg_bf), full(bg)],
        out_specs=[row(w) for w in widths],
        out_shape=[jax.ShapeDtypeStruct((rows, w), d) for w, d in zip(widths, dtypes)],
        compiler_params=_cparams("arbitrary"),
        name="in_proj_c",
    )(x, _mod_arg(shift, rows, seq_len, tm, per_token), _mod_arg(scale, rows, seq_len, tm, per_token),
      g_pre.reshape(1, D_MODEL), w_bf, wa_bf, wg_bf, bg)


def _outproj_kernel(n_parts, *refs):
    parts = refs[:n_parts]
    ws = refs[n_parts:2 * n_parts]
    x_ref, gate_ref, g_ref, o_ref = refs[2 * n_parts:]
    y = _dot(parts[0][...], ws[0][...])
    for p, w in zip(parts[1:], ws[1:]):
        y = y + _dot(p[...], w[...])
    o_ref[...] = x_ref[...] + gate_ref[0] * (_rms(y) * g_ref[...])


def _out_proj(parts, weights, x, gate, g_post, seq_len):
    rows = x.shape[0]
    tm, per_token, tps, mod_spec = _row_cfg(rows, seq_len)
    n = len(parts)
    return pl.pallas_call(
        functools.partial(_outproj_kernel, n),
        grid=(rows // tm,),
        in_specs=([pl.BlockSpec((tm, p.shape[1]), lambda i: (i, 0)) for p in parts]
                  + [pl.BlockSpec(w.shape, lambda i: (0, 0)) for w in weights]
                  + [pl.BlockSpec((tm, D_MODEL), lambda i: (i, 0)), mod_spec,
                     pl.BlockSpec((1, D_MODEL), lambda i: (0, 0))]),
        out_specs=pl.BlockSpec((tm, D_MODEL), lambda i: (i, 0)),
        out_shape=jax.ShapeDtypeStruct((rows, D_MODEL), F32),
        compiler_params=_cparams("arbitrary"),
        name="out_proj",
    )(*parts, *weights, x, _mod_arg(gate, rows, seq_len, tm, per_token), g_post.reshape(1, D_MODEL))


def _ffn_kernel(per_token, seq_len, tiles_per_seq, *refs):
    if per_token:
        (x_ref, sh_ref, sc_ref, gate_ref, gpre_ref, gpost_ref, wup_ref, cw_ref, cb_ref, wdn_ref,
         s1_ref, s2_ref, xo_ref, conv_ref, h_ref, acc_ref, *u_refs) = refs
    else:
        (x_ref, sh_ref, sc_ref, gate_ref, gpre_ref, gpost_ref, wup_ref, cw_ref, cb_ref, wdn_ref,
         xo_ref, conv_ref, h_ref, acc_ref, *u_refs) = refs
    tm = x_ref.shape[0]
    halo = h_ref.shape[0] - tm
    x = x_ref[...]
    if per_token:
        tau = lax.broadcasted_iota(jnp.int32, (tm, FF_CHUNK), 0) % seq_len
    else:
        seq_start = (pl.program_id(0) % tiles_per_seq) == 0

        @pl.when(seq_start)
        def _():
            h_ref[:halo] = jnp.zeros((halo, D_MODEL), BF16)

        @pl.when(jnp.logical_not(seq_start))
        def _():
            h_ref[:halo] = h_ref[tm:]
    h_ref[halo:] = _modulate(x, gpre_ref[...], sh_ref[0], sc_ref[0]).astype(BF16)
    acc_ref[...] = jnp.zeros_like(acc_ref)

    def up(lo):
        return _dot(h_ref[...], wup_ref[:, pl.ds(lo, FF_CHUNK)])

    def conv(slot, half, lo):
        cols = pl.ds(lo, FF_CHUNK)
        cw = cw_ref[:, cols]
        cb = cb_ref[:, cols]
        taps = lambda u2, u1, u0: cb + cw[0:1] * u2 + cw[1:2] * u1 + cw[2:3] * u0
        if per_token:
            u = u_refs[slot][half]
            u1 = jnp.where(tau == 0, s1_ref[:, cols], pltpu.roll(u, 1, axis=0))
            u2 = jnp.where(tau < 2, s2_ref[:, cols], pltpu.roll(u, 2, axis=0))
            conv_ref[:, cols] = u
            return taps(u2, u1, u)
        conv_ref[0, :, cols] = u_refs[slot][half, halo + tm - 2:]
        window = lambda back: u_refs[slot][half, halo - back:halo - back + tm]
        return taps(window(2), window(1), window(0))

    def produce(slot, lo):
        u_refs[slot][0] = up(lo)
        u_refs[slot][1] = up(lo + D_FF)

    def consume(slot, lo):
        g = (jax.nn.gelu(conv(slot, 0, lo)) * conv(slot, 1, lo + D_FF)).astype(BF16)
        acc_ref[...] += _dot(g, wdn_ref[pl.ds(lo, FF_CHUNK), :])

    def body(j, carry):
        lo = pl.multiple_of(j * (2 * FF_CHUNK), 2 * FF_CHUNK)
        produce(1, lo + FF_CHUNK)
        consume(0, lo)
        produce(0, lo + 2 * FF_CHUNK)
        consume(1, lo + FF_CHUNK)
        return carry

    n_chunks = D_FF // FF_CHUNK
    produce(0, 0)
    lax.fori_loop(0, n_chunks // 2, body, 0)
    consume(0, (n_chunks - 1) * FF_CHUNK)
    xo_ref[...] = x + gate_ref[0] * (_rms(acc_ref[...]) * gpost_ref[...])


def _conv_ffn(x, shift, scale, gate, g_pre, g_post, layer, wup_bf, conv_w, conv_b, wdn_bf, seq_len, state=None):
    rows = x.shape[0]
    tm, per_token, tps, mod_spec = _row_cfg(rows, seq_len, FFN_ROW_TILE)
    full = lambda a: pl.BlockSpec(a.shape, lambda i: (0, 0), pipeline_mode=pl.Buffered(1))
    of_layer = lambda a: pl.BlockSpec((None,) + a.shape[1:], lambda i: (layer, 0, 0), pipeline_mode=pl.Buffered(1))
    cb = conv_b.reshape(1, 2 * D_FF)
    in_specs = [pl.BlockSpec((tm, D_MODEL), lambda i: (i, 0)), mod_spec, mod_spec, mod_spec,
                pl.BlockSpec((1, D_MODEL), lambda i: (0, 0)), pl.BlockSpec((1, D_MODEL), lambda i: (0, 0)),
                of_layer(wup_bf), full(conv_w), full(cb), of_layer(wdn_bf)]
    args = [x] + [_mod_arg(m, rows, seq_len, tm, per_token) for m in (shift, scale, gate)] + [
        g_pre.reshape(1, D_MODEL), g_post.reshape(1, D_MODEL), wup_bf, conv_w, cb, wdn_bf]
    halo = 0 if per_token else SAMPLE_PAD
    scratch = [pltpu.VMEM((halo + tm, D_MODEL), BF16), pltpu.VMEM((tm, D_MODEL), F32),
               pltpu.VMEM((2, halo + tm, FF_CHUNK), F32), pltpu.VMEM((2, halo + tm, FF_CHUNK), F32)]
    if per_token:
        batch = rows // seq_len
        zeros = jnp.zeros((batch, seq_len - 1, 2 * D_FF), F32)
        s1 = jnp.concatenate([state[:, 1:2], zeros], axis=1).reshape(rows, 2 * D_FF)
        s2 = jnp.concatenate([state, zeros[:, 1:]], axis=1).reshape(rows, 2 * D_FF)
        in_specs += [pl.BlockSpec((tm, 2 * D_FF), lambda i: (i, 0))] * 2
        args += [s1, s2]
        conv_spec = pl.BlockSpec((tm, 2 * D_FF), lambda i: (i, 0))
        conv_shape = jax.ShapeDtypeStruct((rows, 2 * D_FF), F32)
    else:
        conv_spec = pl.BlockSpec((1, 2, 2 * D_FF), lambda i: (i // tps, 0, 0))
        conv_shape = jax.ShapeDtypeStruct((rows // seq_len, 2, 2 * D_FF), F32)
    return pl.pallas_call(
        functools.partial(_ffn_kernel, per_token, seq_len, tps),
        grid=(rows // tm,),
        in_specs=in_specs,
        out_specs=[pl.BlockSpec((tm, D_MODEL), lambda i: (i, 0)), conv_spec],
        out_shape=[jax.ShapeDtypeStruct((rows, D_MODEL), F32), conv_shape],
        scratch_shapes=scratch,
        compiler_params=_cparams("arbitrary"),
        name="conv_ffn",
    )(*args)


def _ret_tables(chunk, valid):
    log_gamma = jnp.log1p(-jnp.exp2(-5.0 - jnp.arange(H_A, dtype=F32)))
    idx = jnp.arange(chunk, dtype=F32)
    rel = idx[:, None] - idx[None, :]
    intra = jnp.exp(jnp.where(rel[None] >= 0, rel[None] * log_gamma[:, None, None], -jnp.inf))
    dq = jnp.exp((idx + 1.0)[None, :] * log_gamma[:, None])
    dk = jnp.where(idx[None, :] < valid, jnp.exp((valid - 1.0 - idx)[None, :] * log_gamma[:, None]), 0.0)
    dc = jnp.exp(valid * log_gamma)
    rep = lambda t: jnp.broadcast_to(t[..., None], t.shape + (LANES,))
    dc_rows = rep(jnp.repeat(dc, DK_A).reshape(H_A // 2, 2 * DK_A))
    return intra, rep(dq), rep(dk), dc_rows


def _ret_kernel(chunk, n_chunks, seqs, q_ref, k_ref, v_ref, ga_ref, s0_ref, intra_ref, dq_ref, dk_ref, dc_ref,
                o_ref, s_ref, st_ref):
    i = pl.program_id(1)

    @pl.when(i == 0)
    def _():
        st_ref[...] = s0_ref[...]

    lane = lax.broadcasted_iota(jnp.int32, (chunk, LANES), 1)
    for bi in range(seqs):
        for c in range(n_chunks):
            rows = slice(c * chunk, (c + 1) * chunk)
            for pair in range(H_A // 2):
                lanes = slice(pair * LANES, (pair + 1) * LANES)
                qc = q_ref[bi, rows, lanes]
                kc = k_ref[bi, rows, lanes]
                s_pair = st_ref[bi, pair]
                s_bf = s_pair.astype(BF16)
                upd = s_pair * dc_ref[pair]
                for side in range(2):
                    hd = 2 * pair + side
                    mine = (lane >= 64) if side else (lane < 64)
                    hl = slice(hd * DV_A, (hd + 1) * DV_A)
                    qh = jnp.where(mine, qc, jnp.zeros_like(qc))
                    vh = v_ref[bi, rows, hl]
                    sc = _dot_nt(qh, kc) * intra_ref[hd]
                    o = _dot(sc.astype(BF16), vh) + _dot(qh, s_bf) * dq_ref[hd]
                    kh = jnp.where(mine, kc.astype(F32) * dk_ref[hd], 0.0).astype(BF16)
                    upd = upd + _dot_tn(kh, vh)
                    gate = ga_ref[bi, rows, hl].astype(F32)
                    o_ref[bi, rows, hl] = (_rms(o) * (gate * jax.nn.sigmoid(gate))).astype(BF16)
                st_ref[bi, pair] = upd

    @pl.when(i == pl.num_programs(1) - 1)
    def _():
        s_ref[...] = st_ref[...]


def _seqs_per_step(batch, seq_len, blk):
    return min(batch, SHORT_SEQS_PER_STEP) if seq_len == blk else 1


def _retention(q, k, v, ga, s0, chunk, valid):
    b, l, _ = q.shape
    blk = min(l, CHUNKS_PER_STEP * chunk)
    n_chunks = blk // chunk
    seqs = _seqs_per_step(b, l, blk)
    tables = _ret_tables(chunk, valid)
    seq = lambda w: pl.BlockSpec((seqs, blk, w), lambda bi, i: (bi, i, 0))
    const = lambda a: pl.BlockSpec(a.shape, lambda bi, i: (0,) * a.ndim)
    state = pl.BlockSpec((seqs, H_A // 2, 2 * DK_A, DV_A), lambda bi, i: (bi, 0, 0, 0))
    o, s = pl.pallas_call(
        functools.partial(_ret_kernel, chunk, n_chunks, seqs),
        grid=(b // seqs, l // blk),
        in_specs=[seq(256), seq(256), seq(512), seq(512), state] + [const(t) for t in tables],
        out_specs=[seq(512), state],
        out_shape=[jax.ShapeDtypeStruct((b, l, H_A * DV_A), BF16),
                   jax.ShapeDtypeStruct((b, H_A // 2, 2 * DK_A, DV_A), F32)],
        scratch_shapes=[pltpu.VMEM((seqs, H_A // 2, 2 * DK_A, DV_A), F32)],
        compiler_params=_cparams("arbitrary", "arbitrary"),
        name="retention",
    )(q, k, v, ga, s0.reshape(b, H_A // 2, 2 * DK_A, DV_A), *tables)
    return o, s.reshape(b, H_A, DK_A, DV_A)


def _lam_value(lam_ref, lam_init):
    lv = lam_ref[...]
    return (jnp.exp(jnp.sum(lv[0:1] * lv[1:2], axis=-1, keepdims=True))
            - jnp.exp(jnp.sum(lv[2:3] * lv[3:4], axis=-1, keepdims=True)) + lam_init)


def _dattn_kernel(lam_init, tq, tk, q_ref, k_ref, v_ref, lam_ref, g_ref, o_ref, vt_ref, st_ref, m_ref, l_ref, acc_ref):
    qi = pl.program_id(2)

    @pl.when(qi == 0)
    def _():
        vt_ref[...] = v_ref[0].astype(F32).T.astype(BF16)

    q = q_ref[0]
    lane = lax.broadcasted_iota(jnp.int32, q.shape, 1)
    q2 = jnp.concatenate([jnp.where(lane < DH_B, q, jnp.zeros_like(q)),
                          jnp.where(lane >= DH_B, q, jnp.zeros_like(q))], axis=0)
    m_ref[...] = jnp.full_like(m_ref, -jnp.inf)
    l_ref[...] = jnp.zeros_like(l_ref)
    acc_ref[...] = jnp.zeros_like(acc_ref)

    def scores(slot, kb, diag_offset=None):
        start = pl.multiple_of(kb * tk, tk)
        st = _dot_nt(k_ref[0, pl.ds(start, tk), :], q2)
        if diag_offset is not None:
            key = lax.broadcasted_iota(jnp.int32, st.shape, 0) + diag_offset
            qry = jnp.bitwise_and(lax.broadcasted_iota(jnp.int32, st.shape, 1), tq - 1)
            st = jnp.where(key <= qry, st, -jnp.inf)
        st_ref[slot] = st

    def absorb(slot, kb):
        start = pl.multiple_of(kb * tk, tk)
        st = st_ref[slot]
        m_old = m_ref[...]
        m_new = jnp.maximum(m_old, jnp.max(st, axis=0, keepdims=True))
        alpha = jnp.exp2(m_old - m_new)
        pt = jnp.exp2(st - m_new)
        l_ref[...] = alpha * l_ref[...] + jnp.sum(pt, axis=0, keepdims=True)
        acc_ref[...] = alpha * acc_ref[...] + _dot(vt_ref[:, pl.ds(start, tk)], pt.astype(BF16))
        m_ref[...] = m_new

    assert tq == 2 * tk
    first_diag = 2 * qi
    scores(0, first_diag, 0)
    scores(1, first_diag + 1, tk)
    absorb(0, first_diag)

    def body(t, carry):
        scores(0, 2 * t)
        absorb(1, jnp.where(t == 0, first_diag + 1, 2 * t - 1))
        scores(1, 2 * t + 1)
        absorb(0, 2 * t)
        return carry

    lax.fori_loop(0, qi, body, 0)
    absorb(1, jnp.where(qi == 0, first_diag + 1, 2 * qi - 1))

    lam = _lam_value(lam_ref, lam_init)
    inv_l = 1.0 / l_ref[...]
    acc = acc_ref[...] * inv_l
    o = (acc[:, :tq] - lam * acc[:, tq:]).T
    o_ref[0] = ((_rms(o) * g_ref[...]) * (1.0 - lam_init)).astype(BF16)


def _diff_attn_prompt(qb, kb, vb, lam_vecs, g_diff, lam_init):
    b, l, _ = qb.shape
    tq, tk = 1024, 512
    whole = pl.BlockSpec((1, l, LANES), lambda bi, h, qi: (bi, 0, h))
    return pl.pallas_call(
        functools.partial(_dattn_kernel, lam_init, tq, tk),
        grid=(b, H_B, l // tq),
        in_specs=[pl.BlockSpec((1, tq, LANES), lambda bi, h, qi: (bi, qi, h)), whole, whole,
                  pl.BlockSpec(lam_vecs.shape, lambda bi, h, qi: (0, 0)),
                  pl.BlockSpec((1, DV_B), lambda bi, h, qi: (0, 0))],
        out_specs=pl.BlockSpec((1, tq, LANES), lambda bi, h, qi: (bi, qi, h)),
        out_shape=jax.ShapeDtypeStruct((b, l, H_B * DV_B), BF16),
        scratch_shapes=[pltpu.VMEM((DV_B, l), BF16), pltpu.VMEM((2, tk, 2 * tq), F32),
                        pltpu.VMEM((1, 2 * tq), F32), pltpu.VMEM((1, 2 * tq), F32),
                        pltpu.VMEM((DV_B, 2 * tq), F32)],
        compiler_params=_cparams("arbitrary", "arbitrary", "arbitrary"),
        name="diff_attn_prompt",
    )(qb, kb, vb, lam_vecs, g_diff.reshape(1, DV_B))


Q_ROWS = H_B * 2 * 8
KV_ROWS = PAGE_SIZE * H_B


def _dattn_dec_kernel(lam_init, n_pages, pt_ref, q_ref, kn_ref, vn_ref, *refs):
    k_refs = refs[:n_pages]
    v_refs = refs[n_pages:2 * n_pages]
    lam_ref, g_ref, o_ref, m_ref, l_ref, acc_ref = refs[2 * n_pages:]
    step = pl.program_id(1)
    q = q_ref[0]

    @pl.when(step == 0)
    def _():
        s = _dot_nt(q, kn_ref[0])
        r = lax.broadcasted_iota(jnp.int32, s.shape, 0)
        c = lax.broadcasted_iota(jnp.int32, s.shape, 1)
        s = jnp.where((c % H_B == r // 16) & (c // H_B <= r % 8), s, -jnp.inf)
        m = jnp.max(s, axis=-1, keepdims=True)
        p = jnp.exp2(s - m)
        m_ref[...] = m
        l_ref[...] = jnp.sum(p, axis=-1, keepdims=True)
        acc_ref[...] = _dot(p.astype(BF16), vn_ref[0])

    own_head = (lax.broadcasted_iota(jnp.int32, (Q_ROWS, KV_ROWS), 1) % H_B
                == lax.broadcasted_iota(jnp.int32, (Q_ROWS, KV_ROWS), 0) // 16)
    bias = jnp.where(own_head, 0.0, -jnp.inf)
    scores = [_dot_nt(q, kr[0].astype(BF16)) + bias for kr in k_refs]
    m_old = m_ref[...]
    m_new = m_old
    for s in scores:
        m_new = jnp.maximum(m_new, jnp.max(s, axis=-1, keepdims=True))
    alpha = jnp.exp2(m_old - m_new)
    l_new = alpha * l_ref[...]
    acc = alpha * acc_ref[...]
    for s, vr in zip(scores, v_refs):
        p = jnp.exp2(s - m_new)
        l_new = l_new + jnp.sum(p, axis=-1, keepdims=True)
        acc = acc + _dot(p.astype(BF16), vr[0].astype(BF16))
    m_ref[...] = m_new
    l_ref[...] = l_new
    acc_ref[...] = acc

    @pl.when(step == pl.num_programs(1) - 1)
    def _():
        lam = _lam_value(lam_ref, lam_init)
        o = acc / l_new
        for h in range(H_B):
            d = o[16 * h:16 * h + 8] - lam * o[16 * h + 8:16 * h + 16]
            o_ref[0, :, h * DV_B:(h + 1) * DV_B] = ((_rms(d) * g_ref[...]) * (1.0 - lam_init)).astype(BF16)


def _diff_attn_decode(q_rows, k_new, v_new, cache_k, cache_v, page_base, page_table, lam_vecs, g_diff, lam_init):
    b = q_rows.shape[0]
    n_steps = page_table.shape[1] // PAGES_PER_STEP
    per_b = lambda r, w: pl.BlockSpec((1, r, w), lambda bi, s, pt: (bi, 0, 0))

    def page_spec(j):
        return pl.BlockSpec((1, KV_ROWS, LANES),
                            lambda bi, s, pt: (page_base + pt[bi, s * PAGES_PER_STEP + j], 0, 0))

    pages = [page_spec(j) for j in range(PAGES_PER_STEP)]
    grid_spec = pltpu.PrefetchScalarGridSpec(
        num_scalar_prefetch=1,
        grid=(b, n_steps),
        in_specs=[per_b(Q_ROWS, LANES), per_b(DEC_SEQ * H_B, LANES), per_b(DEC_SEQ * H_B, LANES)]
        + pages + pages + [pl.BlockSpec(lam_vecs.shape, lambda bi, s, pt: (0, 0)),
                           pl.BlockSpec((1, DV_B), lambda bi, s, pt: (0, 0))],
        out_specs=per_b(8, H_B * DV_B),
        scratch_shapes=[pltpu.VMEM((Q_ROWS, 1), F32), pltpu.VMEM((Q_ROWS, 1), F32),
                        pltpu.VMEM((Q_ROWS, DV_B), F32)],
    )
    return pl.pallas_call(
        functools.partial(_dattn_dec_kernel, lam_init, PAGES_PER_STEP),
        grid_spec=grid_spec,
        out_shape=jax.ShapeDtypeStruct((b, 8, H_B * DV_B), BF16),
        compiler_params=_cparams("arbitrary", "arbitrary"),
        name="diff_attn_decode",
    )(page_table, q_rows, k_new, v_new, *([cache_k] * PAGES_PER_STEP), *([cache_v] * PAGES_PER_STEP),
      lam_vecs, g_diff.reshape(1, DV_B))


def _decode_q_rows(qb):
    b = qb.shape[0]
    q = qb.reshape(b, DEC_SEQ, H_B, DV_B).transpose(0, 2, 1, 3)
    q = jnp.pad(q, ((0, 0), (0, 0), (0, 8 - DEC_SEQ), (0, 0)))[:, :, None]
    keep = (np.arange(DV_B)[None, :] // DH_B) == np.arange(2)[:, None]
    keep = jnp.asarray(keep)[None, None, :, None, :]
    return jnp.where(keep, q, jnp.zeros_like(q)).reshape(b, Q_ROWS, DV_B)


def _gla_tables(chunk):
    idx = np.arange(chunk)
    i, t = idx[:, None], idx[None, :]
    mats = [t <= i]
    masks = [i == t]
    s = 1
    while s < chunk:
        same = (t // s) == (i // s)
        odd = ((i // s) % 2) == 1
        if 2 * s <= GLA_MATMUL_LEVEL_SPAN:
            mats.append(same & np.where(odd, t <= i, t > i))
        masks.append(((i // (2 * s)) == (t // (2 * s))) & odd & (((t // s) % 2) == 0))
        s *= 2
    return (jnp.asarray(np.concatenate(mats, axis=0).astype(np.float32), BF16),
            jnp.asarray(np.stack(masks).astype(np.float32)))


def _gla_kernel(chunk, n_chunks, seqs, valid, q_ref, k_ref, g_ref, v_ref, r_ref, s0_ref, gn_ref, m_ref, p_ref,
                o_ref, s_ref, st_ref):
    i = pl.program_id(1)

    @pl.when(i == 0)
    def _():
        for bi in range(seqs):
            for h in range(H_C):
                st_ref[bi, h] = s0_ref[bi, h].T

    n_levels = p_ref.shape[0] - 1
    m_all = m_ref[...]
    live = lax.broadcasted_iota(jnp.int32, (chunk, 1), 0) < valid
    for bi, c in itertools.product(range(seqs), range(n_chunks)):
        rows = slice(c * chunk, (c + 1) * chunk)
        g = g_ref[bi, rows,:]
        if valid < chunk:
            g = jnp.where(live, g, 0.0)
        g_hi = g.astype(BF16)
        g_lo = (g - g_hi.astype(F32)).astype(BF16)
        sums = _dot(m_all, g_hi) + _dot(m_all, g_lo)
        cum = sums[0:chunk]
        exponents = [cum, cum[chunk - 1:chunk] - cum]
        n_mats = m_all.shape[0] // chunk
        for lv in range(n_levels):
            if lv + 1 < n_mats:
                exponents.append(sums[(1 + lv) * chunk:(2 + lv) * chunk])
            else:
                s = 2 ** lv
                pairs = cum.reshape(chunk // (2 * s), 2 * s, cum.shape[-1])
                rel = pairs - pairs[:, s - 1:s, :]
                upper = lax.broadcasted_iota(jnp.int32, rel.shape, 1) >= s
                exponents.append(jnp.where(upper, rel, -rel).reshape(cum.shape))
        e_all = jnp.exp(jnp.concatenate(exponents, axis=0))
        heads = []
        for h in range(H_C):
            kl = slice(h * DK_C, (h + 1) * DK_C)
            e = e_all[:, kl]
            q = q_ref[bi, rows,kl]
            k = k_ref[bi, rows,kl]
            v = v_ref[bi, rows,h * DV_C:(h + 1) * DV_C]
            if valid < chunk:
                k = jnp.where(live, k, jnp.zeros_like(k))
                v = jnp.where(live, v, jnp.zeros_like(v))
            qf = q.astype(F32)
            kf = k.astype(F32)
            level = lambda lv: e[(2 + lv) * chunk:(3 + lv) * chunk]
            prods = [_dot_nt(q, k)] + [_dot_nt((qf * level(lv)).astype(BF16), (kf * level(lv)).astype(BF16))
                                       for lv in range(n_levels)]
            heads.append(dict(v=v, prods=prods, q_in=(qf * e[0:chunk]).astype(BF16),
                              k_out=(kf * e[chunk:2 * chunk]).astype(BF16), decay=e[chunk - 1:chunk]))
        for hd in heads:
            att = hd["prods"][0] * p_ref[0]
            for lv in range(n_levels):
                att = att + hd["prods"][1 + lv] * p_ref[1 + lv]
            hd["intra"] = _dot(att.astype(BF16), hd["v"])
            hd["update"] = _dot_tn(hd["v"], hd["k_out"])
        for h, hd in enumerate(heads):
            vl = slice(h * DV_C, (h + 1) * DV_C)
            st = st_ref[bi, h]
            o = hd["intra"] + _dot_nt(hd["q_in"], st.astype(BF16))
            st_ref[bi, h] = st * hd["decay"] + hd["update"]
            gate = r_ref[bi, rows,vl].astype(F32)
            o_ref[bi, rows,vl] = ((_rms(o) * gn_ref[...]) * (gate * jax.nn.sigmoid(gate))).astype(BF16)

    @pl.when(i == pl.num_programs(1) - 1)
    def _():
        for bi in range(seqs):
            for h in range(H_C):
                s_ref[bi, h] = st_ref[bi, h].T


def _gla(q, k, g, v, r, s0, g_norm, chunk, valid):
    b, l, _ = q.shape
    blk = min(l, CHUNKS_PER_STEP * chunk)
    n_chunks = blk // chunk
    seqs = _seqs_per_step(b, l, blk)
    m_all, masks = _gla_tables(chunk)
    qk = pl.BlockSpec((seqs, blk, H_C * DK_C), lambda bi, i: (bi, i, 0))
    vr = pl.BlockSpec((seqs, blk, H_C * DV_C), lambda bi, i: (bi, i, 0))
    state = pl.BlockSpec((seqs, H_C, DK_C, DV_C), lambda bi, i: (bi, 0, 0, 0))
    const = lambda a: pl.BlockSpec(a.shape, lambda bi, i: (0,) * a.ndim)
    gn = g_norm.reshape(1, DV_C)
    return pl.pallas_call(
        functools.partial(_gla_kernel, chunk, n_chunks, seqs, valid),
        grid=(b // seqs, l // blk),
        in_specs=[qk, qk, qk, vr, vr, state, const(gn), const(m_all), const(masks)],
        out_specs=[vr, state],
        out_shape=[jax.ShapeDtypeStruct((b, l, H_C * DV_C), BF16),
                   jax.ShapeDtypeStruct((b, H_C, DK_C, DV_C), F32)],
        scratch_shapes=[pltpu.VMEM((seqs, H_C, DV_C, DK_C), F32)],
        compiler_params=_cparams("arbitrary", "arbitrary"),
        name="gla",
    )(q, k, g, v, r, s0, gn, m_all, masks)


def _rope_tables(pos):
    inv = ROPE_THETA ** (-jnp.arange(0, DK_A, 2, dtype=F32) / DK_A)
    ang = pos.astype(F32)[:, None] * inv[None, :]
    cos = jnp.tile(jnp.cos(ang), (1, 4))
    sin = jnp.tile(jnp.concatenate([-jnp.sin(ang), jnp.sin(ang)], axis=1), (1, 2))
    return cos, sin


def _pad_seq(t, batch, seq_len):
    t = t.reshape(batch, seq_len, t.shape[-1])
    return jnp.pad(t, ((0, 0), (0, SAMPLE_PAD - seq_len), (0, 0)))


def _unpad_seq(t, seq_len):
    return t[:, :seq_len].reshape(t.shape[0] * seq_len, t.shape[-1])


def kernel(x_prompt, x_sample, cache_k, cache_v, state_ret, state_gla, state_conv, page_table, c_prompt, c_sample, w_ada, b_ada, g_pre_mix, g_post_mix, g_pre_ffn, g_post_ffn, w_in_ab, w_out_ab, lam_q1, lam_k1, lam_q2, lam_k2, g_diff, w_in_c, w_gate_c, b_gate_c, g_gla, w_out_c, w_up, conv_w, conv_b, w_down):
    rows_p, rows_s = BATCH * SEQ, DEC_BATCH * DEC_SEQ
    xp = x_prompt.reshape(rows_p, D_MODEL)
    xs = x_sample.reshape(rows_s, D_MODEL)

    n_c = BATCH + DEC_BATCH
    c_all = jnp.pad(jnp.concatenate([c_prompt, c_sample], axis=0), ((0, (-n_c) % 8), (0, 0)))
    ada = _ada(c_all, w_ada, b_ada)

    cos_p, sin_p = _rope_tables(jnp.arange(SEQ))
    cos_s, sin_s = _rope_tables(jnp.tile(PAST_LEN + jnp.arange(DEC_SEQ), DEC_BATCH))

    wup = w_up.astype(BF16)
    wdn = w_down.astype(BF16)
    k_p, v_p, k_s, v_s, ret_p, ret_s, gla_p, gla_s, conv_p, conv_s = ([] for _ in range(10))
    for l in range(DEPTH):
        mp = jnp.split(ada[l, :BATCH], 6, axis=-1)
        ms = jnp.split(ada[l, BATCH:n_c], 6, axis=-1)
        if l % 2 == 0:
            e = l // 2
            lam_init = 0.8 - 0.6 * math.exp(-0.3 * l)
            lam_vecs = jnp.stack([lam_q1[e], lam_k1[e], lam_q2[e], lam_k2[e]]).astype(F32)
            w_in = w_in_ab[e].astype(BF16)
            w_out = w_out_ab[e].astype(BF16)
            w_parts = [w_out[:H_A * DV_A], w_out[H_A * DV_A:]]

            qa, ka, va, ga, qb, kb32, kb16, vb32, vb16 = _in_proj_ab(
                xp, mp[0], mp[1], g_pre_mix[l], w_in, cos_p, sin_p, SEQ)
            seq3 = lambda t: t.reshape(BATCH, SEQ, t.shape[-1])
            ret, sp = _retention(seq3(qa), seq3(ka), seq3(va), seq3(ga),
                                 jnp.zeros((BATCH, H_A, DK_A, DV_A), F32), CHUNK_RET, CHUNK_RET)
            dif = _diff_attn_prompt(seq3(qb), seq3(kb16), seq3(vb16), lam_vecs, g_diff[e], lam_init)
            xp = _out_proj([ret.reshape(rows_p, -1), dif.reshape(rows_p, -1)], w_parts, xp, mp[2],
                           g_post_mix[l], SEQ)
            k_p.append(kb32.reshape(BATCH, SEQ, H_B, 2 * DH_B))
            v_p.append(vb32.reshape(BATCH, SEQ, H_B, DV_B))
            ret_p.append(sp)

            qa, ka, va, ga, qb, kb32, kb16, vb32, vb16 = _in_proj_ab(
                xs, ms[0], ms[1], g_pre_mix[l], w_in, cos_s, sin_s, DEC_SEQ)
            pad = lambda t: _pad_seq(t, DEC_BATCH, DEC_SEQ)
            ret, ss = _retention(pad(qa), pad(ka), pad(va), pad(ga), state_ret[e], SAMPLE_PAD, DEC_SEQ)
            n_phys = cache_k.shape[1]
            kv_rows = lambda t: t.reshape(DEC_BATCH, DEC_SEQ * H_B, DV_B)
            dif = _diff_attn_decode(
                _decode_q_rows(qb.reshape(DEC_BATCH, DEC_SEQ, 512)), kv_rows(kb16), kv_rows(vb16),
                cache_k.reshape(N_EVEN * n_phys, KV_ROWS, LANES), cache_v.reshape(N_EVEN * n_phys, KV_ROWS, LANES),
                e * n_phys, page_table, lam_vecs, g_diff[e], lam_init)
            xs = _out_proj([_unpad_seq(ret, DEC_SEQ), _unpad_seq(dif, DEC_SEQ)], w_parts, xs, ms[2],
                           g_post_mix[l], DEC_SEQ)
            k_s.append(kb32.reshape(DEC_BATCH, DEC_SEQ, H_B, 2 * DH_B))
            v_s.append(vb32.reshape(DEC_BATCH, DEC_SEQ, H_B, DV_B))
            ret_s.append(ss)
        else:
            o = l // 2
            n_main = sum(C_WIDTHS_MAIN)
            w_in = w_in_c[o][:, :n_main].astype(BF16)
            w_a = jnp.pad(w_in_c[o][:, n_main:], ((0, 0), (0, LANES - GATE_RANK))).astype(BF16)
            w_gate = jnp.pad(w_gate_c[o], ((0, LANES - GATE_RANK), (0, 0))).astype(BF16)
            w_out = w_out_c[o].astype(BF16)

            q, k, v, r, lg = _in_proj_c(xp, mp[0], mp[1], g_pre_mix[l], w_in, w_a, w_gate, b_gate_c[o], SEQ)
            seq3 = lambda t: t.reshape(BATCH, SEQ, t.shape[-1])
            og, sp = _gla(seq3(q), seq3(k), seq3(lg), seq3(v), seq3(r),
                          jnp.zeros((BATCH, H_C, DK_C, DV_C), F32), g_gla[o], CHUNK_GLA, CHUNK_GLA)
            xp = _out_proj([og.reshape(rows_p, -1)], [w_out], xp, mp[2], g_post_mix[l], SEQ)
            gla_p.append(sp)

            q, k, v, r, lg = _in_proj_c(xs, ms[0], ms[1], g_pre_mix[l], w_in, w_a, w_gate, b_gate_c[o], DEC_SEQ)
            pad = lambda t: _pad_seq(t, DEC_BATCH, DEC_SEQ)
            og, ss = _gla(pad(q), pad(k), pad(lg), pad(v), pad(r), state_gla[o], g_gla[o], SAMPLE_PAD, DEC_SEQ)
            xs = _out_proj([_unpad_seq(og, DEC_SEQ)], [w_out], xs, ms[2], g_post_mix[l], DEC_SEQ)
            gla_s.append(ss)

        xp, cbp = _conv_ffn(xp, mp[3], mp[4], mp[5], g_pre_ffn[l], g_post_ffn[l], l, wup, conv_w[l], conv_b[l],
                            wdn, SEQ)
        xs, up_s = _conv_ffn(xs, ms[3], ms[4], ms[5], g_pre_ffn[l], g_post_ffn[l], l, wup, conv_w[l], conv_b[l],
                             wdn, DEC_SEQ, state=state_conv[l])
        conv_p.append(cbp)
        conv_s.append(up_s.reshape(DEC_BATCH, DEC_SEQ, 2 * D_FF)[:, DEC_SEQ - (CONV_W - 1):])

    stack = lambda ts: ts[0][None] if len(ts) == 1 else jnp.stack(ts)
    return (xp.reshape(BATCH, SEQ, D_MODEL), xs.reshape(DEC_BATCH, DEC_SEQ, D_MODEL),
            stack(k_p), stack(v_p), stack(k_s), stack(v_s),
            stack(ret_p), stack(ret_s), stack(gla_p), stack(gla_s),
            stack(conv_p), stack(conv_s))
```

```python
import functools
import itertools
import math

import numpy as np
import jax
import jax.numpy as jnp
from jax import lax
from jax.experimental import pallas as pl
from jax.experimental.pallas import tpu as pltpu

D_MODEL = 1024
BATCH = 4
SEQ = 4096
DEPTH = 2
DEC_BATCH = 32
DEC_SEQ = 4
PAST_LEN = 8192
PAGE_SIZE = 128
N_EVEN = (DEPTH + 1) // 2
N_ODD = DEPTH // 2
H_A = 4
DK_A = D_MODEL // 16
DV_A = D_MODEL // 8
CHUNK_RET = 128
H_B = 4
DH_B = D_MODEL // 16
DV_B = 2 * DH_B
H_C = 4
DK_C = D_MODEL // 8
DV_C = D_MODEL // 4
GATE_RANK = 16
GATE_TAU = 16.0
C_WIDTHS_MAIN = (H_C * DK_C, H_C * DK_C, H_C * DV_C, H_C * DV_C)
CHUNK_GLA = 64
D_FF = ((8 * D_MODEL // 3 + 127) // 128) * 128
CONV_W = 3
ROPE_THETA = 10000.0
EPS = 1e-6

LANES = 128
VMEM_LIMIT = 56 * 1024 * 1024
ROW_TILE = 1024
FFN_ROW_TILE = 1024
GLA_MATMUL_LEVEL_SPAN = 8
CHUNKS_PER_STEP = 8
SHORT_SEQS_PER_STEP = 8
SAMPLE_PAD = 16
FF_CHUNK = 512
PAGES_PER_STEP = 32
LOG2E = math.log2(math.e)
BF16 = jnp.bfloat16
F32 = jnp.float32


def _cparams(*sem):
    return pltpu.CompilerParams(dimension_semantics=sem, vmem_limit_bytes=VMEM_LIMIT)


def _dot(a, b):
    return jnp.dot(a, b, preferred_element_type=F32)


def _dot_nt(a, b):
    return lax.dot_general(a, b, (((1,), (1,)), ((), ())), preferred_element_type=F32)


def _dot_tn(a, b):
    return lax.dot_general(a, b, (((0,), (0,)), ((), ())), preferred_element_type=F32)


def _rms(x):
    return x * lax.rsqrt(jnp.mean(x * x, axis=-1, keepdims=True) + EPS)


def _modulate(x, g, shift, scale):
    return (_rms(x) * g) * (1.0 + scale) + shift


def _rope(x, cos, sin_signed):
    w = x.shape[-1]
    fwd = pltpu.roll(x, 32, axis=1)
    bwd = pltpu.roll(x, w - 32, axis=1)
    reps = w // LANES
    first_half = (lax.broadcasted_iota(jnp.int32, x.shape, 1) % 64) < 32
    partner = jnp.where(first_half, bwd, fwd)
    return (x * jnp.concatenate([cos] * reps, axis=1)
            + partner * jnp.concatenate([sin_signed] * reps, axis=1))


def _ada_kernel(c_ref, w_ref, b_ref, o_ref):
    c = c_ref[...]
    s = c * jax.nn.sigmoid(c)
    o_ref[0] = jnp.dot(s, w_ref[0], preferred_element_type=F32, precision=lax.Precision.HIGHEST) + b_ref[0]


def _ada(c_all, w_ada, b_ada):
    rows = c_all.shape[0]
    tn = 1536
    return pl.pallas_call(
        _ada_kernel,
        grid=(DEPTH, 6 * D_MODEL // tn),
        in_specs=[pl.BlockSpec((rows, D_MODEL), lambda l, j: (0, 0)),
                  pl.BlockSpec((1, D_MODEL, tn), lambda l, j: (l, 0, j)),
                  pl.BlockSpec((1, 1, tn), lambda l, j: (l, 0, j))],
        out_specs=pl.BlockSpec((1, rows, tn), lambda l, j: (l, 0, j)),
        out_shape=jax.ShapeDtypeStruct((DEPTH, rows, 6 * D_MODEL), F32),
        compiler_params=_cparams("arbitrary", "arbitrary"),
        name="adaln",
    )(c_all, w_ada, b_ada.reshape(DEPTH, 1, 6 * D_MODEL))


def _row_cfg(rows, seq_len, row_tile=ROW_TILE):
    tm = min(row_tile, rows)
    per_token = seq_len < tm
    tiles_per_seq = 1 if per_token else seq_len // tm
    mod_spec = (pl.BlockSpec((1, tm, D_MODEL), lambda i: (i, 0, 0)) if per_token
                else pl.BlockSpec((1, 1, D_MODEL), lambda i: (i // tiles_per_seq, 0, 0)))
    return tm, per_token, tiles_per_seq, mod_spec


def _mod_arg(m, rows, seq_len, tm, per_token):
    if per_token:
        return jnp.repeat(m, seq_len, axis=0).reshape(rows // tm, tm, D_MODEL)
    return m.reshape(m.shape[0], 1, D_MODEL)


def _inab_kernel(x_ref, sh_ref, sc_ref, g_ref, w_ref, cos_ref, sin_ref,
                 qa_ref, ka_ref, va_ref, ga_ref, qb_ref, kb32_ref, kb16_ref, vb32_ref, vb16_ref):
    h = _modulate(x_ref[...], g_ref[...], sh_ref[0], sc_ref[0]).astype(BF16)
    cos = cos_ref[...]
    sin = sin_ref[...]

    def proj(lo, width):
        return _dot(h, w_ref[:, lo:lo + width])

    qa_ref[...] = _rope(proj(0, 256), cos, sin).astype(BF16)
    ka_ref[...] = (_rope(proj(256, 256), cos, sin) * (DK_A ** -0.5)).astype(BF16)
    va_ref[...] = proj(512, 512).astype(BF16)
    ga_ref[...] = proj(1024, 512).astype(BF16)
    qb_ref[...] = (_rope(proj(1536, 512), cos, sin) * (DH_B ** -0.5 * LOG2E)).astype(BF16)
    kb = _rope(proj(2048, 512), cos, sin)
    kb16_ref[...] = kb.astype(BF16)
    vb = proj(2560, 512)
    vb16_ref[...] = vb.astype(BF16)
    tm = kb.shape[0]
    for h in range(H_B):
        kb32_ref[pl.ds(h, tm, stride=H_B), :] = kb[:, h * DV_B:(h + 1) * DV_B]
        vb32_ref[pl.ds(h, tm, stride=H_B), :] = vb[:, h * DV_B:(h + 1) * DV_B]


def _in_proj_ab(x, shift, scale, g_pre, w_bf, cos, sin, seq_len):
    rows = x.shape[0]
    tm, per_token, tps, mod_spec = _row_cfg(rows, seq_len)
    pos_map = (lambda i: (i, 0)) if per_token else (lambda i: (i % tps, 0))
    row = lambda w: pl.BlockSpec((tm, w), lambda i: (i, 0))
    widths = (256, 256, 512, 512, 512, 512, 512, 512, 512)
    dtypes = (BF16, BF16, BF16, BF16, BF16, F32, BF16, F32, BF16)
    cache_rows = lambda w, d: d == F32
    out_spec = lambda w, d: (pl.BlockSpec((tm * H_B, DV_B), lambda i: (i, 0)) if cache_rows(w, d) else row(w))
    out_shape = lambda w, d: jax.ShapeDtypeStruct((rows * H_B, DV_B) if cache_rows(w, d) else (rows, w), d)
    return pl.pallas_call(
        _inab_kernel,
        grid=(rows // tm,),
        in_specs=[row(D_MODEL), mod_spec, mod_spec,
                  pl.BlockSpec((1, D_MODEL), lambda i: (0, 0)),
                  pl.BlockSpec(w_bf.shape, lambda i: (0, 0)),
                  pl.BlockSpec((tm, LANES), pos_map), pl.BlockSpec((tm, LANES), pos_map)],
        out_specs=[out_spec(w, d) for w, d in zip(widths, dtypes)],
        out_shape=[out_shape(w, d) for w, d in zip(widths, dtypes)],
        compiler_params=_cparams("arbitrary"),
        name="in_proj_ab",
    )(x, _mod_arg(shift, rows, seq_len, tm, per_token), _mod_arg(scale, rows, seq_len, tm, per_token),
      g_pre.reshape(1, D_MODEL), w_bf, cos, sin)


def _inc_kernel(x_ref, sh_ref, sc_ref, g_ref, w_ref, wa_ref, wg_ref, bg_ref,
                q_ref, k_ref, v_ref, r_ref, lg_ref):
    h = _modulate(x_ref[...], g_ref[...], sh_ref[0], sc_ref[0]).astype(BF16)

    def proj(lo, width):
        return _dot(h, w_ref[:, lo:lo + width])

    q_ref[...] = (proj(0, 512) * (DK_C ** -0.5)).astype(BF16)
    k_ref[...] = proj(512, 512).astype(BF16)
    v_ref[...] = proj(1024, 1024).astype(BF16)
    r_ref[...] = proj(2048, 1024).astype(BF16)
    a = _dot(h, wa_ref[...])
    z = _dot(a.astype(BF16), wg_ref[...]) + bg_ref[...]
    softplus_neg = jnp.maximum(-z, 0.0) + jnp.log1p(jnp.exp(-jnp.abs(z)))
    lg_ref[...] = -softplus_neg / GATE_TAU


def _in_proj_c(x, shift, scale, g_pre, w_bf, wa_bf, wg_bf, b_gate, seq_len):
    rows = x.shape[0]
    tm, per_token, tps, mod_spec = _row_cfg(rows, seq_len)
    row = lambda w: pl.BlockSpec((tm, w), lambda i: (i, 0))
    widths = (512, 512, 1024, 1024, 512)
    dtypes = (BF16, BF16, BF16, BF16, F32)
    full = lambda a: pl.BlockSpec(a.shape, lambda i: (0, 0))
    bg = b_gate.reshape(1, -1)
    return pl.pallas_call(
        _inc_kernel,
        grid=(rows // tm,),
        in_specs=[row(D_MODEL), mod_spec, mod_spec, pl.BlockSpec((1, D_MODEL), lambda i: (0, 0)),
                  full(w_bf), full(wa_bf), full(wg_bf), full(bg)],
        out_specs=[row(w) for w in widths],
        out_shape=[jax.ShapeDtypeStruct((rows, w), d) for w, d in zip(widths, dtypes)],
        compiler_params=_cparams("arbitrary"),
        name="in_proj_c",
    )(x, _mod_arg(shift, rows, seq_len, tm, per_token), _mod_arg(scale, rows, seq_len, tm, per_token),
      g_pre.reshape(1, D_MODEL), w_bf, wa_bf, wg_bf, bg)


def _outproj_kernel(n_parts, *refs):
    parts = refs[:n_parts]
    ws = refs[n_parts:2 * n_parts]
    x_ref, gate_ref, g_ref, o_ref = refs[2 * n_parts:]
    y = _dot(parts[0][...], ws[0][...])
    for p, w in zip(parts[1:], ws[1:]):
        y = y + _dot(p[...], w[...])
    o_ref[...] = x_ref[...] + gate_ref[0] * (_rms(y) * g_ref[...])


def _out_proj(parts, weights, x, gate, g_post, seq_len):
    rows = x.shape[0]
    tm, per_token, tps, mod_spec = _row_cfg(rows, seq_len)
    n = len(parts)
    return pl.pallas_call(
        functools.partial(_outproj_kernel, n),
        grid=(rows // tm,),
        in_specs=([pl.BlockSpec((tm, p.shape[1]), lambda i: (i, 0)) for p in parts]
                  + [pl.BlockSpec(w.shape, lambda i: (0, 0)) for w in weights]
                  + [pl.BlockSpec((tm, D_MODEL), lambda i: (i, 0)), mod_spec,
                     pl.BlockSpec((1, D_MODEL), lambda i: (0, 0))]),
        out_specs=pl.BlockSpec((tm, D_MODEL), lambda i: (i, 0)),
        out_shape=jax.ShapeDtypeStruct((rows, D_MODEL), F32),
        compiler_params=_cparams("arbitrary"),
        name="out_proj",
    )(*parts, *weights, x, _mod_arg(gate, rows, seq_len, tm, per_token), g_post.reshape(1, D_MODEL))


def _ffn_kernel(per_token, seq_len, tiles_per_seq, *refs):
    if per_token:
        (x_ref, sh_ref, sc_ref, gate_ref, gpre_ref, gpost_ref, wup_ref, cw_ref, cb_ref, wdn_ref,
         s1_ref, s2_ref, xo_ref, conv_ref, h_ref, *u_refs) = refs
    else:
        (x_ref, sh_ref, sc_ref, gate_ref, gpre_ref, gpost_ref, wup_ref, cw_ref, cb_ref, wdn_ref,
         xo_ref, conv_ref, h_ref, *u_refs) = refs
    tm = x_ref.shape[0]
    halo = h_ref.shape[0] - tm
    x = x_ref[...]
    if per_token:
        tau = lax.broadcasted_iota(jnp.int32, (tm, 1), 0) % seq_len
    else:
        seq_start = (pl.program_id(0) % tiles_per_seq) == 0

        @pl.when(seq_start)
        def _():
            h_ref[:halo] = jnp.zeros((halo, D_MODEL), BF16)

        @pl.when(jnp.logical_not(seq_start))
        def _():
            h_ref[:halo] = h_ref[tm:]
    h_ref[halo:] = _modulate(x, gpre_ref[...], sh_ref[0], sc_ref[0]).astype(BF16)
    def conv(slot, half, lo, width):
        cols = slice(lo, lo + width)
        cw = cw_ref[:, cols]
        cb = cb_ref[:, cols]
        taps = lambda u2, u1, u0: cb + cw[0:1] * u2 + cw[1:2] * u1 + cw[2:3] * u0
        if per_token:
            u = u_refs[slot][half, :, :width]
            u1 = jnp.where(tau == 0, s1_ref[:, cols], pltpu.roll(u, 1, axis=0))
            u2 = jnp.where(tau < 2, s2_ref[:, cols], pltpu.roll(u, 2, axis=0))
            conv_ref[:, cols] = u
            return taps(u2, u1, u)
        conv_ref[0, :, cols] = u_refs[slot][half, halo + tm - 2:, :width]
        window = lambda back: u_refs[slot][half, halo - back:halo - back + tm, :width]
        return taps(window(2), window(1), window(0))

    def produce(slot, lo, width):
        for half in range(2):
            u_refs[slot][half, :, :width] = _dot(h_ref[...], wup_ref[:, half * D_FF + lo:half * D_FF + lo + width])

    def consume(slot, lo, width, first):
        g = (jax.nn.gelu(conv(slot, 0, lo, width)) * conv(slot, 1, lo + D_FF, width)).astype(BF16)
        y = _dot(g, wdn_ref[lo:lo + width, :])
        if first:
            xo_ref[...] = y
        else:
            xo_ref[...] += y

    chunks = [(lo, min(FF_CHUNK, D_FF - lo)) for lo in range(0, D_FF, FF_CHUNK)]
    produce(0, *chunks[0])
    for c, (lo, width) in enumerate(chunks):
        if c + 1 < len(chunks):
            produce((c + 1) % 2, *chunks[c + 1])
        consume(c % 2, lo, width, first=(c == 0))
    xo_ref[...] = x + gate_ref[0] * (_rms(xo_ref[...]) * gpost_ref[...])


def _conv_ffn(x, shift, scale, gate, g_pre, g_post, layer, wup_bf, conv_w, conv_b, wdn_bf, seq_len, state=None):
    rows = x.shape[0]
    tm, per_token, tps, mod_spec = _row_cfg(rows, seq_len, FFN_ROW_TILE)
    full = lambda a: pl.BlockSpec(a.shape, lambda i: (0, 0), pipeline_mode=pl.Buffered(1))
    of_layer = lambda a: pl.BlockSpec((None,) + a.shape[1:], lambda i: (layer, 0, 0), pipeline_mode=pl.Buffered(1))
    cb = conv_b.reshape(1, 2 * D_FF)
    in_specs = [pl.BlockSpec((tm, D_MODEL), lambda i: (i, 0)), mod_spec, mod_spec, mod_spec,
                pl.BlockSpec((1, D_MODEL), lambda i: (0, 0)), pl.BlockSpec((1, D_MODEL), lambda i: (0, 0)),
                of_layer(wup_bf), full(conv_w), full(cb), of_layer(wdn_bf)]
    args = [x] + [_mod_arg(m, rows, seq_len, tm, per_token) for m in (shift, scale, gate)] + [
        g_pre.reshape(1, D_MODEL), g_post.reshape(1, D_MODEL), wup_bf, conv_w, cb, wdn_bf]
    halo = 0 if per_token else SAMPLE_PAD
    scratch = [pltpu.VMEM((halo + tm, D_MODEL), BF16),
               pltpu.VMEM((2, halo + tm, FF_CHUNK), F32), pltpu.VMEM((2, halo + tm, FF_CHUNK), F32)]
    if per_token:
        batch = rows // seq_len
        zeros = jnp.zeros((batch, seq_len - 1, 2 * D_FF), F32)
        s1 = jnp.concatenate([state[:, 1:2], zeros], axis=1).reshape(rows, 2 * D_FF)
        s2 = jnp.concatenate([state, zeros[:, 1:]], axis=1).reshape(rows, 2 * D_FF)
        in_specs += [pl.BlockSpec((tm, 2 * D_FF), lambda i: (i, 0))] * 2
        args += [s1, s2]
        conv_spec = pl.BlockSpec((tm, 2 * D_FF), lambda i: (i, 0))
        conv_shape = jax.ShapeDtypeStruct((rows, 2 * D_FF), F32)
    else:
        conv_spec = pl.BlockSpec((1, 2, 2 * D_FF), lambda i: (i // tps, 0, 0))
        conv_shape = jax.ShapeDtypeStruct((rows // seq_len, 2, 2 * D_FF), F32)
    return pl.pallas_call(
        functools.partial(_ffn_kernel, per_token, seq_len, tps),
        grid=(rows // tm,),
        in_specs=in_specs,
        out_specs=[pl.BlockSpec((tm, D_MODEL), lambda i: (i, 0)), conv_spec],
        out_shape=[jax.ShapeDtypeStruct((rows, D_MODEL), F32), conv_shape],
        scratch_shapes=scratch,
        compiler_params=_cparams("arbitrary"),
        name="conv_ffn",
    )(*args)


def _ret_tables(chunk, valid):
    log_gamma = jnp.log1p(-jnp.exp2(-5.0 - jnp.arange(H_A, dtype=F32)))
    idx = jnp.arange(chunk, dtype=F32)
    rel = idx[:, None] - idx[None, :]
    intra = jnp.exp(jnp.where(rel[None] >= 0, rel[None] * log_gamma[:, None, None], -jnp.inf))
    dq = jnp.exp((idx + 1.0)[None, :] * log_gamma[:, None])
    dk = jnp.where(idx[None, :] < valid, jnp.exp((valid - 1.0 - idx)[None, :] * log_gamma[:, None]), 0.0)
    dc = jnp.exp(valid * log_gamma)
    rep = lambda t: jnp.broadcast_to(t[..., None], t.shape + (LANES,))
    dc_rows = rep(jnp.repeat(dc, DK_A).reshape(H_A // 2, 2 * DK_A))
    return intra, rep(dq), rep(dk), dc_rows


def _ret_kernel(chunk, n_chunks, seqs, q_ref, k_ref, v_ref, ga_ref, s0_ref, intra_ref, dq_ref, dk_ref, dc_ref,
                o_ref, s_ref, st_ref):
    i = pl.program_id(1)

    @pl.when(i == 0)
    def _():
        st_ref[...] = s0_ref[...]

    lane = lax.broadcasted_iota(jnp.int32, (chunk, LANES), 1)
    for bi in range(seqs):
        for c in range(n_chunks):
            rows = slice(c * chunk, (c + 1) * chunk)
            for pair in range(H_A // 2):
                lanes = slice(pair * LANES, (pair + 1) * LANES)
                qc = q_ref[bi, rows, lanes]
                kc = k_ref[bi, rows, lanes]
                s_pair = st_ref[bi, pair]
                s_bf = s_pair.astype(BF16)
                upd = s_pair * dc_ref[pair]
                for side in range(2):
                    hd = 2 * pair + side
                    mine = (lane >= 64) if side else (lane < 64)
                    hl = slice(hd * DV_A, (hd + 1) * DV_A)
                    qh = jnp.where(mine, qc, jnp.zeros_like(qc))
                    vh = v_ref[bi, rows, hl]
                    sc = _dot_nt(qh, kc) * intra_ref[hd]
                    o = _dot(sc.astype(BF16), vh) + _dot(qh, s_bf) * dq_ref[hd]
                    kh = jnp.where(mine, kc.astype(F32) * dk_ref[hd], 0.0).astype(BF16)
                    upd = upd + _dot_tn(kh, vh)
                    gate = ga_ref[bi, rows, hl].astype(F32)
                    o_ref[bi, rows, hl] = (_rms(o) * (gate * jax.nn.sigmoid(gate))).astype(BF16)
                st_ref[bi, pair] = upd

    @pl.when(i == pl.num_programs(1) - 1)
    def _():
        s_ref[...] = st_ref[...]


def _seqs_per_step(batch, seq_len, blk):
    return min(batch, SHORT_SEQS_PER_STEP) if seq_len == blk else 1


def _retention(q, k, v, ga, s0, chunk, valid):
    b, l, _ = q.shape
    blk = min(l, CHUNKS_PER_STEP * chunk)
    n_chunks = blk // chunk
    seqs = _seqs_per_step(b, l, blk)
    tables = _ret_tables(chunk, valid)
    seq = lambda w: pl.BlockSpec((seqs, blk, w), lambda bi, i: (bi, i, 0))
    const = lambda a: pl.BlockSpec(a.shape, lambda bi, i: (0,) * a.ndim)
    state = pl.BlockSpec((seqs, H_A // 2, 2 * DK_A, DV_A), lambda bi, i: (bi, 0, 0, 0))
    o, s = pl.pallas_call(
        functools.partial(_ret_kernel, chunk, n_chunks, seqs),
        grid=(b // seqs, l // blk),
        in_specs=[seq(256), seq(256), seq(512), seq(512), state] + [const(t) for t in tables],
        out_specs=[seq(512), state],
        out_shape=[jax.ShapeDtypeStruct((b, l, H_A * DV_A), BF16),
                   jax.ShapeDtypeStruct((b, H_A // 2, 2 * DK_A, DV_A), F32)],
        scratch_shapes=[pltpu.VMEM((seqs, H_A // 2, 2 * DK_A, DV_A), F32)],
        compiler_params=_cparams("arbitrary", "arbitrary"),
        name="retention",
    )(q, k, v, ga, s0.reshape(b, H_A // 2, 2 * DK_A, DV_A), *tables)
    return o, s.reshape(b, H_A, DK_A, DV_A)


def _lam_value(lam_ref, lam_init):
    lv = lam_ref[...]
    return (jnp.exp(jnp.sum(lv[0:1] * lv[1:2], axis=-1, keepdims=True))
            - jnp.exp(jnp.sum(lv[2:3] * lv[3:4], axis=-1, keepdims=True)) + lam_init)


def _dattn_kernel(lam_init, tq, tk, q_ref, k_ref, v_ref, lam_ref, g_ref, o_ref, vt_ref, st_ref, m_ref, l_ref, acc_ref):
    qi = pl.program_id(2)

    @pl.when(qi == 0)
    def _():
        vt_ref[...] = v_ref[0].astype(F32).T.astype(BF16)

    q = q_ref[0]
    lane = lax.broadcasted_iota(jnp.int32, q.shape, 1)
    q2 = jnp.concatenate([jnp.where(lane < DH_B, q, jnp.zeros_like(q)),
                          jnp.where(lane >= DH_B, q, jnp.zeros_like(q))], axis=0)
    m_ref[...] = jnp.full_like(m_ref, -jnp.inf)
    l_ref[...] = jnp.zeros_like(l_ref)
    acc_ref[...] = jnp.zeros_like(acc_ref)

    def scores(slot, kb, diag_offset=None):
        start = pl.multiple_of(kb * tk, tk)
        st = _dot_nt(k_ref[0, pl.ds(start, tk), :], q2)
        if diag_offset is not None:
            key = lax.broadcasted_iota(jnp.int32, st.shape, 0) + diag_offset
            qry = jnp.bitwise_and(lax.broadcasted_iota(jnp.int32, st.shape, 1), tq - 1)
            st = jnp.where(key <= qry, st, -jnp.inf)
        st_ref[slot] = st

    def absorb(slot, kb):
        start = pl.multiple_of(kb * tk, tk)
        st = st_ref[slot]
        m_old = m_ref[...]
        m_new = jnp.maximum(m_old, jnp.max(st, axis=0, keepdims=True))
        alpha = jnp.exp2(m_old - m_new)
        pt = jnp.exp2(st - m_new)
        l_ref[...] = alpha * l_ref[...] + jnp.sum(pt, axis=0, keepdims=True)
        acc_ref[...] = alpha * acc_ref[...] + _dot(vt_ref[:, pl.ds(start, tk)], pt.astype(BF16))
        m_ref[...] = m_new

    assert tq == 2 * tk
    first_diag = 2 * qi
    scores(0, first_diag, 0)
    scores(1, first_diag + 1, tk)
    absorb(0, first_diag)

    def body(t, carry):
        scores(0, 2 * t)
        absorb(1, jnp.where(t == 0, first_diag + 1, 2 * t - 1))
        scores(1, 2 * t + 1)
        absorb(0, 2 * t)
        return carry

    lax.fori_loop(0, qi, body, 0)
    absorb(1, jnp.where(qi == 0, first_diag + 1, 2 * qi - 1))

    lam = _lam_value(lam_ref, lam_init)
    inv_l = 1.0 / l_ref[...]
    acc = acc_ref[...] * inv_l
    o = (acc[:, :tq] - lam * acc[:, tq:]).T
    o_ref[0] = ((_rms(o) * g_ref[...]) * (1.0 - lam_init)).astype(BF16)


def _diff_attn_prompt(qb, kb, vb, lam_vecs, g_diff, lam_init):
    b, l, _ = qb.shape
    tq, tk = 1024, 512
    whole = pl.BlockSpec((1, l, LANES), lambda bi, h, qi: (bi, 0, h))
    return pl.pallas_call(
        functools.partial(_dattn_kernel, lam_init, tq, tk),
        grid=(b, H_B, l // tq),
        in_specs=[pl.BlockSpec((1, tq, LANES), lambda bi, h, qi: (bi, qi, h)), whole, whole,
                  pl.BlockSpec(lam_vecs.shape, lambda bi, h, qi: (0, 0)),
                  pl.BlockSpec((1, DV_B), lambda bi, h, qi: (0, 0))],
        out_specs=pl.BlockSpec((1, tq, LANES), lambda bi, h, qi: (bi, qi, h)),
        out_shape=jax.ShapeDtypeStruct((b, l, H_B * DV_B), BF16),
        scratch_shapes=[pltpu.VMEM((DV_B, l), BF16), pltpu.VMEM((2, tk, 2 * tq), F32),
                        pltpu.VMEM((1, 2 * tq), F32), pltpu.VMEM((1, 2 * tq), F32),
                        pltpu.VMEM((DV_B, 2 * tq), F32)],
        compiler_params=_cparams("arbitrary", "arbitrary", "arbitrary"),
        name="diff_attn_prompt",
    )(qb, kb, vb, lam_vecs, g_diff.reshape(1, DV_B))


Q_ROWS = H_B * 2 * 8
KV_ROWS = PAGE_SIZE * H_B


def _dattn_dec_kernel(lam_init, n_pages, pt_ref, q_ref, kn_ref, vn_ref, *refs):
    k_refs = refs[:n_pages]
    v_refs = refs[n_pages:2 * n_pages]
    lam_ref, g_ref, o_ref, m_ref, l_ref, acc_ref = refs[2 * n_pages:]
    step = pl.program_id(1)
    q = q_ref[0]

    @pl.when(step == 0)
    def _():
        s = _dot_nt(q, kn_ref[0])
        r = lax.broadcasted_iota(jnp.int32, s.shape, 0)
        c = lax.broadcasted_iota(jnp.int32, s.shape, 1)
        s = jnp.where((c % H_B == r // 16) & (c // H_B <= r % 8), s, -jnp.inf)
        m = jnp.max(s, axis=-1, keepdims=True)
        p = jnp.exp2(s - m)
        m_ref[...] = m
        l_ref[...] = jnp.sum(p, axis=-1, keepdims=True)
        acc_ref[...] = _dot(p.astype(BF16), vn_ref[0])

    own_head = (lax.broadcasted_iota(jnp.int32, (Q_ROWS, KV_ROWS), 1) % H_B
                == lax.broadcasted_iota(jnp.int32, (Q_ROWS, KV_ROWS), 0) // 16)
    bias = jnp.where(own_head, 0.0, -jnp.inf)
    scores = [_dot_nt(q, kr[0].astype(BF16)) + bias for kr in k_refs]
    m_old = m_ref[...]
    m_new = m_old
    for s in scores:
        m_new = jnp.maximum(m_new, jnp.max(s, axis=-1, keepdims=True))
    alpha = jnp.exp2(m_old - m_new)
    l_new = alpha * l_ref[...]
    acc = alpha * acc_ref[...]
    for s, vr in zip(scores, v_refs):
        p = jnp.exp2(s - m_new)
        l_new = l_new + jnp.sum(p, axis=-1, keepdims=True)
        acc = acc + _dot(p.astype(BF16), vr[0].astype(BF16))
    m_ref[...] = m_new
    l_ref[...] = l_new
    acc_ref[...] = acc

    @pl.when(step == pl.num_programs(1) - 1)
    def _():
        lam = _lam_value(lam_ref, lam_init)
        o = acc / l_new
        for h in range(H_B):
            d = o[16 * h:16 * h + 8] - lam * o[16 * h + 8:16 * h + 16]
            o_ref[0, :, h * DV_B:(h + 1) * DV_B] = ((_rms(d) * g_ref[...]) * (1.0 - lam_init)).astype(BF16)


def _diff_attn_decode(q_rows, k_new, v_new, cache_k, cache_v, page_base, page_table, lam_vecs, g_diff, lam_init):
    b = q_rows.shape[0]
    n_steps = page_table.shape[1] // PAGES_PER_STEP
    per_b = lambda r, w: pl.BlockSpec((1, r, w), lambda bi, s, pt: (bi, 0, 0))

    def page_spec(j):
        return pl.BlockSpec((1, KV_ROWS, LANES),
                            lambda bi, s, pt: (page_base + pt[bi, s * PAGES_PER_STEP + j], 0, 0))

    pages = [page_spec(j) for j in range(PAGES_PER_STEP)]
    grid_spec = pltpu.PrefetchScalarGridSpec(
        num_scalar_prefetch=1,
        grid=(b, n_steps),
        in_specs=[per_b(Q_ROWS, LANES), per_b(DEC_SEQ * H_B, LANES), per_b(DEC_SEQ * H_B, LANES)]
        + pages + pages + [pl.BlockSpec(lam_vecs.shape, lambda bi, s, pt: (0, 0)),
                           pl.BlockSpec((1, DV_B), lambda bi, s, pt: (0, 0))],
        out_specs=per_b(8, H_B * DV_B),
        scratch_shapes=[pltpu.VMEM((Q_ROWS, 1), F32), pltpu.VMEM((Q_ROWS, 1), F32),
                        pltpu.VMEM((Q_ROWS, DV_B), F32)],
    )
    return pl.pallas_call(
        functools.partial(_dattn_dec_kernel, lam_init, PAGES_PER_STEP),
        grid_spec=grid_spec,
        out_shape=jax.ShapeDtypeStruct((b, 8, H_B * DV_B), BF16),
        compiler_params=_cparams("arbitrary", "arbitrary"),
        name="diff_attn_decode",
    )(page_table, q_rows, k_new, v_new, *([cache_k] * PAGES_PER_STEP), *([cache_v] * PAGES_PER_STEP),
      lam_vecs, g_diff.reshape(1, DV_B))


def _decode_q_rows(qb):
    b = qb.shape[0]
    q = qb.reshape(b, DEC_SEQ, H_B, DV_B).transpose(0, 2, 1, 3)
    q = jnp.pad(q, ((0, 0), (0, 0), (0, 8 - DEC_SEQ), (0, 0)))[:, :, None]
    keep = (np.arange(DV_B)[None, :] // DH_B) == np.arange(2)[:, None]
    keep = jnp.asarray(keep)[None, None, :, None, :]
    return jnp.where(keep, q, jnp.zeros_like(q)).reshape(b, Q_ROWS, DV_B)


def _gla_tables(chunk):
    idx = np.arange(chunk)
    i, t = idx[:, None], idx[None, :]
    mats = [t <= i]
    masks = [i == t]
    s = 1
    while s < chunk:
        same = (t // s) == (i // s)
        odd = ((i // s) % 2) == 1
        if 2 * s <= GLA_MATMUL_LEVEL_SPAN:
            mats.append(same & np.where(odd, t <= i, t > i))
        masks.append(((i // (2 * s)) == (t // (2 * s))) & odd & (((t // s) % 2) == 0))
        s *= 2
    return (jnp.asarray(np.concatenate(mats, axis=0).astype(np.float32), BF16),
            jnp.asarray(np.stack(masks).astype(np.float32)))


def _gla_kernel(chunk, n_chunks, seqs, valid, q_ref, k_ref, g_ref, v_ref, r_ref, s0_ref, gn_ref, m_ref, p_ref,
                o_ref, s_ref, st_ref):
    i = pl.program_id(1)

    @pl.when(i == 0)
    def _():
        for bi in range(seqs):
            for h in range(H_C):
                st_ref[bi, h] = s0_ref[bi, h].T

    n_levels = p_ref.shape[0] - 1
    m_all = m_ref[...]
    live = lax.broadcasted_iota(jnp.int32, (chunk, 1), 0) < valid
    for bi, c in itertools.product(range(seqs), range(n_chunks)):
        rows = slice(c * chunk, (c + 1) * chunk)
        g = g_ref[bi, rows,:]
        if valid < chunk:
            g = jnp.where(live, g, 0.0)
        g_hi = g.astype(BF16)
        g_lo = (g - g_hi.astype(F32)).astype(BF16)
        sums = _dot(m_all, g_hi) + _dot(m_all, g_lo)
        cum = sums[0:chunk]
        exponents = [cum, cum[chunk - 1:chunk] - cum]
        n_mats = m_all.shape[0] // chunk
        for lv in range(n_levels):
            if lv + 1 < n_mats:
                exponents.append(sums[(1 + lv) * chunk:(2 + lv) * chunk])
            else:
                s = 2 ** lv
                pairs = cum.reshape(chunk // (2 * s), 2 * s, cum.shape[-1])
                rel = pairs - pairs[:, s - 1:s, :]
                upper = lax.broadcasted_iota(jnp.int32, rel.shape, 1) >= s
                exponents.append(jnp.where(upper, rel, -rel).reshape(cum.shape))
        e_all = jnp.exp(jnp.concatenate(exponents, axis=0))
        heads = []
        for h in range(H_C):
            kl = slice(h * DK_C, (h + 1) * DK_C)
            e = e_all[:, kl]
            q = q_ref[bi, rows,kl]
            k = k_ref[bi, rows,kl]
            v = v_ref[bi, rows,h * DV_C:(h + 1) * DV_C]
            if valid < chunk:
                k = jnp.where(live, k, jnp.zeros_like(k))
                v = jnp.where(live, v, jnp.zeros_like(v))
            qf = q.astype(F32)
            kf = k.astype(F32)
            level = lambda lv: e[(2 + lv) * chunk:(3 + lv) * chunk]
            prods = [_dot_nt(q, k)] + [_dot_nt((qf * level(lv)).astype(BF16), (kf * level(lv)).astype(BF16))
                                       for lv in range(n_levels)]
            heads.append(dict(v=v, prods=prods, q_in=(qf * e[0:chunk]).astype(BF16),
                              k_out=(kf * e[chunk:2 * chunk]).astype(BF16), decay=e[chunk - 1:chunk]))
        for hd in heads:
            att = hd["prods"][0] * p_ref[0]
            for lv in range(n_levels):
                att = att + hd["prods"][1 + lv] * p_ref[1 + lv]
            hd["intra"] = _dot(att.astype(BF16), hd["v"])
            hd["update"] = _dot_tn(hd["v"], hd["k_out"])
        for h, hd in enumerate(heads):
            vl = slice(h * DV_C, (h + 1) * DV_C)
            st = st_ref[bi, h]
            o = hd["intra"] + _dot_nt(hd["q_in"], st.astype(BF16))
            st_ref[bi, h] = st * hd["decay"] + hd["update"]
            gate = r_ref[bi, rows,vl].astype(F32)
            o_ref[bi, rows,vl] = ((_rms(o) * gn_ref[...]) * (gate * jax.nn.sigmoid(gate))).astype(BF16)

    @pl.when(i == pl.num_programs(1) - 1)
    def _():
        for bi in range(seqs):
            for h in range(H_C):
                s_ref[bi, h] = st_ref[bi, h].T


def _gla(q, k, g, v, r, s0, g_norm, chunk, valid):
    b, l, _ = q.shape
    blk = min(l, CHUNKS_PER_STEP * chunk)
    n_chunks = blk // chunk
    seqs = _seqs_per_step(b, l, blk)
    m_all, masks = _gla_tables(chunk)
    qk = pl.BlockSpec((seqs, blk, H_C * DK_C), lambda bi, i: (bi, i, 0))
    vr = pl.BlockSpec((seqs, blk, H_C * DV_C), lambda bi, i: (bi, i, 0))
    state = pl.BlockSpec((seqs, H_C, DK_C, DV_C), lambda bi, i: (bi, 0, 0, 0))
    const = lambda a: pl.BlockSpec(a.shape, lambda bi, i: (0,) * a.ndim)
    gn = g_norm.reshape(1, DV_C)
    return pl.pallas_call(
        functools.partial(_gla_kernel, chunk, n_chunks, seqs, valid),
        grid=(b // seqs, l // blk),
        in_specs=[qk, qk, qk, vr, vr, state, const(gn), const(m_all), const(masks)],
        out_specs=[vr, state],
        out_shape=[jax.ShapeDtypeStruct((b, l, H_C * DV_C), BF16),
                   jax.ShapeDtypeStruct((b, H_C, DK_C, DV_C), F32)],
        scratch_shapes=[pltpu.VMEM((seqs, H_C, DV_C, DK_C), F32)],
        compiler_params=_cparams("arbitrary", "arbitrary"),
        name="gla",
    )(q, k, g, v, r, s0, gn, m_all, masks)


def _rope_tables(pos):
    inv = ROPE_THETA ** (-jnp.arange(0, DK_A, 2, dtype=F32) / DK_A)
    ang = pos.astype(F32)[:, None] * inv[None, :]
    cos = jnp.tile(jnp.cos(ang), (1, 4))
    sin = jnp.tile(jnp.concatenate([-jnp.sin(ang), jnp.sin(ang)], axis=1), (1, 2))
    return cos, sin


def _pad_seq(t, batch, seq_len):
    t = t.reshape(batch, seq_len, t.shape[-1])
    return jnp.pad(t, ((0, 0), (0, SAMPLE_PAD - seq_len), (0, 0)))


def _unpad_seq(t, seq_len):
    return t[:, :seq_len].reshape(t.shape[0] * seq_len, t.shape[-1])


def kernel(x_prompt, x_sample, cache_k, cache_v, state_ret, state_gla, state_conv, page_table, c_prompt, c_sample, w_ada, b_ada, g_pre_mix, g_post_mix, g_pre_ffn, g_post_ffn, w_in_ab, w_out_ab, lam_q1, lam_k1, lam_q2, lam_k2, g_diff, w_in_c, w_gate_c, b_gate_c, g_gla, w_out_c, w_up, conv_w, conv_b, w_down):
    rows_p, rows_s = BATCH * SEQ, DEC_BATCH * DEC_SEQ
    xp = x_prompt.reshape(rows_p, D_MODEL)
    xs = x_sample.reshape(rows_s, D_MODEL)

    n_c = BATCH + DEC_BATCH
    c_all = jnp.pad(jnp.concatenate([c_prompt, c_sample], axis=0), ((0, (-n_c) % 8), (0, 0)))
    ada = _ada(c_all, w_ada, b_ada)

    cos_p, sin_p = _rope_tables(jnp.arange(SEQ))
    cos_s, sin_s = _rope_tables(jnp.tile(PAST_LEN + jnp.arange(DEC_SEQ), DEC_BATCH))

    wup = w_up.astype(BF16)
    wdn = w_down.astype(BF16)
    k_p, v_p, k_s, v_s, ret_p, ret_s, gla_p, gla_s, conv_p, conv_s = ([] for _ in range(10))
    for l in range(DEPTH):
        mp = jnp.split(ada[l, :BATCH], 6, axis=-1)
        ms = jnp.split(ada[l, BATCH:n_c], 6, axis=-1)
        if l % 2 == 0:
            e = l // 2
            lam_init = 0.8 - 0.6 * math.exp(-0.3 * l)
            lam_vecs = jnp.stack([lam_q1[e], lam_k1[e], lam_q2[e], lam_k2[e]]).astype(F32)
            w_in = w_in_ab[e].astype(BF16)
            w_out = w_out_ab[e].astype(BF16)
            w_parts = [w_out[:H_A * DV_A], w_out[H_A * DV_A:]]

            qa, ka, va, ga, qb, kb32, kb16, vb32, vb16 = _in_proj_ab(
                xp, mp[0], mp[1], g_pre_mix[l], w_in, cos_p, sin_p, SEQ)
            seq3 = lambda t: t.reshape(BATCH, SEQ, t.shape[-1])
            ret, sp = _retention(seq3(qa), seq3(ka), seq3(va), seq3(ga),
                                 jnp.zeros((BATCH, H_A, DK_A, DV_A), F32), CHUNK_RET, CHUNK_RET)
            dif = _diff_attn_prompt(seq3(qb), seq3(kb16), seq3(vb16), lam_vecs, g_diff[e], lam_init)
            xp = _out_proj([ret.reshape(rows_p, -1), dif.reshape(rows_p, -1)], w_parts, xp, mp[2],
                           g_post_mix[l], SEQ)
            k_p.append(kb32.reshape(BATCH, SEQ, H_B, 2 * DH_B))
            v_p.append(vb32.reshape(BATCH, SEQ, H_B, DV_B))
            ret_p.append(sp)

            qa, ka, va, ga, qb, kb32, kb16, vb32, vb16 = _in_proj_ab(
                xs, ms[0], ms[1], g_pre_mix[l], w_in, cos_s, sin_s, DEC_SEQ)
            pad = lambda t: _pad_seq(t, DEC_BATCH, DEC_SEQ)
            ret, ss = _retention(pad(qa), pad(ka), pad(va), pad(ga), state_ret[e], SAMPLE_PAD, DEC_SEQ)
            n_phys = cache_k.shape[1]
            kv_rows = lambda t: t.reshape(DEC_BATCH, DEC_SEQ * H_B, DV_B)
            dif = _diff_attn_decode(
                _decode_q_rows(qb.reshape(DEC_BATCH, DEC_SEQ, 512)), kv_rows(kb16), kv_rows(vb16),
                cache_k.reshape(N_EVEN * n_phys, KV_ROWS, LANES), cache_v.reshape(N_EVEN * n_phys, KV_ROWS, LANES),
                e * n_phys, page_table, lam_vecs, g_diff[e], lam_init)
            xs = _out_proj([_unpad_seq(ret, DEC_SEQ), _unpad_seq(dif, DEC_SEQ)], w_parts, xs, ms[2],
                           g_post_mix[l], DEC_SEQ)
            k_s.append(kb32.reshape(DEC_BATCH, DEC_SEQ, H_B, 2 * DH_B))
            v_s.append(vb32.reshape(DEC_BATCH, DEC_SEQ, H_B, DV_B))
            ret_s.append(ss)
        else:
            o = l // 2
            n_main = sum(C_WIDTHS_MAIN)
            w_in = w_in_c[o][:, :n_main].astype(BF16)
            w_a = jnp.pad(w_in_c[o][:, n_main:], ((0, 0), (0, LANES - GATE_RANK))).astype(BF16)
            w_gate = jnp.pad(w_gate_c[o], ((0, LANES - GATE_RANK), (0, 0))).astype(BF16)
            w_out = w_out_c[o].astype(BF16)

            q, k, v, r, lg = _in_proj_c(xp, mp[0], mp[1], g_pre_mix[l], w_in, w_a, w_gate, b_gate_c[o], SEQ)
            seq3 = lambda t: t.reshape(BATCH, SEQ, t.shape[-1])
            og, sp = _gla(seq3(q), seq3(k), seq3(lg), seq3(v), seq3(r),
                          jnp.zeros((BATCH, H_C, DK_C, DV_C), F32), g_gla[o], CHUNK_GLA, CHUNK_GLA)
            xp = _out_proj([og.reshape(rows_p, -1)], [w_out], xp, mp[2], g_post_mix[l], SEQ)
            gla_p.append(sp)

            q, k, v, r, lg = _in_proj_c(xs, ms[0], ms[1], g_pre_mix[l], w_in, w_a, w_gate, b_gate_c[o], DEC_SEQ)
            pad = lambda t: _pad_seq(t, DEC_BATCH, DEC_SEQ)
            og, ss = _gla(pad(q), pad(k), pad(lg), pad(v), pad(r), state_gla[o], g_gla[o], SAMPLE_PAD, DEC_SEQ)
            xs = _out_proj([_unpad_seq(og, DEC_SEQ)], [w_out], xs, ms[2], g_post_mix[l], DEC_SEQ)
            gla_s.append(ss)

        xp, cbp = _conv_ffn(xp, mp[3], mp[4], mp[5], g_pre_ffn[l], g_post_ffn[l], l, wup, conv_w[l], conv_b[l],
                            wdn, SEQ)
        xs, up_s = _conv_ffn(xs, ms[3], ms[4], ms[5], g_pre_ffn[l], g_post_ffn[l], l, wup, conv_w[l], conv_b[l],
                             wdn, DEC_SEQ, state=state_conv[l])
        conv_p.append(cbp)
        conv_s.append(up_s.reshape(DEC_BATCH, DEC_SEQ, 2 * D_FF)[:, DEC_SEQ - (CONV_W - 1):])

    stack = lambda ts: ts[0][None] if len(ts) == 1 else jnp.stack(ts)
    return (xp.reshape(BATCH, SEQ, D_MODEL), xs.reshape(DEC_BATCH, DEC_SEQ, D_MODEL),
            stack(k_p), stack(v_p), stack(k_s), stack(v_s),
            stack(ret_p), stack(ret_s), stack(gla_p), stack(gla_s),
            stack(conv_p), stack(conv_s))
```

```python
import functools
import itertools
import math

import numpy as np
import jax
import jax.numpy as jnp
from jax import lax
from jax.experimental import pallas as pl
from jax.experimental.pallas import tpu as pltpu

D_MODEL = 1024
BATCH = 4
SEQ = 4096
DEPTH = 2
DEC_BATCH = 32
DEC_SEQ = 4
PAST_LEN = 8192
PAGE_SIZE = 128
N_EVEN = (DEPTH + 1) // 2
N_ODD = DEPTH // 2
H_A = 4
DK_A = D_MODEL // 16
DV_A = D_MODEL // 8
CHUNK_RET = 128
H_B = 4
DH_B = D_MODEL // 16
DV_B = 2 * DH_B
H_C = 4
DK_C = D_MODEL // 8
DV_C = D_MODEL // 4
GATE_RANK = 16
GATE_TAU = 16.0
C_WIDTHS_MAIN = (H_C * DK_C, H_C * DK_C, H_C * DV_C, H_C * DV_C)
CHUNK_GLA = 64
D_FF = ((8 * D_MODEL // 3 + 127) // 128) * 128
CONV_W = 3
ROPE_THETA = 10000.0
EPS = 1e-6

LANES = 128
VMEM_LIMIT = 56 * 1024 * 1024
ROW_TILE = 1024
FFN_ROW_TILE = 1024
GLA_MATMUL_LEVEL_SPAN = 8
CHUNKS_PER_STEP = 8
SHORT_SEQS_PER_STEP = 8
SAMPLE_PAD = 16
FF_CHUNK = 256
PAGES_PER_STEP = 32
LOG2E = math.log2(math.e)
BF16 = jnp.bfloat16
F32 = jnp.float32


def _cparams(*sem):
    return pltpu.CompilerParams(dimension_semantics=sem, vmem_limit_bytes=VMEM_LIMIT)


def _dot(a, b):
    return jnp.dot(a, b, preferred_element_type=F32)


def _dot_nt(a, b):
    return lax.dot_general(a, b, (((1,), (1,)), ((), ())), preferred_element_type=F32)


def _dot_tn(a, b):
    return lax.dot_general(a, b, (((0,), (0,)), ((), ())), preferred_element_type=F32)


def _rms(x):
    return x * lax.rsqrt(jnp.mean(x * x, axis=-1, keepdims=True) + EPS)


def _modulate(x, g, shift, scale):
    return (_rms(x) * g) * (1.0 + scale) + shift


def _rope(x, cos, sin_signed):
    w = x.shape[-1]
    fwd = pltpu.roll(x, 32, axis=1)
    bwd = pltpu.roll(x, w - 32, axis=1)
    reps = w // LANES
    first_half = (lax.broadcasted_iota(jnp.int32, x.shape, 1) % 64) < 32
    partner = jnp.where(first_half, bwd, fwd)
    return (x * jnp.concatenate([cos] * reps, axis=1)
            + partner * jnp.concatenate([sin_signed] * reps, axis=1))


def _ada_kernel(c_ref, w_ref, b_ref, o_ref):
    c = c_ref[...]
    s = c * jax.nn.sigmoid(c)
    o_ref[0] = jnp.dot(s, w_ref[0], preferred_element_type=F32, precision=lax.Precision.HIGHEST) + b_ref[0]


def _ada(c_all, w_ada, b_ada):
    rows = c_all.shape[0]
    tn = 1536
    return pl.pallas_call(
        _ada_kernel,
        grid=(DEPTH, 6 * D_MODEL // tn),
        in_specs=[pl.BlockSpec((rows, D_MODEL), lambda l, j: (0, 0)),
                  pl.BlockSpec((1, D_MODEL, tn), lambda l, j: (l, 0, j)),
                  pl.BlockSpec((1, 1, tn), lambda l, j: (l, 0, j))],
        out_specs=pl.BlockSpec((1, rows, tn), lambda l, j: (l, 0, j)),
        out_shape=jax.ShapeDtypeStruct((DEPTH, rows, 6 * D_MODEL), F32),
        compiler_params=_cparams("arbitrary", "arbitrary"),
        name="adaln",
    )(c_all, w_ada, b_ada.reshape(DEPTH, 1, 6 * D_MODEL))


def _row_cfg(rows, seq_len, row_tile=ROW_TILE):
    tm = min(row_tile, rows)
    per_token = seq_len < tm
    tiles_per_seq = 1 if per_token else seq_len // tm
    mod_spec = (pl.BlockSpec((1, tm, D_MODEL), lambda i: (i, 0, 0)) if per_token
                else pl.BlockSpec((1, 1, D_MODEL), lambda i: (i // tiles_per_seq, 0, 0)))
    return tm, per_token, tiles_per_seq, mod_spec


def _mod_arg(m, rows, seq_len, tm, per_token):
    if per_token:
        return jnp.repeat(m, seq_len, axis=0).reshape(rows // tm, tm, D_MODEL)
    return m.reshape(m.shape[0], 1, D_MODEL)


def _inab_kernel(x_ref, sh_ref, sc_ref, g_ref, w_ref, cos_ref, sin_ref,
                 qa_ref, ka_ref, va_ref, ga_ref, qb_ref, kb32_ref, kb16_ref, vb32_ref, vb16_ref):
    h = _modulate(x_ref[...], g_ref[...], sh_ref[0], sc_ref[0]).astype(BF16)
    cos = cos_ref[...]
    sin = sin_ref[...]

    def proj(lo, width):
        return _dot(h, w_ref[:, lo:lo + width])

    qa_ref[...] = _rope(proj(0, 256), cos, sin).astype(BF16)
    ka_ref[...] = (_rope(proj(256, 256), cos, sin) * (DK_A ** -0.5)).astype(BF16)
    va_ref[...] = proj(512, 512).astype(BF16)
    ga_ref[...] = proj(1024, 512).astype(BF16)
    qb_ref[...] = (_rope(proj(1536, 512), cos, sin) * (DH_B ** -0.5 * LOG2E)).astype(BF16)
    kb = _rope(proj(2048, 512), cos, sin)
    kb16_ref[...] = kb.astype(BF16)
    vb = proj(2560, 512)
    vb16_ref[...] = vb.astype(BF16)
    tm = kb.shape[0]
    for h in range(H_B):
        kb32_ref[pl.ds(h, tm, stride=H_B), :] = kb[:, h * DV_B:(h + 1) * DV_B]
        vb32_ref[pl.ds(h, tm, stride=H_B), :] = vb[:, h * DV_B:(h + 1) * DV_B]


def _in_proj_ab(x, shift, scale, g_pre, w_bf, cos, sin, seq_len):
    rows = x.shape[0]
    tm, per_token, tps, mod_spec = _row_cfg(rows, seq_len)
    pos_map = (lambda i: (i, 0)) if per_token else (lambda i: (i % tps, 0))
    row = lambda w: pl.BlockSpec((tm, w), lambda i: (i, 0))
    widths = (256, 256, 512, 512, 512, 512, 512, 512, 512)
    dtypes = (BF16, BF16, BF16, BF16, BF16, F32, BF16, F32, BF16)
    cache_rows = lambda w, d: d == F32
    out_spec = lambda w, d: (pl.BlockSpec((tm * H_B, DV_B), lambda i: (i, 0)) if cache_rows(w, d) else row(w))
    out_shape = lambda w, d: jax.ShapeDtypeStruct((rows * H_B, DV_B) if cache_rows(w, d) else (rows, w), d)
    return pl.pallas_call(
        _inab_kernel,
        grid=(rows // tm,),
        in_specs=[row(D_MODEL), mod_spec, mod_spec,
                  pl.BlockSpec((1, D_MODEL), lambda i: (0, 0)),
                  pl.BlockSpec(w_bf.shape, lambda i: (0, 0)),
                  pl.BlockSpec((tm, LANES), pos_map), pl.BlockSpec((tm, LANES), pos_map)],
        out_specs=[out_spec(w, d) for w, d in zip(widths, dtypes)],
        out_shape=[out_shape(w, d) for w, d in zip(widths, dtypes)],
        compiler_params=_cparams("arbitrary"),
        name="in_proj_ab",
    )(x, _mod_arg(shift, rows, seq_len, tm, per_token), _mod_arg(scale, rows, seq_len, tm, per_token),
      g_pre.reshape(1, D_MODEL), w_bf, cos, sin)


def _inc_kernel(x_ref, sh_ref, sc_ref, g_ref, w_ref, wa_ref, wg_ref, bg_ref,
                q_ref, k_ref, v_ref, r_ref, lg_ref):
    h = _modulate(x_ref[...], g_ref[...], sh_ref[0], sc_ref[0]).astype(BF16)

    def proj(lo, width):
        return _dot(h, w_ref[:, lo:lo + width])

    q_ref[...] = (proj(0, 512) * (DK_C ** -0.5)).astype(BF16)
    k_ref[...] = proj(512, 512).astype(BF16)
    v_ref[...] = proj(1024, 1024).astype(BF16)
    r_ref[...] = proj(2048, 1024).astype(BF16)
    a = _dot(h, wa_ref[...])
    z = _dot(a.astype(BF16), wg_ref[...]) + bg_ref[...]
    softplus_neg = jnp.maximum(-z, 0.0) + jnp.log1p(jnp.exp(-jnp.abs(z)))
    lg_ref[...] = -softplus_neg / GATE_TAU


def _in_proj_c(x, shift, scale, g_pre, w_bf, wa_bf, wg_bf, b_gate, seq_len):
    rows = x.shape[0]
    tm, per_token, tps, mod_spec = _row_cfg(rows, seq_len)
    row = lambda w: pl.BlockSpec((tm, w), lambda i: (i, 0))
    widths = (512, 512, 1024, 1024, 512)
    dtypes = (BF16, BF16, BF16, BF16, F32)
    full = lambda a: pl.BlockSpec(a.shape, lambda i: (0, 0))
    bg = b_gate.reshape(1, -1)
    return pl.pallas_call(
        _inc_kernel,
        grid=(rows // tm,),
        in_specs=[row(D_MODEL), mod_spec, mod_spec, pl.BlockSpec((1, D_MODEL), lambda i: (0, 0)),
                  full(w_bf), full(wa_bf), full(wg_bf), full(bg)],
        out_specs=[row(w) for w in widths],
        out_shape=[jax.ShapeDtypeStruct((rows, w), d) for w, d in zip(widths, dtypes)],
        compiler_params=_cparams("arbitrary"),
        name="in_proj_c",
    )(x, _mod_arg(shift, rows, seq_len, tm, per_token), _mod_arg(scale, rows, seq_len, tm, per_token),
      g_pre.reshape(1, D_MODEL), w_bf, wa_bf, wg_bf, bg)


def _outproj_kernel(n_parts, *refs):
    parts = refs[:n_parts]
    ws = refs[n_parts:2 * n_parts]
    x_ref, gate_ref, g_ref, o_ref = refs[2 * n_parts:]
    y = _dot(parts[0][...], ws[0][...])
    for p, w in zip(parts[1:], ws[1:]):
        y = y + _dot(p[...], w[...])
    o_ref[...] = x_ref[...] + gate_ref[0] * (_rms(y) * g_ref[...])


def _out_proj(parts, weights, x, gate, g_post, seq_len):
    rows = x.shape[0]
    tm, per_token, tps, mod_spec = _row_cfg(rows, seq_len)
    n = len(parts)
    return pl.pallas_call(
        functools.partial(_outproj_kernel, n),
        grid=(rows // tm,),
        in_specs=([pl.BlockSpec((tm, p.shape[1]), lambda i: (i, 0)) for p in parts]
                  + [pl.BlockSpec(w.shape, lambda i: (0, 0)) for w in weights]
                  + [pl.BlockSpec((tm, D_MODEL), lambda i: (i, 0)), mod_spec,
                     pl.BlockSpec((1, D_MODEL), lambda i: (0, 0))]),
        out_specs=pl.BlockSpec((tm, D_MODEL), lambda i: (i, 0)),
        out_shape=jax.ShapeDtypeStruct((rows, D_MODEL), F32),
        compiler_params=_cparams("arbitrary"),
        name="out_proj",
    )(*parts, *weights, x, _mod_arg(gate, rows, seq_len, tm, per_token), g_post.reshape(1, D_MODEL))


def _ffn_kernel(per_token, seq_len, tiles_per_seq, *refs):
    if per_token:
        (x_ref, sh_ref, sc_ref, gate_ref, gpre_ref, gpost_ref, wup_ref, cw_ref, cb_ref, wdn_ref,
         s1_ref, s2_ref, xo_ref, conv_ref, h_ref, acc_ref, *u_refs) = refs
    else:
        (x_ref, sh_ref, sc_ref, gate_ref, gpre_ref, gpost_ref, wup_ref, cw_ref, cb_ref, wdn_ref,
         xo_ref, conv_ref, h_ref, acc_ref, *u_refs) = refs
    tm = x_ref.shape[0]
    halo = h_ref.shape[0] - tm
    x = x_ref[...]
    if per_token:
        tau = lax.broadcasted_iota(jnp.int32, (tm, FF_CHUNK), 0) % seq_len
    else:
        seq_start = (pl.program_id(0) % tiles_per_seq) == 0

        @pl.when(seq_start)
        def _():
            h_ref[:halo] = jnp.zeros((halo, D_MODEL), BF16)

        @pl.when(jnp.logical_not(seq_start))
        def _():
            h_ref[:halo] = h_ref[tm:]
    h_ref[halo:] = _modulate(x, gpre_ref[...], sh_ref[0], sc_ref[0]).astype(BF16)
    acc_ref[...] = jnp.zeros_like(acc_ref)

    def up(lo):
        return _dot(h_ref[...], wup_ref[:, pl.ds(lo, FF_CHUNK)])

    def conv(slot, half, lo):
        cols = pl.ds(lo, FF_CHUNK)
        cw = cw_ref[:, cols]
        cb = cb_ref[:, cols]
        taps = lambda u2, u1, u0: cb + cw[0:1] * u2 + cw[1:2] * u1 + cw[2:3] * u0
        if per_token:
            u = u_refs[slot][half]
            u1 = jnp.where(tau == 0, s1_ref[:, cols], pltpu.roll(u, 1, axis=0))
            u2 = jnp.where(tau < 2, s2_ref[:, cols], pltpu.roll(u, 2, axis=0))
            conv_ref[:, cols] = u
            return taps(u2, u1, u)
        conv_ref[0, :, cols] = u_refs[slot][half, halo + tm - 2:]
        window = lambda back: u_refs[slot][half, halo - back:halo - back + tm]
        return taps(window(2), window(1), window(0))

    def produce(slot, lo):
        u_refs[slot][0] = up(lo)
        u_refs[slot][1] = up(lo + D_FF)

    def consume(slot, lo):
        g = (jax.nn.gelu(conv(slot, 0, lo)) * conv(slot, 1, lo + D_FF)).astype(BF16)
        acc_ref[...] += _dot(g, wdn_ref[pl.ds(lo, FF_CHUNK), :])

    def body(j, carry):
        lo = pl.multiple_of(j * (2 * FF_CHUNK), 2 * FF_CHUNK)
        produce(1, lo + FF_CHUNK)
        consume(0, lo)
        produce(0, lo + 2 * FF_CHUNK)
        consume(1, lo + FF_CHUNK)
        return carry

    n_chunks = D_FF // FF_CHUNK
    produce(0, 0)
    lax.fori_loop(0, n_chunks // 2, body, 0)
    consume(0, (n_chunks - 1) * FF_CHUNK)
    xo_ref[...] = x + gate_ref[0] * (_rms(acc_ref[...]) * gpost_ref[...])


def _conv_ffn(x, shift, scale, gate, g_pre, g_post, layer, wup_bf, conv_w, conv_b, wdn_bf, seq_len, state=None):
    rows = x.shape[0]
    tm, per_token, tps, mod_spec = _row_cfg(rows, seq_len, FFN_ROW_TILE)
    full = lambda a: pl.BlockSpec(a.shape, lambda i: (0, 0), pipeline_mode=pl.Buffered(1))
    of_layer = lambda a: pl.BlockSpec((None,) + a.shape[1:], lambda i: (layer, 0, 0), pipeline_mode=pl.Buffered(1))
    cb = conv_b.reshape(1, 2 * D_FF)
    in_specs = [pl.BlockSpec((tm, D_MODEL), lambda i: (i, 0)), mod_spec, mod_spec, mod_spec,
                pl.BlockSpec((1, D_MODEL), lambda i: (0, 0)), pl.BlockSpec((1, D_MODEL), lambda i: (0, 0)),
                of_layer(wup_bf), full(conv_w), full(cb), of_layer(wdn_bf)]
    args = [x] + [_mod_arg(m, rows, seq_len, tm, per_token) for m in (shift, scale, gate)] + [
        g_pre.reshape(1, D_MODEL), g_post.reshape(1, D_MODEL), wup_bf, conv_w, cb, wdn_bf]
    halo = 0 if per_token else SAMPLE_PAD
    scratch = [pltpu.VMEM((halo + tm, D_MODEL), BF16), pltpu.VMEM((tm, D_MODEL), F32),
               pltpu.VMEM((2, halo + tm, FF_CHUNK), F32), pltpu.VMEM((2, halo + tm, FF_CHUNK), F32)]
    if per_token:
        batch = rows // seq_len
        zeros = jnp.zeros((batch, seq_len - 1, 2 * D_FF), F32)
        s1 = jnp.concatenate([state[:, 1:2], zeros], axis=1).reshape(rows, 2 * D_FF)
        s2 = jnp.concatenate([state, zeros[:, 1:]], axis=1).reshape(rows, 2 * D_FF)
        in_specs += [pl.BlockSpec((tm, 2 * D_FF), lambda i: (i, 0))] * 2
        args += [s1, s2]
        conv_spec = pl.BlockSpec((tm, 2 * D_FF), lambda i: (i, 0))
        conv_shape = jax.ShapeDtypeStruct((rows, 2 * D_FF), F32)
    else:
        conv_spec = pl.BlockSpec((1, 2, 2 * D_FF), lambda i: (i // tps, 0, 0))
        conv_shape = jax.ShapeDtypeStruct((rows // seq_len, 2, 2 * D_FF), F32)
    return pl.pallas_call(
        functools.partial(_ffn_kernel, per_token, seq_len, tps),
        grid=(rows // tm,),
        in_specs=in_specs,
        out_specs=[pl.BlockSpec((tm, D_MODEL), lambda i: (i, 0)), conv_spec],
        out_shape=[jax.ShapeDtypeStruct((rows, D_MODEL), F32), conv_shape],
        scratch_shapes=scratch,
        compiler_params=_cparams("arbitrary"),
        name="conv_ffn",
    )(*args)


def _ret_tables(chunk, valid):
    log_gamma = jnp.log1p(-jnp.exp2(-5.0 - jnp.arange(H_A, dtype=F32)))
    idx = jnp.arange(chunk, dtype=F32)
    rel = idx[:, None] - idx[None, :]
    intra = jnp.exp(jnp.where(rel[None] >= 0, rel[None] * log_gamma[:, None, None], -jnp.inf))
    dq = jnp.exp((idx + 1.0)[None, :] * log_gamma[:, None])
    dk = jnp.where(idx[None, :] < valid, jnp.exp((valid - 1.0 - idx)[None, :] * log_gamma[:, None]), 0.0)
    dc = jnp.exp(valid * log_gamma)
    rep = lambda t: jnp.broadcast_to(t[..., None], t.shape + (LANES,))
    dc_rows = rep(jnp.repeat(dc, DK_A).reshape(H_A // 2, 2 * DK_A))
    return intra, rep(dq), rep(dk), dc_rows


def _ret_kernel(chunk, n_chunks, seqs, q_ref, k_ref, v_ref, ga_ref, s0_ref, intra_ref, dq_ref, dk_ref, dc_ref,
                o_ref, s_ref, st_ref):
    i = pl.program_id(1)

    @pl.when(i == 0)
    def _():
        st_ref[...] = s0_ref[...]

    lane = lax.broadcasted_iota(jnp.int32, (chunk, LANES), 1)
    for bi in range(seqs):
        for c in range(n_chunks):
            rows = slice(c * chunk, (c + 1) * chunk)
            for pair in range(H_A // 2):
                lanes = slice(pair * LANES, (pair + 1) * LANES)
                qc = q_ref[bi, rows, lanes]
                kc = k_ref[bi, rows, lanes]
                s_pair = st_ref[bi, pair]
                s_bf = s_pair.astype(BF16)
                upd = s_pair * dc_ref[pair]
                for side in range(2):
                    hd = 2 * pair + side
                    mine = (lane >= 64) if side else (lane < 64)
                    hl = slice(hd * DV_A, (hd + 1) * DV_A)
                    qh = jnp.where(mine, qc, jnp.zeros_like(qc))
                    vh = v_ref[bi, rows, hl]
                    sc = _dot_nt(qh, kc) * intra_ref[hd]
                    o = _dot(sc.astype(BF16), vh) + _dot(qh, s_bf) * dq_ref[hd]
                    kh = jnp.where(mine, kc.astype(F32) * dk_ref[hd], 0.0).astype(BF16)
                    upd = upd + _dot_tn(kh, vh)
                    gate = ga_ref[bi, rows, hl].astype(F32)
                    o_ref[bi, rows, hl] = (_rms(o) * (gate * jax.nn.sigmoid(gate))).astype(BF16)
                st_ref[bi, pair] = upd

    @pl.when(i == pl.num_programs(1) - 1)
    def _():
        s_ref[...] = st_ref[...]


def _seqs_per_step(batch, seq_len, blk):
    return min(batch, SHORT_SEQS_PER_STEP) if seq_len == blk else 1


def _retention(q, k, v, ga, s0, chunk, valid):
    b, l, _ = q.shape
    blk = min(l, CHUNKS_PER_STEP * chunk)
    n_chunks = blk // chunk
    seqs = _seqs_per_step(b, l, blk)
    tables = _ret_tables(chunk, valid)
    seq = lambda w: pl.BlockSpec((seqs, blk, w), lambda bi, i: (bi, i, 0))
    const = lambda a: pl.BlockSpec(a.shape, lambda bi, i: (0,) * a.ndim)
    state = pl.BlockSpec((seqs, H_A // 2, 2 * DK_A, DV_A), lambda bi, i: (bi, 0, 0, 0))
    o, s = pl.pallas_call(
        functools.partial(_ret_kernel, chunk, n_chunks, seqs),
        grid=(b // seqs, l // blk),
        in_specs=[seq(256), seq(256), seq(512), seq(512), state] + [const(t) for t in tables],
        out_specs=[seq(512), state],
        out_shape=[jax.ShapeDtypeStruct((b, l, H_A * DV_A), BF16),
                   jax.ShapeDtypeStruct((b, H_A // 2, 2 * DK_A, DV_A), F32)],
        scratch_shapes=[pltpu.VMEM((seqs, H_A // 2, 2 * DK_A, DV_A), F32)],
        compiler_params=_cparams("arbitrary", "arbitrary"),
        name="retention",
    )(q, k, v, ga, s0.reshape(b, H_A // 2, 2 * DK_A, DV_A), *tables)
    return o, s.reshape(b, H_A, DK_A, DV_A)


def _lam_value(lam_ref, lam_init):
    lv = lam_ref[...]
    return (jnp.exp(jnp.sum(lv[0:1] * lv[1:2], axis=-1, keepdims=True))
            - jnp.exp(jnp.sum(lv[2:3] * lv[3:4], axis=-1, keepdims=True)) + lam_init)


def _dattn_kernel(lam_init, tq, tk, q_ref, k_ref, v_ref, lam_ref, g_ref, o_ref, vt_ref, st_ref, m_ref, l_ref, acc_ref):
    qi = pl.program_id(2)

    @pl.when(qi == 0)
    def _():
        vt_ref[...] = v_ref[0].astype(F32).T.astype(BF16)

    q = q_ref[0]
    lane = lax.broadcasted_iota(jnp.int32, q.shape, 1)
    q2 = jnp.concatenate([jnp.where(lane < DH_B, q, jnp.zeros_like(q)),
                          jnp.where(lane >= DH_B, q, jnp.zeros_like(q))], axis=0)
    m_ref[...] = jnp.full_like(m_ref, -jnp.inf)
    l_ref[...] = jnp.zeros_like(l_ref)
    acc_ref[...] = jnp.zeros_like(acc_ref)

    def scores(slot, kb, diag_offset=None):
        start = pl.multiple_of(kb * tk, tk)
        st = _dot_nt(k_ref[0, pl.ds(start, tk), :], q2)
        if diag_offset is not None:
            key = lax.broadcasted_iota(jnp.int32, st.shape, 0) + diag_offset
            qry = jnp.bitwise_and(lax.broadcasted_iota(jnp.int32, st.shape, 1), tq - 1)
            st = jnp.where(key <= qry, st, -jnp.inf)
        st_ref[slot] = st

    def absorb(slot, kb):
        start = pl.multiple_of(kb * tk, tk)
        st = st_ref[slot]
        m_old = m_ref[...]
        m_new = jnp.maximum(m_old, jnp.max(st, axis=0, keepdims=True))
        alpha = jnp.exp2(m_old - m_new)
        pt = jnp.exp2(st - m_new)
        l_ref[...] = alpha * l_ref[...] + jnp.sum(pt, axis=0, keepdims=True)
        acc_ref[...] = alpha * acc_ref[...] + _dot(vt_ref[:, pl.ds(start, tk)], pt.astype(BF16))
        m_ref[...] = m_new

    assert tq == 2 * tk
    first_diag = 2 * qi
    scores(0, first_diag, 0)
    scores(1, first_diag + 1, tk)
    absorb(0, first_diag)

    def body(t, carry):
        scores(0, 2 * t)
        absorb(1, jnp.where(t == 0, first_diag + 1, 2 * t - 1))
        scores(1, 2 * t + 1)
        absorb(0, 2 * t)
        return carry

    lax.fori_loop(0, qi, body, 0)
    absorb(1, jnp.where(qi == 0, first_diag + 1, 2 * qi - 1))

    lam = _lam_value(lam_ref, lam_init)
    inv_l = 1.0 / l_ref[...]
    acc = acc_ref[...] * inv_l
    o = (acc[:, :tq] - lam * acc[:, tq:]).T
    o_ref[0] = ((_rms(o) * g_ref[...]) * (1.0 - lam_init)).astype(BF16)


def _diff_attn_prompt(qb, kb, vb, lam_vecs, g_diff, lam_init):
    b, l, _ = qb.shape
    tq, tk = 1024, 512
    whole = pl.BlockSpec((1, l, LANES), lambda bi, h, qi: (bi, 0, h))
    return pl.pallas_call(
        functools.partial(_dattn_kernel, lam_init, tq, tk),
        grid=(b, H_B, l // tq),
        in_specs=[pl.BlockSpec((1, tq, LANES), lambda bi, h, qi: (bi, qi, h)), whole, whole,
                  pl.BlockSpec(lam_vecs.shape, lambda bi, h, qi: (0, 0)),
                  pl.BlockSpec((1, DV_B), lambda bi, h, qi: (0, 0))],
        out_specs=pl.BlockSpec((1, tq, LANES), lambda bi, h, qi: (bi, qi, h)),
        out_shape=jax.ShapeDtypeStruct((b, l, H_B * DV_B), BF16),
        scratch_shapes=[pltpu.VMEM((DV_B, l), BF16), pltpu.VMEM((2, tk, 2 * tq), F32),
                        pltpu.VMEM((1, 2 * tq), F32), pltpu.VMEM((1, 2 * tq), F32),
                        pltpu.VMEM((DV_B, 2 * tq), F32)],
        compiler_params=_cparams("arbitrary", "arbitrary", "arbitrary"),
        name="diff_attn_prompt",
    )(qb, kb, vb, lam_vecs, g_diff.reshape(1, DV_B))


Q_ROWS = H_B * 2 * 8
KV_ROWS = PAGE_SIZE * H_B


def _dattn_dec_kernel(lam_init, n_pages, pt_ref, q_ref, kn_ref, vn_ref, *refs):
    k_refs = refs[:n_pages]
    v_refs = refs[n_pages:2 * n_pages]
    lam_ref, g_ref, o_ref, m_ref, l_ref, acc_ref = refs[2 * n_pages:]
    step = pl.program_id(1)
    q = q_ref[0]

    @pl.when(step == 0)
    def _():
        s = _dot_nt(q, kn_ref[0])
        r = lax.broadcasted_iota(jnp.int32, s.shape, 0)
        c = lax.broadcasted_iota(jnp.int32, s.shape, 1)
        s = jnp.where((c % H_B == r // 16) & (c // H_B <= r % 8), s, -jnp.inf)
        m = jnp.max(s, axis=-1, keepdims=True)
        p = jnp.exp2(s - m)
        m_ref[...] = m
        l_ref[...] = jnp.sum(p, axis=-1, keepdims=True)
        acc_ref[...] = _dot(p.astype(BF16), vn_ref[0])

    own_head = (lax.broadcasted_iota(jnp.int32, (Q_ROWS, KV_ROWS), 1) % H_B
                == lax.broadcasted_iota(jnp.int32, (Q_ROWS, KV_ROWS), 0) // 16)
    bias = jnp.where(own_head, 0.0, -jnp.inf)
    scores = [_dot_nt(q, kr[0].astype(BF16)) + bias for kr in k_refs]
    m_old = m_ref[...]
    m_new = m_old
    for s in scores:
        m_new = jnp.maximum(m_new, jnp.max(s, axis=-1, keepdims=True))
    alpha = jnp.exp2(m_old - m_new)
    l_new = alpha * l_ref[...]
    acc = alpha * acc_ref[...]
    for s, vr in zip(scores, v_refs):
        p = jnp.exp2(s - m_new)
        l_new = l_new + jnp.sum(p, axis=-1, keepdims=True)
        acc = acc + _dot(p.astype(BF16), vr[0].astype(BF16))
    m_ref[...] = m_new
    l_ref[...] = l_new
    acc_ref[...] = acc

    @pl.when(step == pl.num_programs(1) - 1)
    def _():
        lam = _lam_value(lam_ref, lam_init)
        o = acc / l_new
        for h in range(H_B):
            d = o[16 * h:16 * h + 8] - lam * o[16 * h + 8:16 * h + 16]
            o_ref[0, :, h * DV_B:(h + 1) * DV_B] = ((_rms(d) * g_ref[...]) * (1.0 - lam_init)).astype(BF16)


def _diff_attn_decode(q_rows, k_new, v_new, cache_k, cache_v, page_base, page_table, lam_vecs, g_diff, lam_init):
    b = q_rows.shape[0]
    n_steps = page_table.shape[1] // PAGES_PER_STEP
    per_b = lambda r, w: pl.BlockSpec((1, r, w), lambda bi, s, pt: (bi, 0, 0))

    def page_spec(j):
        return pl.BlockSpec((1, KV_ROWS, LANES),
                            lambda bi, s, pt: (page_base + pt[bi, s * PAGES_PER_STEP + j], 0, 0))

    pages = [page_spec(j) for j in range(PAGES_PER_STEP)]
    grid_spec = pltpu.PrefetchScalarGridSpec(
        num_scalar_prefetch=1,
        grid=(b, n_steps),
        in_specs=[per_b(Q_ROWS, LANES), per_b(DEC_SEQ * H_B, LANES), per_b(DEC_SEQ * H_B, LANES)]
        + pages + pages + [pl.BlockSpec(lam_vecs.shape, lambda bi, s, pt: (0, 0)),
                           pl.BlockSpec((1, DV_B), lambda bi, s, pt: (0, 0))],
        out_specs=per_b(8, H_B * DV_B),
        scratch_shapes=[pltpu.VMEM((Q_ROWS, 1), F32), pltpu.VMEM((Q_ROWS, 1), F32),
                        pltpu.VMEM((Q_ROWS, DV_B), F32)],
    )
    return pl.pallas_call(
        functools.partial(_dattn_dec_kernel, lam_init, PAGES_PER_STEP),
        grid_spec=grid_spec,
        out_shape=jax.ShapeDtypeStruct((b, 8, H_B * DV_B), BF16),
        compiler_params=_cparams("arbitrary", "arbitrary"),
        name="diff_attn_decode",
    )(page_table, q_rows, k_new, v_new, *([cache_k] * PAGES_PER_STEP), *([cache_v] * PAGES_PER_STEP),
      lam_vecs, g_diff.reshape(1, DV_B))


def _decode_q_rows(qb):
    b = qb.shape[0]
    q = qb.reshape(b, DEC_SEQ, H_B, DV_B).transpose(0, 2, 1, 3)
    q = jnp.pad(q, ((0, 0), (0, 0), (0, 8 - DEC_SEQ), (0, 0)))[:, :, None]
    keep = (np.arange(DV_B)[None, :] // DH_B) == np.arange(2)[:, None]
    keep = jnp.asarray(keep)[None, None, :, None, :]
    return jnp.where(keep, q, jnp.zeros_like(q)).reshape(b, Q_ROWS, DV_B)


def _gla_tables(chunk):
    idx = np.arange(chunk)
    i, t = idx[:, None], idx[None, :]
    mats = [t <= i]
    masks = [i == t]
    s = 1
    while s < chunk:
        same = (t // s) == (i // s)
        odd = ((i // s) % 2) == 1
        if 2 * s <= GLA_MATMUL_LEVEL_SPAN:
            mats.append(same & np.where(odd, t <= i, t > i))
        masks.append(((i // (2 * s)) == (t // (2 * s))) & odd & (((t // s) % 2) == 0))
        s *= 2
    return (jnp.asarray(np.concatenate(mats, axis=0).astype(np.float32), BF16),
            jnp.asarray(np.stack(masks).astype(np.float32)))


def _gla_kernel(chunk, n_chunks, seqs, valid, q_ref, k_ref, g_ref, v_ref, r_ref, s0_ref, gn_ref, m_ref, p_ref,
                o_ref, s_ref, st_ref):
    i = pl.program_id(1)

    @pl.when(i == 0)
    def _():
        for bi in range(seqs):
            for h in range(H_C):
                st_ref[bi, h] = s0_ref[bi, h].T

    n_levels = p_ref.shape[0] - 1
    m_all = m_ref[...]
    live = lax.broadcasted_iota(jnp.int32, (chunk, 1), 0) < valid
    for bi, c in itertools.product(range(seqs), range(n_chunks)):
        rows = slice(c * chunk, (c + 1) * chunk)
        g = g_ref[bi, rows,:]
        if valid < chunk:
            g = jnp.where(live, g, 0.0)
        g_hi = g.astype(BF16)
        g_lo = (g - g_hi.astype(F32)).astype(BF16)
        sums = _dot(m_all, g_hi) + _dot(m_all, g_lo)
        cum = sums[0:chunk]
        exponents = [cum, cum[chunk - 1:chunk] - cum]
        n_mats = m_all.shape[0] // chunk
        for lv in range(n_levels):
            if lv + 1 < n_mats:
                exponents.append(sums[(1 + lv) * chunk:(2 + lv) * chunk])
            else:
                s = 2 ** lv
                pairs = cum.reshape(chunk // (2 * s), 2 * s, cum.shape[-1])
                rel = pairs - pairs[:, s - 1:s, :]
                upper = lax.broadcasted_iota(jnp.int32, rel.shape, 1) >= s
                exponents.append(jnp.where(upper, rel, -rel).reshape(cum.shape))
        e_all = jnp.exp(jnp.concatenate(exponents, axis=0))
        heads = []
        for h in range(H_C):
            kl = slice(h * DK_C, (h + 1) * DK_C)
            e = e_all[:, kl]
            q = q_ref[bi, rows,kl]
            k = k_ref[bi, rows,kl]
            v = v_ref[bi, rows,h * DV_C:(h + 1) * DV_C]
            if valid < chunk:
                k = jnp.where(live, k, jnp.zeros_like(k))
                v = jnp.where(live, v, jnp.zeros_like(v))
            qf = q.astype(F32)
            kf = k.astype(F32)
            level = lambda lv: e[(2 + lv) * chunk:(3 + lv) * chunk]
            prods = [_dot_nt(q, k)] + [_dot_nt((qf * level(lv)).astype(BF16), (kf * level(lv)).astype(BF16))
                                       for lv in range(n_levels)]
            heads.append(dict(v=v, prods=prods, q_in=(qf * e[0:chunk]).astype(BF16),
                              k_out=(kf * e[chunk:2 * chunk]).astype(BF16), decay=e[chunk - 1:chunk]))
        for hd in heads:
            att = hd["prods"][0] * p_ref[0]
            for lv in range(n_levels):
                att = att + hd["prods"][1 + lv] * p_ref[1 + lv]
            hd["intra"] = _dot(att.astype(BF16), hd["v"])
            hd["update"] = _dot_tn(hd["v"], hd["k_out"])
        for h, hd in enumerate(heads):
            vl = slice(h * DV_C, (h + 1) * DV_C)
            st = st_ref[bi, h]
            o = hd["intra"] + _dot_nt(hd["q_in"], st.astype(BF16))
            st_ref[bi, h] = st * hd["decay"] + hd["update"]
            gate = r_ref[bi, rows,vl].astype(F32)
            o_ref[bi, rows,vl] = ((_rms(o) * gn_ref[...]) * (gate * jax.nn.sigmoid(gate))).astype(BF16)

    @pl.when(i == pl.num_programs(1) - 1)
    def _():
        for bi in range(seqs):
            for h in range(H_C):
                s_ref[bi, h] = st_ref[bi, h].T


def _gla(q, k, g, v, r, s0, g_norm, chunk, valid):
    b, l, _ = q.shape
    blk = min(l, CHUNKS_PER_STEP * chunk)
    n_chunks = blk // chunk
    seqs = _seqs_per_step(b, l, blk)
    m_all, masks = _gla_tables(chunk)
    qk = pl.BlockSpec((seqs, blk, H_C * DK_C), lambda bi, i: (bi, i, 0))
    vr = pl.BlockSpec((seqs, blk, H_C * DV_C), lambda bi, i: (bi, i, 0))
    state = pl.BlockSpec((seqs, H_C, DK_C, DV_C), lambda bi, i: (bi, 0, 0, 0))
    const = lambda a: pl.BlockSpec(a.shape, lambda bi, i: (0,) * a.ndim)
    gn = g_norm.reshape(1, DV_C)
    return pl.pallas_call(
        functools.partial(_gla_kernel, chunk, n_chunks, seqs, valid),
        grid=(b // seqs, l // blk),
        in_specs=[qk, qk, qk, vr, vr, state, const(gn), const(m_all), const(masks)],
        out_specs=[vr, state],
        out_shape=[jax.ShapeDtypeStruct((b, l, H_C * DV_C), BF16),
                   jax.ShapeDtypeStruct((b, H_C, DK_C, DV_C), F32)],
        scratch_shapes=[pltpu.VMEM((seqs, H_C, DV_C, DK_C), F32)],
        compiler_params=_cparams("arbitrary", "arbitrary"),
        name="gla",
    )(q, k, g, v, r, s0, gn, m_all, masks)


def _rope_tables(pos):
    inv = ROPE_THETA ** (-jnp.arange(0, DK_A, 2, dtype=F32) / DK_A)
    ang = pos.astype(F32)[:, None] * inv[None, :]
    cos = jnp.tile(jnp.cos(ang), (1, 4))
    sin = jnp.tile(jnp.concatenate([-jnp.sin(ang), jnp.sin(ang)], axis=1), (1, 2))
    return cos, sin


def _pad_seq(t, batch, seq_len):
    t = t.reshape(batch, seq_len, t.shape[-1])
    return jnp.pad(t, ((0, 0), (0, SAMPLE_PAD - seq_len), (0, 0)))


def _unpad_seq(t, seq_len):
    return t[:, :seq_len].reshape(t.shape[0] * seq_len, t.shape[-1])


def kernel(x_prompt, x_sample, cache_k, cache_v, state_ret, state_gla, state_conv, page_table, c_prompt, c_sample, w_ada, b_ada, g_pre_mix, g_post_mix, g_pre_ffn, g_post_ffn, w_in_ab, w_out_ab, lam_q1, lam_k1, lam_q2, lam_k2, g_diff, w_in_c, w_gate_c, b_gate_c, g_gla, w_out_c, w_up, conv_w, conv_b, w_down):
    rows_p, rows_s = BATCH * SEQ, DEC_BATCH * DEC_SEQ
    xp = x_prompt.reshape(rows_p, D_MODEL)
    xs = x_sample.reshape(rows_s, D_MODEL)

    n_c = BATCH + DEC_BATCH
    c_all = jnp.pad(jnp.concatenate([c_prompt, c_sample], axis=0), ((0, (-n_c) % 8), (0, 0)))
    ada = _ada(c_all, w_ada, b_ada)

    cos_p, sin_p = _rope_tables(jnp.arange(SEQ))
    cos_s, sin_s = _rope_tables(jnp.tile(PAST_LEN + jnp.arange(DEC_SEQ), DEC_BATCH))

    wup = w_up.astype(BF16)
    wdn = w_down.astype(BF16)
    k_p, v_p, k_s, v_s, ret_p, ret_s, gla_p, gla_s, conv_p, conv_s = ([] for _ in range(10))
    for l in range(DEPTH):
        mp = jnp.split(ada[l, :BATCH], 6, axis=-1)
        ms = jnp.split(ada[l, BATCH:n_c], 6, axis=-1)
        if l % 2 == 0:
            e = l // 2
            lam_init = 0.8 - 0.6 * math.exp(-0.3 * l)
            lam_vecs = jnp.stack([lam_q1[e], lam_k1[e], lam_q2[e], lam_k2[e]]).astype(F32)
            w_in = w_in_ab[e].astype(BF16)
            w_out = w_out_ab[e].astype(BF16)
            w_parts = [w_out[:H_A * DV_A], w_out[H_A * DV_A:]]

            qa, ka, va, ga, qb, kb32, kb16, vb32, vb16 = _in_proj_ab(
                xp, mp[0], mp[1], g_pre_mix[l], w_in, cos_p, sin_p, SEQ)
            seq3 = lambda t: t.reshape(BATCH, SEQ, t.shape[-1])
            ret, sp = _retention(seq3(qa), seq3(ka), seq3(va), seq3(ga),
                                 jnp.zeros((BATCH, H_A, DK_A, DV_A), F32), CHUNK_RET, CHUNK_RET)
            dif = _diff_attn_prompt(seq3(qb), seq3(kb16), seq3(vb16), lam_vecs, g_diff[e], lam_init)
            xp = _out_proj([ret.reshape(rows_p, -1), dif.reshape(rows_p, -1)], w_parts, xp, mp[2],
                           g_post_mix[l], SEQ)
            k_p.append(kb32.reshape(BATCH, SEQ, H_B, 2 * DH_B))
            v_p.append(vb32.reshape(BATCH, SEQ, H_B, DV_B))
            ret_p.append(sp)

            qa, ka, va, ga, qb, kb32, kb16, vb32, vb16 = _in_proj_ab(
                xs, ms[0], ms[1], g_pre_mix[l], w_in, cos_s, sin_s, DEC_SEQ)
            pad = lambda t: _pad_seq(t, DEC_BATCH, DEC_SEQ)
            ret, ss = _retention(pad(qa), pad(ka), pad(va), pad(ga), state_ret[e], SAMPLE_PAD, DEC_SEQ)
            n_phys = cache_k.shape[1]
            kv_rows = lambda t: t.reshape(DEC_BATCH, DEC_SEQ * H_B, DV_B)
            dif = _diff_attn_decode(
                _decode_q_rows(qb.reshape(DEC_BATCH, DEC_SEQ, 512)), kv_rows(kb16), kv_rows(vb16),
                cache_k.reshape(N_EVEN * n_phys, KV_ROWS, LANES), cache_v.reshape(N_EVEN * n_phys, KV_ROWS, LANES),
                e * n_phys, page_table, lam_vecs, g_diff[e], lam_init)
            xs = _out_proj([_unpad_seq(ret, DEC_SEQ), _unpad_seq(dif, DEC_SEQ)], w_parts, xs, ms[2],
                           g_post_mix[l], DEC_SEQ)
            k_s.append(kb32.reshape(DEC_BATCH, DEC_SEQ, H_B, 2 * DH_B))
            v_s.append(vb32.reshape(DEC_BATCH, DEC_SEQ, H_B, DV_B))
            ret_s.append(ss)
        else:
            o = l // 2
            n_main = sum(C_WIDTHS_MAIN)
            w_in = w_in_c[o][:, :n_main].astype(BF16)
            w_a = jnp.pad(w_in_c[o][:, n_main:], ((0, 0), (0, LANES - GATE_RANK))).astype(BF16)
            w_gate = jnp.pad(w_gate_c[o], ((0, LANES - GATE_RANK), (0, 0))).astype(BF16)
            w_out = w_out_c[o].astype(BF16)

            q, k, v, r, lg = _in_proj_c(xp, mp[0], mp[1], g_pre_mix[l], w_in, w_a, w_gate, b_gate_c[o], SEQ)
            seq3 = lambda t: t.reshape(BATCH, SEQ, t.shape[-1])
            og, sp = _gla(seq3(q), seq3(k), seq3(lg), seq3(v), seq3(r),
                          jnp.zeros((BATCH, H_C, DK_C, DV_C), F32), g_gla[o], CHUNK_GLA, CHUNK_GLA)
            xp = _out_proj([og.reshape(rows_p, -1)], [w_out], xp, mp[2], g_post_mix[l], SEQ)
            gla_p.append(sp)

            q, k, v, r, lg = _in_proj_c(xs, ms[0], ms[1], g_pre_mix[l], w_in, w_a, w_gate, b_gate_c[o], DEC_SEQ)
            pad = lambda t: _pad_seq(t, DEC_BATCH, DEC_SEQ)
            og, ss = _gla(pad(q), pad(k), pad(lg), pad(v), pad(r), state_gla[o], g_gla[o], SAMPLE_PAD, DEC_SEQ)
            xs = _out_proj([_unpad_seq(og, DEC_SEQ)], [w_out], xs, ms[2], g_post_mix[l], DEC_SEQ)
            gla_s.append(ss)

        xp, cbp = _conv_ffn(xp, mp[3], mp[4], mp[5], g_pre_ffn[l], g_post_ffn[l], l, wup, conv_w[l], conv_b[l],
                            wdn, SEQ)
        xs, up_s = _conv_ffn(xs, ms[3], ms[4], ms[5], g_pre_ffn[l], g_post_ffn[l], l, wup, conv_w[l], conv_b[l],
                             wdn, DEC_SEQ, state=state_conv[l])
        conv_p.append(cbp)
        conv_s.append(up_s.reshape(DEC_BATCH, DEC_SEQ, 2 * D_FF)[:, DEC_SEQ - (CONV_W - 1):])

    stack = lambda ts: ts[0][None] if len(ts) == 1 else jnp.stack(ts)
    return (xp.reshape(BATCH, SEQ, D_MODEL), xs.reshape(DEC_BATCH, DEC_SEQ, D_MODEL),
            stack(k_p), stack(v_p), stack(k_s), stack(v_s),
            stack(ret_p), stack(ret_s), stack(gla_p), stack(gla_s),
            stack(conv_p), stack(conv_s))
```

```python
import functools
import itertools
import math
from typing import NamedTuple

import numpy as np
import jax
import jax.numpy as jnp
from jax import lax
from jax.experimental import pallas as pl
from jax.experimental.pallas import tpu as pltpu

D_MODEL = 1024
BATCH = 4
SEQ = 4096
DEPTH = 2
DEC_BATCH = 32
DEC_SEQ = 4
PAST_LEN = 8192
PAGE_SIZE = 128
N_EVEN = (DEPTH + 1) // 2
N_ODD = DEPTH // 2
H_A = 4
DK_A = D_MODEL // 16
DV_A = D_MODEL // 8
CHUNK_RET = 128
H_B = 4
DH_B = D_MODEL // 16
DV_B = 2 * DH_B
H_C = 4
DK_C = D_MODEL // 8
DV_C = D_MODEL // 4
GATE_RANK = 16
GATE_TAU = 16.0
C_WIDTHS_MAIN = (H_C * DK_C, H_C * DK_C, H_C * DV_C, H_C * DV_C)
CHUNK_GLA = 64
D_FF = ((8 * D_MODEL // 3 + 127) // 128) * 128
CONV_W = 3
ROPE_THETA = 10000.0
EPS = 1e-6

LANES = 128
VMEM_LIMIT = 56 * 1024 * 1024
ROW_TILE = 1024
FFN_ROW_TILE = 1024
GLA_MATMUL_LEVEL_SPAN = 8
CHUNKS_PER_STEP = 8
SHORT_SEQS_PER_STEP = 8
SAMPLE_PAD = 16
FF_CHUNK = 256
PAGES_PER_STEP = 32
LOG2E = math.log2(math.e)
BF16 = jnp.bfloat16
F32 = jnp.float32


def _cparams(*sem):
    return pltpu.CompilerParams(dimension_semantics=sem, vmem_limit_bytes=VMEM_LIMIT)


def _dot(a, b):
    return jnp.dot(a, b, preferred_element_type=F32)


def _dot_nt(a, b):
    return lax.dot_general(a, b, (((1,), (1,)), ((), ())), preferred_element_type=F32)


def _dot_tn(a, b):
    return lax.dot_general(a, b, (((0,), (0,)), ((), ())), preferred_element_type=F32)


def _rms(x):
    return x * lax.rsqrt(jnp.mean(x * x, axis=-1, keepdims=True) + EPS)


def _modulate(x, g, shift, scale):
    return (_rms(x) * g) * (1.0 + scale) + shift


def _rope(x, cos, sin_signed):
    w = x.shape[-1]
    fwd = pltpu.roll(x, 32, axis=1)
    bwd = pltpu.roll(x, w - 32, axis=1)
    reps = w // LANES
    first_half = (lax.broadcasted_iota(jnp.int32, x.shape, 1) % 64) < 32
    partner = jnp.where(first_half, bwd, fwd)
    return (x * jnp.concatenate([cos] * reps, axis=1)
            + partner * jnp.concatenate([sin_signed] * reps, axis=1))


def _ada_kernel(c_ref, w_ref, b_ref, o_ref):
    c = c_ref[...]
    s = c * jax.nn.sigmoid(c)
    o_ref[0] = jnp.dot(s, w_ref[0], preferred_element_type=F32, precision=lax.Precision.HIGHEST) + b_ref[0]


def _ada(c_all, w_ada, b_ada):
    rows = c_all.shape[0]
    tn = 1536
    return pl.pallas_call(
        _ada_kernel,
        grid=(DEPTH, 6 * D_MODEL // tn),
        in_specs=[pl.BlockSpec((rows, D_MODEL), lambda l, j: (0, 0)),
                  pl.BlockSpec((1, D_MODEL, tn), lambda l, j: (l, 0, j)),
                  pl.BlockSpec((1, 1, tn), lambda l, j: (l, 0, j))],
        out_specs=pl.BlockSpec((1, rows, tn), lambda l, j: (l, 0, j)),
        out_shape=jax.ShapeDtypeStruct((DEPTH, rows, 6 * D_MODEL), F32),
        compiler_params=_cparams("arbitrary", "arbitrary"),
        name="adaln",
    )(c_all, w_ada, b_ada.reshape(DEPTH, 1, 6 * D_MODEL))


class _Mod(NamedTuple):
    ada: jax.Array
    layer: int
    which: int
    row0: int


def _row_cfg(rows, seq_len, row_tile=ROW_TILE):
    tm = min(row_tile, rows)
    per_token = seq_len < tm
    tiles_per_seq = 1 if per_token else seq_len // tm
    return tm, per_token, tiles_per_seq


def _mod_operand(m, tm, per_token, tiles_per_seq):
    if per_token:
        assert m.row0 % tm == 0
        return pl.BlockSpec((1, tm, D_MODEL), lambda i: (m.layer, m.row0 // tm + i, m.which)), m.ada
    depth, rows, width = m.ada.shape
    spec = pl.BlockSpec((None, 1, 1, D_MODEL), lambda i: (m.layer, m.row0 + i // tiles_per_seq, 0, m.which))
    return spec, m.ada.reshape(depth, rows, 1, width)


def _inab_kernel(x_ref, sh_ref, sc_ref, g_ref, w_ref, cos_ref, sin_ref,
                 qa_ref, ka_ref, va_ref, ga_ref, qb_ref, kb32_ref, kb16_ref, vb32_ref, vb16_ref):
    h = _modulate(x_ref[...], g_ref[...], sh_ref[0], sc_ref[0]).astype(BF16)
    cos = cos_ref[...]
    sin = sin_ref[...]

    def proj(lo, width):
        return _dot(h, w_ref[:, lo:lo + width])

    qa_ref[...] = _rope(proj(0, 256), cos, sin).astype(BF16)
    ka_ref[...] = (_rope(proj(256, 256), cos, sin) * (DK_A ** -0.5)).astype(BF16)
    va_ref[...] = proj(512, 512).astype(BF16)
    ga_ref[...] = proj(1024, 512).astype(BF16)
    qb_ref[...] = (_rope(proj(1536, 512), cos, sin) * (DH_B ** -0.5 * LOG2E)).astype(BF16)
    kb = _rope(proj(2048, 512), cos, sin)
    kb16_ref[...] = kb.astype(BF16)
    vb = proj(2560, 512)
    vb16_ref[...] = vb.astype(BF16)
    tm = kb.shape[0]
    for h in range(H_B):
        kb32_ref[pl.ds(h, tm, stride=H_B), :] = kb[:, h * DV_B:(h + 1) * DV_B]
        vb32_ref[pl.ds(h, tm, stride=H_B), :] = vb[:, h * DV_B:(h + 1) * DV_B]


def _in_proj_ab(x, shift, scale, g_pre, w_bf, cos, sin, seq_len):
    rows = x.shape[0]
    tm, per_token, tps = _row_cfg(rows, seq_len)
    (sh_spec, sh_arg), (sc_spec, sc_arg) = (_mod_operand(m, tm, per_token, tps) for m in (shift, scale))
    pos_map = (lambda i: (i, 0)) if per_token else (lambda i: (i % tps, 0))
    row = lambda w: pl.BlockSpec((tm, w), lambda i: (i, 0))
    widths = (256, 256, 512, 512, 512, 512, 512, 512, 512)
    dtypes = (BF16, BF16, BF16, BF16, BF16, F32, BF16, F32, BF16)
    cache_rows = lambda w, d: d == F32
    out_spec = lambda w, d: (pl.BlockSpec((tm * H_B, DV_B), lambda i: (i, 0)) if cache_rows(w, d) else row(w))
    out_shape = lambda w, d: jax.ShapeDtypeStruct((rows * H_B, DV_B) if cache_rows(w, d) else (rows, w), d)
    return pl.pallas_call(
        _inab_kernel,
        grid=(rows // tm,),
        in_specs=[row(D_MODEL), sh_spec, sc_spec,
                  pl.BlockSpec((1, D_MODEL), lambda i: (0, 0)),
                  pl.BlockSpec(w_bf.shape, lambda i: (0, 0)),
                  pl.BlockSpec((tm, LANES), pos_map), pl.BlockSpec((tm, LANES), pos_map)],
        out_specs=[out_spec(w, d) for w, d in zip(widths, dtypes)],
        out_shape=[out_shape(w, d) for w, d in zip(widths, dtypes)],
        compiler_params=_cparams("arbitrary"),
        name="in_proj_ab",
    )(x, sh_arg, sc_arg, g_pre.reshape(1, D_MODEL), w_bf, cos, sin)


def _inc_kernel(x_ref, sh_ref, sc_ref, g_ref, w_ref, wa_ref, wg_ref, bg_ref,
                q_ref, k_ref, v_ref, r_ref, lg_ref):
    h = _modulate(x_ref[...], g_ref[...], sh_ref[0], sc_ref[0]).astype(BF16)

    def proj(lo, width):
        return _dot(h, w_ref[:, lo:lo + width])

    q_ref[...] = (proj(0, 512) * (DK_C ** -0.5)).astype(BF16)
    k_ref[...] = proj(512, 512).astype(BF16)
    v_ref[...] = proj(1024, 1024).astype(BF16)
    r_ref[...] = proj(2048, 1024).astype(BF16)
    a = _dot(h, wa_ref[...])
    z = _dot(a.astype(BF16), wg_ref[...]) + bg_ref[...]
    softplus_neg = jnp.maximum(-z, 0.0) + jnp.log1p(jnp.exp(-jnp.abs(z)))
    lg_ref[...] = -softplus_neg / GATE_TAU


def _in_proj_c(x, shift, scale, g_pre, w_bf, wa_bf, wg_bf, b_gate, seq_len):
    rows = x.shape[0]
    tm, per_token, tps = _row_cfg(rows, seq_len)
    (sh_spec, sh_arg), (sc_spec, sc_arg) = (_mod_operand(m, tm, per_token, tps) for m in (shift, scale))
    row = lambda w: pl.BlockSpec((tm, w), lambda i: (i, 0))
    widths = (512, 512, 1024, 1024, 512)
    dtypes = (BF16, BF16, BF16, BF16, F32)
    full = lambda a: pl.BlockSpec(a.shape, lambda i: (0, 0))
    bg = b_gate.reshape(1, -1)
    return pl.pallas_call(
        _inc_kernel,
        grid=(rows // tm,),
        in_specs=[row(D_MODEL), sh_spec, sc_spec, pl.BlockSpec((1, D_MODEL), lambda i: (0, 0)),
                  full(w_bf), full(wa_bf), full(wg_bf), full(bg)],
        out_specs=[row(w) for w in widths],
        out_shape=[jax.ShapeDtypeStruct((rows, w), d) for w, d in zip(widths, dtypes)],
        compiler_params=_cparams("arbitrary"),
        name="in_proj_c",
    )(x, sh_arg, sc_arg, g_pre.reshape(1, D_MODEL), w_bf, wa_bf, wg_bf, bg)


def _outproj_kernel(n_parts, *refs):
    parts = refs[:n_parts]
    ws = refs[n_parts:2 * n_parts]
    x_ref, gate_ref, g_ref, o_ref = refs[2 * n_parts:]
    y = _dot(parts[0][...], ws[0][...])
    for p, w in zip(parts[1:], ws[1:]):
        y = y + _dot(p[...], w[...])
    o_ref[...] = x_ref[...] + gate_ref[0] * (_rms(y) * g_ref[...])


def _out_proj(parts, weights, x, gate, g_post, seq_len):
    rows = x.shape[0]
    tm, per_token, tps = _row_cfg(rows, seq_len)
    gate_spec, gate_arg = _mod_operand(gate, tm, per_token, tps)
    n = len(parts)
    return pl.pallas_call(
        functools.partial(_outproj_kernel, n),
        grid=(rows // tm,),
        in_specs=([pl.BlockSpec((tm, p.shape[1]), lambda i: (i, 0)) for p in parts]
                  + [pl.BlockSpec(w.shape, lambda i: (0, 0)) for w in weights]
                  + [pl.BlockSpec((tm, D_MODEL), lambda i: (i, 0)), gate_spec,
                     pl.BlockSpec((1, D_MODEL), lambda i: (0, 0))]),
        out_specs=pl.BlockSpec((tm, D_MODEL), lambda i: (i, 0)),
        out_shape=jax.ShapeDtypeStruct((rows, D_MODEL), F32),
        compiler_params=_cparams("arbitrary"),
        name="out_proj",
    )(*parts, *weights, x, gate_arg, g_post.reshape(1, D_MODEL))


def _ffn_kernel(per_token, seq_len, tiles_per_seq, *refs):
    if per_token:
        (x_ref, sh_ref, sc_ref, gate_ref, gpre_ref, gpost_ref, wup_ref, cw_ref, cb_ref, wdn_ref,
         s1_ref, s2_ref, xo_ref, conv_ref, h_ref, acc_ref, *u_refs) = refs
    else:
        (x_ref, sh_ref, sc_ref, gate_ref, gpre_ref, gpost_ref, wup_ref, cw_ref, cb_ref, wdn_ref,
         xo_ref, conv_ref, h_ref, acc_ref, *u_refs) = refs
    tm = x_ref.shape[0]
    halo = h_ref.shape[0] - tm
    x = x_ref[...]
    if per_token:
        tau = lax.broadcasted_iota(jnp.int32, (tm, FF_CHUNK), 0) % seq_len
    else:
        seq_start = (pl.program_id(0) % tiles_per_seq) == 0

        @pl.when(seq_start)
        def _():
            h_ref[:halo] = jnp.zeros((halo, D_MODEL), BF16)

        @pl.when(jnp.logical_not(seq_start))
        def _():
            h_ref[:halo] = h_ref[tm:]
    h_ref[halo:] = _modulate(x, gpre_ref[...], sh_ref[0], sc_ref[0]).astype(BF16)
    acc_ref[...] = jnp.zeros_like(acc_ref)

    def up(lo):
        return _dot(h_ref[...], wup_ref[:, pl.ds(lo, FF_CHUNK)])

    def conv(slot, half, lo):
        cols = pl.ds(lo, FF_CHUNK)
        cw = cw_ref[:, cols]
        cb = cb_ref[:, cols]
        taps = lambda u2, u1, u0: cb + cw[0:1] * u2 + cw[1:2] * u1 + cw[2:3] * u0
        if per_token:
            u = u_refs[slot][half]
            u1 = jnp.where(tau == 0, s1_ref[:, cols], pltpu.roll(u, 1, axis=0))
            u2 = jnp.where(tau < 2, s2_ref[:, cols], pltpu.roll(u, 2, axis=0))
            conv_ref[:, cols] = u
            return taps(u2, u1, u)
        conv_ref[0, :, cols] = u_refs[slot][half, halo + tm - 2:]
        window = lambda back: u_refs[slot][half, halo - back:halo - back + tm]
        return taps(window(2), window(1), window(0))

    def produce(slot, lo):
        u_refs[slot][0] = up(lo)
        u_refs[slot][1] = up(lo + D_FF)

    def consume(slot, lo):
        g = (jax.nn.gelu(conv(slot, 0, lo)) * conv(slot, 1, lo + D_FF)).astype(BF16)
        acc_ref[...] += _dot(g, wdn_ref[pl.ds(lo, FF_CHUNK), :])

    def body(j, carry):
        lo = pl.multiple_of(j * (2 * FF_CHUNK), 2 * FF_CHUNK)
        produce(1, lo + FF_CHUNK)
        consume(0, lo)
        produce(0, lo + 2 * FF_CHUNK)
        consume(1, lo + FF_CHUNK)
        return carry

    n_chunks = D_FF // FF_CHUNK
    produce(0, 0)
    lax.fori_loop(0, n_chunks // 2, body, 0)
    consume(0, (n_chunks - 1) * FF_CHUNK)
    xo_ref[...] = x + gate_ref[0] * (_rms(acc_ref[...]) * gpost_ref[...])


def _conv_ffn(x, shift, scale, gate, g_pre, g_post, layer, wup_bf, conv_w, conv_b, wdn_bf, seq_len, state=None):
    rows = x.shape[0]
    tm, per_token, tps = _row_cfg(rows, seq_len, FFN_ROW_TILE)
    mods = [_mod_operand(m, tm, per_token, tps) for m in (shift, scale, gate)]
    full = lambda a: pl.BlockSpec(a.shape, lambda i: (0, 0), pipeline_mode=pl.Buffered(1))
    of_layer = lambda a: pl.BlockSpec((None,) + a.shape[1:], lambda i: (layer, 0, 0), pipeline_mode=pl.Buffered(1))
    cb = conv_b.reshape(1, 2 * D_FF)
    in_specs = [pl.BlockSpec((tm, D_MODEL), lambda i: (i, 0))] + [spec for spec, _ in mods] + [
        pl.BlockSpec((1, D_MODEL), lambda i: (0, 0)), pl.BlockSpec((1, D_MODEL), lambda i: (0, 0)),
        of_layer(wup_bf), full(conv_w), full(cb), of_layer(wdn_bf)]
    args = [x] + [arg for _, arg in mods] + [
        g_pre.reshape(1, D_MODEL), g_post.reshape(1, D_MODEL), wup_bf, conv_w, cb, wdn_bf]
    halo = 0 if per_token else SAMPLE_PAD
    scratch = [pltpu.VMEM((halo + tm, D_MODEL), BF16), pltpu.VMEM((tm, D_MODEL), F32),
               pltpu.VMEM((2, halo + tm, FF_CHUNK), F32), pltpu.VMEM((2, halo + tm, FF_CHUNK), F32)]
    if per_token:
        batch = rows // seq_len
        zeros = jnp.zeros((batch, seq_len - 1, 2 * D_FF), F32)
        s1 = jnp.concatenate([state[:, 1:2], zeros], axis=1).reshape(rows, 2 * D_FF)
        s2 = jnp.concatenate([state, zeros[:, 1:]], axis=1).reshape(rows, 2 * D_FF)
        in_specs += [pl.BlockSpec((tm, 2 * D_FF), lambda i: (i, 0))] * 2
        args += [s1, s2]
        conv_spec = pl.BlockSpec((tm, 2 * D_FF), lambda i: (i, 0))
        conv_shape = jax.ShapeDtypeStruct((rows, 2 * D_FF), F32)
    else:
        conv_spec = pl.BlockSpec((1, 2, 2 * D_FF), lambda i: (i // tps, 0, 0))
        conv_shape = jax.ShapeDtypeStruct((rows // seq_len, 2, 2 * D_FF), F32)
    return pl.pallas_call(
        functools.partial(_ffn_kernel, per_token, seq_len, tps),
        grid=(rows // tm,),
        in_specs=in_specs,
        out_specs=[pl.BlockSpec((tm, D_MODEL), lambda i: (i, 0)), conv_spec],
        out_shape=[jax.ShapeDtypeStruct((rows, D_MODEL), F32), conv_shape],
        scratch_shapes=scratch,
        compiler_params=_cparams("arbitrary"),
        name="conv_ffn",
    )(*args)


def _ret_tables(chunk, valid):
    log_gamma = jnp.log1p(-jnp.exp2(-5.0 - jnp.arange(H_A, dtype=F32)))
    idx = jnp.arange(chunk, dtype=F32)
    rel = idx[:, None] - idx[None, :]
    intra = jnp.exp(jnp.where(rel[None] >= 0, rel[None] * log_gamma[:, None, None], -jnp.inf))
    dq = jnp.exp((idx + 1.0)[None, :] * log_gamma[:, None])
    dk = jnp.where(idx[None, :] < valid, jnp.exp((valid - 1.0 - idx)[None, :] * log_gamma[:, None]), 0.0)
    dc = jnp.exp(valid * log_gamma)
    rep = lambda t: jnp.broadcast_to(t[..., None], t.shape + (LANES,))
    dc_rows = rep(jnp.repeat(dc, DK_A).reshape(H_A // 2, 2 * DK_A))
    return intra, rep(dq), rep(dk), dc_rows


def _ret_kernel(chunk, n_chunks, seqs, q_ref, k_ref, v_ref, ga_ref, s0_ref, intra_ref, dq_ref, dk_ref, dc_ref,
                o_ref, s_ref, st_ref):
    i = pl.program_id(1)

    @pl.when(i == 0)
    def _():
        st_ref[...] = s0_ref[...]

    lane = lax.broadcasted_iota(jnp.int32, (chunk, LANES), 1)
    for bi in range(seqs):
        for c in range(n_chunks):
            rows = slice(c * chunk, (c + 1) * chunk)
            for pair in range(H_A // 2):
                lanes = slice(pair * LANES, (pair + 1) * LANES)
                qc = q_ref[bi, rows, lanes]
                kc = k_ref[bi, rows, lanes]
                s_pair = st_ref[bi, pair]
                s_bf = s_pair.astype(BF16)
                upd = s_pair * dc_ref[pair]
                for side in range(2):
                    hd = 2 * pair + side
                    mine = (lane >= 64) if side else (lane < 64)
                    hl = slice(hd * DV_A, (hd + 1) * DV_A)
                    qh = jnp.where(mine, qc, jnp.zeros_like(qc))
                    vh = v_ref[bi, rows, hl]
                    sc = _dot_nt(qh, kc) * intra_ref[hd]
                    o = _dot(sc.astype(BF16), vh) + _dot(qh, s_bf) * dq_ref[hd]
                    kh = jnp.where(mine, kc.astype(F32) * dk_ref[hd], 0.0).astype(BF16)
                    upd = upd + _dot_tn(kh, vh)
                    gate = ga_ref[bi, rows, hl].astype(F32)
                    o_ref[bi, rows, hl] = (_rms(o) * (gate * jax.nn.sigmoid(gate))).astype(BF16)
                st_ref[bi, pair] = upd

    @pl.when(i == pl.num_programs(1) - 1)
    def _():
        s_ref[...] = st_ref[...]


def _seqs_per_step(batch, seq_len, blk):
    return min(batch, SHORT_SEQS_PER_STEP) if seq_len == blk else 1


def _retention(q, k, v, ga, s0, chunk, valid):
    b, l, _ = q.shape
    blk = min(l, CHUNKS_PER_STEP * chunk)
    n_chunks = blk // chunk
    seqs = _seqs_per_step(b, l, blk)
    tables = _ret_tables(chunk, valid)
    seq = lambda w: pl.BlockSpec((seqs, blk, w), lambda bi, i: (bi, i, 0))
    const = lambda a: pl.BlockSpec(a.shape, lambda bi, i: (0,) * a.ndim)
    state = pl.BlockSpec((seqs, H_A // 2, 2 * DK_A, DV_A), lambda bi, i: (bi, 0, 0, 0))
    o, s = pl.pallas_call(
        functools.partial(_ret_kernel, chunk, n_chunks, seqs),
        grid=(b // seqs, l // blk),
        in_specs=[seq(256), seq(256), seq(512), seq(512), state] + [const(t) for t in tables],
        out_specs=[seq(512), state],
        out_shape=[jax.ShapeDtypeStruct((b, l, H_A * DV_A), BF16),
                   jax.ShapeDtypeStruct((b, H_A // 2, 2 * DK_A, DV_A), F32)],
        scratch_shapes=[pltpu.VMEM((seqs, H_A // 2, 2 * DK_A, DV_A), F32)],
        compiler_params=_cparams("arbitrary", "arbitrary"),
        name="retention",
    )(q, k, v, ga, s0.reshape(b, H_A // 2, 2 * DK_A, DV_A), *tables)
    return o, s.reshape(b, H_A, DK_A, DV_A)


def _lam_value(lam_ref, lam_init):
    lv = lam_ref[...]
    return (jnp.exp(jnp.sum(lv[0:1] * lv[1:2], axis=-1, keepdims=True))
            - jnp.exp(jnp.sum(lv[2:3] * lv[3:4], axis=-1, keepdims=True)) + lam_init)


def _dattn_kernel(lam_init, tq, tk, q_ref, k_ref, v_ref, lam_ref, g_ref, o_ref, vt_ref, st_ref, m_ref, l_ref, acc_ref):
    qi = pl.program_id(2)

    @pl.when(qi == 0)
    def _():
        vt_ref[...] = v_ref[0].astype(F32).T.astype(BF16)

    q = q_ref[0]
    lane = lax.broadcasted_iota(jnp.int32, q.shape, 1)
    q2 = jnp.concatenate([jnp.where(lane < DH_B, q, jnp.zeros_like(q)),
                          jnp.where(lane >= DH_B, q, jnp.zeros_like(q))], axis=0)
    m_ref[...] = jnp.full_like(m_ref, -jnp.inf)
    l_ref[...] = jnp.zeros_like(l_ref)
    acc_ref[...] = jnp.zeros_like(acc_ref)

    def scores(slot, kb, diag_offset=None):
        start = pl.multiple_of(kb * tk, tk)
        st = _dot_nt(k_ref[0, pl.ds(start, tk), :], q2)
        if diag_offset is not None:
            key = lax.broadcasted_iota(jnp.int32, st.shape, 0) + diag_offset
            qry = jnp.bitwise_and(lax.broadcasted_iota(jnp.int32, st.shape, 1), tq - 1)
            st = jnp.where(key <= qry, st, -jnp.inf)
        st_ref[slot] = st

    def absorb(slot, kb):
        start = pl.multiple_of(kb * tk, tk)
        st = st_ref[slot]
        m_old = m_ref[...]
        m_new = jnp.maximum(m_old, jnp.max(st, axis=0, keepdims=True))
        alpha = jnp.exp2(m_old - m_new)
        pt = jnp.exp2(st - m_new)
        l_ref[...] = alpha * l_ref[...] + jnp.sum(pt, axis=0, keepdims=True)
        acc_ref[...] = alpha * acc_ref[...] + _dot(vt_ref[:, pl.ds(start, tk)], pt.astype(BF16))
        m_ref[...] = m_new

    assert tq == 2 * tk
    first_diag = 2 * qi
    scores(0, first_diag, 0)
    scores(1, first_diag + 1, tk)
    absorb(0, first_diag)

    def body(t, carry):
        scores(0, 2 * t)
        absorb(1, jnp.where(t == 0, first_diag + 1, 2 * t - 1))
        scores(1, 2 * t + 1)
        absorb(0, 2 * t)
        return carry

    lax.fori_loop(0, qi, body, 0)
    absorb(1, jnp.where(qi == 0, first_diag + 1, 2 * qi - 1))

    lam = _lam_value(lam_ref, lam_init)
    inv_l = 1.0 / l_ref[...]
    acc = acc_ref[...] * inv_l
    o = (acc[:, :tq] - lam * acc[:, tq:]).T
    o_ref[0] = ((_rms(o) * g_ref[...]) * (1.0 - lam_init)).astype(BF16)


def _diff_attn_prompt(qb, kb, vb, lam_vecs, g_diff, lam_init):
    b, l, _ = qb.shape
    tq, tk = 1024, 512
    whole = pl.BlockSpec((1, l, LANES), lambda bi, h, qi: (bi, 0, h))
    return pl.pallas_call(
        functools.partial(_dattn_kernel, lam_init, tq, tk),
        grid=(b, H_B, l // tq),
        in_specs=[pl.BlockSpec((1, tq, LANES), lambda bi, h, qi: (bi, qi, h)), whole, whole,
                  pl.BlockSpec(lam_vecs.shape, lambda bi, h, qi: (0, 0)),
                  pl.BlockSpec((1, DV_B), lambda bi, h, qi: (0, 0))],
        out_specs=pl.BlockSpec((1, tq, LANES), lambda bi, h, qi: (bi, qi, h)),
        out_shape=jax.ShapeDtypeStruct((b, l, H_B * DV_B), BF16),
        scratch_shapes=[pltpu.VMEM((DV_B, l), BF16), pltpu.VMEM((2, tk, 2 * tq), F32),
                        pltpu.VMEM((1, 2 * tq), F32), pltpu.VMEM((1, 2 * tq), F32),
                        pltpu.VMEM((DV_B, 2 * tq), F32)],
        compiler_params=_cparams("arbitrary", "arbitrary", "arbitrary"),
        name="diff_attn_prompt",
    )(qb, kb, vb, lam_vecs, g_diff.reshape(1, DV_B))


Q_ROWS = H_B * 2 * 8
KV_ROWS = PAGE_SIZE * H_B


def _dattn_dec_kernel(lam_init, n_pages, pt_ref, q_ref, kn_ref, vn_ref, *refs):
    k_refs = refs[:n_pages]
    v_refs = refs[n_pages:2 * n_pages]
    lam_ref, g_ref, o_ref, m_ref, l_ref, acc_ref = refs[2 * n_pages:]
    step = pl.program_id(1)
    q = q_ref[0]

    @pl.when(step == 0)
    def _():
        s = _dot_nt(q, kn_ref[0])
        r = lax.broadcasted_iota(jnp.int32, s.shape, 0)
        c = lax.broadcasted_iota(jnp.int32, s.shape, 1)
        s = jnp.where((c % H_B == r // 16) & (c // H_B <= r % 8), s, -jnp.inf)
        m = jnp.max(s, axis=-1, keepdims=True)
        p = jnp.exp2(s - m)
        m_ref[...] = m
        l_ref[...] = jnp.sum(p, axis=-1, keepdims=True)
        acc_ref[...] = _dot(p.astype(BF16), vn_ref[0])

    own_head = (lax.broadcasted_iota(jnp.int32, (Q_ROWS, KV_ROWS), 1) % H_B
                == lax.broadcasted_iota(jnp.int32, (Q_ROWS, KV_ROWS), 0) // 16)
    bias = jnp.where(own_head, 0.0, -jnp.inf)
    scores = [_dot_nt(q, kr[0].astype(BF16)) + bias for kr in k_refs]
    m_old = m_ref[...]
    m_new = m_old
    for s in scores:
        m_new = jnp.maximum(m_new, jnp.max(s, axis=-1, keepdims=True))
    alpha = jnp.exp2(m_old - m_new)
    l_new = alpha * l_ref[...]
    acc = alpha * acc_ref[...]
    for s, vr in zip(scores, v_refs):
        p = jnp.exp2(s - m_new)
        l_new = l_new + jnp.sum(p, axis=-1, keepdims=True)
        acc = acc + _dot(p.astype(BF16), vr[0].astype(BF16))
    m_ref[...] = m_new
    l_ref[...] = l_new
    acc_ref[...] = acc

    @pl.when(step == pl.num_programs(1) - 1)
    def _():
        lam = _lam_value(lam_ref, lam_init)
        o = acc / l_new
        for h in range(H_B):
            d = o[16 * h:16 * h + 8] - lam * o[16 * h + 8:16 * h + 16]
            o_ref[0, :, h * DV_B:(h + 1) * DV_B] = ((_rms(d) * g_ref[...]) * (1.0 - lam_init)).astype(BF16)


def _diff_attn_decode(q_rows, k_new, v_new, cache_k, cache_v, page_base, page_table, lam_vecs, g_diff, lam_init):
    b = q_rows.shape[0]
    n_steps = page_table.shape[1] // PAGES_PER_STEP
    per_b = lambda r, w: pl.BlockSpec((1, r, w), lambda bi, s, pt: (bi, 0, 0))

    def page_spec(j):
        return pl.BlockSpec((1, KV_ROWS, LANES),
                            lambda bi, s, pt: (page_base + pt[bi, s * PAGES_PER_STEP + j], 0, 0))

    pages = [page_spec(j) for j in range(PAGES_PER_STEP)]
    grid_spec = pltpu.PrefetchScalarGridSpec(
        num_scalar_prefetch=1,
        grid=(b, n_steps),
        in_specs=[per_b(Q_ROWS, LANES), per_b(DEC_SEQ * H_B, LANES), per_b(DEC_SEQ * H_B, LANES)]
        + pages + pages + [pl.BlockSpec(lam_vecs.shape, lambda bi, s, pt: (0, 0)),
                           pl.BlockSpec((1, DV_B), lambda bi, s, pt: (0, 0))],
        out_specs=per_b(8, H_B * DV_B),
        scratch_shapes=[pltpu.VMEM((Q_ROWS, 1), F32), pltpu.VMEM((Q_ROWS, 1), F32),
                        pltpu.VMEM((Q_ROWS, DV_B), F32)],
    )
    return pl.pallas_call(
        functools.partial(_dattn_dec_kernel, lam_init, PAGES_PER_STEP),
        grid_spec=grid_spec,
        out_shape=jax.ShapeDtypeStruct((b, 8, H_B * DV_B), BF16),
        compiler_params=_cparams("arbitrary", "arbitrary"),
        name="diff_attn_decode",
    )(page_table, q_rows, k_new, v_new, *([cache_k] * PAGES_PER_STEP), *([cache_v] * PAGES_PER_STEP),
      lam_vecs, g_diff.reshape(1, DV_B))


def _decode_q_rows(qb):
    b = qb.shape[0]
    q = qb.reshape(b, DEC_SEQ, H_B, DV_B).transpose(0, 2, 1, 3)
    q = jnp.pad(q, ((0, 0), (0, 0), (0, 8 - DEC_SEQ), (0, 0)))[:, :, None]
    keep = (np.arange(DV_B)[None, :] // DH_B) == np.arange(2)[:, None]
    keep = jnp.asarray(keep)[None, None, :, None, :]
    return jnp.where(keep, q, jnp.zeros_like(q)).reshape(b, Q_ROWS, DV_B)


def _gla_tables(chunk):
    idx = np.arange(chunk)
    i, t = idx[:, None], idx[None, :]
    mats = [t <= i]
    masks = [i == t]
    s = 1
    while s < chunk:
        same = (t // s) == (i // s)
        odd = ((i // s) % 2) == 1
        if 2 * s <= GLA_MATMUL_LEVEL_SPAN:
            mats.append(same & np.where(odd, t <= i, t > i))
        masks.append(((i // (2 * s)) == (t // (2 * s))) & odd & (((t // s) % 2) == 0))
        s *= 2
    return (jnp.asarray(np.concatenate(mats, axis=0).astype(np.float32), BF16),
            jnp.asarray(np.stack(masks).astype(np.float32)))


def _gla_kernel(chunk, n_chunks, seqs, valid, q_ref, k_ref, g_ref, v_ref, r_ref, s0_ref, gn_ref, m_ref, p_ref,
                o_ref, s_ref, st_ref):
    i = pl.program_id(1)

    @pl.when(i == 0)
    def _():
        for bi in range(seqs):
            for h in range(H_C):
                st_ref[bi, h] = s0_ref[bi, h].T

    n_levels = p_ref.shape[0] - 1
    m_all = m_ref[...]
    live = lax.broadcasted_iota(jnp.int32, (chunk, 1), 0) < valid
    for bi, c in itertools.product(range(seqs), range(n_chunks)):
        rows = slice(c * chunk, (c + 1) * chunk)
        g = g_ref[bi, rows,:]
        if valid < chunk:
            g = jnp.where(live, g, 0.0)
        g_hi = g.astype(BF16)
        g_lo = (g - g_hi.astype(F32)).astype(BF16)
        sums = _dot(m_all, g_hi) + _dot(m_all, g_lo)
        cum = sums[0:chunk]
        exponents = [cum, cum[chunk - 1:chunk] - cum]
        n_mats = m_all.shape[0] // chunk
        for lv in range(n_levels):
            if lv + 1 < n_mats:
                exponents.append(sums[(1 + lv) * chunk:(2 + lv) * chunk])
            else:
                s = 2 ** lv
                pairs = cum.reshape(chunk // (2 * s), 2 * s, cum.shape[-1])
                rel = pairs - pairs[:, s - 1:s, :]
                upper = lax.broadcasted_iota(jnp.int32, rel.shape, 1) >= s
                exponents.append(jnp.where(upper, rel, -rel).reshape(cum.shape))
        e_all = jnp.exp(jnp.concatenate(exponents, axis=0))
        heads = []
        for h in range(H_C):
            kl = slice(h * DK_C, (h + 1) * DK_C)
            e = e_all[:, kl]
            q = q_ref[bi, rows,kl]
            k = k_ref[bi, rows,kl]
            v = v_ref[bi, rows,h * DV_C:(h + 1) * DV_C]
            if valid < chunk:
                k = jnp.where(live, k, jnp.zeros_like(k))
                v = jnp.where(live, v, jnp.zeros_like(v))
            qf = q.astype(F32)
            kf = k.astype(F32)
            level = lambda lv: e[(2 + lv) * chunk:(3 + lv) * chunk]
            prods = [_dot_nt(q, k)] + [_dot_nt((qf * level(lv)).astype(BF16), (kf * level(lv)).astype(BF16))
                                       for lv in range(n_levels)]
            heads.append(dict(v=v, prods=prods, q_in=(qf * e[0:chunk]).astype(BF16),
                              k_out=(kf * e[chunk:2 * chunk]).astype(BF16), decay=e[chunk - 1:chunk]))
        for hd in heads:
            att = hd["prods"][0] * p_ref[0]
            for lv in range(n_levels):
                att = att + hd["prods"][1 + lv] * p_ref[1 + lv]
            hd["intra"] = _dot(att.astype(BF16), hd["v"])
            hd["update"] = _dot_tn(hd["v"], hd["k_out"])
        for h, hd in enumerate(heads):
            vl = slice(h * DV_C, (h + 1) * DV_C)
            st = st_ref[bi, h]
            o = hd["intra"] + _dot_nt(hd["q_in"], st.astype(BF16))
            st_ref[bi, h] = st * hd["decay"] + hd["update"]
            gate = r_ref[bi, rows,vl].astype(F32)
            o_ref[bi, rows,vl] = ((_rms(o) * gn_ref[...]) * (gate * jax.nn.sigmoid(gate))).astype(BF16)

    @pl.when(i == pl.num_programs(1) - 1)
    def _():
        for bi in range(seqs):
            for h in range(H_C):
                s_ref[bi, h] = st_ref[bi, h].T


def _gla(q, k, g, v, r, s0, g_norm, chunk, valid):
    b, l, _ = q.shape
    blk = min(l, CHUNKS_PER_STEP * chunk)
    n_chunks = blk // chunk
    seqs = _seqs_per_step(b, l, blk)
    m_all, masks = _gla_tables(chunk)
    qk = pl.BlockSpec((seqs, blk, H_C * DK_C), lambda bi, i: (bi, i, 0))
    vr = pl.BlockSpec((seqs, blk, H_C * DV_C), lambda bi, i: (bi, i, 0))
    state = pl.BlockSpec((seqs, H_C, DK_C, DV_C), lambda bi, i: (bi, 0, 0, 0))
    const = lambda a: pl.BlockSpec(a.shape, lambda bi, i: (0,) * a.ndim)
    gn = g_norm.reshape(1, DV_C)
    return pl.pallas_call(
        functools.partial(_gla_kernel, chunk, n_chunks, seqs, valid),
        grid=(b // seqs, l // blk),
        in_specs=[qk, qk, qk, vr, vr, state, const(gn), const(m_all), const(masks)],
        out_specs=[vr, state],
        out_shape=[jax.ShapeDtypeStruct((b, l, H_C * DV_C), BF16),
                   jax.ShapeDtypeStruct((b, H_C, DK_C, DV_C), F32)],
        scratch_shapes=[pltpu.VMEM((seqs, H_C, DV_C, DK_C), F32)],
        compiler_params=_cparams("arbitrary", "arbitrary"),
        name="gla",
    )(q, k, g, v, r, s0, gn, m_all, masks)


def _rope_tables(pos):
    inv = ROPE_THETA ** (-jnp.arange(0, DK_A, 2, dtype=F32) / DK_A)
    ang = pos.astype(F32)[:, None] * inv[None, :]
    cos = jnp.tile(jnp.cos(ang), (1, 4))
    sin = jnp.tile(jnp.concatenate([-jnp.sin(ang), jnp.sin(ang)], axis=1), (1, 2))
    return cos, sin


def _pad_seq(t, batch, seq_len):
    t = t.reshape(batch, seq_len, t.shape[-1])
    return jnp.pad(t, ((0, 0), (0, SAMPLE_PAD - seq_len), (0, 0)))


def _unpad_seq(t, seq_len):
    return t[:, :seq_len].reshape(t.shape[0] * seq_len, t.shape[-1])


def kernel(x_prompt, x_sample, cache_k, cache_v, state_ret, state_gla, state_conv, page_table, c_prompt, c_sample, w_ada, b_ada, g_pre_mix, g_post_mix, g_pre_ffn, g_post_ffn, w_in_ab, w_out_ab, lam_q1, lam_k1, lam_q2, lam_k2, g_diff, w_in_c, w_gate_c, b_gate_c, g_gla, w_out_c, w_up, conv_w, conv_b, w_down):
    rows_p, rows_s = BATCH * SEQ, DEC_BATCH * DEC_SEQ
    xp = x_prompt.reshape(rows_p, D_MODEL)
    xs = x_sample.reshape(rows_s, D_MODEL)

    n_c = rows_s + BATCH
    c_all = jnp.pad(jnp.concatenate([jnp.repeat(c_sample, DEC_SEQ, axis=0), c_prompt], axis=0),
                    ((0, (-n_c) % 8), (0, 0)))
    ada = _ada(c_all, w_ada, b_ada)

    cos_p, sin_p = _rope_tables(jnp.arange(SEQ))
    cos_s, sin_s = _rope_tables(jnp.tile(PAST_LEN + jnp.arange(DEC_SEQ), DEC_BATCH))

    wup = w_up.astype(BF16)
    wdn = w_down.astype(BF16)
    k_p, v_p, k_s, v_s, ret_p, ret_s, gla_p, gla_s, conv_p, conv_s = ([] for _ in range(10))
    for l in range(DEPTH):
        ms = [_Mod(ada, l, which, 0) for which in range(6)]
        mp = [_Mod(ada, l, which, rows_s) for which in range(6)]
        if l % 2 == 0:
            e = l // 2
            lam_init = 0.8 - 0.6 * math.exp(-0.3 * l)
            lam_vecs = jnp.stack([lam_q1[e], lam_k1[e], lam_q2[e], lam_k2[e]]).astype(F32)
            w_in = w_in_ab[e].astype(BF16)
            w_out = w_out_ab[e].astype(BF16)
            w_parts = [w_out[:H_A * DV_A], w_out[H_A * DV_A:]]

            qa, ka, va, ga, qb, kb32, kb16, vb32, vb16 = _in_proj_ab(
                xp, mp[0], mp[1], g_pre_mix[l], w_in, cos_p, sin_p, SEQ)
            seq3 = lambda t: t.reshape(BATCH, SEQ, t.shape[-1])
            ret, sp = _retention(seq3(qa), seq3(ka), seq3(va), seq3(ga),
                                 jnp.zeros((BATCH, H_A, DK_A, DV_A), F32), CHUNK_RET, CHUNK_RET)
            dif = _diff_attn_prompt(seq3(qb), seq3(kb16), seq3(vb16), lam_vecs, g_diff[e], lam_init)
            xp = _out_proj([ret.reshape(rows_p, -1), dif.reshape(rows_p, -1)], w_parts, xp, mp[2],
                           g_post_mix[l], SEQ)
            k_p.append(kb32.reshape(BATCH, SEQ, H_B, 2 * DH_B))
            v_p.append(vb32.reshape(BATCH, SEQ, H_B, DV_B))
            ret_p.append(sp)

            qa, ka, va, ga, qb, kb32, kb16, vb32, vb16 = _in_proj_ab(
                xs, ms[0], ms[1], g_pre_mix[l], w_in, cos_s, sin_s, DEC_SEQ)
            pad = lambda t: _pad_seq(t, DEC_BATCH, DEC_SEQ)
            ret, ss = _retention(pad(qa), pad(ka), pad(va), pad(ga), state_ret[e], SAMPLE_PAD, DEC_SEQ)
            n_phys = cache_k.shape[1]
            kv_rows = lambda t: t.reshape(DEC_BATCH, DEC_SEQ * H_B, DV_B)
            dif = _diff_attn_decode(
                _decode_q_rows(qb.reshape(DEC_BATCH, DEC_SEQ, 512)), kv_rows(kb16), kv_rows(vb16),
                cache_k.reshape(N_EVEN * n_phys, KV_ROWS, LANES), cache_v.reshape(N_EVEN * n_phys, KV_ROWS, LANES),
                e * n_phys, page_table, lam_vecs, g_diff[e], lam_init)
            xs = _out_proj([_unpad_seq(ret, DEC_SEQ), _unpad_seq(dif, DEC_SEQ)], w_parts, xs, ms[2],
                           g_post_mix[l], DEC_SEQ)
            k_s.append(kb32.reshape(DEC_BATCH, DEC_SEQ, H_B, 2 * DH_B))
            v_s.append(vb32.reshape(DEC_BATCH, DEC_SEQ, H_B, DV_B))
            ret_s.append(ss)
        else:
            o = l // 2
            n_main = sum(C_WIDTHS_MAIN)
            w_in = w_in_c[o][:, :n_main].astype(BF16)
            w_a = jnp.pad(w_in_c[o][:, n_main:], ((0, 0), (0, LANES - GATE_RANK))).astype(BF16)
            w_gate = jnp.pad(w_gate_c[o], ((0, LANES - GATE_RANK), (0, 0))).astype(BF16)
            w_out = w_out_c[o].astype(BF16)

            q, k, v, r, lg = _in_proj_c(xp, mp[0], mp[1], g_pre_mix[l], w_in, w_a, w_gate, b_gate_c[o], SEQ)
            seq3 = lambda t: t.reshape(BATCH, SEQ, t.shape[-1])
            og, sp = _gla(seq3(q), seq3(k), seq3(lg), seq3(v), seq3(r),
                          jnp.zeros((BATCH, H_C, DK_C, DV_C), F32), g_gla[o], CHUNK_GLA, CHUNK_GLA)
            xp = _out_proj([og.reshape(rows_p, -1)], [w_out], xp, mp[2], g_post_mix[l], SEQ)
            gla_p.append(sp)

            q, k, v, r, lg = _in_proj_c(xs, ms[0], ms[1], g_pre_mix[l], w_in, w_a, w_gate, b_gate_c[o], DEC_SEQ)
            pad = lambda t: _pad_seq(t, DEC_BATCH, DEC_SEQ)
            og, ss = _gla(pad(q), pad(k), pad(lg), pad(v), pad(r), state_gla[o], g_gla[o], SAMPLE_PAD, DEC_SEQ)
            xs = _out_proj([_unpad_seq(og, DEC_SEQ)], [w_out], xs, ms[2], g_post_mix[l], DEC_SEQ)
            gla_s.append(ss)

        xp, cbp = _conv_ffn(xp, mp[3], mp[4], mp[5], g_pre_ffn[l], g_post_ffn[l], l, wup, conv_w[l], conv_b[l],
                            wdn, SEQ)
        xs, up_s = _conv_ffn(xs, ms[3], ms[4], ms[5], g_pre_ffn[l], g_post_ffn[l], l, wup, conv_w[l], conv_b[l],
                             wdn, DEC_SEQ, state=state_conv[l])
        conv_p.append(cbp)
        conv_s.append(up_s.reshape(DEC_BATCH, DEC_SEQ, 2 * D_FF)[:, DEC_SEQ - (CONV_W - 1):])

    stack = lambda ts: ts[0][None] if len(ts) == 1 else jnp.stack(ts)
    return (xp.reshape(BATCH, SEQ, D_MODEL), xs.reshape(DEC_BATCH, DEC_SEQ, D_MODEL),
            stack(k_p), stack(v_p), stack(k_s), stack(v_s),
            stack(ret_p), stack(ret_s), stack(gla_p), stack(gla_s),
            stack(conv_p), stack(conv_s))
```

```python
import functools
import itertools
import math
from typing import NamedTuple

import numpy as np
import jax
import jax.numpy as jnp
from jax import lax
from jax.experimental import pallas as pl
from jax.experimental.pallas import tpu as pltpu

D_MODEL = 1024
BATCH = 4
SEQ = 4096
DEPTH = 2
DEC_BATCH = 32
DEC_SEQ = 4
PAST_LEN = 8192
PAGE_SIZE = 128
N_EVEN = (DEPTH + 1) // 2
N_ODD = DEPTH // 2
H_A = 4
DK_A = D_MODEL // 16
DV_A = D_MODEL // 8
CHUNK_RET = 128
H_B = 4
DH_B = D_MODEL // 16
DV_B = 2 * DH_B
H_C = 4
DK_C = D_MODEL // 8
DV_C = D_MODEL // 4
GATE_RANK = 16
GATE_TAU = 16.0
C_WIDTHS_MAIN = (H_C * DK_C, H_C * DK_C, H_C * DV_C, H_C * DV_C)
CHUNK_GLA = 64
D_FF = ((8 * D_MODEL // 3 + 127) // 128) * 128
CONV_W = 3
ROPE_THETA = 10000.0
EPS = 1e-6

LANES = 128
VMEM_LIMIT = 56 * 1024 * 1024
ROW_TILE = 1024
FFN_ROW_TILE = 1024
F32_SUBLANES = 8
BF16_SUBLANES = 16
GLA_MATMUL_LEVEL_SPAN = F32_SUBLANES
CHUNKS_PER_STEP = 8
SHORT_SEQS_PER_STEP = 8
SAMPLE_PAD = BF16_SUBLANES
FF_CHUNK = 256
PAGES_PER_STEP = 32
ROPE_HALF = DK_A // 2
LOG2E = math.log2(math.e)
BF16 = jnp.bfloat16
F32 = jnp.float32


def _cparams(*sem):
    return pltpu.CompilerParams(dimension_semantics=sem, vmem_limit_bytes=VMEM_LIMIT)


def _dot(a, b):
    return jnp.dot(a, b, preferred_element_type=F32)


def _dot_nt(a, b):
    return lax.dot_general(a, b, (((1,), (1,)), ((), ())), preferred_element_type=F32)


def _dot_tn(a, b):
    return lax.dot_general(a, b, (((0,), (0,)), ((), ())), preferred_element_type=F32)


def _rms(x):
    return x * lax.rsqrt(jnp.mean(x * x, axis=-1, keepdims=True) + EPS)


def _modulate(x, g, shift, scale):
    return (_rms(x) * g) * (1.0 + scale) + shift


def _rope(x, cos, sin_signed):
    w = x.shape[-1]
    fwd = pltpu.roll(x, ROPE_HALF, axis=1)
    bwd = pltpu.roll(x, w - ROPE_HALF, axis=1)
    reps = w // LANES
    first_half = (lax.broadcasted_iota(jnp.int32, x.shape, 1) % DK_A) < ROPE_HALF
    partner = jnp.where(first_half, bwd, fwd)
    return (x * jnp.concatenate([cos] * reps, axis=1)
            + partner * jnp.concatenate([sin_signed] * reps, axis=1))


def _ada_kernel(n_tok, c_ref, w_ref, b_ref, tok_ref, seq_ref):
    c = c_ref[...]
    s = c * jax.nn.sigmoid(c)
    o = jnp.dot(s, w_ref[0], preferred_element_type=F32, precision=lax.Precision.HIGHEST) + b_ref[0]
    tok_ref[0] = o[:n_tok]
    for r in range(seq_ref.shape[1]):
        seq_ref[0, r] = o[n_tok + r:n_tok + r + 1]


def _ada(c_all, n_tok, w_ada, b_ada):
    rows = c_all.shape[0]
    n_seq = rows - n_tok
    tn = 1536
    return pl.pallas_call(
        functools.partial(_ada_kernel, n_tok),
        grid=(DEPTH, 6 * D_MODEL // tn),
        in_specs=[pl.BlockSpec((rows, D_MODEL), lambda l, j: (0, 0)),
                  pl.BlockSpec((1, D_MODEL, tn), lambda l, j: (l, 0, j)),
                  pl.BlockSpec((1, 1, tn), lambda l, j: (l, 0, j))],
        out_specs=[pl.BlockSpec((1, n_tok, tn), lambda l, j: (l, 0, j)),
                   pl.BlockSpec((1, n_seq, 1, tn), lambda l, j: (l, 0, 0, j))],
        out_shape=[jax.ShapeDtypeStruct((DEPTH, n_tok, 6 * D_MODEL), F32),
                   jax.ShapeDtypeStruct((DEPTH, n_seq, 1, 6 * D_MODEL), F32)],
        compiler_params=_cparams("arbitrary", "arbitrary"),
        name="adaln",
    )(c_all, w_ada, b_ada.reshape(DEPTH, 1, 6 * D_MODEL))


class _Mod(NamedTuple):
    ada: jax.Array
    layer: int
    which: int


def _row_cfg(rows, seq_len, row_tile=ROW_TILE):
    tm = min(row_tile, rows)
    per_token = seq_len < tm
    tiles_per_seq = 1 if per_token else seq_len // tm
    return tm, per_token, tiles_per_seq


def _mod_operand(m, tm, per_token, tiles_per_seq):
    if per_token:
        return pl.BlockSpec((1, tm, D_MODEL), lambda i: (m.layer, i, m.which)), m.ada
    return pl.BlockSpec((None, 1, 1, D_MODEL), lambda i: (m.layer, i // tiles_per_seq, 0, m.which)), m.ada


def _inab_kernel(x_ref, sh_ref, sc_ref, g_ref, w_ref, cos_ref, sin_ref,
                 qa_ref, ka_ref, va_ref, ga_ref, qb_ref, kb32_ref, kb16_ref, vb32_ref, vb16_ref):
    h = _modulate(x_ref[...], g_ref[...], sh_ref[0], sc_ref[0]).astype(BF16)
    cos = cos_ref[...]
    sin = sin_ref[...]

    def proj(lo, width):
        return _dot(h, w_ref[:, lo:lo + width])

    qa_ref[...] = _rope(proj(0, 256), cos, sin).astype(BF16)
    ka_ref[...] = (_rope(proj(256, 256), cos, sin) * (DK_A ** -0.5)).astype(BF16)
    va_ref[...] = proj(512, 512).astype(BF16)
    ga_ref[...] = proj(1024, 512).astype(BF16)
    qb_ref[...] = (_rope(proj(1536, 512), cos, sin) * (DH_B ** -0.5 * LOG2E)).astype(BF16)
    kb = _rope(proj(2048, 512), cos, sin)
    kb16_ref[...] = kb.astype(BF16)
    vb = proj(2560, 512)
    vb16_ref[...] = vb.astype(BF16)
    tm = kb.shape[0]
    for h in range(H_B):
        kb32_ref[pl.ds(h, tm, stride=H_B), :] = kb[:, h * DV_B:(h + 1) * DV_B]
        vb32_ref[pl.ds(h, tm, stride=H_B), :] = vb[:, h * DV_B:(h + 1) * DV_B]


def _in_proj_ab(x, shift, scale, g_pre, w_bf, cos, sin, seq_len):
    rows = x.shape[0]
    tm, per_token, tps = _row_cfg(rows, seq_len)
    (sh_spec, sh_arg), (sc_spec, sc_arg) = (_mod_operand(m, tm, per_token, tps) for m in (shift, scale))
    pos_map = (lambda i: (i, 0)) if per_token else (lambda i: (i % tps, 0))
    row = lambda w: pl.BlockSpec((tm, w), lambda i: (i, 0))
    widths = (256, 256, 512, 512, 512, 512, 512, 512, 512)
    dtypes = (BF16, BF16, BF16, BF16, BF16, F32, BF16, F32, BF16)
    cache_rows = lambda w, d: d == F32
    out_spec = lambda w, d: (pl.BlockSpec((tm * H_B, DV_B), lambda i: (i, 0)) if cache_rows(w, d) else row(w))
    out_shape = lambda w, d: jax.ShapeDtypeStruct((rows * H_B, DV_B) if cache_rows(w, d) else (rows, w), d)
    return pl.pallas_call(
        _inab_kernel,
        grid=(rows // tm,),
        in_specs=[row(D_MODEL), sh_spec, sc_spec,
                  pl.BlockSpec((1, D_MODEL), lambda i: (0, 0)),
                  pl.BlockSpec(w_bf.shape, lambda i: (0, 0)),
                  pl.BlockSpec((tm, LANES), pos_map), pl.BlockSpec((tm, LANES), pos_map)],
        out_specs=[out_spec(w, d) for w, d in zip(widths, dtypes)],
        out_shape=[out_shape(w, d) for w, d in zip(widths, dtypes)],
        compiler_params=_cparams("arbitrary"),
        name="in_proj_ab",
    )(x, sh_arg, sc_arg, g_pre.reshape(1, D_MODEL), w_bf, cos, sin)


def _inc_kernel(x_ref, sh_ref, sc_ref, g_ref, w_ref, wa_ref, wg_ref, bg_ref,
                q_ref, k_ref, v_ref, r_ref, lg_ref):
    h = _modulate(x_ref[...], g_ref[...], sh_ref[0], sc_ref[0]).astype(BF16)

    def proj(lo, width):
        return _dot(h, w_ref[:, lo:lo + width])

    q_ref[...] = (proj(0, 512) * (DK_C ** -0.5)).astype(BF16)
    k_ref[...] = proj(512, 512).astype(BF16)
    v_ref[...] = proj(1024, 1024).astype(BF16)
    r_ref[...] = proj(2048, 1024).astype(BF16)
    a = _dot(h, wa_ref[...])
    z = _dot(a.astype(BF16), wg_ref[...]) + bg_ref[...]
    softplus_neg = jnp.maximum(-z, 0.0) + jnp.log1p(jnp.exp(-jnp.abs(z)))
    lg_ref[...] = -softplus_neg / GATE_TAU


def _in_proj_c(x, shift, scale, g_pre, w_bf, wa_bf, wg_bf, b_gate, seq_len):
    rows = x.shape[0]
    tm, per_token, tps = _row_cfg(rows, seq_len)
    (sh_spec, sh_arg), (sc_spec, sc_arg) = (_mod_operand(m, tm, per_token, tps) for m in (shift, scale))
    row = lambda w: pl.BlockSpec((tm, w), lambda i: (i, 0))
    widths = (512, 512, 1024, 1024, 512)
    dtypes = (BF16, BF16, BF16, BF16, F32)
    full = lambda a: pl.BlockSpec(a.shape, lambda i: (0, 0))
    bg = b_gate.reshape(1, -1)
    return pl.pallas_call(
        _inc_kernel,
        grid=(rows // tm,),
        in_specs=[row(D_MODEL), sh_spec, sc_spec, pl.BlockSpec((1, D_MODEL), lambda i: (0, 0)),
                  full(w_bf), full(wa_bf), full(wg_bf), full(bg)],
        out_specs=[row(w) for w in widths],
        out_shape=[jax.ShapeDtypeStruct((rows, w), d) for w, d in zip(widths, dtypes)],
        compiler_params=_cparams("arbitrary"),
        name="in_proj_c",
    )(x, sh_arg, sc_arg, g_pre.reshape(1, D_MODEL), w_bf, wa_bf, wg_bf, bg)


def _outproj_kernel(n_parts, *refs):
    parts = refs[:n_parts]
    ws = refs[n_parts:2 * n_parts]
    x_ref, gate_ref, g_ref, o_ref = refs[2 * n_parts:]
    y = _dot(parts[0][...], ws[0][...])
    for p, w in zip(parts[1:], ws[1:]):
        y = y + _dot(p[...], w[...])
    o_ref[...] = x_ref[...] + gate_ref[0] * (_rms(y) * g_ref[...])


def _out_proj(parts, weights, x, gate, g_post, seq_len):
    rows = x.shape[0]
    tm, per_token, tps = _row_cfg(rows, seq_len)
    gate_spec, gate_arg = _mod_operand(gate, tm, per_token, tps)
    n = len(parts)
    return pl.pallas_call(
        functools.partial(_outproj_kernel, n),
        grid=(rows // tm,),
        in_specs=([pl.BlockSpec((tm, p.shape[1]), lambda i: (i, 0)) for p in parts]
                  + [pl.BlockSpec(w.shape, lambda i: (0, 0)) for w in weights]
                  + [pl.BlockSpec((tm, D_MODEL), lambda i: (i, 0)), gate_spec,
                     pl.BlockSpec((1, D_MODEL), lambda i: (0, 0))]),
        out_specs=pl.BlockSpec((tm, D_MODEL), lambda i: (i, 0)),
        out_shape=jax.ShapeDtypeStruct((rows, D_MODEL), F32),
        compiler_params=_cparams("arbitrary"),
        name="out_proj",
    )(*parts, *weights, x, gate_arg, g_post.reshape(1, D_MODEL))


def _ffn_kernel(per_token, seq_len, tiles_per_seq, *refs):
    if per_token:
        (x_ref, sh_ref, sc_ref, gate_ref, gpre_ref, gpost_ref, wup_ref, cw_ref, cb_ref, wdn_ref,
         s1_ref, s2_ref, xo_ref, conv_ref, h_ref, acc_ref, *u_refs) = refs
    else:
        (x_ref, sh_ref, sc_ref, gate_ref, gpre_ref, gpost_ref, wup_ref, cw_ref, cb_ref, wdn_ref,
         xo_ref, conv_ref, h_ref, acc_ref, *u_refs) = refs
    tm = x_ref.shape[0]
    halo = h_ref.shape[0] - tm
    x = x_ref[...]
    if per_token:
        tau = lax.broadcasted_iota(jnp.int32, (tm, FF_CHUNK), 0) % seq_len
    else:
        seq_start = (pl.program_id(0) % tiles_per_seq) == 0

        @pl.when(seq_start)
        def _():
            h_ref[:halo] = jnp.zeros((halo, D_MODEL), BF16)

        @pl.when(jnp.logical_not(seq_start))
        def _():
            h_ref[:halo] = h_ref[tm:]
    h_ref[halo:] = _modulate(x, gpre_ref[...], sh_ref[0], sc_ref[0]).astype(BF16)
    acc_ref[...] = jnp.zeros_like(acc_ref)

    def up(lo):
        return _dot(h_ref[...], wup_ref[:, pl.ds(lo, FF_CHUNK)])

    def conv(slot, half, lo):
        cols = pl.ds(lo, FF_CHUNK)
        cw = cw_ref[:, cols]
        cb = cb_ref[:, cols]
        taps = lambda u2, u1, u0: cb + cw[0:1] * u2 + cw[1:2] * u1 + cw[2:3] * u0
        if per_token:
            u = u_refs[slot][half]
            u1 = jnp.where(tau == 0, s1_ref[:, cols], pltpu.roll(u, 1, axis=0))
            u2 = jnp.where(tau < 2, s2_ref[:, cols], pltpu.roll(u, 2, axis=0))
            conv_ref[:, cols] = u
            return taps(u2, u1, u)
        conv_ref[0, :, cols] = u_refs[slot][half, halo + tm - 2:]
        window = lambda back: u_refs[slot][half, halo - back:halo - back + tm]
        return taps(window(2), window(1), window(0))

    def produce(slot, lo):
        u_refs[slot][0] = up(lo)
        u_refs[slot][1] = up(lo + D_FF)

    def consume(slot, lo):
        g = (jax.nn.gelu(conv(slot, 0, lo)) * conv(slot, 1, lo + D_FF)).astype(BF16)
        acc_ref[...] += _dot(g, wdn_ref[pl.ds(lo, FF_CHUNK), :])

    def body(j, carry):
        lo = pl.multiple_of(j * (2 * FF_CHUNK), 2 * FF_CHUNK)
        produce(1, lo + FF_CHUNK)
        consume(0, lo)
        produce(0, lo + 2 * FF_CHUNK)
        consume(1, lo + FF_CHUNK)
        return carry

    n_chunks = D_FF // FF_CHUNK
    produce(0, 0)
    lax.fori_loop(0, n_chunks // 2, body, 0)
    consume(0, (n_chunks - 1) * FF_CHUNK)
    xo_ref[...] = x + gate_ref[0] * (_rms(acc_ref[...]) * gpost_ref[...])


def _conv_ffn(x, shift, scale, gate, g_pre, g_post, layer, wup_bf, conv_w, conv_b, wdn_bf, seq_len, state=None):
    rows = x.shape[0]
    tm, per_token, tps = _row_cfg(rows, seq_len, FFN_ROW_TILE)
    mods = [_mod_operand(m, tm, per_token, tps) for m in (shift, scale, gate)]
    full = lambda a: pl.BlockSpec(a.shape, lambda i: (0, 0), pipeline_mode=pl.Buffered(1))
    of_layer = lambda a: pl.BlockSpec((None,) + a.shape[1:], lambda i: (layer, 0, 0), pipeline_mode=pl.Buffered(1))
    cb = conv_b.reshape(1, 2 * D_FF)
    in_specs = [pl.BlockSpec((tm, D_MODEL), lambda i: (i, 0))] + [spec for spec, _ in mods] + [
        pl.BlockSpec((1, D_MODEL), lambda i: (0, 0)), pl.BlockSpec((1, D_MODEL), lambda i: (0, 0)),
        of_layer(wup_bf), full(conv_w), full(cb), of_layer(wdn_bf)]
    args = [x] + [arg for _, arg in mods] + [
        g_pre.reshape(1, D_MODEL), g_post.reshape(1, D_MODEL), wup_bf, conv_w, cb, wdn_bf]
    halo = 0 if per_token else BF16_SUBLANES
    scratch = [pltpu.VMEM((halo + tm, D_MODEL), BF16), pltpu.VMEM((tm, D_MODEL), F32),
               pltpu.VMEM((2, halo + tm, FF_CHUNK), F32), pltpu.VMEM((2, halo + tm, FF_CHUNK), F32)]
    if per_token:
        batch = rows // seq_len
        zeros = jnp.zeros((batch, seq_len - 1, 2 * D_FF), F32)
        s1 = jnp.concatenate([state[:, 1:2], zeros], axis=1).reshape(rows, 2 * D_FF)
        s2 = jnp.concatenate([state, zeros[:, 1:]], axis=1).reshape(rows, 2 * D_FF)
        in_specs += [pl.BlockSpec((tm, 2 * D_FF), lambda i: (i, 0))] * 2
        args += [s1, s2]
        conv_spec = pl.BlockSpec((tm, 2 * D_FF), lambda i: (i, 0))
        conv_shape = jax.ShapeDtypeStruct((rows, 2 * D_FF), F32)
    else:
        conv_spec = pl.BlockSpec((1, 2, 2 * D_FF), lambda i: (i // tps, 0, 0))
        conv_shape = jax.ShapeDtypeStruct((rows // seq_len, 2, 2 * D_FF), F32)
    return pl.pallas_call(
        functools.partial(_ffn_kernel, per_token, seq_len, tps),
        grid=(rows // tm,),
        in_specs=in_specs,
        out_specs=[pl.BlockSpec((tm, D_MODEL), lambda i: (i, 0)), conv_spec],
        out_shape=[jax.ShapeDtypeStruct((rows, D_MODEL), F32), conv_shape],
        scratch_shapes=scratch,
        compiler_params=_cparams("arbitrary"),
        name="conv_ffn",
    )(*args)


def _ret_tables(chunk, valid):
    log_gamma = jnp.log1p(-jnp.exp2(-5.0 - jnp.arange(H_A, dtype=F32)))
    idx = jnp.arange(chunk, dtype=F32)
    rel = idx[:, None] - idx[None, :]
    intra = jnp.exp(jnp.where(rel[None] >= 0, rel[None] * log_gamma[:, None, None], -jnp.inf))
    dq = jnp.exp((idx + 1.0)[None, :] * log_gamma[:, None])
    dk = jnp.where(idx[None, :] < valid, jnp.exp((valid - 1.0 - idx)[None, :] * log_gamma[:, None]), 0.0)
    dc = jnp.exp(valid * log_gamma)
    rep = lambda t: jnp.broadcast_to(t[..., None], t.shape + (LANES,))
    dc_rows = rep(jnp.repeat(dc, DK_A).reshape(H_A // 2, 2 * DK_A))
    return intra, rep(dq), rep(dk), dc_rows


def _ret_kernel(chunk, n_chunks, seqs, q_ref, k_ref, v_ref, ga_ref, s0_ref, intra_ref, dq_ref, dk_ref, dc_ref,
                o_ref, s_ref, st_ref):
    i = pl.program_id(1)

    @pl.when(i == 0)
    def _():
        st_ref[...] = s0_ref[...]

    lane = lax.broadcasted_iota(jnp.int32, (chunk, LANES), 1)
    for bi in range(seqs):
        for c in range(n_chunks):
            rows = slice(c * chunk, (c + 1) * chunk)
            for pair in range(H_A // 2):
                lanes = slice(pair * LANES, (pair + 1) * LANES)
                qc = q_ref[bi, rows, lanes]
                kc = k_ref[bi, rows, lanes]
                s_pair = st_ref[bi, pair]
                s_bf = s_pair.astype(BF16)
                upd = s_pair * dc_ref[pair]
                for side in range(2):
                    hd = 2 * pair + side
                    mine = (lane >= DK_A) if side else (lane < DK_A)
                    hl = slice(hd * DV_A, (hd + 1) * DV_A)
                    qh = jnp.where(mine, qc, jnp.zeros_like(qc))
                    vh = v_ref[bi, rows, hl]
                    sc = _dot_nt(qh, kc) * intra_ref[hd]
                    o = _dot(sc.astype(BF16), vh) + _dot(qh, s_bf) * dq_ref[hd]
                    kh = jnp.where(mine, kc.astype(F32) * dk_ref[hd], 0.0).astype(BF16)
                    upd = upd + _dot_tn(kh, vh)
                    gate = ga_ref[bi, rows, hl].astype(F32)
                    o_ref[bi, rows, hl] = (_rms(o) * (gate * jax.nn.sigmoid(gate))).astype(BF16)
                st_ref[bi, pair] = upd

    @pl.when(i == pl.num_programs(1) - 1)
    def _():
        s_ref[...] = st_ref[...]


def _seqs_per_step(batch, seq_len, blk):
    return min(batch, SHORT_SEQS_PER_STEP) if seq_len == blk else 1


def _retention(q, k, v, ga, s0, chunk, valid):
    b, l, _ = q.shape
    blk = min(l, CHUNKS_PER_STEP * chunk)
    n_chunks = blk // chunk
    seqs = _seqs_per_step(b, l, blk)
    tables = _ret_tables(chunk, valid)
    seq = lambda w: pl.BlockSpec((seqs, blk, w), lambda bi, i: (bi, i, 0))
    const = lambda a: pl.BlockSpec(a.shape, lambda bi, i: (0,) * a.ndim)
    state = pl.BlockSpec((seqs, H_A // 2, 2 * DK_A, DV_A), lambda bi, i: (bi, 0, 0, 0))
    o, s = pl.pallas_call(
        functools.partial(_ret_kernel, chunk, n_chunks, seqs),
        grid=(b // seqs, l // blk),
        in_specs=[seq(256), seq(256), seq(512), seq(512), state] + [const(t) for t in tables],
        out_specs=[seq(512), state],
        out_shape=[jax.ShapeDtypeStruct((b, l, H_A * DV_A), BF16),
                   jax.ShapeDtypeStruct((b, H_A // 2, 2 * DK_A, DV_A), F32)],
        scratch_shapes=[pltpu.VMEM((seqs, H_A // 2, 2 * DK_A, DV_A), F32)],
        compiler_params=_cparams("arbitrary", "arbitrary"),
        name="retention",
    )(q, k, v, ga, s0.reshape(b, H_A // 2, 2 * DK_A, DV_A), *tables)
    return o, s.reshape(b, H_A, DK_A, DV_A)


def _lam_value(lam_ref, lam_init):
    lv = lam_ref[...]
    return (jnp.exp(jnp.sum(lv[0:1] * lv[1:2], axis=-1, keepdims=True))
            - jnp.exp(jnp.sum(lv[2:3] * lv[3:4], axis=-1, keepdims=True)) + lam_init)


def _dattn_kernel(lam_init, tq, tk, q_ref, k_ref, v_ref, lam_ref, g_ref, o_ref, vt_ref, st_ref, m_ref, l_ref, acc_ref):
    qi = pl.program_id(2)

    @pl.when(qi == 0)
    def _():
        vt_ref[...] = v_ref[0].astype(F32).T.astype(BF16)

    q = q_ref[0]
    lane = lax.broadcasted_iota(jnp.int32, q.shape, 1)
    q2 = jnp.concatenate([jnp.where(lane < DH_B, q, jnp.zeros_like(q)),
                          jnp.where(lane >= DH_B, q, jnp.zeros_like(q))], axis=0)
    m_ref[...] = jnp.full_like(m_ref, -jnp.inf)
    l_ref[...] = jnp.zeros_like(l_ref)
    acc_ref[...] = jnp.zeros_like(acc_ref)

    def scores(slot, kb, diag_offset=None):
        start = pl.multiple_of(kb * tk, tk)
        st = _dot_nt(k_ref[0, pl.ds(start, tk), :], q2)
        if diag_offset is not None:
            key = lax.broadcasted_iota(jnp.int32, st.shape, 0) + diag_offset
            qry = jnp.bitwise_and(lax.broadcasted_iota(jnp.int32, st.shape, 1), tq - 1)
            st = jnp.where(key <= qry, st, -jnp.inf)
        st_ref[slot] = st

    def absorb(slot, kb):
        start = pl.multiple_of(kb * tk, tk)
        st = st_ref[slot]
        m_old = m_ref[...]
        m_new = jnp.maximum(m_old, jnp.max(st, axis=0, keepdims=True))
        alpha = jnp.exp2(m_old - m_new)
        pt = jnp.exp2(st - m_new)
        l_ref[...] = alpha * l_ref[...] + jnp.sum(pt, axis=0, keepdims=True)
        acc_ref[...] = alpha * acc_ref[...] + _dot(vt_ref[:, pl.ds(start, tk)], pt.astype(BF16))
        m_ref[...] = m_new

    assert tq == 2 * tk
    first_diag = 2 * qi
    scores(0, first_diag, 0)
    scores(1, first_diag + 1, tk)
    absorb(0, first_diag)

    def body(t, carry):
        scores(0, 2 * t)
        absorb(1, jnp.where(t == 0, first_diag + 1, 2 * t - 1))
        scores(1, 2 * t + 1)
        absorb(0, 2 * t)
        return carry

    lax.fori_loop(0, qi, body, 0)
    absorb(1, jnp.where(qi == 0, first_diag + 1, 2 * qi - 1))

    lam = _lam_value(lam_ref, lam_init)
    inv_l = 1.0 / l_ref[...]
    acc = acc_ref[...] * inv_l
    o = (acc[:, :tq] - lam * acc[:, tq:]).T
    o_ref[0] = ((_rms(o) * g_ref[...]) * (1.0 - lam_init)).astype(BF16)


def _diff_attn_prompt(qb, kb, vb, lam_vecs, g_diff, lam_init):
    b, l, _ = qb.shape
    tq, tk = 1024, 512
    whole = pl.BlockSpec((1, l, LANES), lambda bi, h, qi: (bi, 0, h))
    return pl.pallas_call(
        functools.partial(_dattn_kernel, lam_init, tq, tk),
        grid=(b, H_B, l // tq),
        in_specs=[pl.BlockSpec((1, tq, LANES), lambda bi, h, qi: (bi, qi, h)), whole, whole,
                  pl.BlockSpec(lam_vecs.shape, lambda bi, h, qi: (0, 0)),
                  pl.BlockSpec((1, DV_B), lambda bi, h, qi: (0, 0))],
        out_specs=pl.BlockSpec((1, tq, LANES), lambda bi, h, qi: (bi, qi, h)),
        out_shape=jax.ShapeDtypeStruct((b, l, H_B * DV_B), BF16),
        scratch_shapes=[pltpu.VMEM((DV_B, l), BF16), pltpu.VMEM((2, tk, 2 * tq), F32),
                        pltpu.VMEM((1, 2 * tq), F32), pltpu.VMEM((1, 2 * tq), F32),
                        pltpu.VMEM((DV_B, 2 * tq), F32)],
        compiler_params=_cparams("arbitrary", "arbitrary", "arbitrary"),
        name="diff_attn_prompt",
    )(qb, kb, vb, lam_vecs, g_diff.reshape(1, DV_B))


TOK_PAD = F32_SUBLANES
HEAD_ROWS = 2 * TOK_PAD
Q_ROWS = H_B * HEAD_ROWS
KV_ROWS = PAGE_SIZE * H_B


def _dattn_dec_kernel(lam_init, n_pages, pt_ref, q_ref, kn_ref, vn_ref, *refs):
    k_refs = refs[:n_pages]
    v_refs = refs[n_pages:2 * n_pages]
    lam_ref, g_ref, o_ref, m_ref, l_ref, acc_ref = refs[2 * n_pages:]
    step = pl.program_id(1)
    q = q_ref[0]

    @pl.when(step == 0)
    def _():
        s = _dot_nt(q, kn_ref[0])
        r = lax.broadcasted_iota(jnp.int32, s.shape, 0)
        c = lax.broadcasted_iota(jnp.int32, s.shape, 1)
        s = jnp.where((c % H_B == r // HEAD_ROWS) & (c // H_B <= r % TOK_PAD), s, -jnp.inf)
        m = jnp.max(s, axis=-1, keepdims=True)
        p = jnp.exp2(s - m)
        m_ref[...] = m
        l_ref[...] = jnp.sum(p, axis=-1, keepdims=True)
        acc_ref[...] = _dot(p.astype(BF16), vn_ref[0])

    own_head = (lax.broadcasted_iota(jnp.int32, (Q_ROWS, KV_ROWS), 1) % H_B
                == lax.broadcasted_iota(jnp.int32, (Q_ROWS, KV_ROWS), 0) // HEAD_ROWS)
    bias = jnp.where(own_head, 0.0, -jnp.inf)
    scores = [_dot_nt(q, kr[0].astype(BF16)) + bias for kr in k_refs]
    m_old = m_ref[...]
    m_new = m_old
    for s in scores:
        m_new = jnp.maximum(m_new, jnp.max(s, axis=-1, keepdims=True))
    alpha = jnp.exp2(m_old - m_new)
    l_new = alpha * l_ref[...]
    acc = alpha * acc_ref[...]
    for s, vr in zip(scores, v_refs):
        p = jnp.exp2(s - m_new)
        l_new = l_new + jnp.sum(p, axis=-1, keepdims=True)
        acc = acc + _dot(p.astype(BF16), vr[0].astype(BF16))
    m_ref[...] = m_new
    l_ref[...] = l_new
    acc_ref[...] = acc

    @pl.when(step == pl.num_programs(1) - 1)
    def _():
        lam = _lam_value(lam_ref, lam_init)
        o = acc / l_new
        for h in range(H_B):
            top = HEAD_ROWS * h
            d = o[top:top + TOK_PAD] - lam * o[top + TOK_PAD:top + HEAD_ROWS]
            o_ref[0, :, h * DV_B:(h + 1) * DV_B] = ((_rms(d) * g_ref[...]) * (1.0 - lam_init)).astype(BF16)


def _diff_attn_decode(q_rows, k_new, v_new, cache_k, cache_v, page_base, page_table, lam_vecs, g_diff, lam_init):
    b = q_rows.shape[0]
    n_steps = page_table.shape[1] // PAGES_PER_STEP
    per_b = lambda r, w: pl.BlockSpec((1, r, w), lambda bi, s, pt: (bi, 0, 0))

    def page_spec(j):
        return pl.BlockSpec((1, KV_ROWS, LANES),
                            lambda bi, s, pt: (page_base + pt[bi, s * PAGES_PER_STEP + j], 0, 0))

    pages = [page_spec(j) for j in range(PAGES_PER_STEP)]
    grid_spec = pltpu.PrefetchScalarGridSpec(
        num_scalar_prefetch=1,
        grid=(b, n_steps),
        in_specs=[per_b(Q_ROWS, LANES), per_b(DEC_SEQ * H_B, LANES), per_b(DEC_SEQ * H_B, LANES)]
        + pages + pages + [pl.BlockSpec(lam_vecs.shape, lambda bi, s, pt: (0, 0)),
                           pl.BlockSpec((1, DV_B), lambda bi, s, pt: (0, 0))],
        out_specs=per_b(TOK_PAD, H_B * DV_B),
        scratch_shapes=[pltpu.VMEM((Q_ROWS, 1), F32), pltpu.VMEM((Q_ROWS, 1), F32),
                        pltpu.VMEM((Q_ROWS, DV_B), F32)],
    )
    return pl.pallas_call(
        functools.partial(_dattn_dec_kernel, lam_init, PAGES_PER_STEP),
        grid_spec=grid_spec,
        out_shape=jax.ShapeDtypeStruct((b, TOK_PAD, H_B * DV_B), BF16),
        compiler_params=_cparams("arbitrary", "arbitrary"),
        name="diff_attn_decode",
    )(page_table, q_rows, k_new, v_new, *([cache_k] * PAGES_PER_STEP), *([cache_v] * PAGES_PER_STEP),
      lam_vecs, g_diff.reshape(1, DV_B))


def _decode_q_rows(qb):
    b = qb.shape[0]
    q = qb.reshape(b, DEC_SEQ, H_B, DV_B).transpose(0, 2, 1, 3)
    q = jnp.pad(q, ((0, 0), (0, 0), (0, TOK_PAD - DEC_SEQ), (0, 0)))[:, :, None]
    keep = (np.arange(DV_B)[None, :] // DH_B) == np.arange(2)[:, None]
    keep = jnp.asarray(keep)[None, None, :, None, :]
    return jnp.where(keep, q, jnp.zeros_like(q)).reshape(b, Q_ROWS, DV_B)


def _gla_tables(chunk):
    idx = np.arange(chunk)
    i, t = idx[:, None], idx[None, :]
    mats = [t <= i]
    masks = [i == t]
    s = 1
    while s < chunk:
        same = (t // s) == (i // s)
        odd = ((i // s) % 2) == 1
        if 2 * s <= GLA_MATMUL_LEVEL_SPAN:
            mats.append(same & np.where(odd, t <= i, t > i))
        masks.append(((i // (2 * s)) == (t // (2 * s))) & odd & (((t // s) % 2) == 0))
        s *= 2
    return (jnp.asarray(np.concatenate(mats, axis=0).astype(np.float32), BF16),
            jnp.asarray(np.stack(masks).astype(np.float32)))


def _gla_kernel(chunk, n_chunks, seqs, valid, q_ref, k_ref, g_ref, v_ref, r_ref, s0_ref, gn_ref, m_ref, p_ref,
                o_ref, s_ref, st_ref):
    i = pl.program_id(1)

    @pl.when(i == 0)
    def _():
        for bi in range(seqs):
            for h in range(H_C):
                st_ref[bi, h] = s0_ref[bi, h].T

    n_levels = p_ref.shape[0] - 1
    m_all = m_ref[...]
    live = lax.broadcasted_iota(jnp.int32, (chunk, 1), 0) < valid
    for bi, c in itertools.product(range(seqs), range(n_chunks)):
        rows = slice(c * chunk, (c + 1) * chunk)
        g = g_ref[bi, rows,:]
        if valid < chunk:
            g = jnp.where(live, g, 0.0)
        g_hi = g.astype(BF16)
        g_lo = (g - g_hi.astype(F32)).astype(BF16)
        sums = _dot(m_all, g_hi) + _dot(m_all, g_lo)
        cum = sums[0:chunk]
        exponents = [cum, cum[chunk - 1:chunk] - cum]
        n_mats = m_all.shape[0] // chunk
        for lv in range(n_levels):
            if lv + 1 < n_mats:
                exponents.append(sums[(1 + lv) * chunk:(2 + lv) * chunk])
            else:
                s = 2 ** lv
                pairs = cum.reshape(chunk // (2 * s), 2 * s, cum.shape[-1])
                rel = pairs - pairs[:, s - 1:s, :]
                upper = lax.broadcasted_iota(jnp.int32, rel.shape, 1) >= s
                exponents.append(jnp.where(upper, rel, -rel).reshape(cum.shape))
        e_all = jnp.exp(jnp.concatenate(exponents, axis=0))
        heads = []
        for h in range(H_C):
            kl = slice(h * DK_C, (h + 1) * DK_C)
            e = e_all[:, kl]
            q = q_ref[bi, rows,kl]
            k = k_ref[bi, rows,kl]
            v = v_ref[bi, rows,h * DV_C:(h + 1) * DV_C]
            if valid < chunk:
                k = jnp.where(live, k, jnp.zeros_like(k))
                v = jnp.where(live, v, jnp.zeros_like(v))
            qf = q.astype(F32)
            kf = k.astype(F32)
            level = lambda lv: e[(2 + lv) * chunk:(3 + lv) * chunk]
            prods = [_dot_nt(q, k)] + [_dot_nt((qf * level(lv)).astype(BF16), (kf * level(lv)).astype(BF16))
                                       for lv in range(n_levels)]
            heads.append(dict(v=v, prods=prods, q_in=(qf * e[0:chunk]).astype(BF16),
                              k_out=(kf * e[chunk:2 * chunk]).astype(BF16), decay=e[chunk - 1:chunk]))
        for hd in heads:
            att = hd["prods"][0] * p_ref[0]
            for lv in range(n_levels):
                att = att + hd["prods"][1 + lv] * p_ref[1 + lv]
            hd["intra"] = _dot(att.astype(BF16), hd["v"])
            hd["update"] = _dot_tn(hd["v"], hd["k_out"])
        for h, hd in enumerate(heads):
            vl = slice(h * DV_C, (h + 1) * DV_C)
            st = st_ref[bi, h]
            o = hd["intra"] + _dot_nt(hd["q_in"], st.astype(BF16))
            st_ref[bi, h] = st * hd["decay"] + hd["update"]
            gate = r_ref[bi, rows,vl].astype(F32)
            o_ref[bi, rows,vl] = ((_rms(o) * gn_ref[...]) * (gate * jax.nn.sigmoid(gate))).astype(BF16)

    @pl.when(i == pl.num_programs(1) - 1)
    def _():
        for bi in range(seqs):
            for h in range(H_C):
                s_ref[bi, h] = st_ref[bi, h].T


def _gla(q, k, g, v, r, s0, g_norm, chunk, valid):
    b, l, _ = q.shape
    blk = min(l, CHUNKS_PER_STEP * chunk)
    n_chunks = blk // chunk
    seqs = _seqs_per_step(b, l, blk)
    m_all, masks = _gla_tables(chunk)
    qk = pl.BlockSpec((seqs, blk, H_C * DK_C), lambda bi, i: (bi, i, 0))
    vr = pl.BlockSpec((seqs, blk, H_C * DV_C), lambda bi, i: (bi, i, 0))
    state = pl.BlockSpec((seqs, H_C, DK_C, DV_C), lambda bi, i: (bi, 0, 0, 0))
    const = lambda a: pl.BlockSpec(a.shape, lambda bi, i: (0,) * a.ndim)
    gn = g_norm.reshape(1, DV_C)
    return pl.pallas_call(
        functools.partial(_gla_kernel, chunk, n_chunks, seqs, valid),
        grid=(b // seqs, l // blk),
        in_specs=[qk, qk, qk, vr, vr, state, const(gn), const(m_all), const(masks)],
        out_specs=[vr, state],
        out_shape=[jax.ShapeDtypeStruct((b, l, H_C * DV_C), BF16),
                   jax.ShapeDtypeStruct((b, H_C, DK_C, DV_C), F32)],
        scratch_shapes=[pltpu.VMEM((seqs, H_C, DV_C, DK_C), F32)],
        compiler_params=_cparams("arbitrary", "arbitrary"),
        name="gla",
    )(q, k, g, v, r, s0, gn, m_all, masks)


def _rope_tables(pos):
    inv = ROPE_THETA ** (-jnp.arange(0, DK_A, 2, dtype=F32) / DK_A)
    ang = pos.astype(F32)[:, None] * inv[None, :]
    cos = jnp.tile(jnp.cos(ang), (1, 4))
    sin = jnp.tile(jnp.concatenate([-jnp.sin(ang), jnp.sin(ang)], axis=1), (1, 2))
    return cos, sin


def _pad_seq(t, batch, seq_len):
    t = t.reshape(batch, seq_len, t.shape[-1])
    return jnp.pad(t, ((0, 0), (0, SAMPLE_PAD - seq_len), (0, 0)))


def _unpad_seq(t, seq_len):
    return t[:, :seq_len].reshape(t.shape[0] * seq_len, t.shape[-1])


def kernel(x_prompt, x_sample, cache_k, cache_v, state_ret, state_gla, state_conv, page_table, c_prompt, c_sample, w_ada, b_ada, g_pre_mix, g_post_mix, g_pre_ffn, g_post_ffn, w_in_ab, w_out_ab, lam_q1, lam_k1, lam_q2, lam_k2, g_diff, w_in_c, w_gate_c, b_gate_c, g_gla, w_out_c, w_up, conv_w, conv_b, w_down):
    rows_p, rows_s = BATCH * SEQ, DEC_BATCH * DEC_SEQ
    xp = x_prompt.reshape(rows_p, D_MODEL)
    xs = x_sample.reshape(rows_s, D_MODEL)

    c_all = jnp.pad(jnp.concatenate([jnp.repeat(c_sample, DEC_SEQ, axis=0), c_prompt], axis=0),
                    ((0, (-BATCH) % F32_SUBLANES), (0, 0)))
    ada_s, ada_p = _ada(c_all, rows_s, w_ada, b_ada)

    cos_p, sin_p = _rope_tables(jnp.arange(SEQ))
    cos_s, sin_s = _rope_tables(jnp.tile(PAST_LEN + jnp.arange(DEC_SEQ), DEC_BATCH))

    wup = w_up.astype(BF16)
    wdn = w_down.astype(BF16)
    k_p, v_p, k_s, v_s, ret_p, ret_s, gla_p, gla_s, conv_p, conv_s = ([] for _ in range(10))
    for l in range(DEPTH):
        ms = [_Mod(ada_s, l, which) for which in range(6)]
        mp = [_Mod(ada_p, l, which) for which in range(6)]
        if l % 2 == 0:
            e = l // 2
            lam_init = 0.8 - 0.6 * math.exp(-0.3 * l)
            lam_vecs = jnp.stack([lam_q1[e], lam_k1[e], lam_q2[e], lam_k2[e]]).astype(F32)
            w_in = w_in_ab[e].astype(BF16)
            w_out = w_out_ab[e].astype(BF16)
            w_parts = [w_out[:H_A * DV_A], w_out[H_A * DV_A:]]

            qa, ka, va, ga, qb, kb32, kb16, vb32, vb16 = _in_proj_ab(
                xp, mp[0], mp[1], g_pre_mix[l], w_in, cos_p, sin_p, SEQ)
            seq3 = lambda t: t.reshape(BATCH, SEQ, t.shape[-1])
            ret, sp = _retention(seq3(qa), seq3(ka), seq3(va), seq3(ga),
                                 jnp.zeros((BATCH, H_A, DK_A, DV_A), F32), CHUNK_RET, CHUNK_RET)
            dif = _diff_attn_prompt(seq3(qb), seq3(kb16), seq3(vb16), lam_vecs, g_diff[e], lam_init)
            xp = _out_proj([ret.reshape(rows_p, -1), dif.reshape(rows_p, -1)], w_parts, xp, mp[2],
                           g_post_mix[l], SEQ)
            k_p.append(kb32.reshape(BATCH, SEQ, H_B, 2 * DH_B))
            v_p.append(vb32.reshape(BATCH, SEQ, H_B, DV_B))
            ret_p.append(sp)

            qa, ka, va, ga, qb, kb32, kb16, vb32, vb16 = _in_proj_ab(
                xs, ms[0], ms[1], g_pre_mix[l], w_in, cos_s, sin_s, DEC_SEQ)
            pad = lambda t: _pad_seq(t, DEC_BATCH, DEC_SEQ)
            ret, ss = _retention(pad(qa), pad(ka), pad(va), pad(ga), state_ret[e], SAMPLE_PAD, DEC_SEQ)
            n_phys = cache_k.shape[1]
            kv_rows = lambda t: t.reshape(DEC_BATCH, DEC_SEQ * H_B, DV_B)
            dif = _diff_attn_decode(
                _decode_q_rows(qb.reshape(DEC_BATCH, DEC_SEQ, 512)), kv_rows(kb16), kv_rows(vb16),
                cache_k.reshape(N_EVEN * n_phys, KV_ROWS, LANES), cache_v.reshape(N_EVEN * n_phys, KV_ROWS, LANES),
                e * n_phys, page_table, lam_vecs, g_diff[e], lam_init)
            xs = _out_proj([_unpad_seq(ret, DEC_SEQ), _unpad_seq(dif, DEC_SEQ)], w_parts, xs, ms[2],
                           g_post_mix[l], DEC_SEQ)
            k_s.append(kb32.reshape(DEC_BATCH, DEC_SEQ, H_B, 2 * DH_B))
            v_s.append(vb32.reshape(DEC_BATCH, DEC_SEQ, H_B, DV_B))
            ret_s.append(ss)
        else:
            o = l // 2
            n_main = sum(C_WIDTHS_MAIN)
            w_in = w_in_c[o][:, :n_main].astype(BF16)
            w_a = jnp.pad(w_in_c[o][:, n_main:], ((0, 0), (0, LANES - GATE_RANK))).astype(BF16)
            w_gate = jnp.pad(w_gate_c[o], ((0, LANES - GATE_RANK), (0, 0))).astype(BF16)
            w_out = w_out_c[o].astype(BF16)

            q, k, v, r, lg = _in_proj_c(xp, mp[0], mp[1], g_pre_mix[l], w_in, w_a, w_gate, b_gate_c[o], SEQ)
            seq3 = lambda t: t.reshape(BATCH, SEQ, t.shape[-1])
            og, sp = _gla(seq3(q), seq3(k), seq3(lg), seq3(v), seq3(r),
                          jnp.zeros((BATCH, H_C, DK_C, DV_C), F32), g_gla[o], CHUNK_GLA, CHUNK_GLA)
            xp = _out_proj([og.reshape(rows_p, -1)], [w_out], xp, mp[2], g_post_mix[l], SEQ)
            gla_p.append(sp)

            q, k, v, r, lg = _in_proj_c(xs, ms[0], ms[1], g_pre_mix[l], w_in, w_a, w_gate, b_gate_c[o], DEC_SEQ)
            pad = lambda t: _pad_seq(t, DEC_BATCH, DEC_SEQ)
            og, ss = _gla(pad(q), pad(k), pad(lg), pad(v), pad(r), state_gla[o], g_gla[o], SAMPLE_PAD, DEC_SEQ)
            xs = _out_proj([_unpad_seq(og, DEC_SEQ)], [w_out], xs, ms[2], g_post_mix[l], DEC_SEQ)
            gla_s.append(ss)

        xp, cbp = _conv_ffn(xp, mp[3], mp[4], mp[5], g_pre_ffn[l], g_post_ffn[l], l, wup, conv_w[l], conv_b[l],
                            wdn, SEQ)
        xs, up_s = _conv_ffn(xs, ms[3], ms[4], ms[5], g_pre_ffn[l], g_post_ffn[l], l, wup, conv_w[l], conv_b[l],
                             wdn, DEC_SEQ, state=state_conv[l])
        conv_p.append(cbp)
        conv_s.append(up_s.reshape(DEC_BATCH, DEC_SEQ, 2 * D_FF)[:, DEC_SEQ - (CONV_W - 1):])

    stack = lambda ts: ts[0][None] if len(ts) == 1 else jnp.stack(ts)
    return (xp.reshape(BATCH, SEQ, D_MODEL), xs.reshape(DEC_BATCH, DEC_SEQ, D_MODEL),
            stack(k_p), stack(v_p), stack(k_s), stack(v_s),
            stack(ret_p), stack(ret_s), stack(gla_p), stack(gla_s),
            stack(conv_p), stack(conv_s))
```

```python
import functools
import itertools
import math
from typing import NamedTuple

import numpy as np
import jax
import jax.numpy as jnp
from jax import lax
from jax.experimental import pallas as pl
from jax.experimental.pallas import tpu as pltpu

D_MODEL = 1024
BATCH = 4
SEQ = 4096
DEPTH = 2
DEC_BATCH = 32
DEC_SEQ = 4
PAST_LEN = 8192
PAGE_SIZE = 128
N_EVEN = (DEPTH + 1) // 2
N_ODD = DEPTH // 2
H_A = 4
DK_A = D_MODEL // 16
DV_A = D_MODEL // 8
CHUNK_RET = 128
H_B = 4
DH_B = D_MODEL // 16
DV_B = 2 * DH_B
H_C = 4
DK_C = D_MODEL // 8
DV_C = D_MODEL // 4
GATE_RANK = 16
GATE_TAU = 16.0
C_WIDTHS_MAIN = (H_C * DK_C, H_C * DK_C, H_C * DV_C, H_C * DV_C)
CHUNK_GLA = 64
D_FF = ((8 * D_MODEL // 3 + 127) // 128) * 128
CONV_W = 3
ROPE_THETA = 10000.0
EPS = 1e-6

LANES = 128
VMEM_LIMIT = 56 * 1024 * 1024
ROW_TILE = 1024
FFN_ROW_TILE = 1024
F32_SUBLANES = 8
BF16_SUBLANES = 16
GLA_MATMUL_LEVEL_SPAN = F32_SUBLANES
CHUNKS_PER_STEP = 16
SHORT_SEQS_PER_STEP = 8
SAMPLE_PAD = BF16_SUBLANES
FF_CHUNK = 256
PAGES_PER_STEP = 32
ROPE_HALF = DK_A // 2
LOG2E = math.log2(math.e)
BF16 = jnp.bfloat16
F32 = jnp.float32


def _cparams(*sem):
    return pltpu.CompilerParams(dimension_semantics=sem, vmem_limit_bytes=VMEM_LIMIT)


def _dot(a, b):
    return jnp.dot(a, b, preferred_element_type=F32)


def _dot_nt(a, b):
    return lax.dot_general(a, b, (((1,), (1,)), ((), ())), preferred_element_type=F32)


def _dot_tn(a, b):
    return lax.dot_general(a, b, (((0,), (0,)), ((), ())), preferred_element_type=F32)


def _rms(x):
    return x * lax.rsqrt(jnp.mean(x * x, axis=-1, keepdims=True) + EPS)


def _modulate(x, g, shift, scale):
    return (_rms(x) * g) * (1.0 + scale) + shift


def _rope(x, cos, sin_signed):
    w = x.shape[-1]
    fwd = pltpu.roll(x, ROPE_HALF, axis=1)
    bwd = pltpu.roll(x, w - ROPE_HALF, axis=1)
    reps = w // LANES
    first_half = (lax.broadcasted_iota(jnp.int32, x.shape, 1) % DK_A) < ROPE_HALF
    partner = jnp.where(first_half, bwd, fwd)
    return (x * jnp.concatenate([cos] * reps, axis=1)
            + partner * jnp.concatenate([sin_signed] * reps, axis=1))


def _ada_kernel(n_tok, c_ref, w_ref, b_ref, tok_ref, seq_ref):
    c = c_ref[...]
    s = c * jax.nn.sigmoid(c)
    o = jnp.dot(s, w_ref[0], preferred_element_type=F32, precision=lax.Precision.HIGHEST) + b_ref[0]
    tok_ref[0] = o[:n_tok]
    for r in range(seq_ref.shape[1]):
        seq_ref[0, r] = o[n_tok + r:n_tok + r + 1]


def _ada(c_all, n_tok, w_ada, b_ada):
    rows = c_all.shape[0]
    n_seq = rows - n_tok
    tn = 1536
    return pl.pallas_call(
        functools.partial(_ada_kernel, n_tok),
        grid=(DEPTH, 6 * D_MODEL // tn),
        in_specs=[pl.BlockSpec((rows, D_MODEL), lambda l, j: (0, 0)),
                  pl.BlockSpec((1, D_MODEL, tn), lambda l, j: (l, 0, j)),
                  pl.BlockSpec((1, 1, tn), lambda l, j: (l, 0, j))],
        out_specs=[pl.BlockSpec((1, n_tok, tn), lambda l, j: (l, 0, j)),
                   pl.BlockSpec((1, n_seq, 1, tn), lambda l, j: (l, 0, 0, j))],
        out_shape=[jax.ShapeDtypeStruct((DEPTH, n_tok, 6 * D_MODEL), F32),
                   jax.ShapeDtypeStruct((DEPTH, n_seq, 1, 6 * D_MODEL), F32)],
        compiler_params=_cparams("arbitrary", "arbitrary"),
        name="adaln",
    )(c_all, w_ada, b_ada.reshape(DEPTH, 1, 6 * D_MODEL))


class _Mod(NamedTuple):
    ada: jax.Array
    layer: int
    which: int


def _row_cfg(rows, seq_len, row_tile=ROW_TILE):
    tm = min(row_tile, rows)
    per_token = seq_len < tm
    tiles_per_seq = 1 if per_token else seq_len // tm
    return tm, per_token, tiles_per_seq


def _mod_operand(m, tm, per_token, tiles_per_seq):
    if per_token:
        return pl.BlockSpec((1, tm, D_MODEL), lambda i: (m.layer, i, m.which)), m.ada
    return pl.BlockSpec((None, 1, 1, D_MODEL), lambda i: (m.layer, i // tiles_per_seq, 0, m.which)), m.ada


def _inab_kernel(x_ref, sh_ref, sc_ref, g_ref, w_ref, cos_ref, sin_ref,
                 qa_ref, ka_ref, va_ref, ga_ref, qb_ref, kb32_ref, kb16_ref, vb32_ref, vb16_ref):
    h = _modulate(x_ref[...], g_ref[...], sh_ref[0], sc_ref[0]).astype(BF16)
    cos = cos_ref[...]
    sin = sin_ref[...]

    def proj(lo, width):
        return _dot(h, w_ref[:, lo:lo + width])

    qa_ref[...] = _rope(proj(0, 256), cos, sin).astype(BF16)
    ka_ref[...] = (_rope(proj(256, 256), cos, sin) * (DK_A ** -0.5)).astype(BF16)
    va_ref[...] = proj(512, 512).astype(BF16)
    ga_ref[...] = proj(1024, 512).astype(BF16)
    qb_ref[...] = (_rope(proj(1536, 512), cos, sin) * (DH_B ** -0.5 * LOG2E)).astype(BF16)
    kb = _rope(proj(2048, 512), cos, sin)
    kb16_ref[...] = kb.astype(BF16)
    vb = proj(2560, 512)
    vb16_ref[...] = vb.astype(BF16)
    tm = kb.shape[0]
    for h in range(H_B):
        kb32_ref[pl.ds(h, tm, stride=H_B), :] = kb[:, h * DV_B:(h + 1) * DV_B]
        vb32_ref[pl.ds(h, tm, stride=H_B), :] = vb[:, h * DV_B:(h + 1) * DV_B]


def _in_proj_ab(x, shift, scale, g_pre, w_bf, cos, sin, seq_len):
    rows = x.shape[0]
    tm, per_token, tps = _row_cfg(rows, seq_len)
    (sh_spec, sh_arg), (sc_spec, sc_arg) = (_mod_operand(m, tm, per_token, tps) for m in (shift, scale))
    pos_map = (lambda i: (i, 0)) if per_token else (lambda i: (i % tps, 0))
    row = lambda w: pl.BlockSpec((tm, w), lambda i: (i, 0))
    widths = (256, 256, 512, 512, 512, 512, 512, 512, 512)
    dtypes = (BF16, BF16, BF16, BF16, BF16, F32, BF16, F32, BF16)
    cache_rows = lambda w, d: d == F32
    out_spec = lambda w, d: (pl.BlockSpec((tm * H_B, DV_B), lambda i: (i, 0)) if cache_rows(w, d) else row(w))
    out_shape = lambda w, d: jax.ShapeDtypeStruct((rows * H_B, DV_B) if cache_rows(w, d) else (rows, w), d)
    return pl.pallas_call(
        _inab_kernel,
        grid=(rows // tm,),
        in_specs=[row(D_MODEL), sh_spec, sc_spec,
                  pl.BlockSpec((1, D_MODEL), lambda i: (0, 0)),
                  pl.BlockSpec(w_bf.shape, lambda i: (0, 0)),
                  pl.BlockSpec((tm, LANES), pos_map), pl.BlockSpec((tm, LANES), pos_map)],
        out_specs=[out_spec(w, d) for w, d in zip(widths, dtypes)],
        out_shape=[out_shape(w, d) for w, d in zip(widths, dtypes)],
        compiler_params=_cparams("arbitrary"),
        name="in_proj_ab",
    )(x, sh_arg, sc_arg, g_pre.reshape(1, D_MODEL), w_bf, cos, sin)


def _inc_kernel(x_ref, sh_ref, sc_ref, g_ref, w_ref, wa_ref, wg_ref, bg_ref,
                q_ref, k_ref, v_ref, r_ref, lg_ref):
    h = _modulate(x_ref[...], g_ref[...], sh_ref[0], sc_ref[0]).astype(BF16)

    def proj(lo, width):
        return _dot(h, w_ref[:, lo:lo + width])

    q_ref[...] = (proj(0, 512) * (DK_C ** -0.5)).astype(BF16)
    k_ref[...] = proj(512, 512).astype(BF16)
    v_ref[...] = proj(1024, 1024).astype(BF16)
    r_ref[...] = proj(2048, 1024).astype(BF16)
    a = _dot(h, wa_ref[...])
    z = _dot(a.astype(BF16), wg_ref[...]) + bg_ref[...]
    softplus_neg = jnp.maximum(-z, 0.0) + jnp.log1p(jnp.exp(-jnp.abs(z)))
    lg_ref[...] = -softplus_neg / GATE_TAU


def _in_proj_c(x, shift, scale, g_pre, w_bf, wa_bf, wg_bf, b_gate, seq_len):
    rows = x.shape[0]
    tm, per_token, tps = _row_cfg(rows, seq_len)
    (sh_spec, sh_arg), (sc_spec, sc_arg) = (_mod_operand(m, tm, per_token, tps) for m in (shift, scale))
    row = lambda w: pl.BlockSpec((tm, w), lambda i: (i, 0))
    widths = (512, 512, 1024, 1024, 512)
    dtypes = (BF16, BF16, BF16, BF16, F32)
    full = lambda a: pl.BlockSpec(a.shape, lambda i: (0, 0))
    bg = b_gate.reshape(1, -1)
    return pl.pallas_call(
        _inc_kernel,
        grid=(rows // tm,),
        in_specs=[row(D_MODEL), sh_spec, sc_spec, pl.BlockSpec((1, D_MODEL), lambda i: (0, 0)),
                  full(w_bf), full(wa_bf), full(wg_bf), full(bg)],
        out_specs=[row(w) for w in widths],
        out_shape=[jax.ShapeDtypeStruct((rows, w), d) for w, d in zip(widths, dtypes)],
        compiler_params=_cparams("arbitrary"),
        name="in_proj_c",
    )(x, sh_arg, sc_arg, g_pre.reshape(1, D_MODEL), w_bf, wa_bf, wg_bf, bg)


def _outproj_kernel(n_parts, *refs):
    parts = refs[:n_parts]
    ws = refs[n_parts:2 * n_parts]
    x_ref, gate_ref, g_ref, o_ref = refs[2 * n_parts:]
    y = _dot(parts[0][...], ws[0][...])
    for p, w in zip(parts[1:], ws[1:]):
        y = y + _dot(p[...], w[...])
    o_ref[...] = x_ref[...] + gate_ref[0] * (_rms(y) * g_ref[...])


def _out_proj(parts, weights, x, gate, g_post, seq_len):
    rows = x.shape[0]
    tm, per_token, tps = _row_cfg(rows, seq_len)
    gate_spec, gate_arg = _mod_operand(gate, tm, per_token, tps)
    n = len(parts)
    return pl.pallas_call(
        functools.partial(_outproj_kernel, n),
        grid=(rows // tm,),
        in_specs=([pl.BlockSpec((tm, p.shape[1]), lambda i: (i, 0)) for p in parts]
                  + [pl.BlockSpec(w.shape, lambda i: (0, 0)) for w in weights]
                  + [pl.BlockSpec((tm, D_MODEL), lambda i: (i, 0)), gate_spec,
                     pl.BlockSpec((1, D_MODEL), lambda i: (0, 0))]),
        out_specs=pl.BlockSpec((tm, D_MODEL), lambda i: (i, 0)),
        out_shape=jax.ShapeDtypeStruct((rows, D_MODEL), F32),
        compiler_params=_cparams("arbitrary"),
        name="out_proj",
    )(*parts, *weights, x, gate_arg, g_post.reshape(1, D_MODEL))


def _ffn_kernel(per_token, seq_len, tiles_per_seq, *refs):
    if per_token:
        (x_ref, sh_ref, sc_ref, gate_ref, gpre_ref, gpost_ref, wup_ref, cw_ref, cb_ref, wdn_ref,
         s1_ref, s2_ref, xo_ref, conv_ref, h_ref, acc_ref, *u_refs) = refs
    else:
        (x_ref, sh_ref, sc_ref, gate_ref, gpre_ref, gpost_ref, wup_ref, cw_ref, cb_ref, wdn_ref,
         xo_ref, conv_ref, h_ref, acc_ref, *u_refs) = refs
    tm = x_ref.shape[0]
    halo = h_ref.shape[0] - tm
    x = x_ref[...]
    if per_token:
        tau = lax.broadcasted_iota(jnp.int32, (tm, FF_CHUNK), 0) % seq_len
    else:
        seq_start = (pl.program_id(0) % tiles_per_seq) == 0

        @pl.when(seq_start)
        def _():
            h_ref[:halo] = jnp.zeros((halo, D_MODEL), BF16)

        @pl.when(jnp.logical_not(seq_start))
        def _():
            h_ref[:halo] = h_ref[tm:]
    h_ref[halo:] = _modulate(x, gpre_ref[...], sh_ref[0], sc_ref[0]).astype(BF16)
    acc_ref[...] = jnp.zeros_like(acc_ref)

    def up(lo):
        return _dot(h_ref[...], wup_ref[:, pl.ds(lo, FF_CHUNK)])

    def conv(slot, half, lo):
        cols = pl.ds(lo, FF_CHUNK)
        cw = cw_ref[:, cols]
        cb = cb_ref[:, cols]
        taps = lambda u2, u1, u0: cb + cw[0:1] * u2 + cw[1:2] * u1 + cw[2:3] * u0
        if per_token:
            u = u_refs[slot][half]
            u1 = jnp.where(tau == 0, s1_ref[:, cols], pltpu.roll(u, 1, axis=0))
            u2 = jnp.where(tau < 2, s2_ref[:, cols], pltpu.roll(u, 2, axis=0))
            conv_ref[:, cols] = u
            return taps(u2, u1, u)
        conv_ref[0, :, cols] = u_refs[slot][half, halo + tm - 2:]
        window = lambda back: u_refs[slot][half, halo - back:halo - back + tm]
        return taps(window(2), window(1), window(0))

    def produce(slot, lo):
        u_refs[slot][0] = up(lo)
        u_refs[slot][1] = up(lo + D_FF)

    def consume(slot, lo):
        g = (jax.nn.gelu(conv(slot, 0, lo)) * conv(slot, 1, lo + D_FF)).astype(BF16)
        acc_ref[...] += _dot(g, wdn_ref[pl.ds(lo, FF_CHUNK), :])

    def body(j, carry):
        lo = pl.multiple_of(j * (2 * FF_CHUNK), 2 * FF_CHUNK)
        produce(1, lo + FF_CHUNK)
        consume(0, lo)
        produce(0, lo + 2 * FF_CHUNK)
        consume(1, lo + FF_CHUNK)
        return carry

    n_chunks = D_FF // FF_CHUNK
    produce(0, 0)
    lax.fori_loop(0, n_chunks // 2, body, 0)
    consume(0, (n_chunks - 1) * FF_CHUNK)
    xo_ref[...] = x + gate_ref[0] * (_rms(acc_ref[...]) * gpost_ref[...])


def _conv_ffn(x, shift, scale, gate, g_pre, g_post, layer, wup_bf, conv_w, conv_b, wdn_bf, seq_len, state=None):
    rows = x.shape[0]
    tm, per_token, tps = _row_cfg(rows, seq_len, FFN_ROW_TILE)
    mods = [_mod_operand(m, tm, per_token, tps) for m in (shift, scale, gate)]
    full = lambda a: pl.BlockSpec(a.shape, lambda i: (0, 0), pipeline_mode=pl.Buffered(1))
    of_layer = lambda a: pl.BlockSpec((None,) + a.shape[1:], lambda i: (layer, 0, 0), pipeline_mode=pl.Buffered(1))
    cb = conv_b.reshape(1, 2 * D_FF)
    in_specs = [pl.BlockSpec((tm, D_MODEL), lambda i: (i, 0))] + [spec for spec, _ in mods] + [
        pl.BlockSpec((1, D_MODEL), lambda i: (0, 0)), pl.BlockSpec((1, D_MODEL), lambda i: (0, 0)),
        of_layer(wup_bf), full(conv_w), full(cb), of_layer(wdn_bf)]
    args = [x] + [arg for _, arg in mods] + [
        g_pre.reshape(1, D_MODEL), g_post.reshape(1, D_MODEL), wup_bf, conv_w, cb, wdn_bf]
    halo = 0 if per_token else BF16_SUBLANES
    scratch = [pltpu.VMEM((halo + tm, D_MODEL), BF16), pltpu.VMEM((tm, D_MODEL), F32),
               pltpu.VMEM((2, halo + tm, FF_CHUNK), F32), pltpu.VMEM((2, halo + tm, FF_CHUNK), F32)]
    if per_token:
        batch = rows // seq_len
        zeros = jnp.zeros((batch, seq_len - 1, 2 * D_FF), F32)
        s1 = jnp.concatenate([state[:, 1:2], zeros], axis=1).reshape(rows, 2 * D_FF)
        s2 = jnp.concatenate([state, zeros[:, 1:]], axis=1).reshape(rows, 2 * D_FF)
        in_specs += [pl.BlockSpec((tm, 2 * D_FF), lambda i: (i, 0))] * 2
        args += [s1, s2]
        conv_spec = pl.BlockSpec((tm, 2 * D_FF), lambda i: (i, 0))
        conv_shape = jax.ShapeDtypeStruct((rows, 2 * D_FF), F32)
    else:
        conv_spec = pl.BlockSpec((1, 2, 2 * D_FF), lambda i: (i // tps, 0, 0))
        conv_shape = jax.ShapeDtypeStruct((rows // seq_len, 2, 2 * D_FF), F32)
    return pl.pallas_call(
        functools.partial(_ffn_kernel, per_token, seq_len, tps),
        grid=(rows // tm,),
        in_specs=in_specs,
        out_specs=[pl.BlockSpec((tm, D_MODEL), lambda i: (i, 0)), conv_spec],
        out_shape=[jax.ShapeDtypeStruct((rows, D_MODEL), F32), conv_shape],
        scratch_shapes=scratch,
        compiler_params=_cparams("arbitrary"),
        name="conv_ffn",
    )(*args)


def _ret_tables(chunk, valid):
    log_gamma = jnp.log1p(-jnp.exp2(-5.0 - jnp.arange(H_A, dtype=F32)))
    idx = jnp.arange(chunk, dtype=F32)
    rel = idx[:, None] - idx[None, :]
    intra = jnp.exp(jnp.where(rel[None] >= 0, rel[None] * log_gamma[:, None, None], -jnp.inf))
    dq = jnp.exp((idx + 1.0)[None, :] * log_gamma[:, None])
    dk = jnp.where(idx[None, :] < valid, jnp.exp((valid - 1.0 - idx)[None, :] * log_gamma[:, None]), 0.0)
    dc = jnp.exp(valid * log_gamma)
    rep = lambda t: jnp.broadcast_to(t[..., None], t.shape + (LANES,))
    dc_rows = rep(jnp.repeat(dc, DK_A).reshape(H_A // 2, 2 * DK_A))
    return intra, rep(dq), rep(dk), dc_rows


def _ret_kernel(chunk, n_chunks, seqs, q_ref, k_ref, v_ref, ga_ref, s0_ref, intra_ref, dq_ref, dk_ref, dc_ref,
                o_ref, s_ref, st_ref):
    i = pl.program_id(1)

    @pl.when(i == 0)
    def _():
        st_ref[...] = s0_ref[...]

    lane = lax.broadcasted_iota(jnp.int32, (chunk, LANES), 1)
    for bi in range(seqs):
        for c in range(n_chunks):
            rows = slice(c * chunk, (c + 1) * chunk)
            for pair in range(H_A // 2):
                lanes = slice(pair * LANES, (pair + 1) * LANES)
                qc = q_ref[bi, rows, lanes]
                kc = k_ref[bi, rows, lanes]
                s_pair = st_ref[bi, pair]
                s_bf = s_pair.astype(BF16)
                upd = s_pair * dc_ref[pair]
                for side in range(2):
                    hd = 2 * pair + side
                    mine = (lane >= DK_A) if side else (lane < DK_A)
                    hl = slice(hd * DV_A, (hd + 1) * DV_A)
                    qh = jnp.where(mine, qc, jnp.zeros_like(qc))
                    vh = v_ref[bi, rows, hl]
                    sc = _dot_nt(qh, kc) * intra_ref[hd]
                    o = _dot(sc.astype(BF16), vh) + _dot(qh, s_bf) * dq_ref[hd]
                    kh = jnp.where(mine, kc.astype(F32) * dk_ref[hd], 0.0).astype(BF16)
                    upd = upd + _dot_tn(kh, vh)
                    gate = ga_ref[bi, rows, hl].astype(F32)
                    o_ref[bi, rows, hl] = (_rms(o) * (gate * jax.nn.sigmoid(gate))).astype(BF16)
                st_ref[bi, pair] = upd

    @pl.when(i == pl.num_programs(1) - 1)
    def _():
        s_ref[...] = st_ref[...]


def _seqs_per_step(batch, seq_len, blk):
    return min(batch, SHORT_SEQS_PER_STEP) if seq_len == blk else 1


def _retention(q, k, v, ga, s0, chunk, valid):
    b, l, _ = q.shape
    blk = min(l, CHUNKS_PER_STEP * chunk)
    n_chunks = blk // chunk
    seqs = _seqs_per_step(b, l, blk)
    tables = _ret_tables(chunk, valid)
    seq = lambda w: pl.BlockSpec((seqs, blk, w), lambda bi, i: (bi, i, 0))
    const = lambda a: pl.BlockSpec(a.shape, lambda bi, i: (0,) * a.ndim)
    state = pl.BlockSpec((seqs, H_A // 2, 2 * DK_A, DV_A), lambda bi, i: (bi, 0, 0, 0))
    o, s = pl.pallas_call(
        functools.partial(_ret_kernel, chunk, n_chunks, seqs),
        grid=(b // seqs, l // blk),
        in_specs=[seq(256), seq(256), seq(512), seq(512), state] + [const(t) for t in tables],
        out_specs=[seq(512), state],
        out_shape=[jax.ShapeDtypeStruct((b, l, H_A * DV_A), BF16),
                   jax.ShapeDtypeStruct((b, H_A // 2, 2 * DK_A, DV_A), F32)],
        scratch_shapes=[pltpu.VMEM((seqs, H_A // 2, 2 * DK_A, DV_A), F32)],
        compiler_params=_cparams("arbitrary", "arbitrary"),
        name="retention",
    )(q, k, v, ga, s0.reshape(b, H_A // 2, 2 * DK_A, DV_A), *tables)
    return o, s.reshape(b, H_A, DK_A, DV_A)


def _lam_value(lam_ref, lam_init):
    lv = lam_ref[...]
    return (jnp.exp(jnp.sum(lv[0:1] * lv[1:2], axis=-1, keepdims=True))
            - jnp.exp(jnp.sum(lv[2:3] * lv[3:4], axis=-1, keepdims=True)) + lam_init)


def _dattn_kernel(lam_init, tq, tk, q_ref, k_ref, v_ref, lam_ref, g_ref, o_ref, vt_ref, st_ref, m_ref, l_ref, acc_ref):
    qi = pl.program_id(2)

    @pl.when(qi == 0)
    def _():
        vt_ref[...] = v_ref[0].astype(F32).T.astype(BF16)

    q = q_ref[0]
    lane = lax.broadcasted_iota(jnp.int32, q.shape, 1)
    q2 = jnp.concatenate([jnp.where(lane < DH_B, q, jnp.zeros_like(q)),
                          jnp.where(lane >= DH_B, q, jnp.zeros_like(q))], axis=0)
    m_ref[...] = jnp.full_like(m_ref, -jnp.inf)
    l_ref[...] = jnp.zeros_like(l_ref)
    acc_ref[...] = jnp.zeros_like(acc_ref)

    def scores(slot, kb, diag_offset=None):
        start = pl.multiple_of(kb * tk, tk)
        st = _dot_nt(k_ref[0, pl.ds(start, tk), :], q2)
        if diag_offset is not None:
            key = lax.broadcasted_iota(jnp.int32, st.shape, 0) + diag_offset
            qry = jnp.bitwise_and(lax.broadcasted_iota(jnp.int32, st.shape, 1), tq - 1)
            st = jnp.where(key <= qry, st, -jnp.inf)
        st_ref[slot] = st

    def absorb(slot, kb):
        start = pl.multiple_of(kb * tk, tk)
        st = st_ref[slot]
        m_old = m_ref[...]
        m_new = jnp.maximum(m_old, jnp.max(st, axis=0, keepdims=True))
        alpha = jnp.exp2(m_old - m_new)
        pt = jnp.exp2(st - m_new)
        l_ref[...] = alpha * l_ref[...] + jnp.sum(pt, axis=0, keepdims=True)
        acc_ref[...] = alpha * acc_ref[...] + _dot(vt_ref[:, pl.ds(start, tk)], pt.astype(BF16))
        m_ref[...] = m_new

    assert tq == 2 * tk
    first_diag = 2 * qi
    scores(0, first_diag, 0)
    scores(1, first_diag + 1, tk)
    absorb(0, first_diag)

    def body(t, carry):
        scores(0, 2 * t)
        absorb(1, jnp.where(t == 0, first_diag + 1, 2 * t - 1))
        scores(1, 2 * t + 1)
        absorb(0, 2 * t)
        return carry

    lax.fori_loop(0, qi, body, 0)
    absorb(1, jnp.where(qi == 0, first_diag + 1, 2 * qi - 1))

    lam = _lam_value(lam_ref, lam_init)
    inv_l = 1.0 / l_ref[...]
    acc = acc_ref[...] * inv_l
    o = (acc[:, :tq] - lam * acc[:, tq:]).T
    o_ref[0] = ((_rms(o) * g_ref[...]) * (1.0 - lam_init)).astype(BF16)


def _diff_attn_prompt(qb, kb, vb, lam_vecs, g_diff, lam_init):
    b, l, _ = qb.shape
    tq, tk = 1024, 512
    whole = pl.BlockSpec((1, l, LANES), lambda bi, h, qi: (bi, 0, h))
    return pl.pallas_call(
        functools.partial(_dattn_kernel, lam_init, tq, tk),
        grid=(b, H_B, l // tq),
        in_specs=[pl.BlockSpec((1, tq, LANES), lambda bi, h, qi: (bi, qi, h)), whole, whole,
                  pl.BlockSpec(lam_vecs.shape, lambda bi, h, qi: (0, 0)),
                  pl.BlockSpec((1, DV_B), lambda bi, h, qi: (0, 0))],
        out_specs=pl.BlockSpec((1, tq, LANES), lambda bi, h, qi: (bi, qi, h)),
        out_shape=jax.ShapeDtypeStruct((b, l, H_B * DV_B), BF16),
        scratch_shapes=[pltpu.VMEM((DV_B, l), BF16), pltpu.VMEM((2, tk, 2 * tq), F32),
                        pltpu.VMEM((1, 2 * tq), F32), pltpu.VMEM((1, 2 * tq), F32),
                        pltpu.VMEM((DV_B, 2 * tq), F32)],
        compiler_params=_cparams("arbitrary", "arbitrary", "arbitrary"),
        name="diff_attn_prompt",
    )(qb, kb, vb, lam_vecs, g_diff.reshape(1, DV_B))


TOK_PAD = F32_SUBLANES
HEAD_ROWS = 2 * TOK_PAD
Q_ROWS = H_B * HEAD_ROWS
KV_ROWS = PAGE_SIZE * H_B


def _dattn_dec_kernel(lam_init, n_pages, pt_ref, q_ref, kn_ref, vn_ref, *refs):
    k_refs = refs[:n_pages]
    v_refs = refs[n_pages:2 * n_pages]
    lam_ref, g_ref, o_ref, m_ref, l_ref, acc_ref = refs[2 * n_pages:]
    step = pl.program_id(1)
    q = q_ref[0]

    @pl.when(step == 0)
    def _():
        s = _dot_nt(q, kn_ref[0])
        r = lax.broadcasted_iota(jnp.int32, s.shape, 0)
        c = lax.broadcasted_iota(jnp.int32, s.shape, 1)
        s = jnp.where((c % H_B == r // HEAD_ROWS) & (c // H_B <= r % TOK_PAD), s, -jnp.inf)
        m = jnp.max(s, axis=-1, keepdims=True)
        p = jnp.exp2(s - m)
        m_ref[...] = m
        l_ref[...] = jnp.sum(p, axis=-1, keepdims=True)
        acc_ref[...] = _dot(p.astype(BF16), vn_ref[0])

    own_head = (lax.broadcasted_iota(jnp.int32, (Q_ROWS, KV_ROWS), 1) % H_B
                == lax.broadcasted_iota(jnp.int32, (Q_ROWS, KV_ROWS), 0) // HEAD_ROWS)
    bias = jnp.where(own_head, 0.0, -jnp.inf)
    scores = [_dot_nt(q, kr[0].astype(BF16)) + bias for kr in k_refs]
    m_old = m_ref[...]
    m_new = m_old
    for s in scores:
        m_new = jnp.maximum(m_new, jnp.max(s, axis=-1, keepdims=True))
    alpha = jnp.exp2(m_old - m_new)
    l_new = alpha * l_ref[...]
    acc = alpha * acc_ref[...]
    for s, vr in zip(scores, v_refs):
        p = jnp.exp2(s - m_new)
        l_new = l_new + jnp.sum(p, axis=-1, keepdims=True)
        acc = acc + _dot(p.astype(BF16), vr[0].astype(BF16))
    m_ref[...] = m_new
    l_ref[...] = l_new
    acc_ref[...] = acc

    @pl.when(step == pl.num_programs(1) - 1)
    def _():
        lam = _lam_value(lam_ref, lam_init)
        o = acc / l_new
        for h in range(H_B):
            top = HEAD_ROWS * h
            d = o[top:top + TOK_PAD] - lam * o[top + TOK_PAD:top + HEAD_ROWS]
            o_ref[0, :, h * DV_B:(h + 1) * DV_B] = ((_rms(d) * g_ref[...]) * (1.0 - lam_init)).astype(BF16)


def _diff_attn_decode(q_rows, k_new, v_new, cache_k, cache_v, page_base, page_table, lam_vecs, g_diff, lam_init):
    b = q_rows.shape[0]
    n_steps = page_table.shape[1] // PAGES_PER_STEP
    per_b = lambda r, w: pl.BlockSpec((1, r, w), lambda bi, s, pt: (bi, 0, 0))

    def page_spec(j):
        return pl.BlockSpec((1, KV_ROWS, LANES),
                            lambda bi, s, pt: (page_base + pt[bi, s * PAGES_PER_STEP + j], 0, 0))

    pages = [page_spec(j) for j in range(PAGES_PER_STEP)]
    grid_spec = pltpu.PrefetchScalarGridSpec(
        num_scalar_prefetch=1,
        grid=(b, n_steps),
        in_specs=[per_b(Q_ROWS, LANES), per_b(DEC_SEQ * H_B, LANES), per_b(DEC_SEQ * H_B, LANES)]
        + pages + pages + [pl.BlockSpec(lam_vecs.shape, lambda bi, s, pt: (0, 0)),
                           pl.BlockSpec((1, DV_B), lambda bi, s, pt: (0, 0))],
        out_specs=per_b(TOK_PAD, H_B * DV_B),
        scratch_shapes=[pltpu.VMEM((Q_ROWS, 1), F32), pltpu.VMEM((Q_ROWS, 1), F32),
                        pltpu.VMEM((Q_ROWS, DV_B), F32)],
    )
    return pl.pallas_call(
        functools.partial(_dattn_dec_kernel, lam_init, PAGES_PER_STEP),
        grid_spec=grid_spec,
        out_shape=jax.ShapeDtypeStruct((b, TOK_PAD, H_B * DV_B), BF16),
        compiler_params=_cparams("arbitrary", "arbitrary"),
        name="diff_attn_decode",
    )(page_table, q_rows, k_new, v_new, *([cache_k] * PAGES_PER_STEP), *([cache_v] * PAGES_PER_STEP),
      lam_vecs, g_diff.reshape(1, DV_B))


def _decode_q_rows(qb):
    b = qb.shape[0]
    q = qb.reshape(b, DEC_SEQ, H_B, DV_B).transpose(0, 2, 1, 3)
    q = jnp.pad(q, ((0, 0), (0, 0), (0, TOK_PAD - DEC_SEQ), (0, 0)))[:, :, None]
    keep = (np.arange(DV_B)[None, :] // DH_B) == np.arange(2)[:, None]
    keep = jnp.asarray(keep)[None, None, :, None, :]
    return jnp.where(keep, q, jnp.zeros_like(q)).reshape(b, Q_ROWS, DV_B)


def _gla_tables(chunk):
    idx = np.arange(chunk)
    i, t = idx[:, None], idx[None, :]
    mats = [t <= i]
    masks = [i == t]
    s = 1
    while s < chunk:
        same = (t // s) == (i // s)
        odd = ((i // s) % 2) == 1
        if 2 * s <= GLA_MATMUL_LEVEL_SPAN:
            mats.append(same & np.where(odd, t <= i, t > i))
        masks.append(((i // (2 * s)) == (t // (2 * s))) & odd & (((t // s) % 2) == 0))
        s *= 2
    return (jnp.asarray(np.concatenate(mats, axis=0).astype(np.float32), BF16),
            jnp.asarray(np.stack(masks).astype(np.float32)))


def _gla_kernel(chunk, n_chunks, seqs, valid, q_ref, k_ref, g_ref, v_ref, r_ref, s0_ref, gn_ref, m_ref, p_ref,
                o_ref, s_ref, st_ref):
    i = pl.program_id(1)

    @pl.when(i == 0)
    def _():
        for bi in range(seqs):
            for h in range(H_C):
                st_ref[bi, h] = s0_ref[bi, h].T

    n_levels = p_ref.shape[0] - 1
    m_all = m_ref[...]
    live = lax.broadcasted_iota(jnp.int32, (chunk, 1), 0) < valid
    for bi, c in itertools.product(range(seqs), range(n_chunks)):
        rows = slice(c * chunk, (c + 1) * chunk)
        g = g_ref[bi, rows,:]
        if valid < chunk:
            g = jnp.where(live, g, 0.0)
        g_hi = g.astype(BF16)
        g_lo = (g - g_hi.astype(F32)).astype(BF16)
        sums = _dot(m_all, g_hi) + _dot(m_all, g_lo)
        cum = sums[0:chunk]
        exponents = [cum, cum[chunk - 1:chunk] - cum]
        n_mats = m_all.shape[0] // chunk
        for lv in range(n_levels):
            if lv + 1 < n_mats:
                exponents.append(sums[(1 + lv) * chunk:(2 + lv) * chunk])
            else:
                s = 2 ** lv
                pairs = cum.reshape(chunk // (2 * s), 2 * s, cum.shape[-1])
                rel = pairs - pairs[:, s - 1:s, :]
                upper = lax.broadcasted_iota(jnp.int32, rel.shape, 1) >= s
                exponents.append(jnp.where(upper, rel, -rel).reshape(cum.shape))
        e_all = jnp.exp(jnp.concatenate(exponents, axis=0))
        heads = []
        for h in range(H_C):
            kl = slice(h * DK_C, (h + 1) * DK_C)
            e = e_all[:, kl]
            q = q_ref[bi, rows,kl]
            k = k_ref[bi, rows,kl]
            v = v_ref[bi, rows,h * DV_C:(h + 1) * DV_C]
            if valid < chunk:
                k = jnp.where(live, k, jnp.zeros_like(k))
                v = jnp.where(live, v, jnp.zeros_like(v))
            qf = q.astype(F32)
            kf = k.astype(F32)
            level = lambda lv: e[(2 + lv) * chunk:(3 + lv) * chunk]
            prods = [_dot_nt(q, k)] + [_dot_nt((qf * level(lv)).astype(BF16), (kf * level(lv)).astype(BF16))
                                       for lv in range(n_levels)]
            heads.append(dict(v=v, prods=prods, q_in=(qf * e[0:chunk]).astype(BF16),
                              k_out=(kf * e[chunk:2 * chunk]).astype(BF16), decay=e[chunk - 1:chunk]))
        for hd in heads:
            att = hd["prods"][0] * p_ref[0]
            for lv in range(n_levels):
                att = att + hd["prods"][1 + lv] * p_ref[1 + lv]
            hd["intra"] = _dot(att.astype(BF16), hd["v"])
            hd["update"] = _dot_tn(hd["v"], hd["k_out"])
        for h, hd in enumerate(heads):
            vl = slice(h * DV_C, (h + 1) * DV_C)
            st = st_ref[bi, h]
            o = hd["intra"] + _dot_nt(hd["q_in"], st.astype(BF16))
            st_ref[bi, h] = st * hd["decay"] + hd["update"]
            gate = r_ref[bi, rows,vl].astype(F32)
            o_ref[bi, rows,vl] = ((_rms(o) * gn_ref[...]) * (gate * jax.nn.sigmoid(gate))).astype(BF16)

    @pl.when(i == pl.num_programs(1) - 1)
    def _():
        for bi in range(seqs):
            for h in range(H_C):
                s_ref[bi, h] = st_ref[bi, h].T


def _gla(q, k, g, v, r, s0, g_norm, chunk, valid):
    b, l, _ = q.shape
    blk = min(l, CHUNKS_PER_STEP * chunk)
    n_chunks = blk // chunk
    seqs = _seqs_per_step(b, l, blk)
    m_all, masks = _gla_tables(chunk)
    qk = pl.BlockSpec((seqs, blk, H_C * DK_C), lambda bi, i: (bi, i, 0))
    vr = pl.BlockSpec((seqs, blk, H_C * DV_C), lambda bi, i: (bi, i, 0))
    state = pl.BlockSpec((seqs, H_C, DK_C, DV_C), lambda bi, i: (bi, 0, 0, 0))
    const = lambda a: pl.BlockSpec(a.shape, lambda bi, i: (0,) * a.ndim)
    gn = g_norm.reshape(1, DV_C)
    return pl.pallas_call(
        functools.partial(_gla_kernel, chunk, n_chunks, seqs, valid),
        grid=(b // seqs, l // blk),
        in_specs=[qk, qk, qk, vr, vr, state, const(gn), const(m_all), const(masks)],
        out_specs=[vr, state],
        out_shape=[jax.ShapeDtypeStruct((b, l, H_C * DV_C), BF16),
                   jax.ShapeDtypeStruct((b, H_C, DK_C, DV_C), F32)],
        scratch_shapes=[pltpu.VMEM((seqs, H_C, DV_C, DK_C), F32)],
        compiler_params=_cparams("arbitrary", "arbitrary"),
        name="gla",
    )(q, k, g, v, r, s0, gn, m_all, masks)


def _rope_tables(pos):
    inv = ROPE_THETA ** (-jnp.arange(0, DK_A, 2, dtype=F32) / DK_A)
    ang = pos.astype(F32)[:, None] * inv[None, :]
    cos = jnp.tile(jnp.cos(ang), (1, 4))
    sin = jnp.tile(jnp.concatenate([-jnp.sin(ang), jnp.sin(ang)], axis=1), (1, 2))
    return cos, sin


def _pad_seq(t, batch, seq_len):
    t = t.reshape(batch, seq_len, t.shape[-1])
    return jnp.pad(t, ((0, 0), (0, SAMPLE_PAD - seq_len), (0, 0)))


def _unpad_seq(t, seq_len):
    return t[:, :seq_len].reshape(t.shape[0] * seq_len, t.shape[-1])


def kernel(x_prompt, x_sample, cache_k, cache_v, state_ret, state_gla, state_conv, page_table, c_prompt, c_sample, w_ada, b_ada, g_pre_mix, g_post_mix, g_pre_ffn, g_post_ffn, w_in_ab, w_out_ab, lam_q1, lam_k1, lam_q2, lam_k2, g_diff, w_in_c, w_gate_c, b_gate_c, g_gla, w_out_c, w_up, conv_w, conv_b, w_down):
    rows_p, rows_s = BATCH * SEQ, DEC_BATCH * DEC_SEQ
    xp = x_prompt.reshape(rows_p, D_MODEL)
    xs = x_sample.reshape(rows_s, D_MODEL)

    c_all = jnp.pad(jnp.concatenate([jnp.repeat(c_sample, DEC_SEQ, axis=0), c_prompt], axis=0),
                    ((0, (-BATCH) % F32_SUBLANES), (0, 0)))
    ada_s, ada_p = _ada(c_all, rows_s, w_ada, b_ada)

    cos_p, sin_p = _rope_tables(jnp.arange(SEQ))
    cos_s, sin_s = _rope_tables(jnp.tile(PAST_LEN + jnp.arange(DEC_SEQ), DEC_BATCH))

    wup = w_up.astype(BF16)
    wdn = w_down.astype(BF16)
    k_p, v_p, k_s, v_s, ret_p, ret_s, gla_p, gla_s, conv_p, conv_s = ([] for _ in range(10))
    for l in range(DEPTH):
        ms = [_Mod(ada_s, l, which) for which in range(6)]
        mp = [_Mod(ada_p, l, which) for which in range(6)]
        if l % 2 == 0:
            e = l // 2
            lam_init = 0.8 - 0.6 * math.exp(-0.3 * l)
            lam_vecs = jnp.stack([lam_q1[e], lam_k1[e], lam_q2[e], lam_k2[e]]).astype(F32)
            w_in = w_in_ab[e].astype(BF16)
            w_out = w_out_ab[e].astype(BF16)
            w_parts = [w_out[:H_A * DV_A], w_out[H_A * DV_A:]]

            qa, ka, va, ga, qb, kb32, kb16, vb32, vb16 = _in_proj_ab(
                xp, mp[0], mp[1], g_pre_mix[l], w_in, cos_p, sin_p, SEQ)
            seq3 = lambda t: t.reshape(BATCH, SEQ, t.shape[-1])
            ret, sp = _retention(seq3(qa), seq3(ka), seq3(va), seq3(ga),
                                 jnp.zeros((BATCH, H_A, DK_A, DV_A), F32), CHUNK_RET, CHUNK_RET)
            dif = _diff_attn_prompt(seq3(qb), seq3(kb16), seq3(vb16), lam_vecs, g_diff[e], lam_init)
            xp = _out_proj([ret.reshape(rows_p, -1), dif.reshape(rows_p, -1)], w_parts, xp, mp[2],
                           g_post_mix[l], SEQ)
            k_p.append(kb32.reshape(BATCH, SEQ, H_B, 2 * DH_B))
            v_p.append(vb32.reshape(BATCH, SEQ, H_B, DV_B))
            ret_p.append(sp)

            qa, ka, va, ga, qb, kb32, kb16, vb32, vb16 = _in_proj_ab(
                xs, ms[0], ms[1], g_pre_mix[l], w_in, cos_s, sin_s, DEC_SEQ)
            pad = lambda t: _pad_seq(t, DEC_BATCH, DEC_SEQ)
            ret, ss = _retention(pad(qa), pad(ka), pad(va), pad(ga), state_ret[e], SAMPLE_PAD, DEC_SEQ)
            n_phys = cache_k.shape[1]
            kv_rows = lambda t: t.reshape(DEC_BATCH, DEC_SEQ * H_B, DV_B)
            dif = _diff_attn_decode(
                _decode_q_rows(qb.reshape(DEC_BATCH, DEC_SEQ, 512)), kv_rows(kb16), kv_rows(vb16),
                cache_k.reshape(N_EVEN * n_phys, KV_ROWS, LANES), cache_v.reshape(N_EVEN * n_phys, KV_ROWS, LANES),
                e * n_phys, page_table, lam_vecs, g_diff[e], lam_init)
            xs = _out_proj([_unpad_seq(ret, DEC_SEQ), _unpad_seq(dif, DEC_SEQ)], w_parts, xs, ms[2],
                           g_post_mix[l], DEC_SEQ)
            k_s.append(kb32.reshape(DEC_BATCH, DEC_SEQ, H_B, 2 * DH_B))
            v_s.append(vb32.reshape(DEC_BATCH, DEC_SEQ, H_B, DV_B))
            ret_s.append(ss)
        else:
            o = l // 2
            n_main = sum(C_WIDTHS_MAIN)
            w_in = w_in_c[o][:, :n_main].astype(BF16)
            w_a = jnp.pad(w_in_c[o][:, n_main:], ((0, 0), (0, LANES - GATE_RANK))).astype(BF16)
            w_gate = jnp.pad(w_gate_c[o], ((0, LANES - GATE_RANK), (0, 0))).astype(BF16)
            w_out = w_out_c[o].astype(BF16)

            q, k, v, r, lg = _in_proj_c(xp, mp[0], mp[1], g_pre_mix[l], w_in, w_a, w_gate, b_gate_c[o], SEQ)
            seq3 = lambda t: t.reshape(BATCH, SEQ, t.shape[-1])
            og, sp = _gla(seq3(q), seq3(k), seq3(lg), seq3(v), seq3(r),
                          jnp.zeros((BATCH, H_C, DK_C, DV_C), F32), g_gla[o], CHUNK_GLA, CHUNK_GLA)
            xp = _out_proj([og.reshape(rows_p, -1)], [w_out], xp, mp[2], g_post_mix[l], SEQ)
            gla_p.append(sp)

            q, k, v, r, lg = _in_proj_c(xs, ms[0], ms[1], g_pre_mix[l], w_in, w_a, w_gate, b_gate_c[o], DEC_SEQ)
            pad = lambda t: _pad_seq(t, DEC_BATCH, DEC_SEQ)
            og, ss = _gla(pad(q), pad(k), pad(lg), pad(v), pad(r), state_gla[o], g_gla[o], SAMPLE_PAD, DEC_SEQ)
            xs = _out_proj([_unpad_seq(og, DEC_SEQ)], [w_out], xs, ms[2], g_post_mix[l], DEC_SEQ)
            gla_s.append(ss)

        xp, cbp = _conv_ffn(xp, mp[3], mp[4], mp[5], g_pre_ffn[l], g_post_ffn[l], l, wup, conv_w[l], conv_b[l],
                            wdn, SEQ)
        xs, up_s = _conv_ffn(xs, ms[3], ms[4], ms[5], g_pre_ffn[l], g_post_ffn[l], l, wup, conv_w[l], conv_b[l],
                             wdn, DEC_SEQ, state=state_conv[l])
        conv_p.append(cbp)
        conv_s.append(up_s.reshape(DEC_BATCH, DEC_SEQ, 2 * D_FF)[:, DEC_SEQ - (CONV_W - 1):])

    stack = lambda ts: ts[0][None] if len(ts) == 1 else jnp.stack(ts)
    return (xp.reshape(BATCH, SEQ, D_MODEL), xs.reshape(DEC_BATCH, DEC_SEQ, D_MODEL),
            stack(k_p), stack(v_p), stack(k_s), stack(v_s),
            stack(ret_p), stack(ret_s), stack(gla_p), stack(gla_s),
            stack(conv_p), stack(conv_s))
```

```python
import functools
import itertools
import math
from typing import NamedTuple

import numpy as np
import jax
import jax.numpy as jnp
from jax import lax
from jax.experimental import pallas as pl
from jax.experimental.pallas import tpu as pltpu

D_MODEL = 1024
BATCH = 4
SEQ = 4096
DEPTH = 2
DEC_BATCH = 32
DEC_SEQ = 4
PAST_LEN = 8192
PAGE_SIZE = 128
N_EVEN = (DEPTH + 1) // 2
N_ODD = DEPTH // 2
H_A = 4
DK_A = D_MODEL // 16
DV_A = D_MODEL // 8
CHUNK_RET = 128
H_B = 4
DH_B = D_MODEL // 16
DV_B = 2 * DH_B
H_C = 4
DK_C = D_MODEL // 8
DV_C = D_MODEL // 4
GATE_RANK = 16
GATE_TAU = 16.0
C_WIDTHS_MAIN = (H_C * DK_C, H_C * DK_C, H_C * DV_C, H_C * DV_C)
CHUNK_GLA = 64
D_FF = ((8 * D_MODEL // 3 + 127) // 128) * 128
CONV_W = 3
ROPE_THETA = 10000.0
EPS = 1e-6

LANES = 128
VMEM_LIMIT = 56 * 1024 * 1024
ROW_TILE = 1024
FFN_ROW_TILE = 1024
F32_SUBLANES = 8
BF16_SUBLANES = 16
GLA_MATMUL_LEVEL_SPAN = F32_SUBLANES
CHUNKS_PER_STEP = 16
SHORT_SEQS_PER_STEP = 8
SAMPLE_PAD = BF16_SUBLANES
FF_CHUNK = 256
PAGES_PER_STEP = 32
ROPE_HALF = DK_A // 2
LOG2E = math.log2(math.e)
BF16 = jnp.bfloat16
F32 = jnp.float32


def _cparams(*sem):
    return pltpu.CompilerParams(dimension_semantics=sem, vmem_limit_bytes=VMEM_LIMIT)


def _dot(a, b):
    return jnp.dot(a, b, preferred_element_type=F32)


def _dot_nt(a, b):
    return lax.dot_general(a, b, (((1,), (1,)), ((), ())), preferred_element_type=F32)


def _dot_tn(a, b):
    return lax.dot_general(a, b, (((0,), (0,)), ((), ())), preferred_element_type=F32)


def _rms(x):
    return x * lax.rsqrt(jnp.mean(x * x, axis=-1, keepdims=True) + EPS)


def _modulate(x, g, shift, scale):
    return (_rms(x) * g) * (1.0 + scale) + shift


def _rope(x, cos, sin_signed):
    w = x.shape[-1]
    fwd = pltpu.roll(x, ROPE_HALF, axis=1)
    bwd = pltpu.roll(x, w - ROPE_HALF, axis=1)
    reps = w // LANES
    first_half = (lax.broadcasted_iota(jnp.int32, x.shape, 1) % DK_A) < ROPE_HALF
    partner = jnp.where(first_half, bwd, fwd)
    return (x * jnp.concatenate([cos] * reps, axis=1)
            + partner * jnp.concatenate([sin_signed] * reps, axis=1))


def _ada_kernel(n_tok, c_ref, w_ref, b_ref, tok_ref, seq_ref):
    c = c_ref[...]
    s = c * jax.nn.sigmoid(c)
    o = jnp.dot(s, w_ref[0], preferred_element_type=F32, precision=lax.Precision.HIGHEST) + b_ref[0]
    tok_ref[0] = o[:n_tok]
    for r in range(seq_ref.shape[1]):
        seq_ref[0, r] = o[n_tok + r:n_tok + r + 1]


def _ada(c_all, n_tok, w_ada, b_ada):
    rows = c_all.shape[0]
    n_seq = rows - n_tok
    tn = 1536
    return pl.pallas_call(
        functools.partial(_ada_kernel, n_tok),
        grid=(DEPTH, 6 * D_MODEL // tn),
        in_specs=[pl.BlockSpec((rows, D_MODEL), lambda l, j: (0, 0)),
                  pl.BlockSpec((1, D_MODEL, tn), lambda l, j: (l, 0, j)),
                  pl.BlockSpec((1, 1, tn), lambda l, j: (l, 0, j))],
        out_specs=[pl.BlockSpec((1, n_tok, tn), lambda l, j: (l, 0, j)),
                   pl.BlockSpec((1, n_seq, 1, tn), lambda l, j: (l, 0, 0, j))],
        out_shape=[jax.ShapeDtypeStruct((DEPTH, n_tok, 6 * D_MODEL), F32),
                   jax.ShapeDtypeStruct((DEPTH, n_seq, 1, 6 * D_MODEL), F32)],
        compiler_params=_cparams("arbitrary", "arbitrary"),
        name="adaln",
    )(c_all, w_ada, b_ada.reshape(DEPTH, 1, 6 * D_MODEL))


class _Mod(NamedTuple):
    ada: jax.Array
    layer: int
    which: int


def _row_cfg(rows, seq_len, row_tile=ROW_TILE):
    tm = min(row_tile, rows)
    per_token = seq_len < tm
    tiles_per_seq = 1 if per_token else seq_len // tm
    return tm, per_token, tiles_per_seq


def _mod_operand(m, tm, per_token, tiles_per_seq):
    if per_token:
        return pl.BlockSpec((1, tm, D_MODEL), lambda i: (m.layer, i, m.which)), m.ada
    return pl.BlockSpec((None, 1, 1, D_MODEL), lambda i: (m.layer, i // tiles_per_seq, 0, m.which)), m.ada


def _inab_kernel(x_ref, sh_ref, sc_ref, g_ref, w_ref, cos_ref, sin_ref,
                 qa_ref, ka_ref, va_ref, ga_ref, qb_ref, kb32_ref, kb16_ref, vb32_ref, vb16_ref):
    h = _modulate(x_ref[...], g_ref[...], sh_ref[0], sc_ref[0]).astype(BF16)
    cos = cos_ref[...]
    sin = sin_ref[...]

    def proj(lo, width):
        return _dot(h, w_ref[:, lo:lo + width])

    qa_ref[...] = _rope(proj(0, 256), cos, sin).astype(BF16)
    ka_ref[...] = (_rope(proj(256, 256), cos, sin) * (DK_A ** -0.5)).astype(BF16)
    va_ref[...] = proj(512, 512).astype(BF16)
    ga_ref[...] = proj(1024, 512).astype(BF16)
    qb_ref[...] = (_rope(proj(1536, 512), cos, sin) * (DH_B ** -0.5 * LOG2E)).astype(BF16)
    kb = _rope(proj(2048, 512), cos, sin)
    kb16_ref[...] = kb.astype(BF16)
    vb = proj(2560, 512)
    vb16_ref[...] = vb.astype(BF16)
    tm = kb.shape[0]
    for h in range(H_B):
        kb32_ref[pl.ds(h, tm, stride=H_B), :] = kb[:, h * DV_B:(h + 1) * DV_B]
        vb32_ref[pl.ds(h, tm, stride=H_B), :] = vb[:, h * DV_B:(h + 1) * DV_B]


def _in_proj_ab(x, shift, scale, g_pre, w_bf, cos, sin, seq_len):
    rows = x.shape[0]
    tm, per_token, tps = _row_cfg(rows, seq_len)
    (sh_spec, sh_arg), (sc_spec, sc_arg) = (_mod_operand(m, tm, per_token, tps) for m in (shift, scale))
    pos_map = (lambda i: (i, 0)) if per_token else (lambda i: (i % tps, 0))
    row = lambda w: pl.BlockSpec((tm, w), lambda i: (i, 0))
    widths = (256, 256, 512, 512, 512, 512, 512, 512, 512)
    dtypes = (BF16, BF16, BF16, BF16, BF16, F32, BF16, F32, BF16)
    cache_rows = lambda w, d: d == F32
    out_spec = lambda w, d: (pl.BlockSpec((tm * H_B, DV_B), lambda i: (i, 0)) if cache_rows(w, d) else row(w))
    out_shape = lambda w, d: jax.ShapeDtypeStruct((rows * H_B, DV_B) if cache_rows(w, d) else (rows, w), d)
    return pl.pallas_call(
        _inab_kernel,
        grid=(rows // tm,),
        in_specs=[row(D_MODEL), sh_spec, sc_spec,
                  pl.BlockSpec((1, D_MODEL), lambda i: (0, 0)),
                  pl.BlockSpec(w_bf.shape, lambda i: (0, 0)),
                  pl.BlockSpec((tm, LANES), pos_map), pl.BlockSpec((tm, LANES), pos_map)],
        out_specs=[out_spec(w, d) for w, d in zip(widths, dtypes)],
        out_shape=[out_shape(w, d) for w, d in zip(widths, dtypes)],
        compiler_params=_cparams("arbitrary"),
        name="in_proj_ab",
    )(x, sh_arg, sc_arg, g_pre.reshape(1, D_MODEL), w_bf, cos, sin)


def _inc_kernel(x_ref, sh_ref, sc_ref, g_ref, w_ref, wa_ref, wg_ref, bg_ref,
                q_ref, k_ref, v_ref, r_ref, lg_ref):
    h = _modulate(x_ref[...], g_ref[...], sh_ref[0], sc_ref[0]).astype(BF16)

    def proj(lo, width):
        return _dot(h, w_ref[:, lo:lo + width])

    q_ref[...] = (proj(0, 512) * (DK_C ** -0.5)).astype(BF16)
    k_ref[...] = proj(512, 512).astype(BF16)
    v_ref[...] = proj(1024, 1024).astype(BF16)
    r_ref[...] = proj(2048, 1024).astype(BF16)
    a = _dot(h, wa_ref[...])
    z = _dot(a.astype(BF16), wg_ref[...]) + bg_ref[...]
    softplus_neg = jnp.maximum(-z, 0.0) + jnp.log1p(jnp.exp(-jnp.abs(z)))
    lg_ref[...] = -softplus_neg / GATE_TAU


def _in_proj_c(x, shift, scale, g_pre, w_bf, wa_bf, wg_bf, b_gate, seq_len):
    rows = x.shape[0]
    tm, per_token, tps = _row_cfg(rows, seq_len)
    (sh_spec, sh_arg), (sc_spec, sc_arg) = (_mod_operand(m, tm, per_token, tps) for m in (shift, scale))
    row = lambda w: pl.BlockSpec((tm, w), lambda i: (i, 0))
    widths = (512, 512, 1024, 1024, 512)
    dtypes = (BF16, BF16, BF16, BF16, F32)
    full = lambda a: pl.BlockSpec(a.shape, lambda i: (0, 0))
    bg = b_gate.reshape(1, -1)
    return pl.pallas_call(
        _inc_kernel,
        grid=(rows // tm,),
        in_specs=[row(D_MODEL), sh_spec, sc_spec, pl.BlockSpec((1, D_MODEL), lambda i: (0, 0)),
                  full(w_bf), full(wa_bf), full(wg_bf), full(bg)],
        out_specs=[row(w) for w in widths],
        out_shape=[jax.ShapeDtypeStruct((rows, w), d) for w, d in zip(widths, dtypes)],
        compiler_params=_cparams("arbitrary"),
        name="in_proj_c",
    )(x, sh_arg, sc_arg, g_pre.reshape(1, D_MODEL), w_bf, wa_bf, wg_bf, bg)


def _outproj_kernel(n_parts, *refs):
    parts = refs[:n_parts]
    ws = refs[n_parts:2 * n_parts]
    x_ref, gate_ref, g_ref, o_ref = refs[2 * n_parts:]
    y = _dot(parts[0][...], ws[0][...])
    for p, w in zip(parts[1:], ws[1:]):
        y = y + _dot(p[...], w[...])
    o_ref[...] = x_ref[...] + gate_ref[0] * (_rms(y) * g_ref[...])


def _out_proj(parts, weights, x, gate, g_post, seq_len):
    rows = x.shape[0]
    tm, per_token, tps = _row_cfg(rows, seq_len)
    gate_spec, gate_arg = _mod_operand(gate, tm, per_token, tps)
    n = len(parts)
    return pl.pallas_call(
        functools.partial(_outproj_kernel, n),
        grid=(rows // tm,),
        in_specs=([pl.BlockSpec((tm, p.shape[1]), lambda i: (i, 0)) for p in parts]
                  + [pl.BlockSpec(w.shape, lambda i: (0, 0)) for w in weights]
                  + [pl.BlockSpec((tm, D_MODEL), lambda i: (i, 0)), gate_spec,
                     pl.BlockSpec((1, D_MODEL), lambda i: (0, 0))]),
        out_specs=pl.BlockSpec((tm, D_MODEL), lambda i: (i, 0)),
        out_shape=jax.ShapeDtypeStruct((rows, D_MODEL), F32),
        compiler_params=_cparams("arbitrary"),
        name="out_proj",
    )(*parts, *weights, x, gate_arg, g_post.reshape(1, D_MODEL))


def _ffn_kernel(per_token, seq_len, tiles_per_seq, *refs):
    if per_token:
        (x_ref, sh_ref, sc_ref, gate_ref, gpre_ref, gpost_ref, wup_ref, cw_ref, cb_ref, wdn_ref,
         s1_ref, s2_ref, xo_ref, conv_ref, h_ref, acc_ref, *u_refs) = refs
    else:
        (x_ref, sh_ref, sc_ref, gate_ref, gpre_ref, gpost_ref, wup_ref, cw_ref, cb_ref, wdn_ref,
         xo_ref, conv_ref, h_ref, acc_ref, *u_refs) = refs
    tm = x_ref.shape[0]
    halo = h_ref.shape[0] - tm
    x = x_ref[...]
    if per_token:
        tau = lax.broadcasted_iota(jnp.int32, (tm, FF_CHUNK), 0) % seq_len
    else:
        seq_start = (pl.program_id(0) % tiles_per_seq) == 0

        @pl.when(seq_start)
        def _():
            h_ref[:halo] = jnp.zeros((halo, D_MODEL), BF16)

        @pl.when(jnp.logical_not(seq_start))
        def _():
            h_ref[:halo] = h_ref[tm:]
    h_ref[halo:] = _modulate(x, gpre_ref[...], sh_ref[0], sc_ref[0]).astype(BF16)
    acc_ref[...] = jnp.zeros_like(acc_ref)

    def up(lo):
        return _dot(h_ref[...], wup_ref[:, pl.ds(lo, FF_CHUNK)])

    def conv(slot, half, lo):
        cols = pl.ds(lo, FF_CHUNK)
        cw = cw_ref[:, cols]
        cb = cb_ref[:, cols]
        taps = lambda u2, u1, u0: cb + cw[0:1] * u2 + cw[1:2] * u1 + cw[2:3] * u0
        if per_token:
            u = u_refs[slot][half]
            u1 = jnp.where(tau == 0, s1_ref[:, cols], pltpu.roll(u, 1, axis=0))
            u2 = jnp.where(tau < 2, s2_ref[:, cols], pltpu.roll(u, 2, axis=0))
            conv_ref[:, cols] = u
            return taps(u2, u1, u)
        conv_ref[0, :, cols] = u_refs[slot][half, halo + tm - 2:]
        window = lambda back: u_refs[slot][half, halo - back:halo - back + tm]
        return taps(window(2), window(1), window(0))

    def produce(slot, lo):
        u_refs[slot][0] = up(lo)
        u_refs[slot][1] = up(lo + D_FF)

    def consume(slot, lo):
        g = (jax.nn.gelu(conv(slot, 0, lo)) * conv(slot, 1, lo + D_FF)).astype(BF16)
        acc_ref[...] += _dot(g, wdn_ref[pl.ds(lo, FF_CHUNK), :])

    def body(j, carry):
        lo = pl.multiple_of(j * (2 * FF_CHUNK), 2 * FF_CHUNK)
        produce(1, lo + FF_CHUNK)
        consume(0, lo)
        produce(0, lo + 2 * FF_CHUNK)
        consume(1, lo + FF_CHUNK)
        return carry

    n_chunks = D_FF // FF_CHUNK
    produce(0, 0)
    lax.fori_loop(0, n_chunks // 2, body, 0)
    consume(0, (n_chunks - 1) * FF_CHUNK)
    xo_ref[...] = x + gate_ref[0] * (_rms(acc_ref[...]) * gpost_ref[...])


def _conv_ffn(x, shift, scale, gate, g_pre, g_post, layer, wup_bf, conv_w, conv_b, wdn_bf, seq_len, state=None):
    rows = x.shape[0]
    tm, per_token, tps = _row_cfg(rows, seq_len, FFN_ROW_TILE)
    mods = [_mod_operand(m, tm, per_token, tps) for m in (shift, scale, gate)]
    full = lambda a: pl.BlockSpec(a.shape, lambda i: (0, 0), pipeline_mode=pl.Buffered(1))
    of_layer = lambda a: pl.BlockSpec((None,) + a.shape[1:], lambda i: (layer, 0, 0), pipeline_mode=pl.Buffered(1))
    cb = conv_b.reshape(1, 2 * D_FF)
    in_specs = [pl.BlockSpec((tm, D_MODEL), lambda i: (i, 0))] + [spec for spec, _ in mods] + [
        pl.BlockSpec((1, D_MODEL), lambda i: (0, 0)), pl.BlockSpec((1, D_MODEL), lambda i: (0, 0)),
        of_layer(wup_bf), full(conv_w), full(cb), of_layer(wdn_bf)]
    args = [x] + [arg for _, arg in mods] + [
        g_pre.reshape(1, D_MODEL), g_post.reshape(1, D_MODEL), wup_bf, conv_w, cb, wdn_bf]
    halo = 0 if per_token else BF16_SUBLANES
    scratch = [pltpu.VMEM((halo + tm, D_MODEL), BF16), pltpu.VMEM((tm, D_MODEL), F32),
               pltpu.VMEM((2, halo + tm, FF_CHUNK), F32), pltpu.VMEM((2, halo + tm, FF_CHUNK), F32)]
    if per_token:
        batch = rows // seq_len
        zeros = jnp.zeros((batch, seq_len - 1, 2 * D_FF), F32)
        s1 = jnp.concatenate([state[:, 1:2], zeros], axis=1).reshape(rows, 2 * D_FF)
        s2 = jnp.concatenate([state, zeros[:, 1:]], axis=1).reshape(rows, 2 * D_FF)
        in_specs += [pl.BlockSpec((tm, 2 * D_FF), lambda i: (i, 0))] * 2
        args += [s1, s2]
        conv_spec = pl.BlockSpec((tm, 2 * D_FF), lambda i: (i, 0))
        conv_shape = jax.ShapeDtypeStruct((rows, 2 * D_FF), F32)
    else:
        conv_spec = pl.BlockSpec((1, 2, 2 * D_FF), lambda i: (i // tps, 0, 0))
        conv_shape = jax.ShapeDtypeStruct((rows // seq_len, 2, 2 * D_FF), F32)
    return pl.pallas_call(
        functools.partial(_ffn_kernel, per_token, seq_len, tps),
        grid=(rows // tm,),
        in_specs=in_specs,
        out_specs=[pl.BlockSpec((tm, D_MODEL), lambda i: (i, 0)), conv_spec],
        out_shape=[jax.ShapeDtypeStruct((rows, D_MODEL), F32), conv_shape],
        scratch_shapes=scratch,
        compiler_params=_cparams("arbitrary"),
        name="conv_ffn",
    )(*args)


def _ret_tables(chunk, valid):
    log_gamma = jnp.log1p(-jnp.exp2(-5.0 - jnp.arange(H_A, dtype=F32)))
    idx = jnp.arange(chunk, dtype=F32)
    rel = idx[:, None] - idx[None, :]
    intra = jnp.exp(jnp.where(rel[None] >= 0, rel[None] * log_gamma[:, None, None], -jnp.inf))
    dq = jnp.exp((idx + 1.0)[None, :] * log_gamma[:, None])
    dk = jnp.where(idx[None, :] < valid, jnp.exp((valid - 1.0 - idx)[None, :] * log_gamma[:, None]), 0.0)
    dc = jnp.exp(valid * log_gamma)
    rep = lambda t: jnp.broadcast_to(t[..., None], t.shape + (LANES,))
    dc_rows = rep(jnp.repeat(dc, DK_A).reshape(H_A // 2, 2 * DK_A))
    return intra, rep(dq), rep(dk), dc_rows


def _ret_kernel(chunk, n_chunks, seqs, q_ref, k_ref, v_ref, ga_ref, s0_ref, intra_ref, dq_ref, dk_ref, dc_ref,
                o_ref, s_ref, st_ref):
    i = pl.program_id(1)

    @pl.when(i == 0)
    def _():
        st_ref[...] = s0_ref[...]

    lane = lax.broadcasted_iota(jnp.int32, (chunk, LANES), 1)
    for bi in range(seqs):
        for c in range(n_chunks):
            rows = slice(c * chunk, (c + 1) * chunk)
            heads = []
            for hd in range(H_A):
                pair, side = divmod(hd, 2)
                lanes = slice(pair * LANES, (pair + 1) * LANES)
                qc = q_ref[bi, rows, lanes]
                kc = k_ref[bi, rows, lanes]
                mine = (lane >= DK_A) if side else (lane < DK_A)
                qh = jnp.where(mine, qc, jnp.zeros_like(qc))
                vh = v_ref[bi, rows, hd * DV_A:(hd + 1) * DV_A]
                sc = (_dot_nt(qh, kc) * intra_ref[hd]).astype(BF16)
                kh = jnp.where(mine, kc.astype(F32) * dk_ref[hd], 0.0).astype(BF16)
                heads.append(dict(qh=qh, intra=_dot(sc, vh), update=_dot_tn(kh, vh)))
            for pair in range(H_A // 2):
                s_pair = st_ref[bi, pair]
                s_bf = s_pair.astype(BF16)
                upd = s_pair * dc_ref[pair]
                for side in range(2):
                    hd = 2 * pair + side
                    hl = slice(hd * DV_A, (hd + 1) * DV_A)
                    o = heads[hd]["intra"] + _dot(heads[hd]["qh"], s_bf) * dq_ref[hd]
                    upd = upd + heads[hd]["update"]
                    gate = ga_ref[bi, rows, hl].astype(F32)
                    o_ref[bi, rows, hl] = (_rms(o) * (gate * jax.nn.sigmoid(gate))).astype(BF16)
                st_ref[bi, pair] = upd

    @pl.when(i == pl.num_programs(1) - 1)
    def _():
        s_ref[...] = st_ref[...]


def _seqs_per_step(batch, seq_len, blk):
    return min(batch, SHORT_SEQS_PER_STEP) if seq_len == blk else 1


def _retention(q, k, v, ga, s0, chunk, valid):
    b, l, _ = q.shape
    blk = min(l, CHUNKS_PER_STEP * chunk)
    n_chunks = blk // chunk
    seqs = _seqs_per_step(b, l, blk)
    tables = _ret_tables(chunk, valid)
    seq = lambda w: pl.BlockSpec((seqs, blk, w), lambda bi, i: (bi, i, 0))
    const = lambda a: pl.BlockSpec(a.shape, lambda bi, i: (0,) * a.ndim)
    state = pl.BlockSpec((seqs, H_A // 2, 2 * DK_A, DV_A), lambda bi, i: (bi, 0, 0, 0))
    o, s = pl.pallas_call(
        functools.partial(_ret_kernel, chunk, n_chunks, seqs),
        grid=(b // seqs, l // blk),
        in_specs=[seq(256), seq(256), seq(512), seq(512), state] + [const(t) for t in tables],
        out_specs=[seq(512), state],
        out_shape=[jax.ShapeDtypeStruct((b, l, H_A * DV_A), BF16),
                   jax.ShapeDtypeStruct((b, H_A // 2, 2 * DK_A, DV_A), F32)],
        scratch_shapes=[pltpu.VMEM((seqs, H_A // 2, 2 * DK_A, DV_A), F32)],
        compiler_params=_cparams("arbitrary", "arbitrary"),
        name="retention",
    )(q, k, v, ga, s0.reshape(b, H_A // 2, 2 * DK_A, DV_A), *tables)
    return o, s.reshape(b, H_A, DK_A, DV_A)


def _lam_value(lam_ref, lam_init):
    lv = lam_ref[...]
    return (jnp.exp(jnp.sum(lv[0:1] * lv[1:2], axis=-1, keepdims=True))
            - jnp.exp(jnp.sum(lv[2:3] * lv[3:4], axis=-1, keepdims=True)) + lam_init)


def _dattn_kernel(lam_init, tq, tk, q_ref, k_ref, v_ref, lam_ref, g_ref, o_ref, vt_ref, st_ref, m_ref, l_ref, acc_ref):
    qi = pl.program_id(2)

    @pl.when(qi == 0)
    def _():
        vt_ref[...] = v_ref[0].astype(F32).T.astype(BF16)

    q = q_ref[0]
    lane = lax.broadcasted_iota(jnp.int32, q.shape, 1)
    q2 = jnp.concatenate([jnp.where(lane < DH_B, q, jnp.zeros_like(q)),
                          jnp.where(lane >= DH_B, q, jnp.zeros_like(q))], axis=0)
    m_ref[...] = jnp.full_like(m_ref, -jnp.inf)
    l_ref[...] = jnp.zeros_like(l_ref)
    acc_ref[...] = jnp.zeros_like(acc_ref)

    def scores(slot, kb, diag_offset=None):
        start = pl.multiple_of(kb * tk, tk)
        st = _dot_nt(k_ref[0, pl.ds(start, tk), :], q2)
        if diag_offset is not None:
            key = lax.broadcasted_iota(jnp.int32, st.shape, 0) + diag_offset
            qry = jnp.bitwise_and(lax.broadcasted_iota(jnp.int32, st.shape, 1), tq - 1)
            st = jnp.where(key <= qry, st, -jnp.inf)
        st_ref[slot] = st

    def absorb(slot, kb):
        start = pl.multiple_of(kb * tk, tk)
        st = st_ref[slot]
        m_old = m_ref[...]
        m_new = jnp.maximum(m_old, jnp.max(st, axis=0, keepdims=True))
        alpha = jnp.exp2(m_old - m_new)
        pt = jnp.exp2(st - m_new)
        l_ref[...] = alpha * l_ref[...] + jnp.sum(pt, axis=0, keepdims=True)
        acc_ref[...] = alpha * acc_ref[...] + _dot(vt_ref[:, pl.ds(start, tk)], pt.astype(BF16))
        m_ref[...] = m_new

    assert tq == 2 * tk
    first_diag = 2 * qi
    scores(0, first_diag, 0)
    scores(1, first_diag + 1, tk)
    absorb(0, first_diag)

    def body(t, carry):
        scores(0, 2 * t)
        absorb(1, jnp.where(t == 0, first_diag + 1, 2 * t - 1))
        scores(1, 2 * t + 1)
        absorb(0, 2 * t)
        return carry

    lax.fori_loop(0, qi, body, 0)
    absorb(1, jnp.where(qi == 0, first_diag + 1, 2 * qi - 1))

    lam = _lam_value(lam_ref, lam_init)
    inv_l = 1.0 / l_ref[...]
    acc = acc_ref[...] * inv_l
    o = (acc[:, :tq] - lam * acc[:, tq:]).T
    o_ref[0] = ((_rms(o) * g_ref[...]) * (1.0 - lam_init)).astype(BF16)


def _diff_attn_prompt(qb, kb, vb, lam_vecs, g_diff, lam_init):
    b, l, _ = qb.shape
    tq, tk = 1024, 512
    whole = pl.BlockSpec((1, l, LANES), lambda bi, h, qi: (bi, 0, h))
    return pl.pallas_call(
        functools.partial(_dattn_kernel, lam_init, tq, tk),
        grid=(b, H_B, l // tq),
        in_specs=[pl.BlockSpec((1, tq, LANES), lambda bi, h, qi: (bi, qi, h)), whole, whole,
                  pl.BlockSpec(lam_vecs.shape, lambda bi, h, qi: (0, 0)),
                  pl.BlockSpec((1, DV_B), lambda bi, h, qi: (0, 0))],
        out_specs=pl.BlockSpec((1, tq, LANES), lambda bi, h, qi: (bi, qi, h)),
        out_shape=jax.ShapeDtypeStruct((b, l, H_B * DV_B), BF16),
        scratch_shapes=[pltpu.VMEM((DV_B, l), BF16), pltpu.VMEM((2, tk, 2 * tq), F32),
                        pltpu.VMEM((1, 2 * tq), F32), pltpu.VMEM((1, 2 * tq), F32),
                        pltpu.VMEM((DV_B, 2 * tq), F32)],
        compiler_params=_cparams("arbitrary", "arbitrary", "arbitrary"),
        name="diff_attn_prompt",
    )(qb, kb, vb, lam_vecs, g_diff.reshape(1, DV_B))


TOK_PAD = F32_SUBLANES
HEAD_ROWS = 2 * TOK_PAD
Q_ROWS = H_B * HEAD_ROWS
KV_ROWS = PAGE_SIZE * H_B


def _dattn_dec_kernel(lam_init, n_pages, pt_ref, q_ref, kn_ref, vn_ref, *refs):
    k_refs = refs[:n_pages]
    v_refs = refs[n_pages:2 * n_pages]
    lam_ref, g_ref, o_ref, m_ref, l_ref, acc_ref = refs[2 * n_pages:]
    step = pl.program_id(1)
    q = q_ref[0]

    @pl.when(step == 0)
    def _():
        s = _dot_nt(q, kn_ref[0])
        r = lax.broadcasted_iota(jnp.int32, s.shape, 0)
        c = lax.broadcasted_iota(jnp.int32, s.shape, 1)
        s = jnp.where((c % H_B == r // HEAD_ROWS) & (c // H_B <= r % TOK_PAD), s, -jnp.inf)
        m = jnp.max(s, axis=-1, keepdims=True)
        p = jnp.exp2(s - m)
        m_ref[...] = m
        l_ref[...] = jnp.sum(p, axis=-1, keepdims=True)
        acc_ref[...] = _dot(p.astype(BF16), vn_ref[0])

    own_head = (lax.broadcasted_iota(jnp.int32, (Q_ROWS, KV_ROWS), 1) % H_B
                == lax.broadcasted_iota(jnp.int32, (Q_ROWS, KV_ROWS), 0) // HEAD_ROWS)
    bias = jnp.where(own_head, 0.0, -jnp.inf)
    scores = [_dot_nt(q, kr[0].astype(BF16)) + bias for kr in k_refs]
    m_old = m_ref[...]
    m_new = m_old
    for s in scores:
        m_new = jnp.maximum(m_new, jnp.max(s, axis=-1, keepdims=True))
    alpha = jnp.exp2(m_old - m_new)
    l_new = alpha * l_ref[...]
    acc = alpha * acc_ref[...]
    for s, vr in zip(scores, v_refs):
        p = jnp.exp2(s - m_new)
        l_new = l_new + jnp.sum(p, axis=-1, keepdims=True)
        acc = acc + _dot(p.astype(BF16), vr[0].astype(BF16))
    m_ref[...] = m_new
    l_ref[...] = l_new
    acc_ref[...] = acc

    @pl.when(step == pl.num_programs(1) - 1)
    def _():
        lam = _lam_value(lam_ref, lam_init)
        o = acc / l_new
        for h in range(H_B):
            top = HEAD_ROWS * h
            d = o[top:top + TOK_PAD] - lam * o[top + TOK_PAD:top + HEAD_ROWS]
            o_ref[0, :, h * DV_B:(h + 1) * DV_B] = ((_rms(d) * g_ref[...]) * (1.0 - lam_init)).astype(BF16)


def _diff_attn_decode(q_rows, k_new, v_new, cache_k, cache_v, page_base, page_table, lam_vecs, g_diff, lam_init):
    b = q_rows.shape[0]
    n_steps = page_table.shape[1] // PAGES_PER_STEP
    per_b = lambda r, w: pl.BlockSpec((1, r, w), lambda bi, s, pt: (bi, 0, 0))

    def page_spec(j):
        return pl.BlockSpec((1, KV_ROWS, LANES),
                            lambda bi, s, pt: (page_base + pt[bi, s * PAGES_PER_STEP + j], 0, 0))

    pages = [page_spec(j) for j in range(PAGES_PER_STEP)]
    grid_spec = pltpu.PrefetchScalarGridSpec(
        num_scalar_prefetch=1,
        grid=(b, n_steps),
        in_specs=[per_b(Q_ROWS, LANES), per_b(DEC_SEQ * H_B, LANES), per_b(DEC_SEQ * H_B, LANES)]
        + pages + pages + [pl.BlockSpec(lam_vecs.shape, lambda bi, s, pt: (0, 0)),
                           pl.BlockSpec((1, DV_B), lambda bi, s, pt: (0, 0))],
        out_specs=per_b(TOK_PAD, H_B * DV_B),
        scratch_shapes=[pltpu.VMEM((Q_ROWS, 1), F32), pltpu.VMEM((Q_ROWS, 1), F32),
                        pltpu.VMEM((Q_ROWS, DV_B), F32)],
    )
    return pl.pallas_call(
        functools.partial(_dattn_dec_kernel, lam_init, PAGES_PER_STEP),
        grid_spec=grid_spec,
        out_shape=jax.ShapeDtypeStruct((b, TOK_PAD, H_B * DV_B), BF16),
        compiler_params=_cparams("arbitrary", "arbitrary"),
        name="diff_attn_decode",
    )(page_table, q_rows, k_new, v_new, *([cache_k] * PAGES_PER_STEP), *([cache_v] * PAGES_PER_STEP),
      lam_vecs, g_diff.reshape(1, DV_B))


def _decode_q_rows(qb):
    b = qb.shape[0]
    q = qb.reshape(b, DEC_SEQ, H_B, DV_B).transpose(0, 2, 1, 3)
    q = jnp.pad(q, ((0, 0), (0, 0), (0, TOK_PAD - DEC_SEQ), (0, 0)))[:, :, None]
    keep = (np.arange(DV_B)[None, :] // DH_B) == np.arange(2)[:, None]
    keep = jnp.asarray(keep)[None, None, :, None, :]
    return jnp.where(keep, q, jnp.zeros_like(q)).reshape(b, Q_ROWS, DV_B)


def _gla_tables(chunk):
    idx = np.arange(chunk)
    i, t = idx[:, None], idx[None, :]
    mats = [t <= i]
    masks = [i == t]
    s = 1
    while s < chunk:
        same = (t // s) == (i // s)
        odd = ((i // s) % 2) == 1
        if 2 * s <= GLA_MATMUL_LEVEL_SPAN:
            mats.append(same & np.where(odd, t <= i, t > i))
        masks.append(((i // (2 * s)) == (t // (2 * s))) & odd & (((t // s) % 2) == 0))
        s *= 2
    return (jnp.asarray(np.concatenate(mats, axis=0).astype(np.float32), BF16),
            jnp.asarray(np.stack(masks).astype(np.float32)))


def _gla_kernel(chunk, n_chunks, seqs, valid, q_ref, k_ref, g_ref, v_ref, r_ref, s0_ref, gn_ref, m_ref, p_ref,
                o_ref, s_ref, st_ref):
    i = pl.program_id(1)

    @pl.when(i == 0)
    def _():
        for bi in range(seqs):
            for h in range(H_C):
                st_ref[bi, h] = s0_ref[bi, h].T

    n_levels = p_ref.shape[0] - 1
    m_all = m_ref[...]
    live = lax.broadcasted_iota(jnp.int32, (chunk, 1), 0) < valid
    for bi, c in itertools.product(range(seqs), range(n_chunks)):
        rows = slice(c * chunk, (c + 1) * chunk)
        g = g_ref[bi, rows,:]
        if valid < chunk:
            g = jnp.where(live, g, 0.0)
        g_hi = g.astype(BF16)
        g_lo = (g - g_hi.astype(F32)).astype(BF16)
        sums = _dot(m_all, g_hi) + _dot(m_all, g_lo)
        cum = sums[0:chunk]
        exponents = [cum, cum[chunk - 1:chunk] - cum]
        n_mats = m_all.shape[0] // chunk
        for lv in range(n_levels):
            if lv + 1 < n_mats:
                exponents.append(sums[(1 + lv) * chunk:(2 + lv) * chunk])
            else:
                s = 2 ** lv
                pairs = cum.reshape(chunk // (2 * s), 2 * s, cum.shape[-1])
                rel = pairs - pairs[:, s - 1:s, :]
                upper = lax.broadcasted_iota(jnp.int32, rel.shape, 1) >= s
                exponents.append(jnp.where(upper, rel, -rel).reshape(cum.shape))
        e_all = jnp.exp(jnp.concatenate(exponents, axis=0))
        heads = []
        for h in range(H_C):
            kl = slice(h * DK_C, (h + 1) * DK_C)
            e = e_all[:, kl]
            q = q_ref[bi, rows,kl]
            k = k_ref[bi, rows,kl]
            v = v_ref[bi, rows,h * DV_C:(h + 1) * DV_C]
            if valid < chunk:
                k = jnp.where(live, k, jnp.zeros_like(k))
                v = jnp.where(live, v, jnp.zeros_like(v))
            qf = q.astype(F32)
            kf = k.astype(F32)
            level = lambda lv: e[(2 + lv) * chunk:(3 + lv) * chunk]
            prods = [_dot_nt(q, k)] + [_dot_nt((qf * level(lv)).astype(BF16), (kf * level(lv)).astype(BF16))
                                       for lv in range(n_levels)]
            heads.append(dict(v=v, prods=prods, q_in=(qf * e[0:chunk]).astype(BF16),
                              k_out=(kf * e[chunk:2 * chunk]).astype(BF16), decay=e[chunk - 1:chunk]))
        for hd in heads:
            att = hd["prods"][0] * p_ref[0]
            for lv in range(n_levels):
                att = att + hd["prods"][1 + lv] * p_ref[1 + lv]
            hd["intra"] = _dot(att.astype(BF16), hd["v"])
            hd["update"] = _dot_tn(hd["v"], hd["k_out"])
        for h, hd in enumerate(heads):
            vl = slice(h * DV_C, (h + 1) * DV_C)
            st = st_ref[bi, h]
            o = hd["intra"] + _dot_nt(hd["q_in"], st.astype(BF16))
            st_ref[bi, h] = st * hd["decay"] + hd["update"]
            gate = r_ref[bi, rows,vl].astype(F32)
            o_ref[bi, rows,vl] = ((_rms(o) * gn_ref[...]) * (gate * jax.nn.sigmoid(gate))).astype(BF16)

    @pl.when(i == pl.num_programs(1) - 1)
    def _():
        for bi in range(seqs):
            for h in range(H_C):
                s_ref[bi, h] = st_ref[bi, h].T


def _gla(q, k, g, v, r, s0, g_norm, chunk, valid):
    b, l, _ = q.shape
    blk = min(l, CHUNKS_PER_STEP * chunk)
    n_chunks = blk // chunk
    seqs = _seqs_per_step(b, l, blk)
    m_all, masks = _gla_tables(chunk)
    qk = pl.BlockSpec((seqs, blk, H_C * DK_C), lambda bi, i: (bi, i, 0))
    vr = pl.BlockSpec((seqs, blk, H_C * DV_C), lambda bi, i: (bi, i, 0))
    state = pl.BlockSpec((seqs, H_C, DK_C, DV_C), lambda bi, i: (bi, 0, 0, 0))
    const = lambda a: pl.BlockSpec(a.shape, lambda bi, i: (0,) * a.ndim)
    gn = g_norm.reshape(1, DV_C)
    return pl.pallas_call(
        functools.partial(_gla_kernel, chunk, n_chunks, seqs, valid),
        grid=(b // seqs, l // blk),
        in_specs=[qk, qk, qk, vr, vr, state, const(gn), const(m_all), const(masks)],
        out_specs=[vr, state],
        out_shape=[jax.ShapeDtypeStruct((b, l, H_C * DV_C), BF16),
                   jax.ShapeDtypeStruct((b, H_C, DK_C, DV_C), F32)],
        scratch_shapes=[pltpu.VMEM((seqs, H_C, DV_C, DK_C), F32)],
        compiler_params=_cparams("arbitrary", "arbitrary"),
        name="gla",
    )(q, k, g, v, r, s0, gn, m_all, masks)


def _rope_tables(pos):
    inv = ROPE_THETA ** (-jnp.arange(0, DK_A, 2, dtype=F32) / DK_A)
    ang = pos.astype(F32)[:, None] * inv[None, :]
    cos = jnp.tile(jnp.cos(ang), (1, 4))
    sin = jnp.tile(jnp.concatenate([-jnp.sin(ang), jnp.sin(ang)], axis=1), (1, 2))
    return cos, sin


def _pad_seq(t, batch, seq_len):
    t = t.reshape(batch, seq_len, t.shape[-1])
    return jnp.pad(t, ((0, 0), (0, SAMPLE_PAD - seq_len), (0, 0)))


def _unpad_seq(t, seq_len):
    return t[:, :seq_len].reshape(t.shape[0] * seq_len, t.shape[-1])


def kernel(x_prompt, x_sample, cache_k, cache_v, state_ret, state_gla, state_conv, page_table, c_prompt, c_sample, w_ada, b_ada, g_pre_mix, g_post_mix, g_pre_ffn, g_post_ffn, w_in_ab, w_out_ab, lam_q1, lam_k1, lam_q2, lam_k2, g_diff, w_in_c, w_gate_c, b_gate_c, g_gla, w_out_c, w_up, conv_w, conv_b, w_down):
    rows_p, rows_s = BATCH * SEQ, DEC_BATCH * DEC_SEQ
    xp = x_prompt.reshape(rows_p, D_MODEL)
    xs = x_sample.reshape(rows_s, D_MODEL)

    c_all = jnp.pad(jnp.concatenate([jnp.repeat(c_sample, DEC_SEQ, axis=0), c_prompt], axis=0),
                    ((0, (-BATCH) % F32_SUBLANES), (0, 0)))
    ada_s, ada_p = _ada(c_all, rows_s, w_ada, b_ada)

    cos_p, sin_p = _rope_tables(jnp.arange(SEQ))
    cos_s, sin_s = _rope_tables(jnp.tile(PAST_LEN + jnp.arange(DEC_SEQ), DEC_BATCH))

    wup = w_up.astype(BF16)
    wdn = w_down.astype(BF16)
    k_p, v_p, k_s, v_s, ret_p, ret_s, gla_p, gla_s, conv_p, conv_s = ([] for _ in range(10))
    for l in range(DEPTH):
        ms = [_Mod(ada_s, l, which) for which in range(6)]
        mp = [_Mod(ada_p, l, which) for which in range(6)]
        if l % 2 == 0:
            e = l // 2
            lam_init = 0.8 - 0.6 * math.exp(-0.3 * l)
            lam_vecs = jnp.stack([lam_q1[e], lam_k1[e], lam_q2[e], lam_k2[e]]).astype(F32)
            w_in = w_in_ab[e].astype(BF16)
            w_out = w_out_ab[e].astype(BF16)
            w_parts = [w_out[:H_A * DV_A], w_out[H_A * DV_A:]]

            qa, ka, va, ga, qb, kb32, kb16, vb32, vb16 = _in_proj_ab(
                xp, mp[0], mp[1], g_pre_mix[l], w_in, cos_p, sin_p, SEQ)
            seq3 = lambda t: t.reshape(BATCH, SEQ, t.shape[-1])
            ret, sp = _retention(seq3(qa), seq3(ka), seq3(va), seq3(ga),
                                 jnp.zeros((BATCH, H_A, DK_A, DV_A), F32), CHUNK_RET, CHUNK_RET)
            dif = _diff_attn_prompt(seq3(qb), seq3(kb16), seq3(vb16), lam_vecs, g_diff[e], lam_init)
            xp = _out_proj([ret.reshape(rows_p, -1), dif.reshape(rows_p, -1)], w_parts, xp, mp[2],
                           g_post_mix[l], SEQ)
            k_p.append(kb32.reshape(BATCH, SEQ, H_B, 2 * DH_B))
            v_p.append(vb32.reshape(BATCH, SEQ, H_B, DV_B))
            ret_p.append(sp)

            qa, ka, va, ga, qb, kb32, kb16, vb32, vb16 = _in_proj_ab(
                xs, ms[0], ms[1], g_pre_mix[l], w_in, cos_s, sin_s, DEC_SEQ)
            pad = lambda t: _pad_seq(t, DEC_BATCH, DEC_SEQ)
            ret, ss = _retention(pad(qa), pad(ka), pad(va), pad(ga), state_ret[e], SAMPLE_PAD, DEC_SEQ)
            n_phys = cache_k.shape[1]
            kv_rows = lambda t: t.reshape(DEC_BATCH, DEC_SEQ * H_B, DV_B)
            dif = _diff_attn_decode(
                _decode_q_rows(qb.reshape(DEC_BATCH, DEC_SEQ, 512)), kv_rows(kb16), kv_rows(vb16),
                cache_k.reshape(N_EVEN * n_phys, KV_ROWS, LANES), cache_v.reshape(N_EVEN * n_phys, KV_ROWS, LANES),
                e * n_phys, page_table, lam_vecs, g_diff[e], lam_init)
            xs = _out_proj([_unpad_seq(ret, DEC_SEQ), _unpad_seq(dif, DEC_SEQ)], w_parts, xs, ms[2],
                           g_post_mix[l], DEC_SEQ)
            k_s.append(kb32.reshape(DEC_BATCH, DEC_SEQ, H_B, 2 * DH_B))
            v_s.append(vb32.reshape(DEC_BATCH, DEC_SEQ, H_B, DV_B))
            ret_s.append(ss)
        else:
            o = l // 2
            n_main = sum(C_WIDTHS_MAIN)
            w_in = w_in_c[o][:, :n_main].astype(BF16)
            w_a = jnp.pad(w_in_c[o][:, n_main:], ((0, 0), (0, LANES - GATE_RANK))).astype(BF16)
            w_gate = jnp.pad(w_gate_c[o], ((0, LANES - GATE_RANK), (0, 0))).astype(BF16)
            w_out = w_out_c[o].astype(BF16)

            q, k, v, r, lg = _in_proj_c(xp, mp[0], mp[1], g_pre_mix[l], w_in, w_a, w_gate, b_gate_c[o], SEQ)
            seq3 = lambda t: t.reshape(BATCH, SEQ, t.shape[-1])
            og, sp = _gla(seq3(q), seq3(k), seq3(lg), seq3(v), seq3(r),
                          jnp.zeros((BATCH, H_C, DK_C, DV_C), F32), g_gla[o], CHUNK_GLA, CHUNK_GLA)
            xp = _out_proj([og.reshape(rows_p, -1)], [w_out], xp, mp[2], g_post_mix[l], SEQ)
            gla_p.append(sp)

            q, k, v, r, lg = _in_proj_c(xs, ms[0], ms[1], g_pre_mix[l], w_in, w_a, w_gate, b_gate_c[o], DEC_SEQ)
            pad = lambda t: _pad_seq(t, DEC_BATCH, DEC_SEQ)
            og, ss = _gla(pad(q), pad(k), pad(lg), pad(v), pad(r), state_gla[o], g_gla[o], SAMPLE_PAD, DEC_SEQ)
            xs = _out_proj([_unpad_seq(og, DEC_SEQ)], [w_out], xs, ms[2], g_post_mix[l], DEC_SEQ)
            gla_s.append(ss)

        xp, cbp = _conv_ffn(xp, mp[3], mp[4], mp[5], g_pre_ffn[l], g_post_ffn[l], l, wup, conv_w[l], conv_b[l],
                            wdn, SEQ)
        xs, up_s = _conv_ffn(xs, ms[3], ms[4], ms[5], g_pre_ffn[l], g_post_ffn[l], l, wup, conv_w[l], conv_b[l],
                             wdn, DEC_SEQ, state=state_conv[l])
        conv_p.append(cbp)
        conv_s.append(up_s.reshape(DEC_BATCH, DEC_SEQ, 2 * D_FF)[:, DEC_SEQ - (CONV_W - 1):])

    stack = lambda ts: ts[0][None] if len(ts) == 1 else jnp.stack(ts)
    return (xp.reshape(BATCH, SEQ, D_MODEL), xs.reshape(DEC_BATCH, DEC_SEQ, D_MODEL),
            stack(k_p), stack(v_p), stack(k_s), stack(v_s),
            stack(ret_p), stack(ret_s), stack(gla_p), stack(gla_s),
            stack(conv_p), stack(conv_s))
```
